```python
import math
import jax
import jax.numpy as jnp
from jax import lax
import numpy as np

D_MODEL = 2048
BATCH = 4
SEQ = 4096
DEPTH = 4

HEAD_DIM = 64
MOBA_HEADS = 12
MOBA_BLOCK = 256
MOBA_TOPK = 3
MOBA_Q_CHUNK = 32
MOBA_WIDTH = MOBA_HEADS * HEAD_DIM
RWKV_HEADS = 12
RWKV_WIDTH = RWKV_HEADS * HEAD_DIM
RWKV_DECAY_LORA = 64
RWKV_A_LORA = 64
RWKV_MV_LORA = 32
RWKV_GATE_LORA = 128
RWKV_COLS = 3 * RWKV_WIDTH + RWKV_DECAY_LORA + RWKV_A_LORA + RWKV_GATE_LORA
RWKV_GN_EPS = 64e-5
DIL_GROUPS = ((128, 1), (512, 4), (2048, 16))
DIL_HEADS_PER_GROUP = 4
DIL_HEADS = DIL_HEADS_PER_GROUP * len(DIL_GROUPS)
DIL_WIDTH = DIL_HEADS * HEAD_DIM
DIL_OUT_WIDTH = DIL_HEADS_PER_GROUP * HEAD_DIM
REL_BUCKETS = 32
REL_MAX_DISTANCE = 2048
ATTN_HEADS = MOBA_HEADS + DIL_HEADS
N_BRANCH = 3
OFF_A = 0
OFF_C = OFF_A + 3 * MOBA_WIDTH
OFF_B = OFF_C + 3 * DIL_WIDTH
OFF_G = OFF_B + RWKV_COLS
IN_COLS = OFF_G + N_BRANCH * D_MODEL
MOE_GROUPS = 8
MOE_EXPERTS_PER_GROUP = 8
MOE_EXPERTS = MOE_GROUPS * MOE_EXPERTS_PER_GROUP
MOE_TOPK = 2
EXPERT_FF = D_MODEL // 8
MOE_BLOCK = 256
LN_EPS = 1e-5
DEEPNORM_ALPHA = (2 * DEPTH) ** 0.25
DEEPNORM_BETA = (8 * DEPTH) ** -0.25

kernel_name = 'hybrid_moba_rwkv7_dilated_hmoe'


def layer_norm(x, g, b):
    xf = x.astype(jnp.float32)
    mu = jnp.mean(xf, axis=-1, keepdims=True)
    var = jnp.mean(jnp.square(xf - mu), axis=-1, keepdims=True)
    return ((xf - mu) * lax.rsqrt(var + LN_EPS) * g + b).astype(x.dtype)


def t5_bucket(dist):
    n = jnp.maximum(dist, 0)
    max_exact = REL_BUCKETS // 2
    nf = jnp.maximum(n, 1).astype(jnp.float32)
    large = max_exact + (jnp.log(nf / max_exact) / math.log(REL_MAX_DISTANCE / max_exact)
                         * (REL_BUCKETS - max_exact)).astype(jnp.int32)
    large = jnp.minimum(large, REL_BUCKETS - 1)
    return jnp.where(n < max_exact, n, large)


def moba_attention(q, k, v, bias_h):
    B, H, L, hd = q.shape
    nb = L // MOBA_BLOCK
    n_sel = min(MOBA_TOPK, nb)
    scale = hd ** -0.5
    kb = k.reshape(B, H, nb, MOBA_BLOCK, hd)
    vb = v.reshape(B, H, nb, MOBA_BLOCK, hd)
    k_mean = jnp.mean(kb.astype(jnp.float32), axis=3)
    blk_ids = jnp.arange(nb)
    in_blk = jnp.arange(MOBA_BLOCK)
    head_ix = jnp.arange(H)[None, :, None, None, None]
    gather = jax.vmap(jax.vmap(lambda xb, idx: xb[idx]))

    def one_chunk(ci):
        start = ci * MOBA_Q_CHUNK
        qc = lax.dynamic_slice_in_dim(q, start, MOBA_Q_CHUNK, axis=2)
        qpos = start + jnp.arange(MOBA_Q_CHUNK)
        qblk = start // MOBA_BLOCK
        gate = jnp.einsum('bhqd,bhnd->bhqn', qc.astype(jnp.float32), k_mean)
        gate = jnp.where(blk_ids < qblk, gate, -jnp.inf)
        _, sel = lax.top_k(gate, n_sel)
        k_sel = gather(kb, sel)
        v_sel = gather(vb, sel)
        s_sel = jnp.einsum('bhqd,bhqnjd->bhqnj', qc, k_sel).astype(jnp.float32) * scale
        kpos = sel[..., None] * MOBA_BLOCK + in_blk
        s_sel = s_sel + bias_h[head_ix, t5_bucket(qpos[:, None, None] - kpos)]
        s_sel = jnp.where((sel < qblk)[..., None], s_sel, -jnp.inf)
        k_own = lax.dynamic_index_in_dim(kb, qblk, axis=2, keepdims=False)
        v_own = lax.dynamic_index_in_dim(vb, qblk, axis=2, keepdims=False)
        s_own = jnp.einsum('bhqd,bhjd->bhqj', qc, k_own).astype(jnp.float32) * scale
        d_own = qpos[:, None] - (qblk * MOBA_BLOCK + in_blk)[None, :]
        s_own = s_own + bias_h[:, t5_bucket(d_own)]
        s_own = jnp.where(d_own >= 0, s_own, -jnp.inf)
        logits = jnp.concatenate([s_sel.reshape(B, H, MOBA_Q_CHUNK, n_sel * MOBA_BLOCK), s_own], axis=-1)
        p = jax.nn.softmax(logits, axis=-1)
        p_sel = p[..., :n_sel * MOBA_BLOCK].reshape(B, H, MOBA_Q_CHUNK, n_sel, MOBA_BLOCK).astype(v.dtype)
        p_own = p[..., n_sel * MOBA_BLOCK:].astype(v.dtype)
        return (jnp.einsum('bhqnj,bhqnjd->bhqd', p_sel, v_sel)
                + jnp.einsum('bhqj,bhjd->bhqd', p_own, v_own))

    out = lax.map(one_chunk, jnp.arange(L // MOBA_Q_CHUNK))
    return out.transpose(1, 2, 0, 3, 4).reshape(B, H, L, hd)


def dilated_group(q, k, v, bias_h, window, dilation):
    B, G, L, hd = q.shape
    span = window // dilation
    M = L // dilation
    Mp = -(-M // span) * span
    nb = Mp // span

    def to_res(t):
        t = t.reshape(B, G, M, dilation, hd).transpose(0, 1, 3, 2, 4)
        t = jnp.pad(t, ((0, 0), (0, 0), (0, 0), (0, Mp - M), (0, 0)))
        return t.reshape(B, G, dilation, nb, span, hd)

    def with_prev(t):
        prev = jnp.pad(t, ((0, 0), (0, 0), (0, 0), (1, 0), (0, 0), (0, 0)))[:, :, :, :nb]
        return jnp.concatenate([prev, t], axis=4)

    qb, kb, vb = to_res(q), to_res(k), to_res(v)
    kc, vc = with_prev(kb), with_prev(vb)
    s = jnp.einsum('bgrnid,bgrnjd->bgrnij', qb, kc).astype(jnp.float32) * hd ** -0.5
    rel = span + jnp.arange(span)[:, None] - jnp.arange(2 * span)[None, :]
    n_k = jnp.arange(nb)[:, None, None] * span - span + jnp.arange(2 * span)[None, None, :]
    valid = (rel >= 0) & (rel <= span) & (n_k >= 0)
    bias = bias_h[:, t5_bucket(rel * dilation)]
    s = jnp.where(valid, s + bias[None, :, None, None], -jnp.inf)
    m = jnp.max(s, axis=-1)
    e = jnp.exp(s - m[..., None])
    l = jnp.sum(e, axis=-1)
    o = jnp.einsum('bgrnij,bgrnjd->bgrnid', e, vc.astype(jnp.float32)) / l[..., None]

    def from_res(t):
        tail = t.shape[5:]
        t = t.reshape((B, G, dilation, Mp) + tail)[:, :, :, :M]
        return jnp.moveaxis(t, 2, 3).reshape((B, G, L) + tail)

    return from_res(o), from_res(m), from_res(l)


def dilated_attention(q, k, v, bias_h):
    outs, maxs, dens = [], [], []
    for g, (window, dilation) in enumerate(DIL_GROUPS):
        sl = slice(g * DIL_HEADS_PER_GROUP, (g + 1) * DIL_HEADS_PER_GROUP)
        o, m, l = dilated_group(q[:, sl], k[:, sl], v[:, sl], bias_h[sl], window, dilation)
        outs.append(o)
        maxs.append(m)
        dens.append(l)
    o_all, m_all, l_all = jnp.stack(outs), jnp.stack(maxs), jnp.stack(dens)
    wgt = l_all * jnp.exp(m_all - jnp.max(m_all, axis=0))
    return jnp.sum(wgt[..., None] * o_all, axis=0) / jnp.sum(wgt, axis=0)[..., None]


def rwkv7_time_mix(z, v_first, v_res, mu, w0, w_up, a0, a_up, g_up, k_k, k_a, r_k, ln_g, ln_b):
    B, S, _ = z.shape
    H, N, W = RWKV_HEADS, HEAD_DIM, RWKV_WIDTH
    zf = z.astype(jnp.float32)
    z_prev = jnp.pad(zf, ((0, 0), (1, 0), (0, 0)))[:, :S]
    zf = zf + mu * (z_prev - zf)
    r, k, v = zf[..., :W], zf[..., W:2 * W], zf[..., 2 * W:3 * W]
    o = 3 * W
    w_lo = zf[..., o:o + RWKV_DECAY_LORA]
    a_lo = zf[..., o + RWKV_DECAY_LORA:o + RWKV_DECAY_LORA + RWKV_A_LORA]
    g_lo = zf[..., o + RWKV_DECAY_LORA + RWKV_A_LORA:]
    log_w = -jax.nn.softplus(-(w0 + jnp.tanh(w_lo) @ w_up)) - 0.5
    decay = jnp.exp(-jnp.exp(log_w))
    a = jax.nn.sigmoid(a0 + a_lo @ a_up)
    g = jax.nn.sigmoid(g_lo) @ g_up
    if v_res is None:
        v_first = v
    else:
        v0, mv_down, mv_up = v_res
        v = v + (v_first - v) * jax.nn.sigmoid(v0 + (v @ mv_down) @ mv_up)
    to_h = lambda t: t.reshape(B, S, H, N)
    kk = to_h(k * k_k)
    kk = kk / jnp.maximum(jnp.sqrt(jnp.sum(kk * kk, axis=-1, keepdims=True)), 1e-12)
    k = k * (1.0 + (a - 1.0) * k_a)
    rh, wh, kh, vh, ah = to_h(r), to_h(decay), to_h(k), to_h(v), to_h(a)
    xs = tuple(jnp.moveaxis(t, 1, 0) for t in (rh, wh, kh, vh, kk, ah))

    def step(state, inp):
        r_t, w_t, k_t, v_t, kk_t, a_t = inp
        s_kk = jnp.einsum('bhij,bhj->bhi', state, kk_t)
        state = (state * w_t[:, :, None, :]
                 - s_kk[..., None] * (kk_t * a_t)[:, :, None, :]
                 + v_t[..., None] * k_t[:, :, None, :])
        return state, jnp.einsum('bhij,bhj->bhi', state, r_t)

    state0 = jnp.zeros((B, H, N, N), jnp.float32)
    _, y = lax.scan(step, state0, xs)
    y = jnp.moveaxis(y, 0, 1)
    mean = jnp.mean(y, axis=-1, keepdims=True)
    var = jnp.mean(jnp.square(y - mean), axis=-1, keepdims=True)
    y = ((y - mean) * lax.rsqrt(var + RWKV_GN_EPS)).reshape(B, S, W) * ln_g + ln_b
    bonus = jnp.sum(rh * kh * r_k.reshape(H, N), axis=-1, keepdims=True) * vh
    y = (y + bonus.reshape(B, S, W)) * g
    return y, v_first


def token_mixer(h, bias_h, v_first, v_res, w_in, p_a, p_b, p_c, w_o,
                mu, w0, w_up, a0, a_up, g_up, k_k, k_a, r_k, ln_g, ln_b):
    B, S, D = h.shape
    L = -(-S // MOBA_BLOCK) * MOBA_BLOCK
    proj = h @ w_in

    def cols(off, width):
        return proj[..., off:off + width]

    def heads(t, n):
        t = jnp.pad(t, ((0, 0), (0, L - S), (0, 0)))
        return t.reshape(B, L, n, HEAD_DIM).transpose(0, 2, 1, 3)

    qa = heads(cols(OFF_A, MOBA_WIDTH), MOBA_HEADS)
    ka = heads(cols(OFF_A + MOBA_WIDTH, MOBA_WIDTH), MOBA_HEADS)
    va = heads(cols(OFF_A + 2 * MOBA_WIDTH, MOBA_WIDTH), MOBA_HEADS)
    o_a = moba_attention(qa, ka, va, bias_h[:MOBA_HEADS])
    o_a = o_a[:, :, :S].transpose(0, 2, 1, 3).reshape(B, S, MOBA_WIDTH).astype(h.dtype)
    qc = heads(cols(OFF_C, DIL_WIDTH), DIL_HEADS)
    kc = heads(cols(OFF_C + DIL_WIDTH, DIL_WIDTH), DIL_HEADS)
    vc = heads(cols(OFF_C + 2 * DIL_WIDTH, DIL_WIDTH), DIL_HEADS)
    o_c = dilated_attention(qc, kc, vc, bias_h[MOBA_HEADS:])
    o_c = o_c[:, :, :S].transpose(0, 2, 1, 3).reshape(B, S, DIL_OUT_WIDTH).astype(h.dtype)
    o_b, v_first = rwkv7_time_mix(cols(OFF_B, RWKV_COLS), v_first, v_res, mu, w0, w_up,
                                  a0, a_up, g_up, k_k, k_a, r_k, ln_g, ln_b)
    o_b = o_b.astype(h.dtype)
    gates = jax.nn.sigmoid(cols(OFF_G, N_BRANCH * D).astype(jnp.float32)).astype(h.dtype)
    g_a, g_b, g_c = gates[..., :D], gates[..., D:2 * D], gates[..., 2 * D:]
    merged = g_a * (o_a @ p_a) + g_b * (o_b @ p_b) + g_c * (o_c @ p_c)
    return merged @ w_o, v_first


def routed_experts(t, expert_id, gate_w, w_gate, w_up, w_down):
    T, D = t.shape
    E = w_gate.shape[0]
    A = T * MOE_TOPK
    n_blocks = (A + E * (MOE_BLOCK - 1) + MOE_BLOCK - 1) // MOE_BLOCK
    flat_e = expert_id.reshape(A)
    flat_tok = jnp.repeat(jnp.arange(T, dtype=jnp.int32), MOE_TOPK)
    flat_w = gate_w.reshape(A)
    order = jnp.argsort(flat_e)
    e_s, tok_s, w_s = flat_e[order], flat_tok[order], flat_w[order]
    counts = jnp.bincount(flat_e, length=E)
    padded = (counts + MOE_BLOCK - 1) // MOE_BLOCK * MOE_BLOCK
    pad_end = jnp.cumsum(padded)
    pad_start = pad_end - padded
    start = jnp.cumsum(counts) - counts
    dest = pad_start[e_s] + jnp.arange(A) - start[e_s]
    slot_tok = jnp.full((n_blocks * MOE_BLOCK,), T, jnp.int32).at[dest].set(tok_s)
    slot_w = jnp.zeros((n_blocks * MOE_BLOCK,), jnp.float32).at[dest].set(w_s)
    block_e = jnp.minimum(jnp.searchsorted(pad_end, jnp.arange(n_blocks) * MOE_BLOCK, side='right'), E - 1)
    t_pad = jnp.concatenate([t, jnp.zeros((1, D), t.dtype)], axis=0)

    def run_block(args):
        toks, e = args
        xb = t_pad[toks]
        hid = jax.nn.silu(xb @ w_gate[e]) * (xb @ w_up[e])
        return hid @ w_down[e]

    out = lax.map(run_block, (slot_tok.reshape(n_blocks, MOE_BLOCK), block_e))
    out = out.reshape(n_blocks * MOE_BLOCK, D) * slot_w[:, None].astype(t.dtype)
    return jnp.zeros((T + 1, D), t.dtype).at[slot_tok].add(out)[:T]


def hier_moe(h, w_grp, b_grp, w_exp, b_exp, w_gate, w_up, w_down):
    B, S, D = h.shape
    T = B * S
    t = h.reshape(T, D)
    tf = t.astype(jnp.float32)
    grp_prob = jax.nn.softmax(tf @ w_grp.astype(jnp.float32) + b_grp.astype(jnp.float32), axis=-1)
    grp_p, grp_i = lax.top_k(grp_prob, 1)
    exp_logits = (tf @ w_exp.astype(jnp.float32) + b_exp.astype(jnp.float32)).reshape(T, MOE_GROUPS, MOE_EXPERTS_PER_GROUP)
    in_grp = exp_logits[jnp.arange(T), grp_i[:, 0]]
    top_l, top_i = lax.top_k(in_grp, MOE_TOPK)
    gate_w = grp_p * jax.nn.softmax(top_l, axis=-1)
    expert_id = grp_i * MOE_EXPERTS_PER_GROUP + top_i
    return routed_experts(t, expert_id, gate_w, w_gate, w_up, w_down).reshape(B, S, D)


def setup_inputs(seed: int = 0) -> dict:
    key = jax.random.key(seed)
    keys = iter(jax.random.split(key, 48))

    def nrm(shape, scale):
        return jax.random.normal(next(keys), shape, jnp.float32) * scale

    def unif(shape):
        return jax.random.uniform(next(keys), shape, jnp.float32)

    D, L, W = D_MODEL, DEPTH, RWKV_WIDTH
    E, F, G = MOE_EXPERTS, EXPERT_FF, MOE_GROUPS
    n_res = max(L - 1, 0)
    col_scale = np.ones((IN_COLS,), np.float32)
    col_scale[OFF_A + 2 * MOBA_WIDTH:OFF_A + 3 * MOBA_WIDTH] = DEEPNORM_BETA
    col_scale[OFF_C + 2 * DIL_WIDTH:OFF_C + 3 * DIL_WIDTH] = DEEPNORM_BETA
    col_scale[OFF_B + 2 * W:OFF_B + 3 * W] = DEEPNORM_BETA
    return {
        'x': nrm((BATCH, SEQ, D), 1.0),
        'c': nrm((BATCH, D), 1.0),
        'rel_bias': nrm((REL_BUCKETS, ATTN_HEADS), 0.5),
        'w_in': nrm((L, D, IN_COLS), D ** -0.5) * jnp.asarray(col_scale),
        'p_a': nrm((L, MOBA_WIDTH, D), MOBA_WIDTH ** -0.5),
        'p_b': nrm((L, W, D), W ** -0.5),
        'p_c': nrm((L, DIL_OUT_WIDTH, D), DIL_OUT_WIDTH ** -0.5),
        'w_o': nrm((L, D, D), DEEPNORM_BETA * D ** -0.5),
        'rwkv_mu': unif((L, RWKV_COLS)),
        'rwkv_w0': -6.5 + 5.0 * unif((L, W)) ** 0.85,
        'rwkv_w_up': nrm((L, RWKV_DECAY_LORA, W), 0.5 * RWKV_DECAY_LORA ** -0.5),
        'rwkv_a0': nrm((L, W), 0.1),
        'rwkv_a_up': nrm((L, RWKV_A_LORA, W), 0.5 * RWKV_A_LORA ** -0.5),
        'rwkv_g_up': nrm((L, RWKV_GATE_LORA, W), RWKV_GATE_LORA ** -0.5),
        'rwkv_k_k': 0.85 + nrm((L, W), 0.02),
        'rwkv_k_a': 1.0 + nrm((L, W), 0.02),
        'rwkv_r_k': nrm((L, W), 0.1),
        'rwkv_ln_g': 1.0 + nrm((L, W), 0.02),
        'rwkv_ln_b': nrm((L, W), 0.02),
        'rwkv_v0': 1.0 + nrm((n_res, W), 0.1),
        'rwkv_mv_down': nrm((n_res, W, RWKV_MV_LORA), W ** -0.5),
        'rwkv_mv_up': nrm((n_res, RWKV_MV_LORA, W), 0.5 * RWKV_MV_LORA ** -0.5),
        'w_ada': nrm((L, D, 6 * D), 0.2 * D ** -0.5),
        'b_ada': nrm((L, 6 * D), 0.02),
        'ln1_g': 1.0 + nrm((L, D), 0.02),
        'ln1_b': nrm((L, D), 0.02),
        'ln2_g': 1.0 + nrm((L, D), 0.02),
        'ln2_b': nrm((L, D), 0.02),
        'router_grp_w': nrm((L, D, G), D ** -0.5),
        'router_grp_b': nrm((L, G), 0.01),
        'router_exp_w': nrm((L, D, E), D ** -0.5),
        'router_exp_b': nrm((L, E), 0.01),
        'exp_w_gate': nrm((L, E, D, F), D ** -0.5),
        'exp_w_up': nrm((L, E, D, F), D ** -0.5),
        'exp_w_down': nrm((L, E, F, D), DEEPNORM_BETA * F ** -0.5),
    }


def reference(x, c, rel_bias, w_in, p_a, p_b, p_c, w_o, rwkv_mu, rwkv_w0, rwkv_w_up, rwkv_a0,
              rwkv_a_up, rwkv_g_up, rwkv_k_k, rwkv_k_a, rwkv_r_k, rwkv_ln_g, rwkv_ln_b, rwkv_v0,
              rwkv_mv_down, rwkv_mv_up, w_ada, b_ada, ln1_g, ln1_b, ln2_g, ln2_b, router_grp_w,
              router_grp_b, router_exp_w, router_exp_b, exp_w_gate, exp_w_up, exp_w_down):
    bias_h = rel_bias.T.astype(jnp.float32)
    cond = jax.nn.silu(c)
    v_first = None
    for l in range(DEPTH):
        mod = cond @ w_ada[l] + b_ada[l]
        sh1, sc1, g1, sh2, sc2, g2 = [m[:, None, :] for m in jnp.split(mod, 6, axis=-1)]
        h = x * (1.0 + sc1) + sh1
        v_res = None if l == 0 else (rwkv_v0[l - 1], rwkv_mv_down[l - 1], rwkv_mv_up[l - 1])
        mix, v_first = token_mixer(h, bias_h, v_first, v_res, w_in[l], p_a[l], p_b[l], p_c[l], w_o[l],
                                   rwkv_mu[l], rwkv_w0[l], rwkv_w_up[l], rwkv_a0[l], rwkv_a_up[l],
                                   rwkv_g_up[l], rwkv_k_k[l], rwkv_k_a[l], rwkv_r_k[l],
                                   rwkv_ln_g[l], rwkv_ln_b[l])
        x = layer_norm(DEEPNORM_ALPHA * x + (1.0 + g1) * mix, ln1_g[l], ln1_b[l])
        h = x * (1.0 + sc2) + sh2
        ffn = hier_moe(h, router_grp_w[l], router_grp_b[l], router_exp_w[l], router_exp_b[l],
                       exp_w_gate[l], exp_w_up[l], exp_w_down[l])
        x = layer_norm(DEEPNORM_ALPHA * x + (1.0 + g2) * ffn, ln2_g[l], ln2_b[l])
    return x
```

```python
import functools
import math

import jax
import jax.numpy as jnp
import numpy as np
from jax import lax
from jax.experimental import pallas as pl
from jax.experimental.pallas import tpu as pltpu

HEAD_DIM = 64
LANES = 128
MOBA_HEADS = 12
MOBA_BLOCK = 256
MOBA_TOPK = 3
RWKV_HEADS = 12
RWKV_DECAY_LORA = 64
RWKV_A_LORA = 64
RWKV_MV_LORA = 32
RWKV_GATE_LORA = 128
RWKV_GN_EPS = 64e-5
RWKV_CHUNK = 64
DIL_GROUPS = ((128, 1), (512, 4), (2048, 16))
DIL_HEADS_PER_GROUP = 4
DIL_SPAN = 128
REL_BUCKETS = 32
REL_MAX_DISTANCE = 2048
MOE_GROUPS = 8
MOE_EXPERTS_PER_GROUP = 8
MOE_TOPK = 2
MOE_BLOCK = 256
LN_EPS = 1e-5
NEG = -1e30
VMEM_LIMIT = 56 * 1024 * 1024
HI = lax.Precision.HIGHEST


def _params(sem):
    return pltpu.CompilerParams(dimension_semantics=sem, vmem_limit_bytes=VMEM_LIMIT)


def _sigmoid(x):
    return 1.0 / (1.0 + jnp.exp(-x))


def _dot(a, b, precision=None):
    return jnp.dot(a, b, preferred_element_type=jnp.float32, precision=precision)


def _dot_nt(a, b, precision=None):
    return lax.dot_general(a, b, (((1,), (1,)), ((), ())), preferred_element_type=jnp.float32,
                           precision=precision)


def _dot_tn(a, b, precision=None):
    return lax.dot_general(a, b, (((0,), (0,)), ((), ())), preferred_element_type=jnp.float32,
                           precision=precision)


def _layer_norm(y, g, b):
    mu = jnp.mean(y, axis=-1, keepdims=True)
    d = y - mu
    var = jnp.mean(d * d, axis=-1, keepdims=True)
    return d * lax.rsqrt(var + LN_EPS) * g + b


def _ada_kernel(c_ref, w_ref, b_ref, o_ref):
    c = c_ref[...]
    cond = c * _sigmoid(c)
    o_ref[...] = _dot(cond, w_ref[...]) + b_ref[...]


def ada_mod(c8, w_ada, b_ada):
    L, D, N = w_ada.shape
    tn = 1024
    return pl.pallas_call(
        _ada_kernel,
        grid=(L, N // tn),
        in_specs=[pl.BlockSpec((8, D), lambda l, j: (0, 0)),
                  pl.BlockSpec((None, D, tn), lambda l, j: (l, 0, j)),
                  pl.BlockSpec((None, 1, tn), lambda l, j: (l, 0, j))],
        out_specs=pl.BlockSpec((None, 8, tn), lambda l, j: (l, 0, j)),
        out_shape=jax.ShapeDtypeStruct((L, 8, N), jnp.float32),
        compiler_params=_params(("parallel", "parallel")),
        name="ada_mod",
    )(c8, w_ada, b_ada.reshape(L, 1, N))


def _in_proj_kernel(x_ref, sc_ref, sh_ref, w_ref, o_ref, h_ref):
    @pl.when(pl.program_id(1) == 0)
    def _():
        h_ref[...] = (x_ref[...] * (1.0 + sc_ref[...]) + sh_ref[...]).astype(h_ref.dtype)

    o_ref[...] = _dot(h_ref[...], w_ref[...])


def in_proj(x2, sc, sh, w, seq, tm, tn):
    T, D = x2.shape
    N = w.shape[1]
    per = seq // tm
    return pl.pallas_call(
        _in_proj_kernel,
        grid=(T // tm, N // tn),
        in_specs=[pl.BlockSpec((tm, D), lambda i, j: (i, 0)),
                  pl.BlockSpec((None, 1, D), lambda i, j: (i // per, 0, 0)),
                  pl.BlockSpec((None, 1, D), lambda i, j: (i // per, 0, 0)),
                  pl.BlockSpec((D, tn), lambda i, j: (0, j))],
        out_specs=pl.BlockSpec((tm, tn), lambda i, j: (i, j)),
        out_shape=jax.ShapeDtypeStruct((T, N), jnp.float32),
        scratch_shapes=[pltpu.VMEM((tm, D), jnp.bfloat16)],
        compiler_params=_params(("parallel", "arbitrary")),
        name="in_proj",
    )(x2, sc, sh, w)


def _t5_bucket(dist):
    n = jnp.maximum(dist, 0)
    max_exact = REL_BUCKETS // 2
    nf = jnp.maximum(n, 1).astype(jnp.float32)
    large = max_exact + (jnp.log(nf / max_exact) / math.log(REL_MAX_DISTANCE / max_exact)
                         * (REL_BUCKETS - max_exact)).astype(jnp.int32)
    large = jnp.minimum(large, REL_BUCKETS - 1)
    return jnp.where(n < max_exact, n, large)


def _moba_n_delta(nb):
    last_start = 1
    d = np.arange(1, nb * MOBA_BLOCK + 1)
    large = 16 + (np.log(d / 16.0) / math.log(REL_MAX_DISTANCE / 16.0) * 16).astype(np.int64)
    bucket = np.where(d < 16, d, np.minimum(large, REL_BUCKETS - 1))
    last_start = int(d[bucket < REL_BUCKETS - 1].max()) + 1 if (bucket < REL_BUCKETS - 1).any() else 1
    delta = 1
    while delta * MOBA_BLOCK - (MOBA_BLOCK - 1) < last_start + 2:
        delta += 1
    return min(delta + 1, nb)


def moba_bias_table(bias_a, nb):
    nd = _moba_n_delta(nb)
    i = jnp.arange(MOBA_BLOCK)[:, None]
    j = jnp.arange(MOBA_BLOCK)[None, :]
    dist = jnp.arange(nd)[:, None, None] * MOBA_BLOCK + (i - j)[None]
    tab = bias_a[:, _t5_bucket(dist)]
    return jnp.where((dist >= 0)[None], tab, NEG)


def dil_bias_table(bias_c):
    span = DIL_SPAN
    rel = span + jnp.arange(span)[:, None] - jnp.arange(2 * span)[None, :]
    valid = (rel >= 0) & (rel <= span)
    tabs = []
    for g, (_, dilation) in enumerate(DIL_GROUPS):
        bh = bias_c[g * DIL_HEADS_PER_GROUP:(g + 1) * DIL_HEADS_PER_GROUP]
        tabs.append(jnp.where(valid[None], bh[:, _t5_bucket(rel * dilation)], NEG))
    return jnp.stack(tabs)


def _moba_kernel(q_ref, k_ref, v_ref, bias_ref, o_ref,
                 kaug_ref, vb_ref, kmaug_ref, acc_ref, *, nb, n_delta):
    qb = pl.program_id(2)
    bs = MOBA_BLOCK
    S = nb * bs
    lane = lax.broadcasted_iota(jnp.int32, (1, LANES), 1)
    head_lo = (0, HEAD_DIM)
    sel_lo = (HEAD_DIM, 0)

    @pl.when(qb == 0)
    def _():
        k = k_ref[...]
        vb_ref[...] = v_ref[...].astype(vb_ref.dtype)
        rowblk = lax.broadcasted_iota(jnp.int32, (S, LANES), 0) // bs
        lanes = lax.broadcasted_iota(jnp.int32, (S, LANES), 1)
        kmean = jnp.mean(k.reshape(nb, bs, LANES), axis=1)
        for t in range(2):
            in_head = (lanes >= head_lo[t]) & (lanes < head_lo[t] + HEAD_DIM)
            onehot = (lanes - sel_lo[t] == rowblk).astype(jnp.float32)
            kaug_ref[t] = jnp.where(in_head, k, onehot).astype(kaug_ref.dtype)
            hm = (lane >= head_lo[t]) & (lane < head_lo[t] + HEAD_DIM)
            kmaug_ref[t] = jnp.zeros((LANES, LANES), jnp.float32)
            kmaug_ref[t, sel_lo[t]:sel_lo[t] + nb, :] = jnp.where(hm, kmean, 0.0)

    q = q_ref[...]
    scale = HEAD_DIM ** -0.5
    q_aug = []
    for t in range(2):
        hm = (lane >= head_lo[t]) & (lane < head_lo[t] + HEAD_DIM)
        qh = jnp.where(hm, q, 0.0)
        gate = _dot_nt(qh, kmaug_ref[t], precision=HI)
        blk = lane - sel_lo[t]
        is_sel_lane = (blk >= 0) & (blk < nb)
        past = is_sel_lane & (blk < qb)
        gate = jnp.where(past, gate, -jnp.inf)
        cnt = jnp.zeros((bs, LANES), jnp.int32)
        for m in range(nb):
            gm = gate[:, sel_lo[t] + m:sel_lo[t] + m + 1]
            ahead = (gm > gate) | ((gm == gate) & (m < blk))
            cnt = cnt + ahead.astype(jnp.int32)
        chosen = (past & (cnt < MOBA_TOPK)) | (blk == qb)
        pen = jnp.where(is_sel_lane, jnp.where(chosen, 0.0, NEG), 0.0)
        q_aug.append(jnp.where(hm, q * scale, pen).astype(jnp.bfloat16))
        acc_ref[t] = jnp.zeros((bs, LANES), jnp.float32)

    def body(n, carry):
        out = []
        row = pl.multiple_of(n * bs, bs)
        delta = jnp.minimum(qb - n, n_delta - 1)
        vblk = vb_ref[pl.ds(row, bs), :]
        for t in range(2):
            m_prev, l_prev = carry[t]
            s = _dot_nt(q_aug[t], kaug_ref[t, pl.ds(row, bs), :]) + bias_ref[t, delta]
            m_new = jnp.maximum(m_prev, jnp.max(s, axis=-1, keepdims=True))
            alpha = jnp.exp(m_prev - m_new)
            p = jnp.exp(s - m_new)
            l_new = alpha * l_prev + jnp.sum(p, axis=-1, keepdims=True)
            acc_ref[t] = alpha * acc_ref[t] + _dot(p.astype(jnp.bfloat16), vblk)
            out.append((m_new, l_new))
        return tuple(out)

    init = tuple((jnp.full((bs, 1), NEG, jnp.float32), jnp.zeros((bs, 1), jnp.float32)) for _ in range(2))
    (m0, l0), (m1, l1) = lax.fori_loop(0, qb + 1, body, init)
    del m0, m1
    o_ref[...] = jnp.where(lane < HEAD_DIM, acc_ref[0] / l0, acc_ref[1] / l1)


def moba_attention(proj3, bias_tab, q_off, k_off, v_off):
    B, S, _ = proj3.shape
    nb = S // MOBA_BLOCK
    n_delta = bias_tab.shape[1]
    pairs = MOBA_HEADS // 2
    kern = functools.partial(_moba_kernel, nb=nb, n_delta=n_delta)
    return pl.pallas_call(
        kern,
        grid=(pairs, B, nb),
        in_specs=[pl.BlockSpec((None, MOBA_BLOCK, LANES), lambda h, b, i: (b, i, q_off + h)),
                  pl.BlockSpec((None, S, LANES), lambda h, b, i: (b, 0, k_off + h)),
                  pl.BlockSpec((None, S, LANES), lambda h, b, i: (b, 0, v_off + h)),
                  pl.BlockSpec((2, n_delta, MOBA_BLOCK, MOBA_BLOCK), lambda h, b, i: (h, 0, 0, 0))],
        out_specs=pl.BlockSpec((None, MOBA_BLOCK, LANES), lambda h, b, i: (b, i, h)),
        out_shape=jax.ShapeDtypeStruct((B, S, pairs * LANES), jnp.float32),
        scratch_shapes=[pltpu.VMEM((2, S, LANES), jnp.bfloat16),
                        pltpu.VMEM((S, LANES), jnp.bfloat16),
                        pltpu.VMEM((2, LANES, LANES), jnp.float32),
                        pltpu.VMEM((2, MOBA_BLOCK, LANES), jnp.float32)],
        compiler_params=_params(("parallel", "parallel", "arbitrary")),
        name="moba",
    )(proj3, proj3, proj3, bias_tab)


def _dil_kernel(q_ref, kp_ref, kc_ref, vp_ref, vc_ref, bias_ref, acc_ref, m_ref, l_ref, *, blocks_per_seq0):
    g = pl.program_id(1)
    blk = pl.program_id(2)
    span = DIL_SPAN
    width = DIL_HEADS_PER_GROUP * HEAD_DIM
    bps = jnp.right_shift(blocks_per_seq0, 2 * g)
    first = (blk & (bps - 1)) == 0
    lane = lax.broadcasted_iota(jnp.int32, (1, width), 1)
    col = lax.broadcasted_iota(jnp.int32, (1, 2 * span), 1)
    no_prev = first & (col < span)
    q = q_ref[...] * (HEAD_DIM ** -0.5)
    kcat = jnp.concatenate([kp_ref[...], kc_ref[...]], axis=0).astype(jnp.bfloat16)
    vcat = jnp.concatenate([vp_ref[...], vc_ref[...]], axis=0).astype(jnp.bfloat16)
    acc = jnp.zeros((span, width), jnp.float32)
    mb = jnp.zeros((span, width), jnp.float32)
    lb = jnp.zeros((span, width), jnp.float32)
    for t in range(DIL_HEADS_PER_GROUP):
        hm = (lane >= t * HEAD_DIM) & (lane < (t + 1) * HEAD_DIM)
        qh = jnp.where(hm, q, 0.0).astype(jnp.bfloat16)
        s = _dot_nt(qh, kcat) + bias_ref[t]
        s = jnp.where(no_prev, NEG, s)
        m = jnp.max(s, axis=-1, keepdims=True)
        e = jnp.exp(s - m)
        l = jnp.sum(e, axis=-1, keepdims=True)
        pv = _dot(e.astype(jnp.bfloat16), vcat)
        acc = jnp.where(hm, pv, acc)
        mb = jnp.where(hm, m, mb)
        lb = jnp.where(hm, l, lb)
    acc_ref[...] = acc
    m_ref[...] = mb
    l_ref[...] = lb


def dilated_attention(qp, kp, vp, bias_tab):
    B, G, S, W = qp.shape
    span = DIL_SPAN
    nblk = S // span
    cur = pl.BlockSpec((None, None, span, W), lambda b, g, i: (b, g, i, 0))
    prev = pl.BlockSpec((None, None, span, W), lambda b, g, i: (b, g, jnp.maximum(i - 1, 0), 0))
    out = jax.ShapeDtypeStruct((B, G, S, W), jnp.float32)
    kern = functools.partial(_dil_kernel, blocks_per_seq0=nblk)
    return pl.pallas_call(
        kern,
        grid=(B, G, nblk),
        in_specs=[cur, prev, cur, prev, cur,
                  pl.BlockSpec((None, DIL_HEADS_PER_GROUP, span, 2 * span), lambda b, g, i: (g, 0, 0, 0))],
        out_specs=[cur, cur, cur],
        out_shape=[out, out, out],
        compiler_params=_params(("parallel", "parallel", "parallel")),
        name="dilated",
    )(qp, kp, kp, vp, vp, bias_tab)


def _rwkv_prep_kernel(*refs, width, has_res, rows_per_seq):
    if has_res:
        (z_ref, zl_ref, mu_ref, w0_ref, a0_ref, wa_ref, gup_ref, kk_ref, ka_ref, bd_ref,
         vf_ref, v0_ref, mvd_ref, mvu_ref,
         r_o, lw_o, k_o, v_o, kn_o, b_o, g_o) = refs
    else:
        (z_ref, zl_ref, mu_ref, w0_ref, a0_ref, wa_ref, gup_ref, kk_ref, ka_ref, bd_ref,
         r_o, lw_o, k_o, v_o, kn_o, b_o, g_o) = refs
    i = pl.program_id(0)
    W = width
    z = z_ref[...]
    tm = z.shape[0]
    row = lax.broadcasted_iota(jnp.int32, (tm, 1), 0)
    seq_start = (i % rows_per_seq) == 0
    last = jnp.where(seq_start, 0.0, zl_ref[7:8, :])
    zp = jnp.where(row == 0, last, pltpu.roll(z, 1, 0))
    zf = z + mu_ref[...] * (zp - z)
    lora = zf[:, 3 * W:3 * W + LANES]
    lane = lax.broadcasted_iota(jnp.int32, (1, LANES), 1)
    lora = jnp.where(lane < RWKV_DECAY_LORA, jnp.tanh(lora), lora)
    wa = _dot(lora, wa_ref[...], precision=HI)
    g = _dot(_sigmoid(zf[:, 3 * W + LANES:3 * W + 2 * LANES]), gup_ref[...], precision=HI)
    g_o[...] = g
    v_all = zf[:, 2 * W:3 * W]
    if has_res:
        mix = _dot(_dot(v_all, mvd_ref[...], precision=HI), mvu_ref[...], precision=HI)
    for c in range(W // LANES):
        sl = slice(c * LANES, (c + 1) * LANES)
        x = w0_ref[:, sl] + wa[:, sl]
        sp = jnp.maximum(-x, 0.0) + jnp.log(1.0 + jnp.exp(-jnp.abs(x)))
        lw_o[:, sl] = -jnp.exp(-sp - 0.5)
        a = _sigmoid(a0_ref[:, sl] + wa[:, W + c * LANES:W + (c + 1) * LANES])
        r_o[:, sl] = zf[:, sl]
        k = zf[:, W + c * LANES:W + (c + 1) * LANES]
        v = v_all[:, sl]
        if has_res:
            v = v + (vf_ref[:, sl] - v) * _sigmoid(v0_ref[:, sl] + mix[:, sl])
        v_o[:, sl] = v
        kk = k * kk_ref[:, sl]
        ss = _dot(kk * kk, bd_ref[...], precision=HI)
        kn = kk / jnp.maximum(jnp.sqrt(ss), 1e-12)
        kn_o[:, sl] = kn
        b_o[:, sl] = kn * a
        k_o[:, sl] = k * (1.0 + (a - 1.0) * ka_ref[:, sl])


def rwkv_prep(proj, seq, tm, mu, w0, a0, wa_up, g_up, k_k, k_a, bd, res):
    T = proj.shape[0]
    W = RWKV_HEADS * HEAD_DIM
    cols = mu.shape[-1]
    row1 = lambda n: pl.BlockSpec((1, n), lambda i: (0, 0))
    full = lambda a: pl.BlockSpec(a.shape, lambda i: (0, 0))
    tile = pl.BlockSpec((tm, W), lambda i: (i, 0))
    in_specs = [pl.BlockSpec((tm, cols), lambda i: (i, 0)),
                pl.BlockSpec((8, cols), lambda i: (jnp.maximum(i * (tm // 8) - 1, 0), 0)),
                row1(cols), row1(W), row1(W), full(wa_up), full(g_up), row1(W), row1(W), full(bd)]
    args = [proj, proj, mu, w0, a0, wa_up, g_up, k_k, k_a, bd]
    if res is not None:
        v_first, v0, mvd, mvu = res
        in_specs += [tile, row1(W), full(mvd), full(mvu)]
        args += [v_first, v0, mvd, mvu]
    out = jax.ShapeDtypeStruct((T, W), jnp.float32)
    kern = functools.partial(_rwkv_prep_kernel, width=W, has_res=res is not None, rows_per_seq=seq // tm)
    return pl.pallas_call(
        kern,
        grid=(T // tm,),
        in_specs=in_specs,
        out_specs=[tile] * 7,
        out_shape=[out] * 7,
        compiler_params=_params(("parallel",)),
        name="rwkv_prep",
    )(*args)


def _rwkv_scan_kernel(r_ref, lw_ref, k_ref, v_ref, kn_ref, b_ref, g_ref, rk_ref, lng_ref, lnb_ref, bd_ref,
                      o_ref, state_ref):
    C = RWKV_CHUNK
    f32 = jnp.float32

    @pl.when(pl.program_id(2) == 0)
    def _():
        state_ref[...] = jnp.zeros_like(state_ref)

    r, lw, k, v, kn, bb = r_ref[...], lw_ref[...], k_ref[...], v_ref[...], kn_ref[...], b_ref[...]
    ti = lax.broadcasted_iota(jnp.int32, (C, C), 0)
    si = lax.broadcasted_iota(jnp.int32, (C, C), 1)
    incl = ti >= si
    eye = (ti == si).astype(f32)
    lane = lax.broadcasted_iota(jnp.int32, (1, LANES), 1)
    cum = _dot(incl.astype(f32), lw, precision=HI)
    cum_last = cum[C - 1:C, :]
    e_in = jnp.exp(cum)
    e_out = jnp.exp(-cum)
    e_tail = jnp.exp(cum_last - cum)
    a_t = -kn * jnp.exp(cum - lw)
    r_t = r * e_in
    b_t = bb * e_out
    k_t = k * e_out
    state = state_ref[...]
    a_s = _dot_nt(a_t, state, precision=HI)
    r_s = _dot_nt(r_t, state, precision=HI)
    u = jnp.zeros((C, LANES), f32)
    y = jnp.zeros((C, LANES), f32)
    ar_t = jnp.concatenate([a_t, r_t], axis=0)
    bk_t = jnp.concatenate([b_t, k_t], axis=0)
    zv = jnp.concatenate([jnp.zeros((C, LANES), f32), v], axis=0)
    t2 = lax.broadcasted_iota(jnp.int32, (2 * C, 2 * C), 0)
    s2 = lax.broadcasted_iota(jnp.int32, (2 * C, 2 * C), 1)
    s2 = jnp.where(s2 >= C, s2 - C, s2)
    keep = jnp.where(t2 >= C, t2 - C, t2 - 1) >= s2
    for t in range(2):
        hm = (lane >= t * HEAD_DIM) & (lane < (t + 1) * HEAD_DIM)
        quad = jnp.where(keep, _dot_nt(jnp.where(hm, ar_t, 0.0), bk_t, precision=HI), 0.0)
        l_abk = quad[:C, :]
        m_rbk = quad[C:, :]
        l_ab = l_abk[:, :C]
        t_inv = eye + l_ab
        pw = l_ab
        for _ in range(int(math.log2(C)) - 1):
            pw = _dot(pw, pw, precision=HI)
            t_inv = t_inv + _dot(t_inv, pw, precision=HI)
        u_h = _dot(t_inv, a_s + _dot(l_abk, zv, precision=HI), precision=HI)
        u = jnp.where(hm, u_h, u)
        y_h = _dot(m_rbk, jnp.concatenate([u_h, v], axis=0), precision=HI)
        y = jnp.where(hm, y_h, y)
    y = y + r_s
    upd = _dot_tn(u, bb * e_tail, precision=HI) + _dot_tn(v, k * e_tail, precision=HI)
    same_head = (lax.broadcasted_iota(jnp.int32, (LANES, LANES), 0) // HEAD_DIM
                 == lax.broadcasted_iota(jnp.int32, (LANES, LANES), 1) // HEAD_DIM)
    state_ref[...] = state * jnp.exp(cum_last) + jnp.where(same_head, upd, 0.0)
    bd = bd_ref[...]
    inv_n = 1.0 / HEAD_DIM
    mean = _dot(y, bd, precision=HI) * inv_n
    d = y - mean
    var = _dot(d * d, bd, precision=HI) * inv_n
    yn = d * lax.rsqrt(var + RWKV_GN_EPS) * lng_ref[...] + lnb_ref[...]
    bonus = _dot(r * k * rk_ref[...], bd, precision=HI) * v
    o_ref[...] = (yn + bonus) * g_ref[...]


def rwkv_scan(r, lw, k, v, kn, b, g, r_k, ln_g, ln_b, bd, batch):
    T, W = r.shape
    S = T // batch
    C = RWKV_CHUNK
    nc = S // C
    pairs = W // LANES
    tile = pl.BlockSpec((C, LANES), lambda bi, h, c: (bi * nc + c, h))
    row = pl.BlockSpec((1, LANES), lambda bi, h, c: (0, h))
    return pl.pallas_call(
        _rwkv_scan_kernel,
        grid=(batch, pairs, nc),
        in_specs=[tile] * 7 + [row] * 3 + [pl.BlockSpec((LANES, LANES), lambda bi, h, c: (0, 0))],
        out_specs=tile,
        out_shape=jax.ShapeDtypeStruct((T, W), jnp.float32),
        scratch_shapes=[pltpu.VMEM((LANES, LANES), jnp.float32)],
        compiler_params=_params(("parallel", "parallel", "arbitrary")),
        name="rwkv_scan",
    )(r, lw, k, v, kn, b, g, r_k, ln_g, ln_b, bd)


def _merge_kernel(oa_ref, ob_ref, acc_ref, m_ref, l_ref, ga_ref, gb_ref, gc_ref,
                  pa_ref, pb_ref, pc_ref, o_ref):
    m = m_ref[...]
    mx = jnp.max(m, axis=0)
    wgt = jnp.exp(m - mx[None])
    o_c = jnp.sum(wgt * acc_ref[...], axis=0) / jnp.sum(wgt * l_ref[...], axis=0)
    bf = jnp.bfloat16
    merged = (_sigmoid(ga_ref[...]) * _dot(oa_ref[...].astype(bf), pa_ref[...])
              + _sigmoid(gb_ref[...]) * _dot(ob_ref[...].astype(bf), pb_ref[...])
              + _sigmoid(gc_ref[...]) * _dot(o_c.astype(bf), pc_ref[...]))
    o_ref[...] = merged.astype(o_ref.dtype)


def merge_branches(o_a, o_b, acc, m, l, proj, gate_off, p_a, p_b, p_c, tm, tn):
    T = o_a.shape[0]
    D = p_a.shape[1]
    nj = D // tn
    Wc = acc.shape[-1]
    grp = pl.BlockSpec((3, tm, Wc), lambda i, j: (0, i, 0))
    gate = lambda n: pl.BlockSpec((tm, tn), lambda i, j: (i, gate_off + n * nj + j))
    wspec = lambda a: pl.BlockSpec((a.shape[0], tn), lambda i, j: (0, j))
    return pl.pallas_call(
        _merge_kernel,
        grid=(T // tm, nj),
        in_specs=[pl.BlockSpec((tm, o_a.shape[1]), lambda i, j: (i, 0)),
                  pl.BlockSpec((tm, o_b.shape[1]), lambda i, j: (i, 0)),
                  grp, grp, grp, gate(0), gate(1), gate(2), wspec(p_a), wspec(p_b), wspec(p_c)],
        out_specs=pl.BlockSpec((tm, tn), lambda i, j: (i, j)),
        out_shape=jax.ShapeDtypeStruct((T, D), jnp.bfloat16),
        compiler_params=_params(("parallel", "arbitrary")),
        name="merge",
    )(o_a, o_b, acc, m, l, proj, proj, proj, p_a, p_b, p_c)


def _out_ln_kernel(mg_ref, w_ref, x_ref, g1_ref, lg_ref, lb_ref, o_ref, *, alpha):
    mix = _dot(mg_ref[...], w_ref[...])
    y = alpha * x_ref[...] + (1.0 + g1_ref[...]) * mix
    o_ref[...] = _layer_norm(y, lg_ref[...], lb_ref[...])


def out_ln(merged, w_o, x2, g1, ln_g, ln_b, seq, tm, alpha):
    T, D = x2.shape
    per = seq // tm
    tile = pl.BlockSpec((tm, D), lambda i: (i, 0))
    row = pl.BlockSpec((1, D), lambda i: (0, 0))
    return pl.pallas_call(
        functools.partial(_out_ln_kernel, alpha=alpha),
        grid=(T // tm,),
        in_specs=[tile, pl.BlockSpec((D, D), lambda i: (0, 0)), tile,
                  pl.BlockSpec((None, 1, D), lambda i: (i // per, 0, 0)), row, row],
        out_specs=tile,
        out_shape=jax.ShapeDtypeStruct((T, D), jnp.float32),
        compiler_params=_params(("parallel",)),
        name="out_ln",
    )(merged, w_o, x2, g1, ln_g, ln_b)


def _router_kernel(x_ref, sc_ref, sh_ref, w_ref, b_ref, lg_ref, h_ref):
    h = x_ref[...] * (1.0 + sc_ref[...]) + sh_ref[...]
    h_ref[...] = h.astype(h_ref.dtype)
    lg_ref[...] = _dot(h, w_ref[...], precision=HI) + b_ref[...]


def router(x2, sc, sh, w_r, b_r, seq, tm):
    T, D = x2.shape
    per = seq // tm
    mod = pl.BlockSpec((None, 1, D), lambda i: (i // per, 0, 0))
    return pl.pallas_call(
        _router_kernel,
        grid=(T // tm,),
        in_specs=[pl.BlockSpec((tm, D), lambda i: (i, 0)), mod, mod,
                  pl.BlockSpec((D, LANES), lambda i: (0, 0)), pl.BlockSpec((1, LANES), lambda i: (0, 0))],
        out_specs=[pl.BlockSpec((tm, LANES), lambda i: (i, 0)), pl.BlockSpec((tm, D), lambda i: (i, 0))],
        out_shape=[jax.ShapeDtypeStruct((T, LANES), jnp.float32), jax.ShapeDtypeStruct((T, D), jnp.bfloat16)],
        compiler_params=_params(("parallel",)),
        name="router",
    )(x2, sc, sh, w_r, b_r)


def _experts_kernel(be_ref, nu_ref, x_ref, sw_ref, wg_ref, wu_ref, wd_ref, o_ref):
    del be_ref
    i = pl.program_id(0)

    @pl.when(i < nu_ref[0])
    def _():
        x = x_ref[...]
        gate = _dot(x, wg_ref[...])
        hid = gate * _sigmoid(gate) * _dot(x, wu_ref[...])
        o_ref[...] = _dot(hid.astype(jnp.bfloat16), wd_ref[...]) * sw_ref[...]

    @pl.when(i >= nu_ref[0])
    def _():
        o_ref[...] = jnp.zeros_like(o_ref)


def experts(xs, slot_w, block_e, n_used, w_gate, w_up, w_down):
    R, D = xs.shape
    F = w_gate.shape[-1]
    n_blocks = R // MOE_BLOCK
    grid_spec = pltpu.PrefetchScalarGridSpec(
        num_scalar_prefetch=2,
        grid=(n_blocks,),
        in_specs=[pl.BlockSpec((MOE_BLOCK, D), lambda i, be, nu: (i, 0)),
                  pl.BlockSpec((MOE_BLOCK, 1), lambda i, be, nu: (i, 0)),
                  pl.BlockSpec((None, D, F), lambda i, be, nu: (be[i], 0, 0)),
                  pl.BlockSpec((None, D, F), lambda i, be, nu: (be[i], 0, 0)),
                  pl.BlockSpec((None, F, D), lambda i, be, nu: (be[i], 0, 0))],
        out_specs=pl.BlockSpec((MOE_BLOCK, D), lambda i, be, nu: (i, 0)),
    )
    return pl.pallas_call(
        _experts_kernel,
        grid_spec=grid_spec,
        out_shape=jax.ShapeDtypeStruct((R, D), jnp.float32),
        compiler_params=_params(("arbitrary",)),
        name="experts",
    )(block_e, n_used, xs, slot_w, w_gate, w_up, w_down)


def _combine_ln_kernel(x_ref, f0_ref, f1_ref, g2_ref, lg_ref, lb_ref, o_ref, *, alpha):
    y = alpha * x_ref[...] + (1.0 + g2_ref[...]) * (f0_ref[...] + f1_ref[...])
    o_ref[...] = _layer_norm(y, lg_ref[...], lb_ref[...])


def combine_ln(x2, f0, f1, g2, ln_g, ln_b, seq, tm, alpha):
    T, D = x2.shape
    per = seq // tm
    tile = pl.BlockSpec((tm, D), lambda i: (i, 0))
    row = pl.BlockSpec((1, D), lambda i: (0, 0))
    return pl.pallas_call(
        functools.partial(_combine_ln_kernel, alpha=alpha),
        grid=(T // tm,),
        in_specs=[tile, tile, tile, pl.BlockSpec((None, 1, D), lambda i: (i // per, 0, 0)), row, row],
        out_specs=tile,
        out_shape=jax.ShapeDtypeStruct((T, D), jnp.float32),
        compiler_params=_params(("parallel",)),
        name="combine_ln",
    )(x2, f0, f1, g2, ln_g, ln_b)


def _to_residue(t, dilation):
    B, S, W = t.shape
    return t.reshape(B, S // dilation, dilation, W).transpose(0, 2, 1, 3).reshape(B, S, W)


def _from_residue(t, dilation):
    B, S, W = t.shape
    return t.reshape(B, dilation, S // dilation, W).transpose(0, 2, 1, 3).reshape(B, S, W)


def _route(logits, n_tokens):
    T = n_tokens
    E = MOE_GROUPS * MOE_EXPERTS_PER_GROUP
    grp_prob = jax.nn.softmax(logits[:, :MOE_GROUPS], axis=-1)
    grp_p, grp_i = lax.top_k(grp_prob, 1)
    exp_logits = logits[:, MOE_GROUPS:MOE_GROUPS + E].reshape(T, MOE_GROUPS, MOE_EXPERTS_PER_GROUP)
    in_grp = jnp.take_along_axis(exp_logits, grp_i[:, :, None], axis=1)[:, 0]
    top_l, top_i = lax.top_k(in_grp, MOE_TOPK)
    gate_w = grp_p * jax.nn.softmax(top_l, axis=-1)
    expert_id = grp_i * MOE_EXPERTS_PER_GROUP + top_i
    A = T * MOE_TOPK
    n_blocks = (A + E * (MOE_BLOCK - 1) + MOE_BLOCK - 1) // MOE_BLOCK
    flat_e = expert_id.reshape(A).astype(jnp.int32)
    flat_tok = jnp.repeat(jnp.arange(T, dtype=jnp.int32), MOE_TOPK)
    flat_w = gate_w.reshape(A)
    order = jnp.argsort(flat_e)
    e_s = flat_e[order]
    counts = jnp.bincount(flat_e, length=E)
    padded = (counts + MOE_BLOCK - 1) // MOE_BLOCK * MOE_BLOCK
    pad_end = jnp.cumsum(padded)
    pad_start = pad_end - padded
    start = jnp.cumsum(counts) - counts
    dest = (pad_start[e_s] + jnp.arange(A) - start[e_s]).astype(jnp.int32)
    slot_tok = jnp.full((n_blocks * MOE_BLOCK,), T, jnp.int32).at[dest].set(flat_tok[order])
    slot_w = jnp.zeros((n_blocks * MOE_BLOCK,), jnp.float32).at[dest].set(flat_w[order])
    slot_of = jnp.zeros((A,), jnp.int32).at[order].set(dest).reshape(T, MOE_TOPK)
    block_e = jnp.minimum(jnp.searchsorted(pad_end, jnp.arange(n_blocks) * MOE_BLOCK, side='right'),
                          E - 1).astype(jnp.int32)
    n_used = (pad_end[-1] // MOE_BLOCK).astype(jnp.int32).reshape(1)
    return slot_tok, slot_w, slot_of, block_e, n_used


def kernel(x, c, rel_bias, w_in, p_a, p_b, p_c, w_o, rwkv_mu, rwkv_w0, rwkv_w_up, rwkv_a0, rwkv_a_up,
           rwkv_g_up, rwkv_k_k, rwkv_k_a, rwkv_r_k, rwkv_ln_g, rwkv_ln_b, rwkv_v0, rwkv_mv_down,
           rwkv_mv_up, w_ada, b_ada, ln1_g, ln1_b, ln2_g, ln2_b, router_grp_w, router_grp_b,
           router_exp_w, router_exp_b, exp_w_gate, exp_w_up, exp_w_down):
    B, S, D = x.shape
    depth = w_in.shape[0]
    T = B * S
    bf = jnp.bfloat16
    W = RWKV_HEADS * HEAD_DIM
    wa_w = MOBA_HEADS * HEAD_DIM
    wc_w = len(DIL_GROUPS) * DIL_HEADS_PER_GROUP * HEAD_DIM
    rw_cols = rwkv_mu.shape[-1]
    alpha = (2 * depth) ** 0.25
    off_c = 3 * wa_w
    off_b = off_c + 3 * wc_w
    off_g = off_b + rw_cols
    new_a = rw_cols
    new_c = new_a + 3 * wa_w
    new_g = new_c + 3 * wc_w
    w_in_p = jnp.concatenate([w_in[:, :, off_b:off_g], w_in[:, :, :off_b], w_in[:, :, off_g:]], axis=-1).astype(bf)
    p_a_b, p_b_b, p_c_b, w_o_b = p_a.astype(bf), p_b.astype(bf), p_c.astype(bf), w_o.astype(bf)
    wg_b, wu_b, wd_b = exp_w_gate.astype(bf), exp_w_up.astype(bf), exp_w_down.astype(bf)

    bias_h = rel_bias.T.astype(jnp.float32)
    moba_tab = moba_bias_table(bias_h[:MOBA_HEADS], S // MOBA_BLOCK)
    dil_tab = dil_bias_table(bias_h[MOBA_HEADS:])

    c8 = jnp.zeros((8, D), jnp.float32).at[:B].set(c)
    mod = ada_mod(c8, w_ada, b_ada)[:, :B]

    hd_idx = jnp.arange(LANES) // HEAD_DIM
    bd = (hd_idx[:, None] == hd_idx[None, :]).astype(jnp.float32)
    zeros_w = jnp.zeros((RWKV_DECAY_LORA, W), jnp.float32)
    pad_lora = LANES - RWKV_MV_LORA

    x2 = x.reshape(T, D)
    v_first = None
    for l in range(depth):
        sh1, sc1, g1, sh2, sc2, g2 = [m.reshape(B, 1, D) for m in jnp.split(mod[l], 6, axis=-1)]
        proj = in_proj(x2, sc1, sh1, w_in_p[l], S, 1024, 1024)
        proj3 = proj.reshape(B, S, -1)
        o_a = moba_attention(proj3, moba_tab, new_a // LANES, (new_a + wa_w) // LANES,
                             (new_a + 2 * wa_w) // LANES).reshape(T, wa_w)
        gw = DIL_HEADS_PER_GROUP * HEAD_DIM
        qs, ks, vs = [], [], []
        for g, (_, dil) in enumerate(DIL_GROUPS):
            qs.append(_to_residue(proj3[:, :, new_c + g * gw:new_c + (g + 1) * gw], dil))
            ks.append(_to_residue(proj3[:, :, new_c + wc_w + g * gw:new_c + wc_w + (g + 1) * gw], dil))
            vs.append(_to_residue(proj3[:, :, new_c + 2 * wc_w + g * gw:new_c + 2 * wc_w + (g + 1) * gw], dil))
        acc, mm, ll = dilated_attention(jnp.stack(qs, 1), jnp.stack(ks, 1), jnp.stack(vs, 1), dil_tab)
        unperm = lambda t: jnp.stack([_from_residue(t[:, g], dil).reshape(T, gw)
                                      for g, (_, dil) in enumerate(DIL_GROUPS)])
        acc, mm, ll = unperm(acc), unperm(mm), unperm(ll)
        wa_up = jnp.concatenate([jnp.concatenate([rwkv_w_up[l], zeros_w], axis=1),
                                 jnp.concatenate([zeros_w, rwkv_a_up[l]], axis=1)], axis=0)
        res = None
        if l > 0:
            res = (v_first, rwkv_v0[l - 1][None],
                   jnp.pad(rwkv_mv_down[l - 1], ((0, 0), (0, pad_lora))),
                   jnp.pad(rwkv_mv_up[l - 1], ((0, pad_lora), (0, 0))))
        r_, lw_, k_, v_, kn_, b_, g_ = rwkv_prep(proj, S, 256, rwkv_mu[l][None], rwkv_w0[l][None],
                                                 rwkv_a0[l][None], wa_up, rwkv_g_up[l], rwkv_k_k[l][None],
                                                 rwkv_k_a[l][None], bd, res)
        if l == 0:
            v_first = v_
        o_b = rwkv_scan(r_, lw_, k_, v_, kn_, b_, g_, rwkv_r_k[l][None], rwkv_ln_g[l][None],
                        rwkv_ln_b[l][None], bd, B)
        merged = merge_branches(o_a, o_b, acc, mm, ll, proj, new_g // 1024, p_a_b[l], p_b_b[l], p_c_b[l],
                                256, 1024)
        x2 = out_ln(merged, w_o_b[l], x2, g1, ln1_g[l][None], ln1_b[l][None], S, 256, alpha)
        w_r = jnp.zeros((D, LANES), jnp.float32)
        w_r = w_r.at[:, :MOE_GROUPS].set(router_grp_w[l]).at[:, MOE_GROUPS:MOE_GROUPS + router_exp_w.shape[-1]].set(
            router_exp_w[l])
        b_r = jnp.zeros((1, LANES), jnp.float32)
        b_r = b_r.at[0, :MOE_GROUPS].set(router_grp_b[l]).at[0, MOE_GROUPS:MOE_GROUPS + router_exp_b.shape[-1]].set(
            router_exp_b[l])
        logits, h2 = router(x2, sc2, sh2, w_r, b_r, S, 512)
        slot_tok, slot_w, slot_of, block_e, n_used = _route(logits, T)
        h2p = jnp.concatenate([h2, jnp.zeros((1, D), h2.dtype)], axis=0)
        y = experts(h2p[slot_tok], slot_w[:, None], block_e, n_used, wg_b[l], wu_b[l], wd_b[l])
        x2 = combine_ln(x2, y[slot_of[:, 0]], y[slot_of[:, 1]], g2, ln2_g[l][None], ln2_b[l][None], S, 512, alpha)
    return x2.reshape(B, S, D)
```

```python
import functools
import math

import jax
import jax.numpy as jnp
import numpy as np
from jax import lax
from jax.experimental import pallas as pl
from jax.experimental.pallas import tpu as pltpu

HEAD_DIM = 64
LANES = 128
MOBA_HEADS = 12
MOBA_BLOCK = 256
MOBA_TOPK = 3
RWKV_HEADS = 12
RWKV_DECAY_LORA = 64
RWKV_A_LORA = 64
RWKV_MV_LORA = 32
RWKV_GATE_LORA = 128
RWKV_GN_EPS = 64e-5
RWKV_CHUNK = 64
DIL_GROUPS = ((128, 1), (512, 4), (2048, 16))
DIL_HEADS_PER_GROUP = 4
DIL_SPAN = 128
REL_BUCKETS = 32
REL_MAX_DISTANCE = 2048
MOE_GROUPS = 8
MOE_EXPERTS_PER_GROUP = 8
MOE_TOPK = 2
MOE_BLOCK = 256
LN_EPS = 1e-5
NEG = -1e30
VMEM_LIMIT = 56 * 1024 * 1024
HI = lax.Precision.HIGHEST


def _params(sem):
    return pltpu.CompilerParams(dimension_semantics=sem, vmem_limit_bytes=VMEM_LIMIT)


def _sigmoid(x):
    return 1.0 / (1.0 + jnp.exp(-x))


def _dot(a, b, precision=None):
    return jnp.dot(a, b, preferred_element_type=jnp.float32, precision=precision)


def _dot_nt(a, b, precision=None):
    return lax.dot_general(a, b, (((1,), (1,)), ((), ())), preferred_element_type=jnp.float32,
                           precision=precision)


def _dot_tn(a, b, precision=None):
    return lax.dot_general(a, b, (((0,), (0,)), ((), ())), preferred_element_type=jnp.float32,
                           precision=precision)


def _split2(x):
    hi = x.astype(jnp.bfloat16)
    return hi, (x - hi.astype(jnp.float32)).astype(jnp.bfloat16)


def _dot3(a, b, dims=((1,), (0,))):
    (ca,), (cb,) = dims
    ah, al = _split2(a)
    bh, bl = _split2(b)
    return lax.dot_general(jnp.concatenate([ah, ah, al], axis=ca), jnp.concatenate([bh, bl, bh], axis=cb),
                           (dims, ((), ())), preferred_element_type=jnp.float32)


def _dot_sel(x, sel):
    xh, xl = _split2(x)
    sb = sel.astype(jnp.bfloat16)
    return _dot(jnp.concatenate([xh, xl], axis=1), jnp.concatenate([sb, sb], axis=0))


_NT = ((1,), (1,))
_TN = ((0,), (0,))


def _layer_norm(y, g, b):
    mu = jnp.mean(y, axis=-1, keepdims=True)
    d = y - mu
    var = jnp.mean(d * d, axis=-1, keepdims=True)
    return d * lax.rsqrt(var + LN_EPS) * g + b


def _ada_kernel(c_ref, w_ref, b_ref, o_ref):
    c = c_ref[...]
    cond = c * _sigmoid(c)
    o_ref[...] = _dot(cond, w_ref[...]) + b_ref[...]


def ada_mod(c8, w_ada, b_ada):
    L, D, N = w_ada.shape
    tn = 1024
    return pl.pallas_call(
        _ada_kernel,
        grid=(L, N // tn),
        in_specs=[pl.BlockSpec((8, D), lambda l, j: (0, 0)),
                  pl.BlockSpec((None, D, tn), lambda l, j: (l, 0, j)),
                  pl.BlockSpec((None, 1, tn), lambda l, j: (l, 0, j))],
        out_specs=pl.BlockSpec((None, 8, tn), lambda l, j: (l, 0, j)),
        out_shape=jax.ShapeDtypeStruct((L, 8, N), jnp.float32),
        compiler_params=_params(("parallel", "parallel")),
        name="ada_mod",
    )(c8, w_ada, b_ada.reshape(L, 1, N))


def _in_proj_kernel(x_ref, sc_ref, sh_ref, w_ref, o_ref, h_ref):
    @pl.when(pl.program_id(1) == 0)
    def _():
        h_ref[...] = (x_ref[...] * (1.0 + sc_ref[...]) + sh_ref[...]).astype(h_ref.dtype)

    o_ref[...] = _dot(h_ref[...], w_ref[...])


def in_proj(x2, sc, sh, w, seq, tm, tn):
    T, D = x2.shape
    N = w.shape[1]
    per = seq // tm
    return pl.pallas_call(
        _in_proj_kernel,
        grid=(T // tm, N // tn),
        in_specs=[pl.BlockSpec((tm, D), lambda i, j: (i, 0)),
                  pl.BlockSpec((None, 1, D), lambda i, j: (i // per, 0, 0)),
                  pl.BlockSpec((None, 1, D), lambda i, j: (i // per, 0, 0)),
                  pl.BlockSpec((D, tn), lambda i, j: (0, j))],
        out_specs=pl.BlockSpec((tm, tn), lambda i, j: (i, j)),
        out_shape=jax.ShapeDtypeStruct((T, N), jnp.float32),
        scratch_shapes=[pltpu.VMEM((tm, D), jnp.bfloat16)],
        compiler_params=_params(("parallel", "arbitrary")),
        name="in_proj",
    )(x2, sc, sh, w)


def _t5_bucket(dist):
    n = jnp.maximum(dist, 0)
    max_exact = REL_BUCKETS // 2
    nf = jnp.maximum(n, 1).astype(jnp.float32)
    large = max_exact + (jnp.log(nf / max_exact) / math.log(REL_MAX_DISTANCE / max_exact)
                         * (REL_BUCKETS - max_exact)).astype(jnp.int32)
    large = jnp.minimum(large, REL_BUCKETS - 1)
    return jnp.where(n < max_exact, n, large)


def _moba_n_delta(nb):
    last_start = 1
    d = np.arange(1, nb * MOBA_BLOCK + 1)
    large = 16 + (np.log(d / 16.0) / math.log(REL_MAX_DISTANCE / 16.0) * 16).astype(np.int64)
    bucket = np.where(d < 16, d, np.minimum(large, REL_BUCKETS - 1))
    last_start = int(d[bucket < REL_BUCKETS - 1].max()) + 1 if (bucket < REL_BUCKETS - 1).any() else 1
    delta = 1
    while delta * MOBA_BLOCK - (MOBA_BLOCK - 1) < last_start + 2:
        delta += 1
    return min(delta + 1, nb)


def moba_bias_table(bias_a, nb):
    nd = _moba_n_delta(nb)
    i = jnp.arange(MOBA_BLOCK)[:, None]
    j = jnp.arange(MOBA_BLOCK)[None, :]
    dist = jnp.arange(nd)[:, None, None] * MOBA_BLOCK + (i - j)[None]
    bucket = _t5_bucket(dist)[None]
    tab = jnp.zeros((bias_a.shape[0],) + dist.shape, jnp.float32)
    for b in range(REL_BUCKETS):
        tab = jnp.where(bucket == b, bias_a[:, b][:, None, None, None], tab)
    return jnp.where((dist >= 0)[None], tab, NEG)


def dil_bias_table(bias_c):
    span = DIL_SPAN
    rel = span + jnp.arange(span)[:, None] - jnp.arange(2 * span)[None, :]
    valid = (rel >= 0) & (rel <= span)
    tabs = []
    for g, (_, dilation) in enumerate(DIL_GROUPS):
        bh = bias_c[g * DIL_HEADS_PER_GROUP:(g + 1) * DIL_HEADS_PER_GROUP]
        tabs.append(jnp.where(valid[None], bh[:, _t5_bucket(rel * dilation)], NEG))
    return jnp.stack(tabs)


def _moba_kernel(q_ref, k_ref, v_ref, bias_ref, o_ref,
                 kaug_ref, vb_ref, kmaug_ref, acc_ref, *, nb, n_delta):
    qb = pl.program_id(2)
    bs = MOBA_BLOCK
    S = nb * bs
    lane = lax.broadcasted_iota(jnp.int32, (1, LANES), 1)
    head_lo = (0, HEAD_DIM)
    sel_lo = (HEAD_DIM, 0)

    @pl.when(qb == 0)
    def _():
        k = k_ref[...]
        vb_ref[...] = v_ref[...].astype(vb_ref.dtype)
        rowblk = lax.broadcasted_iota(jnp.int32, (S, LANES), 0) // bs
        lanes = lax.broadcasted_iota(jnp.int32, (S, LANES), 1)
        kmean = jnp.mean(k.reshape(nb, bs, LANES), axis=1)
        for t in range(2):
            in_head = (lanes >= head_lo[t]) & (lanes < head_lo[t] + HEAD_DIM)
            onehot = (lanes - sel_lo[t] == rowblk).astype(jnp.float32)
            kaug_ref[t] = jnp.where(in_head, k, onehot).astype(kaug_ref.dtype)
            hm = (lane >= head_lo[t]) & (lane < head_lo[t] + HEAD_DIM)
            kmaug_ref[t] = jnp.zeros((LANES, LANES), jnp.float32)
            kmaug_ref[t, sel_lo[t]:sel_lo[t] + nb, :] = jnp.where(hm, kmean, 0.0)

    q = q_ref[...]
    scale = HEAD_DIM ** -0.5
    q_aug = []
    for t in range(2):
        hm = (lane >= head_lo[t]) & (lane < head_lo[t] + HEAD_DIM)
        qh = jnp.where(hm, q, 0.0)
        gate = _dot_nt(qh, kmaug_ref[t], precision=HI)
        blk = lane - sel_lo[t]
        is_sel_lane = (blk >= 0) & (blk < nb)
        past = is_sel_lane & (blk < qb)
        gate = jnp.where(past, gate, -jnp.inf)
        cnt = jnp.zeros((bs, LANES), jnp.int32)
        for m in range(nb):
            gm = gate[:, sel_lo[t] + m:sel_lo[t] + m + 1]
            ahead = (gm > gate) | ((gm == gate) & (m < blk))
            cnt = cnt + ahead.astype(jnp.int32)
        chosen = (past & (cnt < MOBA_TOPK)) | (blk == qb)
        pen = jnp.where(is_sel_lane, jnp.where(chosen, 0.0, NEG), 0.0)
        q_aug.append(jnp.where(hm, q * scale, pen).astype(jnp.bfloat16))
        acc_ref[t] = jnp.zeros((bs, LANES), jnp.float32)

    def body(n, carry):
        out = []
        row = pl.multiple_of(n * bs, bs)
        delta = jnp.minimum(qb - n, n_delta - 1)
        vblk = vb_ref[pl.ds(row, bs), :]
        for t in range(2):
            m_prev, l_prev = carry[t]
            s = _dot_nt(q_aug[t], kaug_ref[t, pl.ds(row, bs), :]) + bias_ref[t, delta]
            m_new = jnp.maximum(m_prev, jnp.max(s, axis=-1, keepdims=True))
            alpha = jnp.exp(m_prev - m_new)
            p = jnp.exp(s - m_new)
            l_new = alpha * l_prev + jnp.sum(p, axis=-1, keepdims=True)
            acc_ref[t] = alpha * acc_ref[t] + _dot(p.astype(jnp.bfloat16), vblk)
            out.append((m_new, l_new))
        return tuple(out)

    init = tuple((jnp.full((bs, 1), NEG, jnp.float32), jnp.zeros((bs, 1), jnp.float32)) for _ in range(2))
    (m0, l0), (m1, l1) = lax.fori_loop(0, qb + 1, body, init)
    del m0, m1
    o_ref[...] = jnp.where(lane < HEAD_DIM, acc_ref[0] / l0, acc_ref[1] / l1)


def moba_attention(proj3, bias_tab, q_off, k_off, v_off):
    B, S, _ = proj3.shape
    nb = S // MOBA_BLOCK
    n_delta = bias_tab.shape[1]
    pairs = MOBA_HEADS // 2
    kern = functools.partial(_moba_kernel, nb=nb, n_delta=n_delta)
    return pl.pallas_call(
        kern,
        grid=(pairs, B, nb),
        in_specs=[pl.BlockSpec((None, MOBA_BLOCK, LANES), lambda h, b, i: (b, i, q_off + h)),
                  pl.BlockSpec((None, S, LANES), lambda h, b, i: (b, 0, k_off + h)),
                  pl.BlockSpec((None, S, LANES), lambda h, b, i: (b, 0, v_off + h)),
                  pl.BlockSpec((2, n_delta, MOBA_BLOCK, MOBA_BLOCK), lambda h, b, i: (h, 0, 0, 0))],
        out_specs=pl.BlockSpec((None, MOBA_BLOCK, LANES), lambda h, b, i: (b, i, h)),
        out_shape=jax.ShapeDtypeStruct((B, S, pairs * LANES), jnp.float32),
        scratch_shapes=[pltpu.VMEM((2, S, LANES), jnp.bfloat16),
                        pltpu.VMEM((S, LANES), jnp.bfloat16),
                        pltpu.VMEM((2, LANES, LANES), jnp.float32),
                        pltpu.VMEM((2, MOBA_BLOCK, LANES), jnp.float32)],
        compiler_params=_params(("parallel", "parallel", "arbitrary")),
        name="moba",
    )(proj3, proj3, proj3, bias_tab)


def _dil_kernel(q_ref, kp_ref, kc_ref, vp_ref, vc_ref, bias_ref, acc_ref, m_ref, l_ref, *, blocks_per_seq0):
    g = pl.program_id(1)
    blk = pl.program_id(2)
    span = DIL_SPAN
    width = DIL_HEADS_PER_GROUP * HEAD_DIM
    bps = jnp.right_shift(blocks_per_seq0, 2 * g)
    first = (blk & (bps - 1)) == 0
    lane = lax.broadcasted_iota(jnp.int32, (1, width), 1)
    col = lax.broadcasted_iota(jnp.int32, (1, 2 * span), 1)
    no_prev = first & (col < span)
    q = q_ref[...] * (HEAD_DIM ** -0.5)
    kcat = jnp.concatenate([kp_ref[...], kc_ref[...]], axis=0).astype(jnp.bfloat16)
    vcat = jnp.concatenate([vp_ref[...], vc_ref[...]], axis=0).astype(jnp.bfloat16)
    acc = jnp.zeros((span, width), jnp.float32)
    mb = jnp.zeros((span, width), jnp.float32)
    lb = jnp.zeros((span, width), jnp.float32)
    for t in range(DIL_HEADS_PER_GROUP):
        hm = (lane >= t * HEAD_DIM) & (lane < (t + 1) * HEAD_DIM)
        qh = jnp.where(hm, q, 0.0).astype(jnp.bfloat16)
        s = _dot_nt(qh, kcat) + bias_ref[t]
        s = jnp.where(no_prev, NEG, s)
        m = jnp.max(s, axis=-1, keepdims=True)
        e = jnp.exp(s - m)
        l = jnp.sum(e, axis=-1, keepdims=True)
        pv = _dot(e.astype(jnp.bfloat16), vcat)
        acc = jnp.where(hm, pv, acc)
        mb = jnp.where(hm, m, mb)
        lb = jnp.where(hm, l, lb)
    acc_ref[...] = acc
    m_ref[...] = mb
    l_ref[...] = lb


def dilated_attention(qp, kp, vp, bias_tab):
    B, G, S, W = qp.shape
    span = DIL_SPAN
    nblk = S // span
    cur = pl.BlockSpec((None, None, span, W), lambda b, g, i: (b, g, i, 0))
    prev = pl.BlockSpec((None, None, span, W), lambda b, g, i: (b, g, jnp.maximum(i - 1, 0), 0))
    out = jax.ShapeDtypeStruct((B, G, S, W), jnp.float32)
    kern = functools.partial(_dil_kernel, blocks_per_seq0=nblk)
    return pl.pallas_call(
        kern,
        grid=(B, G, nblk),
        in_specs=[cur, prev, cur, prev, cur,
                  pl.BlockSpec((None, DIL_HEADS_PER_GROUP, span, 2 * span), lambda b, g, i: (g, 0, 0, 0))],
        out_specs=[cur, cur, cur],
        out_shape=[out, out, out],
        compiler_params=_params(("parallel", "parallel", "parallel")),
        name="dilated",
    )(qp, kp, kp, vp, vp, bias_tab)


def _rwkv_prep_kernel(*refs, width, has_res, rows_per_seq):
    if has_res:
        (z_ref, zl_ref, mu_ref, w0_ref, a0_ref, wa_ref, gup_ref, kk_ref, ka_ref, bd_ref,
         vf_ref, v0_ref, mvd_ref, mvu_ref,
         r_o, lw_o, k_o, v_o, kn_o, b_o, g_o) = refs
    else:
        (z_ref, zl_ref, mu_ref, w0_ref, a0_ref, wa_ref, gup_ref, kk_ref, ka_ref, bd_ref,
         r_o, lw_o, k_o, v_o, kn_o, b_o, g_o) = refs
    i = pl.program_id(0)
    W = width
    z = z_ref[...]
    tm = z.shape[0]
    row = lax.broadcasted_iota(jnp.int32, (tm, 1), 0)
    seq_start = (i % rows_per_seq) == 0
    last = jnp.where(seq_start, 0.0, zl_ref[7:8, :])
    zp = jnp.where(row == 0, last, pltpu.roll(z, 1, 0))
    zf = z + mu_ref[...] * (zp - z)
    lora = zf[:, 3 * W:3 * W + LANES]
    lane = lax.broadcasted_iota(jnp.int32, (1, LANES), 1)
    lora = jnp.where(lane < RWKV_DECAY_LORA, jnp.tanh(lora), lora)
    wa = _dot(lora, wa_ref[...], precision=HI)
    g = _dot(_sigmoid(zf[:, 3 * W + LANES:3 * W + 2 * LANES]), gup_ref[...], precision=HI)
    g_o[...] = g
    v_all = zf[:, 2 * W:3 * W]
    if has_res:
        mix = _dot(_dot(v_all, mvd_ref[...], precision=HI), mvu_ref[...], precision=HI)
    for c in range(W // LANES):
        sl = slice(c * LANES, (c + 1) * LANES)
        x = w0_ref[:, sl] + wa[:, sl]
        sp = jnp.maximum(-x, 0.0) + jnp.log(1.0 + jnp.exp(-jnp.abs(x)))
        lw_o[:, sl] = -jnp.exp(-sp - 0.5)
        a = _sigmoid(a0_ref[:, sl] + wa[:, W + c * LANES:W + (c + 1) * LANES])
        r_o[:, sl] = zf[:, sl]
        k = zf[:, W + c * LANES:W + (c + 1) * LANES]
        v = v_all[:, sl]
        if has_res:
            v = v + (vf_ref[:, sl] - v) * _sigmoid(v0_ref[:, sl] + mix[:, sl])
        v_o[:, sl] = v
        kk = k * kk_ref[:, sl]
        ss = _dot_sel(kk * kk, bd_ref[...])
        kn = kk / jnp.maximum(jnp.sqrt(ss), 1e-12)
        kn_o[:, sl] = kn
        b_o[:, sl] = kn * a
        k_o[:, sl] = k * (1.0 + (a - 1.0) * ka_ref[:, sl])


def rwkv_prep(proj, seq, tm, mu, w0, a0, wa_up, g_up, k_k, k_a, bd, res):
    T = proj.shape[0]
    W = RWKV_HEADS * HEAD_DIM
    cols = mu.shape[-1]
    row1 = lambda n: pl.BlockSpec((1, n), lambda i: (0, 0))
    full = lambda a: pl.BlockSpec(a.shape, lambda i: (0, 0))
    tile = pl.BlockSpec((tm, W), lambda i: (i, 0))
    in_specs = [pl.BlockSpec((tm, cols), lambda i: (i, 0)),
                pl.BlockSpec((8, cols), lambda i: (jnp.maximum(i * (tm // 8) - 1, 0), 0)),
                row1(cols), row1(W), row1(W), full(wa_up), full(g_up), row1(W), row1(W), full(bd)]
    args = [proj, proj, mu, w0, a0, wa_up, g_up, k_k, k_a, bd]
    if res is not None:
        v_first, v0, mvd, mvu = res
        in_specs += [tile, row1(W), full(mvd), full(mvu)]
        args += [v_first, v0, mvd, mvu]
    out = jax.ShapeDtypeStruct((T, W), jnp.float32)
    kern = functools.partial(_rwkv_prep_kernel, width=W, has_res=res is not None, rows_per_seq=seq // tm)
    return pl.pallas_call(
        kern,
        grid=(T // tm,),
        in_specs=in_specs,
        out_specs=[tile] * 7,
        out_shape=[out] * 7,
        compiler_params=_params(("parallel",)),
        name="rwkv_prep",
    )(*args)


def _stack_heads(x, lane):
    return jnp.concatenate([jnp.where(lane < HEAD_DIM, x, 0.0), jnp.where(lane >= HEAD_DIM, x, 0.0)], axis=0)


def _rwkv_chunk_kernel(r_ref, lw_ref, k_ref, v_ref, kn_ref, b_ref, g_ref, rk_ref, bd_ref,
                       rhat_ref, y0_ref, e_ref, g_out_ref, d_out_ref, *, chunks):
    C = RWKV_CHUNK
    C2, C4 = 2 * C, 4 * C
    f32 = jnp.float32
    ti = lax.broadcasted_iota(jnp.int32, (C, C), 0)
    si = lax.broadcasted_iota(jnp.int32, (C, C), 1)
    tri = (ti >= si).astype(f32)
    lane = lax.broadcasted_iota(jnp.int32, (1, LANES), 1)
    rho = lax.broadcasted_iota(jnp.int32, (C4, C4), 0)
    sig = lax.broadcasted_iota(jnp.int32, (C4, C4), 1)
    keep = jnp.where(rho >= C2, rho & (C - 1), (rho & (C - 1)) - 1) >= (sig & (C - 1))
    eye2 = (lax.broadcasted_iota(jnp.int32, (C2, C2), 0) == lax.broadcasted_iota(jnp.int32, (C2, C2), 1)).astype(f32)
    eye_l = (lax.broadcasted_iota(jnp.int32, (LANES, LANES), 0)
             == lax.broadcasted_iota(jnp.int32, (LANES, LANES), 1)).astype(f32)
    zeros2 = jnp.zeros((C2, LANES), f32)
    for c in range(chunks):
        rows = slice(c * C, (c + 1) * C)
        r, lw, k, v = r_ref[rows, :], lw_ref[rows, :], k_ref[rows, :], v_ref[rows, :]
        kn, bb = kn_ref[rows, :], b_ref[rows, :]
        cum = _dot(tri, lw, precision=HI)
        cum_last = cum[C - 1:C, :]
        e_out = jnp.exp(-cum)
        e_tail = jnp.exp(cum_last - cum)
        r_t = r * jnp.exp(cum)
        a2 = _stack_heads(-kn * jnp.exp(cum - lw), lane)
        r2 = _stack_heads(r_t, lane)
        b2 = _stack_heads(bb * e_out, lane)
        k2 = _stack_heads(k * e_out, lane)
        v2 = _stack_heads(v, lane)
        bh2 = _stack_heads(bb * e_tail, lane)
        kh2 = _stack_heads(k * e_tail, lane)
        quad = _dot3(jnp.concatenate([a2, r2], axis=0), jnp.concatenate([b2, k2], axis=0), _NT)
        quad = jnp.where(keep, quad, 0.0)
        l_ab = quad[:C2, :C2]
        t_inv = eye2 + l_ab
        pw = l_ab
        for _ in range(int(math.log2(C)) - 1):
            pw = _dot3(pw, pw)
            t_inv = t_inv + _dot3(t_inv, pw)
        x = _dot3(quad[:C2, C2:], v2)
        au = _dot3(t_inv, jnp.concatenate([a2, x], axis=1))
        low = jnp.concatenate([au, jnp.concatenate([zeros2, v2], axis=1)], axis=0)
        my = _dot3(quad[C2:, :], low)
        rhat_ref[rows, :] = r_t + my[:C, :LANES] + my[C:, :LANES]
        y0_ref[rows, :] = my[:C, LANES:] + my[C:, LANES:]
        g_out_ref[c] = eye_l * jnp.exp(cum_last) + _dot3(au[:, :LANES], bh2, _TN)
        d_out_ref[c] = _dot3(jnp.concatenate([au[:, LANES:], v2], axis=0),
                             jnp.concatenate([bh2, kh2], axis=0), _TN)
        bonus = _dot_sel(r * k * rk_ref[...], bd_ref[...]) * v
        e_ref[rows, :] = bonus * g_ref[rows, :]


def rwkv_chunk(r, lw, k, v, kn, b, g, r_k, bd, batch, chunks=4):
    T, W = r.shape
    S = T // batch
    C = RWKV_CHUNK
    nc = S // C
    pairs = W // LANES
    steps = nc // chunks
    tile = pl.BlockSpec((chunks * C, LANES), lambda bi, h, c: (bi * steps + c, h))
    mat = pl.BlockSpec((None, None, chunks, LANES, LANES), lambda bi, h, c: (bi, h, c, 0, 0))
    tw = jax.ShapeDtypeStruct((T, W), jnp.float32)
    gd = jax.ShapeDtypeStruct((batch, pairs, nc, LANES, LANES), jnp.float32)
    return pl.pallas_call(
        functools.partial(_rwkv_chunk_kernel, chunks=chunks),
        grid=(batch, pairs, steps),
        in_specs=[tile] * 7 + [pl.BlockSpec((1, LANES), lambda bi, h, c: (0, h)),
                               pl.BlockSpec((LANES, LANES), lambda bi, h, c: (0, 0))],
        out_specs=[tile, tile, tile, mat, mat],
        out_shape=[tw, tw, tw, gd, gd],
        compiler_params=_params(("parallel", "parallel", "parallel")),
        name="rwkv_chunk",
    )(r, lw, k, v, kn, b, g, r_k, bd)


def _rwkv_state_kernel(rhat_ref, y0_ref, g_ref, e_ref, gm_ref, dm_ref, lng_ref, lnb_ref, bd_ref,
                       o_ref, state_ref, *, chunks, group):
    C = RWKV_CHUNK

    @pl.when(pl.program_id(2) == 0)
    def _():
        state_ref[...] = jnp.zeros_like(state_ref)

    bd = bd_ref[...]
    inv_n = 1.0 / HEAD_DIM

    def body(c, carry):
        rows = pl.ds(pl.multiple_of(c * C, C), C)
        for p in range(group):
            cols = slice(p * LANES, (p + 1) * LANES)
            state = state_ref[p]
            y = _dot3(rhat_ref[rows, cols], state, _NT) + y0_ref[rows, cols]
            state_ref[p] = _dot3(state, gm_ref[p, c]) + dm_ref[p, c]
            mean = _dot_sel(y, bd) * inv_n
            d = y - mean
            var = _dot_sel(d * d, bd) * inv_n
            yn = d * lax.rsqrt(var + RWKV_GN_EPS) * lng_ref[:, cols] + lnb_ref[:, cols]
            o_ref[rows, cols] = yn * g_ref[rows, cols] + e_ref[rows, cols]
        return carry

    lax.fori_loop(0, chunks, body, 0)


def rwkv_state(rhat, y0, g, e, gmat, dmat, ln_g, ln_b, bd, chunks=16, group=3):
    T, W = rhat.shape
    batch, pairs, nc = gmat.shape[:3]
    C = RWKV_CHUNK
    steps = nc // chunks
    gw = group * LANES
    tile = pl.BlockSpec((chunks * C, gw), lambda bi, h, c: (bi * steps + c, h))
    mat = pl.BlockSpec((None, group, chunks, LANES, LANES), lambda bi, h, c: (bi, h, c, 0, 0))
    row = pl.BlockSpec((1, gw), lambda bi, h, c: (0, h))
    return pl.pallas_call(
        functools.partial(_rwkv_state_kernel, chunks=chunks, group=group),
        grid=(batch, pairs // group, steps),
        in_specs=[tile] * 4 + [mat, mat, row, row, pl.BlockSpec((LANES, LANES), lambda bi, h, c: (0, 0))],
        out_specs=tile,
        out_shape=jax.ShapeDtypeStruct((T, W), jnp.float32),
        scratch_shapes=[pltpu.VMEM((group, LANES, LANES), jnp.float32)],
        compiler_params=_params(("parallel", "parallel", "arbitrary")),
        name="rwkv_state",
    )(rhat, y0, g, e, gmat, dmat, ln_g, ln_b, bd)


def _merge_kernel(oa_ref, ob_ref, acc_ref, m_ref, l_ref, ga_ref, gb_ref, gc_ref,
                  pa_ref, pb_ref, pc_ref, o_ref):
    m = m_ref[...]
    mx = jnp.max(m, axis=0)
    wgt = jnp.exp(m - mx[None])
    o_c = jnp.sum(wgt * acc_ref[...], axis=0) / jnp.sum(wgt * l_ref[...], axis=0)
    bf = jnp.bfloat16
    merged = (_sigmoid(ga_ref[...]) * _dot(oa_ref[...].astype(bf), pa_ref[...])
              + _sigmoid(gb_ref[...]) * _dot(ob_ref[...].astype(bf), pb_ref[...])
              + _sigmoid(gc_ref[...]) * _dot(o_c.astype(bf), pc_ref[...]))
    o_ref[...] = merged.astype(o_ref.dtype)


def merge_branches(o_a, o_b, acc, m, l, proj, gate_off, p_a, p_b, p_c, tm, tn):
    T = o_a.shape[0]
    D = p_a.shape[1]
    nj = D // tn
    Wc = acc.shape[-1]
    grp = pl.BlockSpec((3, tm, Wc), lambda i, j: (0, i, 0))
    gate = lambda n: pl.BlockSpec((tm, tn), lambda i, j: (i, gate_off + n * nj + j))
    wspec = lambda a: pl.BlockSpec((a.shape[0], tn), lambda i, j: (0, j))
    return pl.pallas_call(
        _merge_kernel,
        grid=(T // tm, nj),
        in_specs=[pl.BlockSpec((tm, o_a.shape[1]), lambda i, j: (i, 0)),
                  pl.BlockSpec((tm, o_b.shape[1]), lambda i, j: (i, 0)),
                  grp, grp, grp, gate(0), gate(1), gate(2), wspec(p_a), wspec(p_b), wspec(p_c)],
        out_specs=pl.BlockSpec((tm, tn), lambda i, j: (i, j)),
        out_shape=jax.ShapeDtypeStruct((T, D), jnp.bfloat16),
        compiler_params=_params(("parallel", "arbitrary")),
        name="merge",
    )(o_a, o_b, acc, m, l, proj, proj, proj, p_a, p_b, p_c)


def _out_ln_kernel(mg_ref, w_ref, x_ref, g1_ref, lg_ref, lb_ref, o_ref, *, alpha):
    mix = _dot(mg_ref[...], w_ref[...])
    y = alpha * x_ref[...] + (1.0 + g1_ref[...]) * mix
    o_ref[...] = _layer_norm(y, lg_ref[...], lb_ref[...])


def out_ln(merged, w_o, x2, g1, ln_g, ln_b, seq, tm, alpha):
    T, D = x2.shape
    per = seq // tm
    tile = pl.BlockSpec((tm, D), lambda i: (i, 0))
    row = pl.BlockSpec((1, D), lambda i: (0, 0))
    return pl.pallas_call(
        functools.partial(_out_ln_kernel, alpha=alpha),
        grid=(T // tm,),
        in_specs=[tile, pl.BlockSpec((D, D), lambda i: (0, 0)), tile,
                  pl.BlockSpec((None, 1, D), lambda i: (i // per, 0, 0)), row, row],
        out_specs=tile,
        out_shape=jax.ShapeDtypeStruct((T, D), jnp.float32),
        compiler_params=_params(("parallel",)),
        name="out_ln",
    )(merged, w_o, x2, g1, ln_g, ln_b)


def _router_kernel(x_ref, sc_ref, sh_ref, w_ref, b_ref, id_ref, gw_ref, h_ref):
    h = x_ref[...] * (1.0 + sc_ref[...]) + sh_ref[...]
    h_ref[...] = h.astype(h_ref.dtype)
    lg = _dot(h, w_ref[...], precision=HI) + b_ref[...]
    G, EPG = MOE_GROUPS, MOE_EXPERTS_PER_GROUP
    lane = lax.broadcasted_iota(jnp.int32, (1, LANES), 1).astype(jnp.float32)
    first = lambda hit: jnp.min(jnp.where(hit, lane, float(LANES)), axis=-1, keepdims=True)
    is_grp = lane < G
    gmax = jnp.max(jnp.where(is_grp, lg, -jnp.inf), axis=-1, keepdims=True)
    ge = jnp.where(is_grp, jnp.exp(jnp.where(is_grp, lg, gmax) - gmax), 0.0)
    prob = ge / jnp.sum(ge, axis=-1, keepdims=True)
    grp_p = jnp.max(prob, axis=-1, keepdims=True)
    grp_i = first(is_grp & (prob == grp_p))
    lo = G + grp_i * EPG
    el = jnp.where((lane >= lo) & (lane < lo + EPG), lg, -jnp.inf)
    l1 = jnp.max(el, axis=-1, keepdims=True)
    i1 = first(el == l1)
    el = jnp.where(lane == i1, -jnp.inf, el)
    l2 = jnp.max(el, axis=-1, keepdims=True)
    i2 = first(el == l2)
    t = jnp.exp(l2 - l1)
    w1 = grp_p / (1.0 + t)
    id_ref[...] = jnp.where(lane == 0, i1 - G, jnp.where(lane == 1, i2 - G, 0.0)).astype(jnp.int32)
    gw_ref[...] = jnp.where(lane == 0, w1, jnp.where(lane == 1, w1 * t, 0.0))


def router(x2, sc, sh, w_r, b_r, seq, tm):
    T, D = x2.shape
    per = seq // tm
    mod = pl.BlockSpec((None, 1, D), lambda i: (i // per, 0, 0))
    narrow = pl.BlockSpec((tm, LANES), lambda i: (i, 0))
    return pl.pallas_call(
        _router_kernel,
        grid=(T // tm,),
        in_specs=[pl.BlockSpec((tm, D), lambda i: (i, 0)), mod, mod,
                  pl.BlockSpec((D, LANES), lambda i: (0, 0)), pl.BlockSpec((1, LANES), lambda i: (0, 0))],
        out_specs=[narrow, narrow, pl.BlockSpec((tm, D), lambda i: (i, 0))],
        out_shape=[jax.ShapeDtypeStruct((T, LANES), jnp.int32), jax.ShapeDtypeStruct((T, LANES), jnp.float32),
                   jax.ShapeDtypeStruct((T, D), jnp.bfloat16)],
        compiler_params=_params(("parallel",)),
        name="router",
    )(x2, sc, sh, w_r, b_r)


def _experts_kernel(be_ref, nu_ref, x_ref, sw_ref, wg_ref, wu_ref, wd_ref, o_ref):
    del be_ref
    i = pl.program_id(0)

    @pl.when(i < nu_ref[0])
    def _():
        x = x_ref[...]
        gate = _dot(x, wg_ref[...])
        hid = gate * _sigmoid(gate) * _dot(x, wu_ref[...])
        o_ref[...] = _dot(hid.astype(jnp.bfloat16), wd_ref[...]) * sw_ref[...]

    @pl.when(i >= nu_ref[0])
    def _():
        o_ref[...] = jnp.zeros_like(o_ref)


def experts(xs, slot_w, block_e, n_used, w_gate, w_up, w_down):
    R, D = xs.shape
    F = w_gate.shape[-1]
    n_blocks = R // MOE_BLOCK
    grid_spec = pltpu.PrefetchScalarGridSpec(
        num_scalar_prefetch=2,
        grid=(n_blocks,),
        in_specs=[pl.BlockSpec((MOE_BLOCK, D), lambda i, be, nu: (i, 0)),
                  pl.BlockSpec((MOE_BLOCK, 1), lambda i, be, nu: (i, 0)),
                  pl.BlockSpec((None, D, F), lambda i, be, nu: (be[i], 0, 0)),
                  pl.BlockSpec((None, D, F), lambda i, be, nu: (be[i], 0, 0)),
                  pl.BlockSpec((None, F, D), lambda i, be, nu: (be[i], 0, 0))],
        out_specs=pl.BlockSpec((MOE_BLOCK, D), lambda i, be, nu: (i, 0)),
    )
    return pl.pallas_call(
        _experts_kernel,
        grid_spec=grid_spec,
        out_shape=jax.ShapeDtypeStruct((R, D), jnp.float32),
        compiler_params=_params(("arbitrary",)),
        name="experts",
    )(block_e, n_used, xs, slot_w, w_gate, w_up, w_down)


def _combine_ln_kernel(x_ref, f0_ref, f1_ref, g2_ref, lg_ref, lb_ref, o_ref, *, alpha):
    y = alpha * x_ref[...] + (1.0 + g2_ref[...]) * (f0_ref[...] + f1_ref[...])
    o_ref[...] = _layer_norm(y, lg_ref[...], lb_ref[...])


def combine_ln(x2, f0, f1, g2, ln_g, ln_b, seq, tm, alpha):
    T, D = x2.shape
    per = seq // tm
    tile = pl.BlockSpec((tm, D), lambda i: (i, 0))
    row = pl.BlockSpec((1, D), lambda i: (0, 0))
    return pl.pallas_call(
        functools.partial(_combine_ln_kernel, alpha=alpha),
        grid=(T // tm,),
        in_specs=[tile, tile, tile, pl.BlockSpec((None, 1, D), lambda i: (i // per, 0, 0)), row, row],
        out_specs=tile,
        out_shape=jax.ShapeDtypeStruct((T, D), jnp.float32),
        compiler_params=_params(("parallel",)),
        name="combine_ln",
    )(x2, f0, f1, g2, ln_g, ln_b)


def _to_residue(t, dilation):
    B, S, W = t.shape
    return t.reshape(B, S // dilation, dilation, W).transpose(0, 2, 1, 3).reshape(B, S, W)


def _from_residue(t, dilation):
    B, S, W = t.shape
    return t.reshape(B, dilation, S // dilation, W).transpose(0, 2, 1, 3).reshape(B, S, W)


def _route(expert_id, gate_w):
    T = expert_id.shape[0]
    E = MOE_GROUPS * MOE_EXPERTS_PER_GROUP
    A = T * MOE_TOPK
    n_blocks = (A + E * (MOE_BLOCK - 1) + MOE_BLOCK - 1) // MOE_BLOCK
    flat_e = expert_id.reshape(A)
    flat_w = gate_w.reshape(A)
    e_s, order = lax.sort_key_val(flat_e, jnp.arange(A, dtype=jnp.int32))
    counts = jnp.sum(flat_e[None, :] == jnp.arange(E, dtype=jnp.int32)[:, None], axis=1, dtype=jnp.int32)
    padded = (counts + MOE_BLOCK - 1) // MOE_BLOCK * MOE_BLOCK
    pad_end = jnp.cumsum(padded)
    pad_start = pad_end - padded
    start = jnp.cumsum(counts) - counts
    block_e = jnp.minimum(jnp.sum(pad_end[None, :] <= (jnp.arange(n_blocks) * MOE_BLOCK)[:, None], axis=1),
                          E - 1).astype(jnp.int32)
    slot = jnp.arange(n_blocks * MOE_BLOCK, dtype=jnp.int32)
    slot_e = jnp.repeat(block_e, MOE_BLOCK)
    rank = slot - pad_start[slot_e]
    valid = rank < counts[slot_e]
    src = order[jnp.clip(start[slot_e] + rank, 0, A - 1)]
    slot_tok = jnp.where(valid, src // MOE_TOPK, T).astype(jnp.int32)
    slot_w = jnp.where(valid, flat_w[src], 0.0)
    dest = (pad_start[e_s] + jnp.arange(A, dtype=jnp.int32) - start[e_s]).astype(jnp.int32)
    _, slot_of = lax.sort_key_val(order, dest)
    n_used = (pad_end[-1] // MOE_BLOCK).astype(jnp.int32).reshape(1)
    return slot_tok, slot_w, slot_of.reshape(T, MOE_TOPK), block_e, n_used


def kernel(x, c, rel_bias, w_in, p_a, p_b, p_c, w_o, rwkv_mu, rwkv_w0, rwkv_w_up, rwkv_a0, rwkv_a_up,
           rwkv_g_up, rwkv_k_k, rwkv_k_a, rwkv_r_k, rwkv_ln_g, rwkv_ln_b, rwkv_v0, rwkv_mv_down,
           rwkv_mv_up, w_ada, b_ada, ln1_g, ln1_b, ln2_g, ln2_b, router_grp_w, router_grp_b,
           router_exp_w, router_exp_b, exp_w_gate, exp_w_up, exp_w_down):
    B, S, D = x.shape
    depth = w_in.shape[0]
    T = B * S
    bf = jnp.bfloat16
    W = RWKV_HEADS * HEAD_DIM
    wa_w = MOBA_HEADS * HEAD_DIM
    wc_w = len(DIL_GROUPS) * DIL_HEADS_PER_GROUP * HEAD_DIM
    rw_cols = rwkv_mu.shape[-1]
    alpha = (2 * depth) ** 0.25
    off_c = 3 * wa_w
    off_b = off_c + 3 * wc_w
    off_g = off_b + rw_cols
    new_a = rw_cols
    new_c = new_a + 3 * wa_w
    new_g = new_c + 3 * wc_w
    w_in_p = jnp.concatenate([w_in[:, :, off_b:off_g], w_in[:, :, :off_b], w_in[:, :, off_g:]], axis=-1).astype(bf)
    p_a_b, p_b_b, p_c_b, w_o_b = p_a.astype(bf), p_b.astype(bf), p_c.astype(bf), w_o.astype(bf)
    wg_b, wu_b, wd_b = exp_w_gate.astype(bf), exp_w_up.astype(bf), exp_w_down.astype(bf)

    bias_h = rel_bias.T.astype(jnp.float32)
    moba_tab = moba_bias_table(bias_h[:MOBA_HEADS], S // MOBA_BLOCK)
    dil_tab = dil_bias_table(bias_h[MOBA_HEADS:])

    c8 = jnp.zeros((8, D), jnp.float32).at[:B].set(c)
    mod = ada_mod(c8, w_ada, b_ada)[:, :B]

    hd_idx = jnp.arange(LANES) // HEAD_DIM
    bd = (hd_idx[:, None] == hd_idx[None, :]).astype(jnp.float32)
    zeros_w = jnp.zeros((RWKV_DECAY_LORA, W), jnp.float32)
    pad_lora = LANES - RWKV_MV_LORA

    x2 = x.reshape(T, D)
    v_first = None
    for l in range(depth):
        sh1, sc1, g1, sh2, sc2, g2 = [m.reshape(B, 1, D) for m in jnp.split(mod[l], 6, axis=-1)]
        proj = in_proj(x2, sc1, sh1, w_in_p[l], S, 1024, 1024)
        proj3 = proj.reshape(B, S, -1)
        o_a = moba_attention(proj3, moba_tab, new_a // LANES, (new_a + wa_w) // LANES,
                             (new_a + 2 * wa_w) // LANES).reshape(T, wa_w)
        gw = DIL_HEADS_PER_GROUP * HEAD_DIM
        qs, ks, vs = [], [], []
        for g, (_, dil) in enumerate(DIL_GROUPS):
            qs.append(_to_residue(proj3[:, :, new_c + g * gw:new_c + (g + 1) * gw], dil))
            ks.append(_to_residue(proj3[:, :, new_c + wc_w + g * gw:new_c + wc_w + (g + 1) * gw], dil))
            vs.append(_to_residue(proj3[:, :, new_c + 2 * wc_w + g * gw:new_c + 2 * wc_w + (g + 1) * gw], dil))
        acc, mm, ll = dilated_attention(jnp.stack(qs, 1), jnp.stack(ks, 1), jnp.stack(vs, 1), dil_tab)
        unperm = lambda t: jnp.stack([_from_residue(t[:, g], dil).reshape(T, gw)
                                      for g, (_, dil) in enumerate(DIL_GROUPS)])
        acc, mm, ll = unperm(acc), unperm(mm), unperm(ll)
        wa_up = jnp.concatenate([jnp.concatenate([rwkv_w_up[l], zeros_w], axis=1),
                                 jnp.concatenate([zeros_w, rwkv_a_up[l]], axis=1)], axis=0)
        res = None
        if l > 0:
            res = (v_first, rwkv_v0[l - 1][None],
                   jnp.pad(rwkv_mv_down[l - 1], ((0, 0), (0, pad_lora))),
                   jnp.pad(rwkv_mv_up[l - 1], ((0, pad_lora), (0, 0))))
        r_, lw_, k_, v_, kn_, b_, g_ = rwkv_prep(proj, S, 256, rwkv_mu[l][None], rwkv_w0[l][None],
                                                 rwkv_a0[l][None], wa_up, rwkv_g_up[l], rwkv_k_k[l][None],
                                                 rwkv_k_a[l][None], bd, res)
        if l == 0:
            v_first = v_
        rhat, y0, e_, gmat, dmat = rwkv_chunk(r_, lw_, k_, v_, kn_, b_, g_, rwkv_r_k[l][None], bd, B)
        o_b = rwkv_state(rhat, y0, g_, e_, gmat, dmat, rwkv_ln_g[l][None], rwkv_ln_b[l][None], bd)
        merged = merge_branches(o_a, o_b, acc, mm, ll, proj, new_g // 1024, p_a_b[l], p_b_b[l], p_c_b[l],
                                256, 1024)
        x2 = out_ln(merged, w_o_b[l], x2, g1, ln1_g[l][None], ln1_b[l][None], S, 256, alpha)
        w_r = jnp.zeros((D, LANES), jnp.float32)
        w_r = w_r.at[:, :MOE_GROUPS].set(router_grp_w[l]).at[:, MOE_GROUPS:MOE_GROUPS + router_exp_w.shape[-1]].set(
            router_exp_w[l])
        b_r = jnp.zeros((1, LANES), jnp.float32)
        b_r = b_r.at[0, :MOE_GROUPS].set(router_grp_b[l]).at[0, MOE_GROUPS:MOE_GROUPS + router_exp_b.shape[-1]].set(
            router_exp_b[l])
        ids, gws, h2 = router(x2, sc2, sh2, w_r, b_r, S, 512)
        slot_tok, slot_w, slot_of, block_e, n_used = _route(ids[:, :MOE_TOPK], gws[:, :MOE_TOPK])
        h2p = jnp.concatenate([h2, jnp.zeros((1, D), h2.dtype)], axis=0)
        y = experts(h2p[slot_tok], slot_w[:, None], block_e, n_used, wg_b[l], wu_b[l], wd_b[l])
        x2 = combine_ln(x2, y[slot_of[:, 0]], y[slot_of[:, 1]], g2, ln2_g[l][None], ln2_b[l][None], S, 512, alpha)
    return x2.reshape(B, S, D)
```

```python
import functools
import math

import jax
import jax.numpy as jnp
import numpy as np
from jax import lax
from jax.experimental import pallas as pl
from jax.experimental.pallas import tpu as pltpu

HEAD_DIM = 64
LANES = 128
MOBA_HEADS = 12
MOBA_BLOCK = 256
MOBA_TOPK = 3
RWKV_HEADS = 12
RWKV_DECAY_LORA = 64
RWKV_A_LORA = 64
RWKV_MV_LORA = 32
RWKV_GATE_LORA = 128
RWKV_GN_EPS = 64e-5
RWKV_CHUNK = 64
DIL_GROUPS = ((128, 1), (512, 4), (2048, 16))
DIL_HEADS_PER_GROUP = 4
DIL_SPAN = 128
REL_BUCKETS = 32
REL_MAX_DISTANCE = 2048
MOE_GROUPS = 8
MOE_EXPERTS_PER_GROUP = 8
MOE_TOPK = 2
MOE_BLOCK = 256
LN_EPS = 1e-5
NEG = -1e30
VMEM_LIMIT = 56 * 1024 * 1024
HI = lax.Precision.HIGHEST


def _params(sem):
    return pltpu.CompilerParams(dimension_semantics=sem, vmem_limit_bytes=VMEM_LIMIT)


def _sigmoid(x):
    return 1.0 / (1.0 + jnp.exp(-x))


def _dot(a, b, precision=None):
    return jnp.dot(a, b, preferred_element_type=jnp.float32, precision=precision)


def _dot_nt(a, b, precision=None):
    return lax.dot_general(a, b, (((1,), (1,)), ((), ())), preferred_element_type=jnp.float32,
                           precision=precision)


def _dot_tn(a, b, precision=None):
    return lax.dot_general(a, b, (((0,), (0,)), ((), ())), preferred_element_type=jnp.float32,
                           precision=precision)


def _split2(x):
    hi = x.astype(jnp.bfloat16)
    return hi, (x - hi.astype(jnp.float32)).astype(jnp.bfloat16)


def _dot3(a, b, dims=((1,), (0,))):
    (ca,), (cb,) = dims
    ah, al = _split2(a)
    bh, bl = _split2(b)
    return lax.dot_general(jnp.concatenate([ah, ah, al], axis=ca), jnp.concatenate([bh, bl, bh], axis=cb),
                           (dims, ((), ())), preferred_element_type=jnp.float32)


def _dot_sel(x, sel):
    xh, xl = _split2(x)
    sb = sel.astype(jnp.bfloat16)
    return _dot(jnp.concatenate([xh, xl], axis=1), jnp.concatenate([sb, sb], axis=0))


_NT = ((1,), (1,))
_TN = ((0,), (0,))


def _layer_norm(y, g, b):
    mu = jnp.mean(y, axis=-1, keepdims=True)
    d = y - mu
    var = jnp.mean(d * d, axis=-1, keepdims=True)
    return d * lax.rsqrt(var + LN_EPS) * g + b


def _ada_kernel(c_ref, w_ref, b_ref, o_ref):
    c = c_ref[...]
    cond = c * _sigmoid(c)
    o_ref[...] = _dot(cond, w_ref[...]) + b_ref[...]


def ada_mod(c8, w_ada, b_ada):
    L, D, N = w_ada.shape
    tn = 1024
    return pl.pallas_call(
        _ada_kernel,
        grid=(L, N // tn),
        in_specs=[pl.BlockSpec((8, D), lambda l, j: (0, 0)),
                  pl.BlockSpec((None, D, tn), lambda l, j: (l, 0, j)),
                  pl.BlockSpec((None, 1, tn), lambda l, j: (l, 0, j))],
        out_specs=pl.BlockSpec((None, 8, tn), lambda l, j: (l, 0, j)),
        out_shape=jax.ShapeDtypeStruct((L, 8, N), jnp.float32),
        compiler_params=_params(("parallel", "parallel")),
        name="ada_mod",
    )(c8, w_ada, b_ada.reshape(L, 1, N))


def _in_proj_kernel(x_ref, sc_ref, sh_ref, w_ref, o_ref, h_ref):
    @pl.when(pl.program_id(1) == 0)
    def _():
        h_ref[...] = (x_ref[...] * (1.0 + sc_ref[...]) + sh_ref[...]).astype(h_ref.dtype)

    o_ref[...] = _dot(h_ref[...], w_ref[...])


def in_proj(x2, sc, sh, w, seq, tm, tn):
    T, D = x2.shape
    N = w.shape[1]
    per = seq // tm
    return pl.pallas_call(
        _in_proj_kernel,
        grid=(T // tm, N // tn),
        in_specs=[pl.BlockSpec((tm, D), lambda i, j: (i, 0)),
                  pl.BlockSpec((None, 1, D), lambda i, j: (i // per, 0, 0)),
                  pl.BlockSpec((None, 1, D), lambda i, j: (i // per, 0, 0)),
                  pl.BlockSpec((D, tn), lambda i, j: (0, j))],
        out_specs=pl.BlockSpec((tm, tn), lambda i, j: (i, j)),
        out_shape=jax.ShapeDtypeStruct((T, N), jnp.float32),
        scratch_shapes=[pltpu.VMEM((tm, D), jnp.bfloat16)],
        compiler_params=_params(("parallel", "arbitrary")),
        name="in_proj",
    )(x2, sc, sh, w)


def _t5_bucket(dist):
    n = jnp.maximum(dist, 0)
    max_exact = REL_BUCKETS // 2
    nf = jnp.maximum(n, 1).astype(jnp.float32)
    large = max_exact + (jnp.log(nf / max_exact) / math.log(REL_MAX_DISTANCE / max_exact)
                         * (REL_BUCKETS - max_exact)).astype(jnp.int32)
    large = jnp.minimum(large, REL_BUCKETS - 1)
    return jnp.where(n < max_exact, n, large)


def _moba_n_delta(nb):
    last_start = 1
    d = np.arange(1, nb * MOBA_BLOCK + 1)
    large = 16 + (np.log(d / 16.0) / math.log(REL_MAX_DISTANCE / 16.0) * 16).astype(np.int64)
    bucket = np.where(d < 16, d, np.minimum(large, REL_BUCKETS - 1))
    last_start = int(d[bucket < REL_BUCKETS - 1].max()) + 1 if (bucket < REL_BUCKETS - 1).any() else 1
    delta = 1
    while delta * MOBA_BLOCK - (MOBA_BLOCK - 1) < last_start + 2:
        delta += 1
    return min(delta + 1, nb)


def moba_bias_table(bias_a, nb):
    nd = _moba_n_delta(nb)
    key = jnp.arange(MOBA_BLOCK)[:, None]
    qry = jnp.arange(MOBA_BLOCK)[None, :]
    dist = jnp.arange(nd)[:, None, None] * MOBA_BLOCK + (qry - key)[None]
    bucket = _t5_bucket(dist)[None]
    tab = jnp.zeros((bias_a.shape[0],) + dist.shape, jnp.float32)
    for b in range(REL_BUCKETS):
        tab = jnp.where(bucket == b, bias_a[:, b][:, None, None, None], tab)
    return jnp.where((dist >= 0)[None], tab, NEG)


def dil_bias_table(bias_c):
    span = DIL_SPAN
    rel = span + jnp.arange(span)[:, None] - jnp.arange(2 * span)[None, :]
    valid = (rel >= 0) & (rel <= span)
    tabs = []
    for g, (_, dilation) in enumerate(DIL_GROUPS):
        bh = bias_c[g * DIL_HEADS_PER_GROUP:(g + 1) * DIL_HEADS_PER_GROUP]
        tabs.append(jnp.where(valid[None], bh[:, _t5_bucket(rel * dilation)], NEG))
    return jnp.stack(tabs)


def _moba_kernel(q_ref, k_ref, v_ref, bias_ref, o_ref,
                 kaug_ref, vt_ref, km_ref, acc_ref, *, nb, n_delta):
    qb = pl.program_id(2)
    bs = MOBA_BLOCK
    S = nb * bs
    half = bs // 2
    lane = lax.broadcasted_iota(jnp.int32, (1, LANES), 1)
    f32 = jnp.float32

    @pl.when(qb == 0)
    def _():
        k = k_ref[...]
        rowblk = lax.broadcasted_iota(jnp.int32, (S, LANES), 0) // bs
        lanes = lax.broadcasted_iota(jnp.int32, (S, LANES), 1)
        km_ref[...] = jnp.mean(k.reshape(nb, bs, LANES), axis=1)
        kaug_ref[0] = jnp.where(lanes < HEAD_DIM, k, (lanes - HEAD_DIM == rowblk).astype(f32)).astype(kaug_ref.dtype)
        kaug_ref[1] = jnp.where(lanes >= HEAD_DIM, k, (lanes == rowblk).astype(f32)).astype(kaug_ref.dtype)
        for i in range(nb):
            vt_ref[i] = v_ref[i * bs:(i + 1) * bs, :].T.astype(vt_ref.dtype)

    q_t = q_ref[...].T
    scale = HEAD_DIM ** -0.5
    blk = lax.broadcasted_iota(jnp.int32, (nb, 1), 0)
    past = blk < qb
    rhs = []
    for t in range(2):
        hm = (lane >= t * HEAD_DIM) & (lane < (t + 1) * HEAD_DIM)
        gate = _dot(jnp.where(hm, km_ref[...], 0.0), q_t, precision=HI)
        gate = jnp.where(past, gate, -jnp.inf)
        cnt = jnp.zeros((nb, bs), jnp.int32)
        for m in range(nb):
            gm = gate[m:m + 1, :]
            ahead = (gm > gate) | ((gm == gate) & (m < blk))
            cnt = cnt + ahead.astype(jnp.int32)
        chosen = (past & (cnt < MOBA_TOPK)) | (blk == qb)
        pen = jnp.where(chosen, 0.0, NEG)
        qh = q_t[t * HEAD_DIM:(t + 1) * HEAD_DIM, :] * scale
        if t == 0:
            parts = [qh, pen, jnp.zeros((LANES - HEAD_DIM - nb, bs), f32)]
        else:
            parts = [pen, jnp.zeros((HEAD_DIM - nb, bs), f32), qh]
        rhs.append(jnp.concatenate(parts, axis=0).astype(jnp.bfloat16))
    acc_ref[...] = jnp.zeros((LANES, bs), f32)

    chains = [(t, hq) for t in range(2) for hq in range(2)]
    cols = [slice(hq * half, (hq + 1) * half) for _, hq in chains]

    def step(blocks, carry):
        rows = [pl.multiple_of(n * bs, bs) for n in blocks]
        delta = [jnp.minimum(qb - n, n_delta - 1) for n in blocks]
        s = [[_dot(kaug_ref[t, pl.ds(rows[j], bs), :], rhs[t][:, cols[c]]) + bias_ref[t, delta[j], :, cols[c]]
              for j in range(len(blocks))] for c, (t, _) in enumerate(chains)]
        m_new = [functools.reduce(jnp.maximum, [carry[c][0]] + [jnp.max(x, axis=0, keepdims=True) for x in s[c]])
                 for c in range(4)]
        alpha = [jnp.exp(carry[c][0] - m_new[c]) for c in range(4)]
        p = [[jnp.exp(x - m_new[c]) for x in s[c]] for c in range(4)]
        l_new = [alpha[c] * carry[c][1] + sum(jnp.sum(x, axis=0, keepdims=True) for x in p[c]) for c in range(4)]
        pv = [sum(_dot(vt_ref[n, t * HEAD_DIM:(t + 1) * HEAD_DIM, :], p[c][j].astype(jnp.bfloat16))
                  for j, n in enumerate(blocks)) for c, (t, _) in enumerate(chains)]
        pieces = [alpha[c] * acc_ref[t * HEAD_DIM:(t + 1) * HEAD_DIM, cols[c]] + pv[c]
                  for c, (t, _) in enumerate(chains)]
        acc_ref[...] = jnp.concatenate([jnp.concatenate(pieces[:2], axis=1),
                                        jnp.concatenate(pieces[2:], axis=1)], axis=0)
        return tuple((m_new[c], l_new[c]) for c in range(4))

    init = tuple((jnp.full((1, half), NEG, f32), jnp.zeros((1, half), f32)) for _ in range(4))
    n_blocks = qb + 1
    fin = lax.fori_loop(0, n_blocks // 2, lambda i, c: step([2 * i, 2 * i + 1], c), init)
    fin = lax.cond(n_blocks % 2 == 1, lambda c: step([qb], c), lambda c: c, fin)
    l_all = jnp.concatenate([jnp.broadcast_to(jnp.concatenate([fin[2 * t][1], fin[2 * t + 1][1]], axis=1),
                                              (HEAD_DIM, bs)) for t in range(2)], axis=0)
    o_ref[...] = (acc_ref[...] / l_all).T


def moba_attention(proj3, bias_tab, q_off, k_off, v_off):
    B, S, _ = proj3.shape
    nb = S // MOBA_BLOCK
    n_delta = bias_tab.shape[1]
    pairs = MOBA_HEADS // 2
    kern = functools.partial(_moba_kernel, nb=nb, n_delta=n_delta)
    return pl.pallas_call(
        kern,
        grid=(pairs, B, nb),
        in_specs=[pl.BlockSpec((None, MOBA_BLOCK, LANES), lambda h, b, i: (b, i, q_off + h)),
                  pl.BlockSpec((None, S, LANES), lambda h, b, i: (b, 0, k_off + h)),
                  pl.BlockSpec((None, S, LANES), lambda h, b, i: (b, 0, v_off + h)),
                  pl.BlockSpec((2, n_delta, MOBA_BLOCK, MOBA_BLOCK), lambda h, b, i: (h, 0, 0, 0))],
        out_specs=pl.BlockSpec((None, MOBA_BLOCK, LANES), lambda h, b, i: (b, i, h)),
        out_shape=jax.ShapeDtypeStruct((B, S, pairs * LANES), jnp.float32),
        scratch_shapes=[pltpu.VMEM((2, S, LANES), jnp.bfloat16),
                        pltpu.VMEM((nb, LANES, MOBA_BLOCK), jnp.bfloat16),
                        pltpu.VMEM((nb, LANES), jnp.float32),
                        pltpu.VMEM((LANES, MOBA_BLOCK), jnp.float32)],
        compiler_params=_params(("parallel", "parallel", "arbitrary")),
        name="moba",
    )(proj3, proj3, proj3, bias_tab)


def _dil_kernel(q_ref, kp_ref, kc_ref, vp_ref, vc_ref, bias_ref, acc_ref, m_ref, l_ref, *, blocks_per_seq0):
    g = pl.program_id(1)
    blk = pl.program_id(2)
    span = DIL_SPAN
    width = DIL_HEADS_PER_GROUP * HEAD_DIM
    bps = jnp.right_shift(blocks_per_seq0, 2 * g)
    first = (blk & (bps - 1)) == 0
    lane = lax.broadcasted_iota(jnp.int32, (1, width), 1)
    col = lax.broadcasted_iota(jnp.int32, (1, 2 * span), 1)
    no_prev = first & (col < span)
    q = q_ref[...] * (HEAD_DIM ** -0.5)
    kcat = jnp.concatenate([kp_ref[...], kc_ref[...]], axis=0).astype(jnp.bfloat16)
    vcat = jnp.concatenate([vp_ref[...], vc_ref[...]], axis=0).astype(jnp.bfloat16)
    acc = jnp.zeros((span, width), jnp.float32)
    mb = jnp.zeros((span, width), jnp.float32)
    lb = jnp.zeros((span, width), jnp.float32)
    for t in range(DIL_HEADS_PER_GROUP):
        hm = (lane >= t * HEAD_DIM) & (lane < (t + 1) * HEAD_DIM)
        qh = jnp.where(hm, q, 0.0).astype(jnp.bfloat16)
        s = _dot_nt(qh, kcat) + bias_ref[t]
        s = jnp.where(no_prev, NEG, s)
        m = jnp.max(s, axis=-1, keepdims=True)
        e = jnp.exp(s - m)
        l = jnp.sum(e, axis=-1, keepdims=True)
        pv = _dot(e.astype(jnp.bfloat16), vcat)
        acc = jnp.where(hm, pv, acc)
        mb = jnp.where(hm, m, mb)
        lb = jnp.where(hm, l, lb)
    acc_ref[...] = acc
    m_ref[...] = mb
    l_ref[...] = lb


def dilated_attention(qp, kp, vp, bias_tab):
    B, G, S, W = qp.shape
    span = DIL_SPAN
    nblk = S // span
    cur = pl.BlockSpec((None, None, span, W), lambda b, g, i: (b, g, i, 0))
    prev = pl.BlockSpec((None, None, span, W), lambda b, g, i: (b, g, jnp.maximum(i - 1, 0), 0))
    out = jax.ShapeDtypeStruct((B, G, S, W), jnp.float32)
    kern = functools.partial(_dil_kernel, blocks_per_seq0=nblk)
    return pl.pallas_call(
        kern,
        grid=(B, G, nblk),
        in_specs=[cur, prev, cur, prev, cur,
                  pl.BlockSpec((None, DIL_HEADS_PER_GROUP, span, 2 * span), lambda b, g, i: (g, 0, 0, 0))],
        out_specs=[cur, cur, cur],
        out_shape=[out, out, out],
        compiler_params=_params(("parallel", "parallel", "parallel")),
        name="dilated",
    )(qp, kp, kp, vp, vp, bias_tab)


def _rwkv_prep_kernel(*refs, width, has_res, rows_per_seq):
    if has_res:
        (z_ref, zl_ref, mu_ref, w0_ref, a0_ref, wa_ref, gup_ref, kk_ref, ka_ref, bd_ref,
         vf_ref, v0_ref, mvd_ref, mvu_ref,
         r_o, lw_o, k_o, v_o, kn_o, b_o, g_o) = refs
    else:
        (z_ref, zl_ref, mu_ref, w0_ref, a0_ref, wa_ref, gup_ref, kk_ref, ka_ref, bd_ref,
         r_o, lw_o, k_o, v_o, kn_o, b_o, g_o) = refs
    i = pl.program_id(0)
    W = width
    z = z_ref[...]
    tm = z.shape[0]
    row = lax.broadcasted_iota(jnp.int32, (tm, 1), 0)
    seq_start = (i % rows_per_seq) == 0
    last = jnp.where(seq_start, 0.0, zl_ref[7:8, :])
    zp = jnp.where(row == 0, last, pltpu.roll(z, 1, 0))
    zf = z + mu_ref[...] * (zp - z)
    lora = zf[:, 3 * W:3 * W + LANES]
    lane = lax.broadcasted_iota(jnp.int32, (1, LANES), 1)
    lora = jnp.where(lane < RWKV_DECAY_LORA, jnp.tanh(lora), lora)
    wa = _dot(lora, wa_ref[...], precision=HI)
    g = _dot(_sigmoid(zf[:, 3 * W + LANES:3 * W + 2 * LANES]), gup_ref[...], precision=HI)
    g_o[...] = g
    v_all = zf[:, 2 * W:3 * W]
    if has_res:
        mix = _dot(_dot(v_all, mvd_ref[...], precision=HI), mvu_ref[...], precision=HI)
    for c in range(W // LANES):
        sl = slice(c * LANES, (c + 1) * LANES)
        x = w0_ref[:, sl] + wa[:, sl]
        sp = jnp.maximum(-x, 0.0) + jnp.log(1.0 + jnp.exp(-jnp.abs(x)))
        lw_o[:, sl] = -jnp.exp(-sp - 0.5)
        a = _sigmoid(a0_ref[:, sl] + wa[:, W + c * LANES:W + (c + 1) * LANES])
        r_o[:, sl] = zf[:, sl]
        k = zf[:, W + c * LANES:W + (c + 1) * LANES]
        v = v_all[:, sl]
        if has_res:
            v = v + (vf_ref[:, sl] - v) * _sigmoid(v0_ref[:, sl] + mix[:, sl])
        v_o[:, sl] = v
        kk = k * kk_ref[:, sl]
        ss = _dot_sel(kk * kk, bd_ref[...])
        kn = kk / jnp.maximum(jnp.sqrt(ss), 1e-12)
        kn_o[:, sl] = kn
        b_o[:, sl] = kn * a
        k_o[:, sl] = k * (1.0 + (a - 1.0) * ka_ref[:, sl])


def rwkv_prep(proj, seq, tm, mu, w0, a0, wa_up, g_up, k_k, k_a, bd, res):
    T = proj.shape[0]
    W = RWKV_HEADS * HEAD_DIM
    cols = mu.shape[-1]
    row1 = lambda n: pl.BlockSpec((1, n), lambda i: (0, 0))
    full = lambda a: pl.BlockSpec(a.shape, lambda i: (0, 0))
    tile = pl.BlockSpec((tm, W), lambda i: (i, 0))
    in_specs = [pl.BlockSpec((tm, cols), lambda i: (i, 0)),
                pl.BlockSpec((8, cols), lambda i: (jnp.maximum(i * (tm // 8) - 1, 0), 0)),
                row1(cols), row1(W), row1(W), full(wa_up), full(g_up), row1(W), row1(W), full(bd)]
    args = [proj, proj, mu, w0, a0, wa_up, g_up, k_k, k_a, bd]
    if res is not None:
        v_first, v0, mvd, mvu = res
        in_specs += [tile, row1(W), full(mvd), full(mvu)]
        args += [v_first, v0, mvd, mvu]
    out = jax.ShapeDtypeStruct((T, W), jnp.float32)
    kern = functools.partial(_rwkv_prep_kernel, width=W, has_res=res is not None, rows_per_seq=seq // tm)
    return pl.pallas_call(
        kern,
        grid=(T // tm,),
        in_specs=in_specs,
        out_specs=[tile] * 7,
        out_shape=[out] * 7,
        compiler_params=_params(("parallel",)),
        name="rwkv_prep",
    )(*args)


def _stack_heads(x, lane):
    return jnp.concatenate([jnp.where(lane < HEAD_DIM, x, 0.0), jnp.where(lane >= HEAD_DIM, x, 0.0)], axis=0)


def _rwkv_chunk_kernel(r_ref, lw_ref, k_ref, v_ref, kn_ref, b_ref, g_ref, rk_ref, bd_ref,
                       rhat_ref, y0_ref, e_ref, g_out_ref, d_out_ref, *, chunks):
    C = RWKV_CHUNK
    C2, C4 = 2 * C, 4 * C
    f32 = jnp.float32
    ti = lax.broadcasted_iota(jnp.int32, (C, C), 0)
    si = lax.broadcasted_iota(jnp.int32, (C, C), 1)
    tri = (ti >= si).astype(f32)
    lane = lax.broadcasted_iota(jnp.int32, (1, LANES), 1)
    rho = lax.broadcasted_iota(jnp.int32, (C4, C4), 0)
    sig = lax.broadcasted_iota(jnp.int32, (C4, C4), 1)
    keep = jnp.where(rho >= C2, rho & (C - 1), (rho & (C - 1)) - 1) >= (sig & (C - 1))
    eye2 = (lax.broadcasted_iota(jnp.int32, (C2, C2), 0) == lax.broadcasted_iota(jnp.int32, (C2, C2), 1)).astype(f32)
    eye_l = (lax.broadcasted_iota(jnp.int32, (LANES, LANES), 0)
             == lax.broadcasted_iota(jnp.int32, (LANES, LANES), 1)).astype(f32)
    zeros2 = jnp.zeros((C2, LANES), f32)
    ch = range(chunks)
    rows = [slice(c * C, (c + 1) * C) for c in ch]
    cum = [_dot(tri, lw_ref[rows[c], :], precision=HI) for c in ch]
    st = []
    for c in ch:
        r, lw, k, v = r_ref[rows[c], :], lw_ref[rows[c], :], k_ref[rows[c], :], v_ref[rows[c], :]
        kn, bb = kn_ref[rows[c], :], b_ref[rows[c], :]
        cum_last = cum[c][C - 1:C, :]
        e_out = jnp.exp(-cum[c])
        e_tail = jnp.exp(cum_last - cum[c])
        r_t = r * jnp.exp(cum[c])
        st.append(dict(
            r_t=r_t, decay=jnp.exp(cum_last),
            a2=_stack_heads(-kn * jnp.exp(cum[c] - lw), lane), r2=_stack_heads(r_t, lane),
            b2=_stack_heads(bb * e_out, lane), k2=_stack_heads(k * e_out, lane), v2=_stack_heads(v, lane),
            bh2=_stack_heads(bb * e_tail, lane), kh2=_stack_heads(k * e_tail, lane),
            e=_dot_sel(r * k * rk_ref[...], bd_ref[...]) * v * g_ref[rows[c], :]))
    quad = [jnp.where(keep, _dot3(jnp.concatenate([s["a2"], s["r2"]], axis=0),
                                  jnp.concatenate([s["b2"], s["k2"]], axis=0), _NT), 0.0) for s in st]
    pw = [q[:C2, :C2] for q in quad]
    t_inv = [eye2 + p for p in pw]
    x = [_dot3(q[:C2, C2:], s["v2"]) for q, s in zip(quad, st)]
    for _ in range(int(math.log2(C)) - 1):
        pw = [_dot3(p, p) for p in pw]
        t_inv = [t + _dot3(t, p) for t, p in zip(t_inv, pw)]
    au = [_dot3(t, jnp.concatenate([s["a2"], xx], axis=1)) for t, s, xx in zip(t_inv, st, x)]
    my = [_dot3(q[C2:, :], jnp.concatenate([a, jnp.concatenate([zeros2, s["v2"]], axis=1)], axis=0))
          for q, a, s in zip(quad, au, st)]
    gm = [eye_l * s["decay"] + _dot3(a[:, :LANES], s["bh2"], _TN) for a, s in zip(au, st)]
    dm = [_dot3(jnp.concatenate([a[:, LANES:], s["v2"]], axis=0),
                jnp.concatenate([s["bh2"], s["kh2"]], axis=0), _TN) for a, s in zip(au, st)]
    rhat_ref[...] = jnp.concatenate([s["r_t"] + m[:C, :LANES] + m[C:, :LANES] for s, m in zip(st, my)], axis=0)
    y0_ref[...] = jnp.concatenate([m[:C, LANES:] + m[C:, LANES:] for m in my], axis=0)
    e_ref[...] = jnp.concatenate([s["e"] for s in st], axis=0)
    g_out_ref[...] = jnp.stack(gm)
    d_out_ref[...] = jnp.stack(dm)


def rwkv_chunk(r, lw, k, v, kn, b, g, r_k, bd, batch, chunks=4):
    T, W = r.shape
    S = T // batch
    C = RWKV_CHUNK
    nc = S // C
    pairs = W // LANES
    steps = nc // chunks
    tile = pl.BlockSpec((chunks * C, LANES), lambda bi, h, c: (bi * steps + c, h))
    mat = pl.BlockSpec((None, None, chunks, LANES, LANES), lambda bi, h, c: (bi, h, c, 0, 0))
    tw = jax.ShapeDtypeStruct((T, W), jnp.float32)
    gd = jax.ShapeDtypeStruct((batch, pairs, nc, LANES, LANES), jnp.float32)
    return pl.pallas_call(
        functools.partial(_rwkv_chunk_kernel, chunks=chunks),
        grid=(batch, pairs, steps),
        in_specs=[tile] * 7 + [pl.BlockSpec((1, LANES), lambda bi, h, c: (0, h)),
                               pl.BlockSpec((LANES, LANES), lambda bi, h, c: (0, 0))],
        out_specs=[tile, tile, tile, mat, mat],
        out_shape=[tw, tw, tw, gd, gd],
        compiler_params=_params(("parallel", "parallel", "parallel")),
        name="rwkv_chunk",
    )(r, lw, k, v, kn, b, g, r_k, bd)


def _rwkv_state_kernel(rhat_ref, y0_ref, g_ref, e_ref, gm_ref, dm_ref, lng_ref, lnb_ref, bd_ref,
                       o_ref, state_ref, *, chunks, group):
    C = RWKV_CHUNK

    @pl.when(pl.program_id(2) == 0)
    def _():
        state_ref[...] = jnp.zeros_like(state_ref)

    bd = bd_ref[...]
    inv_n = 1.0 / HEAD_DIM

    def body(c, carry):
        rows = pl.ds(pl.multiple_of(c * C, C), C)
        for p in range(group):
            cols = slice(p * LANES, (p + 1) * LANES)
            state = state_ref[p]
            y = _dot3(rhat_ref[rows, cols], state, _NT) + y0_ref[rows, cols]
            state_ref[p] = _dot3(state, gm_ref[p, c]) + dm_ref[p, c]
            mean = _dot_sel(y, bd) * inv_n
            d = y - mean
            var = _dot_sel(d * d, bd) * inv_n
            yn = d * lax.rsqrt(var + RWKV_GN_EPS) * lng_ref[:, cols] + lnb_ref[:, cols]
            o_ref[rows, cols] = yn * g_ref[rows, cols] + e_ref[rows, cols]
        return carry

    lax.fori_loop(0, chunks, body, 0)


def rwkv_state(rhat, y0, g, e, gmat, dmat, ln_g, ln_b, bd, chunks=16, group=3):
    T, W = rhat.shape
    batch, pairs, nc = gmat.shape[:3]
    C = RWKV_CHUNK
    steps = nc // chunks
    gw = group * LANES
    tile = pl.BlockSpec((chunks * C, gw), lambda bi, h, c: (bi * steps + c, h))
    mat = pl.BlockSpec((None, group, chunks, LANES, LANES), lambda bi, h, c: (bi, h, c, 0, 0))
    row = pl.BlockSpec((1, gw), lambda bi, h, c: (0, h))
    return pl.pallas_call(
        functools.partial(_rwkv_state_kernel, chunks=chunks, group=group),
        grid=(batch, pairs // group, steps),
        in_specs=[tile] * 4 + [mat, mat, row, row, pl.BlockSpec((LANES, LANES), lambda bi, h, c: (0, 0))],
        out_specs=tile,
        out_shape=jax.ShapeDtypeStruct((T, W), jnp.float32),
        scratch_shapes=[pltpu.VMEM((group, LANES, LANES), jnp.float32)],
        compiler_params=_params(("parallel", "parallel", "arbitrary")),
        name="rwkv_state",
    )(rhat, y0, g, e, gmat, dmat, ln_g, ln_b, bd)


def _merge_kernel(oa_ref, ob_ref, acc_ref, m_ref, l_ref, ga_ref, gb_ref, gc_ref,
                  pa_ref, pb_ref, pc_ref, o_ref):
    m = m_ref[...]
    mx = jnp.max(m, axis=0)
    wgt = jnp.exp(m - mx[None])
    o_c = jnp.sum(wgt * acc_ref[...], axis=0) / jnp.sum(wgt * l_ref[...], axis=0)
    bf = jnp.bfloat16
    merged = (_sigmoid(ga_ref[...]) * _dot(oa_ref[...].astype(bf), pa_ref[...])
              + _sigmoid(gb_ref[...]) * _dot(ob_ref[...].astype(bf), pb_ref[...])
              + _sigmoid(gc_ref[...]) * _dot(o_c.astype(bf), pc_ref[...]))
    o_ref[...] = merged.astype(o_ref.dtype)


def merge_branches(o_a, o_b, acc, m, l, proj, gate_off, p_a, p_b, p_c, tm, tn):
    T = o_a.shape[0]
    D = p_a.shape[1]
    nj = D // tn
    Wc = acc.shape[-1]
    grp = pl.BlockSpec((3, tm, Wc), lambda i, j: (0, i, 0))
    gate = lambda n: pl.BlockSpec((tm, tn), lambda i, j: (i, gate_off + n * nj + j))
    wspec = lambda a: pl.BlockSpec((a.shape[0], tn), lambda i, j: (0, j))
    return pl.pallas_call(
        _merge_kernel,
        grid=(T // tm, nj),
        in_specs=[pl.BlockSpec((tm, o_a.shape[1]), lambda i, j: (i, 0)),
                  pl.BlockSpec((tm, o_b.shape[1]), lambda i, j: (i, 0)),
                  grp, grp, grp, gate(0), gate(1), gate(2), wspec(p_a), wspec(p_b), wspec(p_c)],
        out_specs=pl.BlockSpec((tm, tn), lambda i, j: (i, j)),
        out_shape=jax.ShapeDtypeStruct((T, D), jnp.bfloat16),
        compiler_params=_params(("parallel", "arbitrary")),
        name="merge",
    )(o_a, o_b, acc, m, l, proj, proj, proj, p_a, p_b, p_c)


def _out_ln_kernel(mg_ref, w_ref, x_ref, g1_ref, lg_ref, lb_ref, o_ref, *, alpha):
    mix = _dot(mg_ref[...], w_ref[...])
    y = alpha * x_ref[...] + (1.0 + g1_ref[...]) * mix
    o_ref[...] = _layer_norm(y, lg_ref[...], lb_ref[...])


def out_ln(merged, w_o, x2, g1, ln_g, ln_b, seq, tm, alpha):
    T, D = x2.shape
    per = seq // tm
    tile = pl.BlockSpec((tm, D), lambda i: (i, 0))
    row = pl.BlockSpec((1, D), lambda i: (0, 0))
    return pl.pallas_call(
        functools.partial(_out_ln_kernel, alpha=alpha),
        grid=(T // tm,),
        in_specs=[tile, pl.BlockSpec((D, D), lambda i: (0, 0)), tile,
                  pl.BlockSpec((None, 1, D), lambda i: (i // per, 0, 0)), row, row],
        out_specs=tile,
        out_shape=jax.ShapeDtypeStruct((T, D), jnp.float32),
        compiler_params=_params(("parallel",)),
        name="out_ln",
    )(merged, w_o, x2, g1, ln_g, ln_b)


def _router_kernel(x_ref, sc_ref, sh_ref, w_ref, b_ref, id_ref, gw_ref, h_ref):
    h = x_ref[...] * (1.0 + sc_ref[...]) + sh_ref[...]
    h_ref[...] = h.astype(h_ref.dtype)
    lg = _dot(h, w_ref[...], precision=HI) + b_ref[...]
    G, EPG = MOE_GROUPS, MOE_EXPERTS_PER_GROUP
    lane = lax.broadcasted_iota(jnp.int32, (1, LANES), 1).astype(jnp.float32)
    first = lambda hit: jnp.min(jnp.where(hit, lane, float(LANES)), axis=-1, keepdims=True)
    is_grp = lane < G
    gmax = jnp.max(jnp.where(is_grp, lg, -jnp.inf), axis=-1, keepdims=True)
    ge = jnp.where(is_grp, jnp.exp(jnp.where(is_grp, lg, gmax) - gmax), 0.0)
    prob = ge / jnp.sum(ge, axis=-1, keepdims=True)
    grp_p = jnp.max(prob, axis=-1, keepdims=True)
    grp_i = first(is_grp & (prob == grp_p))
    lo = G + grp_i * EPG
    el = jnp.where((lane >= lo) & (lane < lo + EPG), lg, -jnp.inf)
    l1 = jnp.max(el, axis=-1, keepdims=True)
    i1 = first(el == l1)
    el = jnp.where(lane == i1, -jnp.inf, el)
    l2 = jnp.max(el, axis=-1, keepdims=True)
    i2 = first(el == l2)
    t = jnp.exp(l2 - l1)
    w1 = grp_p / (1.0 + t)
    id_ref[...] = jnp.where(lane == 0, i1 - G, jnp.where(lane == 1, i2 - G, 0.0)).astype(jnp.int32)
    gw_ref[...] = jnp.where(lane == 0, w1, jnp.where(lane == 1, w1 * t, 0.0))


def router(x2, sc, sh, w_r, b_r, seq, tm):
    T, D = x2.shape
    per = seq // tm
    mod = pl.BlockSpec((None, 1, D), lambda i: (i // per, 0, 0))
    narrow = pl.BlockSpec((tm, LANES), lambda i: (i, 0))
    return pl.pallas_call(
        _router_kernel,
        grid=(T // tm,),
        in_specs=[pl.BlockSpec((tm, D), lambda i: (i, 0)), mod, mod,
                  pl.BlockSpec((D, LANES), lambda i: (0, 0)), pl.BlockSpec((1, LANES), lambda i: (0, 0))],
        out_specs=[narrow, narrow, pl.BlockSpec((tm, D), lambda i: (i, 0))],
        out_shape=[jax.ShapeDtypeStruct((T, LANES), jnp.int32), jax.ShapeDtypeStruct((T, LANES), jnp.float32),
                   jax.ShapeDtypeStruct((T, D), jnp.bfloat16)],
        compiler_params=_params(("parallel",)),
        name="router",
    )(x2, sc, sh, w_r, b_r)


def _experts_kernel(be_ref, nu_ref, x_ref, sw_ref, wg_ref, wu_ref, wd_ref, o_ref):
    del be_ref
    i = pl.program_id(0)

    @pl.when(i < nu_ref[0])
    def _():
        x = x_ref[...]
        gate = _dot(x, wg_ref[...])
        hid = gate * _sigmoid(gate) * _dot(x, wu_ref[...])
        o_ref[...] = _dot(hid.astype(jnp.bfloat16), wd_ref[...]) * sw_ref[...]

    @pl.when(i >= nu_ref[0])
    def _():
        o_ref[...] = jnp.zeros_like(o_ref)


def experts(xs, slot_w, block_e, n_used, w_gate, w_up, w_down):
    R, D = xs.shape
    F = w_gate.shape[-1]
    n_blocks = R // MOE_BLOCK
    grid_spec = pltpu.PrefetchScalarGridSpec(
        num_scalar_prefetch=2,
        grid=(n_blocks,),
        in_specs=[pl.BlockSpec((MOE_BLOCK, D), lambda i, be, nu: (i, 0)),
                  pl.BlockSpec((MOE_BLOCK, 1), lambda i, be, nu: (i, 0)),
                  pl.BlockSpec((None, D, F), lambda i, be, nu: (be[i], 0, 0)),
                  pl.BlockSpec((None, D, F), lambda i, be, nu: (be[i], 0, 0)),
                  pl.BlockSpec((None, F, D), lambda i, be, nu: (be[i], 0, 0))],
        out_specs=pl.BlockSpec((MOE_BLOCK, D), lambda i, be, nu: (i, 0)),
    )
    return pl.pallas_call(
        _experts_kernel,
        grid_spec=grid_spec,
        out_shape=jax.ShapeDtypeStruct((R, D), jnp.float32),
        compiler_params=_params(("arbitrary",)),
        name="experts",
    )(block_e, n_used, xs, slot_w, w_gate, w_up, w_down)


def _combine_ln_kernel(x_ref, f0_ref, f1_ref, g2_ref, lg_ref, lb_ref, o_ref, *, alpha):
    y = alpha * x_ref[...] + (1.0 + g2_ref[...]) * (f0_ref[...] + f1_ref[...])
    o_ref[...] = _layer_norm(y, lg_ref[...], lb_ref[...])


def combine_ln(x2, f0, f1, g2, ln_g, ln_b, seq, tm, alpha):
    T, D = x2.shape
    per = seq // tm
    tile = pl.BlockSpec((tm, D), lambda i: (i, 0))
    row = pl.BlockSpec((1, D), lambda i: (0, 0))
    return pl.pallas_call(
        functools.partial(_combine_ln_kernel, alpha=alpha),
        grid=(T // tm,),
        in_specs=[tile, tile, tile, pl.BlockSpec((None, 1, D), lambda i: (i // per, 0, 0)), row, row],
        out_specs=tile,
        out_shape=jax.ShapeDtypeStruct((T, D), jnp.float32),
        compiler_params=_params(("parallel",)),
        name="combine_ln",
    )(x2, f0, f1, g2, ln_g, ln_b)


def _to_residue(t, dilation):
    B, S, W = t.shape
    return t.reshape(B, S // dilation, dilation, W).transpose(0, 2, 1, 3).reshape(B, S, W)


def _from_residue(t, dilation):
    B, S, W = t.shape
    return t.reshape(B, dilation, S // dilation, W).transpose(0, 2, 1, 3).reshape(B, S, W)


def _route(expert_id, gate_w):
    T = expert_id.shape[0]
    E = MOE_GROUPS * MOE_EXPERTS_PER_GROUP
    A = T * MOE_TOPK
    n_blocks = (A + E * (MOE_BLOCK - 1) + MOE_BLOCK - 1) // MOE_BLOCK
    flat_e = expert_id.reshape(A)
    flat_w = gate_w.reshape(A)
    e_s, order = lax.sort_key_val(flat_e, jnp.arange(A, dtype=jnp.int32))
    bounds = jnp.searchsorted(e_s, jnp.arange(E + 1, dtype=jnp.int32), side='left').astype(jnp.int32)
    start = bounds[:E]
    counts = bounds[1:] - start
    padded = (counts + MOE_BLOCK - 1) // MOE_BLOCK * MOE_BLOCK
    pad_end = jnp.cumsum(padded)
    pad_start = pad_end - padded
    block_e = jnp.minimum(jnp.sum(pad_end[None, :] <= (jnp.arange(n_blocks) * MOE_BLOCK)[:, None], axis=1),
                          E - 1).astype(jnp.int32)
    slot = jnp.arange(n_blocks * MOE_BLOCK, dtype=jnp.int32)
    slot_e = jnp.repeat(block_e, MOE_BLOCK)
    rank = slot - pad_start[slot_e]
    valid = rank < counts[slot_e]
    src = order[jnp.clip(start[slot_e] + rank, 0, A - 1)]
    slot_tok = jnp.where(valid, src // MOE_TOPK, T).astype(jnp.int32)
    slot_w = jnp.where(valid, flat_w[src], 0.0)
    dest = (pad_start[e_s] + jnp.arange(A, dtype=jnp.int32) - start[e_s]).astype(jnp.int32)
    _, slot_of = lax.sort_key_val(order, dest)
    n_used = (pad_end[-1] // MOE_BLOCK).astype(jnp.int32).reshape(1)
    return slot_tok, slot_w, slot_of.reshape(T, MOE_TOPK), block_e, n_used


def kernel(x, c, rel_bias, w_in, p_a, p_b, p_c, w_o, rwkv_mu, rwkv_w0, rwkv_w_up, rwkv_a0, rwkv_a_up,
           rwkv_g_up, rwkv_k_k, rwkv_k_a, rwkv_r_k, rwkv_ln_g, rwkv_ln_b, rwkv_v0, rwkv_mv_down,
           rwkv_mv_up, w_ada, b_ada, ln1_g, ln1_b, ln2_g, ln2_b, router_grp_w, router_grp_b,
           router_exp_w, router_exp_b, exp_w_gate, exp_w_up, exp_w_down):
    B, S, D = x.shape
    depth = w_in.shape[0]
    T = B * S
    bf = jnp.bfloat16
    W = RWKV_HEADS * HEAD_DIM
    wa_w = MOBA_HEADS * HEAD_DIM
    wc_w = len(DIL_GROUPS) * DIL_HEADS_PER_GROUP * HEAD_DIM
    rw_cols = rwkv_mu.shape[-1]
    alpha = (2 * depth) ** 0.25
    off_c = 3 * wa_w
    off_b = off_c + 3 * wc_w
    off_g = off_b + rw_cols
    new_a = rw_cols
    new_c = new_a + 3 * wa_w
    new_g = new_c + 3 * wc_w
    w_in_p = jnp.concatenate([w_in[:, :, off_b:off_g], w_in[:, :, :off_b], w_in[:, :, off_g:]], axis=-1).astype(bf)
    p_a_b, p_b_b, p_c_b, w_o_b = p_a.astype(bf), p_b.astype(bf), p_c.astype(bf), w_o.astype(bf)
    wg_b, wu_b, wd_b = exp_w_gate.astype(bf), exp_w_up.astype(bf), exp_w_down.astype(bf)

    bias_h = rel_bias.T.astype(jnp.float32)
    moba_tab = moba_bias_table(bias_h[:MOBA_HEADS], S // MOBA_BLOCK)
    dil_tab = dil_bias_table(bias_h[MOBA_HEADS:])

    c8 = jnp.zeros((8, D), jnp.float32).at[:B].set(c)
    mod = ada_mod(c8, w_ada, b_ada)[:, :B]

    hd_idx = jnp.arange(LANES) // HEAD_DIM
    bd = (hd_idx[:, None] == hd_idx[None, :]).astype(jnp.float32)
    zeros_w = jnp.zeros((RWKV_DECAY_LORA, W), jnp.float32)
    pad_lora = LANES - RWKV_MV_LORA

    x2 = x.reshape(T, D)
    v_first = None
    for l in range(depth):
        sh1, sc1, g1, sh2, sc2, g2 = [m.reshape(B, 1, D) for m in jnp.split(mod[l], 6, axis=-1)]
        proj = in_proj(x2, sc1, sh1, w_in_p[l], S, 1024, 1024)
        proj3 = proj.reshape(B, S, -1)
        o_a = moba_attention(proj3, moba_tab, new_a // LANES, (new_a + wa_w) // LANES,
                             (new_a + 2 * wa_w) // LANES).reshape(T, wa_w)
        gw = DIL_HEADS_PER_GROUP * HEAD_DIM
        qs, ks, vs = [], [], []
        for g, (_, dil) in enumerate(DIL_GROUPS):
            qs.append(_to_residue(proj3[:, :, new_c + g * gw:new_c + (g + 1) * gw], dil))
            ks.append(_to_residue(proj3[:, :, new_c + wc_w + g * gw:new_c + wc_w + (g + 1) * gw], dil))
            vs.append(_to_residue(proj3[:, :, new_c + 2 * wc_w + g * gw:new_c + 2 * wc_w + (g + 1) * gw], dil))
        acc, mm, ll = dilated_attention(jnp.stack(qs, 1), jnp.stack(ks, 1), jnp.stack(vs, 1), dil_tab)
        unperm = lambda t: jnp.stack([_from_residue(t[:, g], dil).reshape(T, gw)
                                      for g, (_, dil) in enumerate(DIL_GROUPS)])
        acc, mm, ll = unperm(acc), unperm(mm), unperm(ll)
        wa_up = jnp.concatenate([jnp.concatenate([rwkv_w_up[l], zeros_w], axis=1),
                                 jnp.concatenate([zeros_w, rwkv_a_up[l]], axis=1)], axis=0)
        res = None
        if l > 0:
            res = (v_first, rwkv_v0[l - 1][None],
                   jnp.pad(rwkv_mv_down[l - 1], ((0, 0), (0, pad_lora))),
                   jnp.pad(rwkv_mv_up[l - 1], ((0, pad_lora), (0, 0))))
        r_, lw_, k_, v_, kn_, b_, g_ = rwkv_prep(proj, S, 256, rwkv_mu[l][None], rwkv_w0[l][None],
                                                 rwkv_a0[l][None], wa_up, rwkv_g_up[l], rwkv_k_k[l][None],
                                                 rwkv_k_a[l][None], bd, res)
        if l == 0:
            v_first = v_
        rhat, y0, e_, gmat, dmat = rwkv_chunk(r_, lw_, k_, v_, kn_, b_, g_, rwkv_r_k[l][None], bd, B)
        o_b = rwkv_state(rhat, y0, g_, e_, gmat, dmat, rwkv_ln_g[l][None], rwkv_ln_b[l][None], bd)
        merged = merge_branches(o_a, o_b, acc, mm, ll, proj, new_g // 1024, p_a_b[l], p_b_b[l], p_c_b[l],
                                256, 1024)
        x2 = out_ln(merged, w_o_b[l], x2, g1, ln1_g[l][None], ln1_b[l][None], S, 256, alpha)
        w_r = jnp.zeros((D, LANES), jnp.float32)
        w_r = w_r.at[:, :MOE_GROUPS].set(router_grp_w[l]).at[:, MOE_GROUPS:MOE_GROUPS + router_exp_w.shape[-1]].set(
            router_exp_w[l])
        b_r = jnp.zeros((1, LANES), jnp.float32)
        b_r = b_r.at[0, :MOE_GROUPS].set(router_grp_b[l]).at[0, MOE_GROUPS:MOE_GROUPS + router_exp_b.shape[-1]].set(
            router_exp_b[l])
        ids, gws, h2 = router(x2, sc2, sh2, w_r, b_r, S, 512)
        slot_tok, slot_w, slot_of, block_e, n_used = _route(ids[:, :MOE_TOPK], gws[:, :MOE_TOPK])
        h2p = jnp.concatenate([h2, jnp.zeros((1, D), h2.dtype)], axis=0)
        y = experts(h2p[slot_tok], slot_w[:, None], block_e, n_used, wg_b[l], wu_b[l], wd_b[l])
        x2 = combine_ln(x2, y[slot_of[:, 0]], y[slot_of[:, 1]], g2, ln2_g[l][None], ln2_b[l][None], S, 512, alpha)
    return x2.reshape(B, S, D)
```

```python
import functools
import math

import jax
import jax.numpy as jnp
import numpy as np
from jax import lax
from jax.experimental import pallas as pl
from jax.experimental.pallas import tpu as pltpu

HEAD_DIM = 64
LANES = 128
MOBA_HEADS = 12
MOBA_BLOCK = 256
MOBA_TOPK = 3
RWKV_HEADS = 12
RWKV_DECAY_LORA = 64
RWKV_A_LORA = 64
RWKV_MV_LORA = 32
RWKV_GATE_LORA = 128
RWKV_GN_EPS = 64e-5
RWKV_CHUNK = 64
DIL_GROUPS = ((128, 1), (512, 4), (2048, 16))
DIL_HEADS_PER_GROUP = 4
DIL_SPAN = 128
REL_BUCKETS = 32
REL_MAX_DISTANCE = 2048
MOE_GROUPS = 8
MOE_EXPERTS_PER_GROUP = 8
MOE_TOPK = 2
MOE_BLOCK = 256
LN_EPS = 1e-5
NEG = -1e30
VMEM_LIMIT = 56 * 1024 * 1024
HI = lax.Precision.HIGHEST


def _params(sem):
    return pltpu.CompilerParams(dimension_semantics=sem, vmem_limit_bytes=VMEM_LIMIT)


def _sigmoid(x):
    return 1.0 / (1.0 + jnp.exp(-x))


def _dot(a, b, precision=None):
    return jnp.dot(a, b, preferred_element_type=jnp.float32, precision=precision)


def _dot_nt(a, b, precision=None):
    return lax.dot_general(a, b, (((1,), (1,)), ((), ())), preferred_element_type=jnp.float32,
                           precision=precision)


def _dot_tn(a, b, precision=None):
    return lax.dot_general(a, b, (((0,), (0,)), ((), ())), preferred_element_type=jnp.float32,
                           precision=precision)


def _split2(x):
    hi = x.astype(jnp.bfloat16)
    return hi, (x - hi.astype(jnp.float32)).astype(jnp.bfloat16)


def _dot3(a, b, dims=((1,), (0,))):
    (ca,), (cb,) = dims
    ah, al = _split2(a)
    bh, bl = _split2(b)
    return lax.dot_general(jnp.concatenate([ah, ah, al], axis=ca), jnp.concatenate([bh, bl, bh], axis=cb),
                           (dims, ((), ())), preferred_element_type=jnp.float32)


def _dot_sel(x, sel):
    xh, xl = _split2(x)
    sb = sel.astype(jnp.bfloat16)
    return _dot(jnp.concatenate([xh, xl], axis=1), jnp.concatenate([sb, sb], axis=0))


_NT = ((1,), (1,))
_TN = ((0,), (0,))


def _layer_norm(y, g, b):
    mu = jnp.mean(y, axis=-1, keepdims=True)
    d = y - mu
    var = jnp.mean(d * d, axis=-1, keepdims=True)
    return d * lax.rsqrt(var + LN_EPS) * g + b


def _ada_kernel(c_ref, w_ref, b_ref, o_ref):
    c = c_ref[...]
    cond = c * _sigmoid(c)
    o_ref[...] = _dot(cond, w_ref[...]) + b_ref[...]


def ada_mod(c8, w_ada, b_ada):
    L, D, N = w_ada.shape
    tn = 1024
    return pl.pallas_call(
        _ada_kernel,
        grid=(L, N // tn),
        in_specs=[pl.BlockSpec((8, D), lambda l, j: (0, 0)),
                  pl.BlockSpec((None, D, tn), lambda l, j: (l, 0, j)),
                  pl.BlockSpec((None, 1, tn), lambda l, j: (l, 0, j))],
        out_specs=pl.BlockSpec((None, 8, tn), lambda l, j: (l, 0, j)),
        out_shape=jax.ShapeDtypeStruct((L, 8, N), jnp.float32),
        compiler_params=_params(("parallel", "parallel")),
        name="ada_mod",
    )(c8, w_ada, b_ada.reshape(L, 1, N))


def _in_proj_kernel(x_ref, sc_ref, sh_ref, w_ref, o_ref, h_ref):
    @pl.when(pl.program_id(1) == 0)
    def _():
        h_ref[...] = (x_ref[...] * (1.0 + sc_ref[...]) + sh_ref[...]).astype(h_ref.dtype)

    o_ref[...] = _dot(h_ref[...], w_ref[...])


def in_proj(x2, sc, sh, w, layer, seq, tm, tn):
    T, D = x2.shape
    N = w.shape[-1]
    per = seq // tm
    return pl.pallas_call(
        _in_proj_kernel,
        grid=(T // tm, N // tn),
        in_specs=[pl.BlockSpec((tm, D), lambda i, j: (i, 0)),
                  pl.BlockSpec((None, 1, D), lambda i, j: (i // per, 0, 0)),
                  pl.BlockSpec((None, 1, D), lambda i, j: (i // per, 0, 0)),
                  pl.BlockSpec((None, D, tn), lambda i, j: (layer, 0, j))],
        out_specs=pl.BlockSpec((tm, tn), lambda i, j: (i, j)),
        out_shape=jax.ShapeDtypeStruct((T, N), jnp.float32),
        scratch_shapes=[pltpu.VMEM((tm, D), jnp.bfloat16)],
        compiler_params=_params(("parallel", "arbitrary")),
        name="in_proj",
    )(x2, sc, sh, w)


def _t5_bucket(dist):
    n = jnp.maximum(dist, 0)
    max_exact = REL_BUCKETS // 2
    nf = jnp.maximum(n, 1).astype(jnp.float32)
    large = max_exact + (jnp.log(nf / max_exact) / math.log(REL_MAX_DISTANCE / max_exact)
                         * (REL_BUCKETS - max_exact)).astype(jnp.int32)
    large = jnp.minimum(large, REL_BUCKETS - 1)
    return jnp.where(n < max_exact, n, large)


def _moba_n_delta(nb):
    last_start = 1
    d = np.arange(1, nb * MOBA_BLOCK + 1)
    large = 16 + (np.log(d / 16.0) / math.log(REL_MAX_DISTANCE / 16.0) * 16).astype(np.int64)
    bucket = np.where(d < 16, d, np.minimum(large, REL_BUCKETS - 1))
    last_start = int(d[bucket < REL_BUCKETS - 1].max()) + 1 if (bucket < REL_BUCKETS - 1).any() else 1
    delta = 1
    while delta * MOBA_BLOCK - (MOBA_BLOCK - 1) < last_start + 2:
        delta += 1
    return min(delta + 1, nb)


def moba_bias_table(bias_a, nb):
    nd = _moba_n_delta(nb)
    key = jnp.arange(MOBA_BLOCK)[:, None]
    qry = jnp.arange(MOBA_BLOCK)[None, :]
    dist = jnp.arange(nd)[:, None, None] * MOBA_BLOCK + (qry - key)[None]
    bucket = _t5_bucket(dist)[None]
    tab = jnp.zeros((bias_a.shape[0],) + dist.shape, jnp.float32)
    for b in range(REL_BUCKETS):
        tab = jnp.where(bucket == b, bias_a[:, b][:, None, None, None], tab)
    return jnp.where((dist >= 0)[None], tab, NEG)


def dil_bias_table(bias_c):
    span = DIL_SPAN
    rel = span + jnp.arange(span)[:, None] - jnp.arange(2 * span)[None, :]
    valid = (rel >= 0) & (rel <= span)
    tabs = []
    for g, (_, dilation) in enumerate(DIL_GROUPS):
        bh = bias_c[g * DIL_HEADS_PER_GROUP:(g + 1) * DIL_HEADS_PER_GROUP]
        tabs.append(jnp.where(valid[None], bh[:, _t5_bucket(rel * dilation)], NEG))
    return jnp.stack(tabs)


def _moba_kernel(q_ref, k_ref, v_ref, bias_ref, o_ref,
                 kaug_ref, vt_ref, km_ref, acc_ref, *, nb, n_delta):
    qb = pl.program_id(2)
    bs = MOBA_BLOCK
    S = nb * bs
    half = bs // 2
    lane = lax.broadcasted_iota(jnp.int32, (1, LANES), 1)
    f32 = jnp.float32

    @pl.when(qb == 0)
    def _():
        k = k_ref[...]
        rowblk = lax.broadcasted_iota(jnp.int32, (S, LANES), 0) // bs
        lanes = lax.broadcasted_iota(jnp.int32, (S, LANES), 1)
        km_ref[...] = jnp.mean(k.reshape(nb, bs, LANES), axis=1)
        kaug_ref[0] = jnp.where(lanes < HEAD_DIM, k, (lanes - HEAD_DIM == rowblk).astype(f32)).astype(kaug_ref.dtype)
        kaug_ref[1] = jnp.where(lanes >= HEAD_DIM, k, (lanes == rowblk).astype(f32)).astype(kaug_ref.dtype)
        for i in range(nb):
            vt_ref[i] = v_ref[i * bs:(i + 1) * bs, :].T.astype(vt_ref.dtype)

    q_t = q_ref[...].T
    scale = HEAD_DIM ** -0.5
    blk = lax.broadcasted_iota(jnp.int32, (nb, 1), 0)
    past = blk < qb
    rhs = []
    for t in range(2):
        hm = (lane >= t * HEAD_DIM) & (lane < (t + 1) * HEAD_DIM)
        gate = _dot(jnp.where(hm, km_ref[...], 0.0), q_t, precision=HI)
        gate = jnp.where(past, gate, -jnp.inf)
        cnt = jnp.zeros((nb, bs), jnp.int32)
        for m in range(nb):
            gm = gate[m:m + 1, :]
            ahead = (gm > gate) | ((gm == gate) & (m < blk))
            cnt = cnt + ahead.astype(jnp.int32)
        chosen = (past & (cnt < MOBA_TOPK)) | (blk == qb)
        pen = jnp.where(chosen, 0.0, NEG)
        qh = q_t[t * HEAD_DIM:(t + 1) * HEAD_DIM, :] * scale
        if t == 0:
            parts = [qh, pen, jnp.zeros((LANES - HEAD_DIM - nb, bs), f32)]
        else:
            parts = [pen, jnp.zeros((HEAD_DIM - nb, bs), f32), qh]
        rhs.append(jnp.concatenate(parts, axis=0).astype(jnp.bfloat16))
    acc_ref[...] = jnp.zeros((LANES, bs), f32)

    chains = [(t, hq) for t in range(2) for hq in range(2)]
    cols = [slice(hq * half, (hq + 1) * half) for _, hq in chains]

    def step(blocks, carry):
        rows = [pl.multiple_of(n * bs, bs) for n in blocks]
        delta = [jnp.minimum(qb - n, n_delta - 1) for n in blocks]
        s = [[_dot(kaug_ref[t, pl.ds(rows[j], bs), :], rhs[t][:, cols[c]]) + bias_ref[t, delta[j], :, cols[c]]
              for j in range(len(blocks))] for c, (t, _) in enumerate(chains)]
        m_new = [functools.reduce(jnp.maximum, [carry[c][0]] + [jnp.max(x, axis=0, keepdims=True) for x in s[c]])
                 for c in range(4)]
        alpha = [jnp.exp(carry[c][0] - m_new[c]) for c in range(4)]
        p = [[jnp.exp(x - m_new[c]) for x in s[c]] for c in range(4)]
        l_new = [alpha[c] * carry[c][1] + sum(jnp.sum(x, axis=0, keepdims=True) for x in p[c]) for c in range(4)]
        pv = [sum(_dot(vt_ref[n, t * HEAD_DIM:(t + 1) * HEAD_DIM, :], p[c][j].astype(jnp.bfloat16))
                  for j, n in enumerate(blocks)) for c, (t, _) in enumerate(chains)]
        pieces = [alpha[c] * acc_ref[t * HEAD_DIM:(t + 1) * HEAD_DIM, cols[c]] + pv[c]
                  for c, (t, _) in enumerate(chains)]
        acc_ref[...] = jnp.concatenate([jnp.concatenate(pieces[:2], axis=1),
                                        jnp.concatenate(pieces[2:], axis=1)], axis=0)
        return tuple((m_new[c], l_new[c]) for c in range(4))

    init = tuple((jnp.full((1, half), NEG, f32), jnp.zeros((1, half), f32)) for _ in range(4))
    n_blocks = qb + 1
    fin = lax.fori_loop(0, n_blocks // 2, lambda i, c: step([2 * i, 2 * i + 1], c), init)
    fin = lax.cond(n_blocks % 2 == 1, lambda c: step([qb], c), lambda c: c, fin)
    l_all = jnp.concatenate([jnp.broadcast_to(jnp.concatenate([fin[2 * t][1], fin[2 * t + 1][1]], axis=1),
                                              (HEAD_DIM, bs)) for t in range(2)], axis=0)
    o_ref[...] = (acc_ref[...] / l_all).T


def moba_attention(proj3, bias_tab, q_off, k_off, v_off):
    B, S, _ = proj3.shape
    nb = S // MOBA_BLOCK
    n_delta = bias_tab.shape[1]
    pairs = MOBA_HEADS // 2
    kern = functools.partial(_moba_kernel, nb=nb, n_delta=n_delta)
    return pl.pallas_call(
        kern,
        grid=(pairs, B, nb),
        in_specs=[pl.BlockSpec((None, MOBA_BLOCK, LANES), lambda h, b, i: (b, i, q_off + h)),
                  pl.BlockSpec((None, S, LANES), lambda h, b, i: (b, 0, k_off + h)),
                  pl.BlockSpec((None, S, LANES), lambda h, b, i: (b, 0, v_off + h)),
                  pl.BlockSpec((2, n_delta, MOBA_BLOCK, MOBA_BLOCK), lambda h, b, i: (h, 0, 0, 0))],
        out_specs=pl.BlockSpec((None, MOBA_BLOCK, LANES), lambda h, b, i: (b, i, h)),
        out_shape=jax.ShapeDtypeStruct((B, S, pairs * LANES), jnp.float32),
        scratch_shapes=[pltpu.VMEM((2, S, LANES), jnp.bfloat16),
                        pltpu.VMEM((nb, LANES, MOBA_BLOCK), jnp.bfloat16),
                        pltpu.VMEM((nb, LANES), jnp.float32),
                        pltpu.VMEM((LANES, MOBA_BLOCK), jnp.float32)],
        compiler_params=_params(("parallel", "parallel", "arbitrary")),
        name="moba",
    )(proj3, proj3, proj3, bias_tab)


def _dil_kernel(q_ref, kp_ref, kc_ref, vp_ref, vc_ref, bias_ref, acc_ref, m_ref, l_ref, *, blocks_per_seq0):
    g = pl.program_id(1)
    blk = pl.program_id(2)
    span = DIL_SPAN
    width = DIL_HEADS_PER_GROUP * HEAD_DIM
    bps = jnp.right_shift(blocks_per_seq0, 2 * g)
    first = (blk & (bps - 1)) == 0
    lane = lax.broadcasted_iota(jnp.int32, (1, width), 1)
    col = lax.broadcasted_iota(jnp.int32, (1, 2 * span), 1)
    no_prev = first & (col < span)
    q = q_ref[...] * (HEAD_DIM ** -0.5)
    kcat = jnp.concatenate([kp_ref[...], kc_ref[...]], axis=0).astype(jnp.bfloat16)
    vcat = jnp.concatenate([vp_ref[...], vc_ref[...]], axis=0).astype(jnp.bfloat16)
    heads = range(DIL_HEADS_PER_GROUP)
    hm = [(lane >= t * HEAD_DIM) & (lane < (t + 1) * HEAD_DIM) for t in heads]
    s = [jnp.where(no_prev, NEG, _dot_nt(jnp.where(hm[t], q, 0.0).astype(jnp.bfloat16), kcat) + bias_ref[t])
         for t in heads]
    m = [jnp.max(s[t], axis=-1, keepdims=True) for t in heads]
    e = [jnp.exp(s[t] - m[t]) for t in heads]
    l = [jnp.sum(e[t], axis=-1, keepdims=True) for t in heads]
    pv = [_dot(e[t].astype(jnp.bfloat16), vcat) for t in heads]
    acc = jnp.zeros((span, width), jnp.float32)
    mb = jnp.zeros((span, width), jnp.float32)
    lb = jnp.zeros((span, width), jnp.float32)
    for t in heads:
        acc = jnp.where(hm[t], pv[t], acc)
        mb = jnp.where(hm[t], m[t], mb)
        lb = jnp.where(hm[t], l[t], lb)
    acc_ref[...] = acc
    m_ref[...] = mb
    l_ref[...] = lb


def dilated_attention(qp, kp, vp, bias_tab):
    B, G, S, W = qp.shape
    span = DIL_SPAN
    nblk = S // span
    cur = pl.BlockSpec((None, None, span, W), lambda b, g, i: (b, g, i, 0))
    prev = pl.BlockSpec((None, None, span, W), lambda b, g, i: (b, g, jnp.maximum(i - 1, 0), 0))
    out = jax.ShapeDtypeStruct((B, G, S, W), jnp.float32)
    kern = functools.partial(_dil_kernel, blocks_per_seq0=nblk)
    return pl.pallas_call(
        kern,
        grid=(B, G, nblk),
        in_specs=[cur, prev, cur, prev, cur,
                  pl.BlockSpec((None, DIL_HEADS_PER_GROUP, span, 2 * span), lambda b, g, i: (g, 0, 0, 0))],
        out_specs=[cur, cur, cur],
        out_shape=[out, out, out],
        compiler_params=_params(("parallel", "parallel", "parallel")),
        name="dilated",
    )(qp, kp, kp, vp, vp, bias_tab)


def _rwkv_prep_kernel(*refs, width, has_res, rows_per_seq):
    if has_res:
        (z_ref, zl_ref, mu_ref, w0_ref, a0_ref, wa_ref, gup_ref, kk_ref, ka_ref, bd_ref,
         vf_ref, v0_ref, mvd_ref, mvu_ref,
         r_o, lw_o, k_o, v_o, kn_o, b_o, g_o) = refs
    else:
        (z_ref, zl_ref, mu_ref, w0_ref, a0_ref, wa_ref, gup_ref, kk_ref, ka_ref, bd_ref,
         r_o, lw_o, k_o, v_o, kn_o, b_o, g_o) = refs
    i = pl.program_id(0)
    W = width
    z = z_ref[...]
    tm = z.shape[0]
    row = lax.broadcasted_iota(jnp.int32, (tm, 1), 0)
    seq_start = (i % rows_per_seq) == 0
    last = jnp.where(seq_start, 0.0, zl_ref[7:8, :])
    zp = jnp.where(row == 0, last, pltpu.roll(z, 1, 0))
    zf = z + mu_ref[...] * (zp - z)
    lora = zf[:, 3 * W:3 * W + LANES]
    lane = lax.broadcasted_iota(jnp.int32, (1, LANES), 1)
    lora = jnp.where(lane < RWKV_DECAY_LORA, jnp.tanh(lora), lora)
    wa = _dot(lora, wa_ref[...], precision=HI)
    g = _dot(_sigmoid(zf[:, 3 * W + LANES:3 * W + 2 * LANES]), gup_ref[...], precision=HI)
    g_o[...] = g
    v_all = zf[:, 2 * W:3 * W]
    if has_res:
        mix = _dot(_dot(v_all, mvd_ref[...], precision=HI), mvu_ref[...], precision=HI)
    for c in range(W // LANES):
        sl = slice(c * LANES, (c + 1) * LANES)
        x = w0_ref[:, sl] + wa[:, sl]
        sp = jnp.maximum(-x, 0.0) + jnp.log(1.0 + jnp.exp(-jnp.abs(x)))
        lw_o[:, sl] = -jnp.exp(-sp - 0.5)
        a = _sigmoid(a0_ref[:, sl] + wa[:, W + c * LANES:W + (c + 1) * LANES])
        r_o[:, sl] = zf[:, sl]
        k = zf[:, W + c * LANES:W + (c + 1) * LANES]
        v = v_all[:, sl]
        if has_res:
            v = v + (vf_ref[:, sl] - v) * _sigmoid(v0_ref[:, sl] + mix[:, sl])
        v_o[:, sl] = v
        kk = k * kk_ref[:, sl]
        ss = _dot_sel(kk * kk, bd_ref[...])
        kn = kk / jnp.maximum(jnp.sqrt(ss), 1e-12)
        kn_o[:, sl] = kn
        b_o[:, sl] = kn * a
        k_o[:, sl] = k * (1.0 + (a - 1.0) * ka_ref[:, sl])


def rwkv_prep(proj, seq, tm, mu, w0, a0, wa_up, g_up, k_k, k_a, bd, res):
    T = proj.shape[0]
    W = RWKV_HEADS * HEAD_DIM
    cols = mu.shape[-1]
    row1 = lambda n: pl.BlockSpec((1, n), lambda i: (0, 0))
    full = lambda a: pl.BlockSpec(a.shape, lambda i: (0, 0))
    tile = pl.BlockSpec((tm, W), lambda i: (i, 0))
    in_specs = [pl.BlockSpec((tm, cols), lambda i: (i, 0)),
                pl.BlockSpec((8, cols), lambda i: (jnp.maximum(i * (tm // 8) - 1, 0), 0)),
                row1(cols), row1(W), row1(W), full(wa_up), full(g_up), row1(W), row1(W), full(bd)]
    args = [proj, proj, mu, w0, a0, wa_up, g_up, k_k, k_a, bd]
    if res is not None:
        v_first, v0, mvd, mvu = res
        in_specs += [tile, row1(W), full(mvd), full(mvu)]
        args += [v_first, v0, mvd, mvu]
    out = jax.ShapeDtypeStruct((T, W), jnp.float32)
    kern = functools.partial(_rwkv_prep_kernel, width=W, has_res=res is not None, rows_per_seq=seq // tm)
    return pl.pallas_call(
        kern,
        grid=(T // tm,),
        in_specs=in_specs,
        out_specs=[tile] * 7,
        out_shape=[out] * 7,
        compiler_params=_params(("parallel",)),
        name="rwkv_prep",
    )(*args)


def _stack_heads(x, lane):
    return jnp.concatenate([jnp.where(lane < HEAD_DIM, x, 0.0), jnp.where(lane >= HEAD_DIM, x, 0.0)], axis=0)


def _rwkv_chunk_kernel(r_ref, lw_ref, k_ref, v_ref, kn_ref, b_ref, g_ref, rk_ref, bd_ref,
                       rhat_ref, y0_ref, e_ref, g_out_ref, d_out_ref, *, chunks):
    C = RWKV_CHUNK
    C2, C4 = 2 * C, 4 * C
    f32 = jnp.float32
    ti = lax.broadcasted_iota(jnp.int32, (C, C), 0)
    si = lax.broadcasted_iota(jnp.int32, (C, C), 1)
    tri = (ti >= si).astype(f32)
    lane = lax.broadcasted_iota(jnp.int32, (1, LANES), 1)
    rho = lax.broadcasted_iota(jnp.int32, (C4, C4), 0)
    sig = lax.broadcasted_iota(jnp.int32, (C4, C4), 1)
    keep = jnp.where(rho >= C2, rho & (C - 1), (rho & (C - 1)) - 1) >= (sig & (C - 1))
    eye2 = (lax.broadcasted_iota(jnp.int32, (C2, C2), 0) == lax.broadcasted_iota(jnp.int32, (C2, C2), 1)).astype(f32)
    eye_l = (lax.broadcasted_iota(jnp.int32, (LANES, LANES), 0)
             == lax.broadcasted_iota(jnp.int32, (LANES, LANES), 1)).astype(f32)
    zeros2 = jnp.zeros((C2, LANES), f32)
    ch = range(chunks)
    rows = [slice(c * C, (c + 1) * C) for c in ch]
    cum = [_dot(tri, lw_ref[rows[c], :], precision=HI) for c in ch]
    st = []
    for c in ch:
        r, lw, k, v = r_ref[rows[c], :], lw_ref[rows[c], :], k_ref[rows[c], :], v_ref[rows[c], :]
        kn, bb = kn_ref[rows[c], :], b_ref[rows[c], :]
        cum_last = cum[c][C - 1:C, :]
        e_out = jnp.exp(-cum[c])
        e_tail = jnp.exp(cum_last - cum[c])
        r_t = r * jnp.exp(cum[c])
        st.append(dict(
            r_t=r_t, decay=jnp.exp(cum_last),
            a2=_stack_heads(-kn * jnp.exp(cum[c] - lw), lane), r2=_stack_heads(r_t, lane),
            b2=_stack_heads(bb * e_out, lane), k2=_stack_heads(k * e_out, lane), v2=_stack_heads(v, lane),
            bh2=_stack_heads(bb * e_tail, lane), kh2=_stack_heads(k * e_tail, lane),
            e=_dot_sel(r * k * rk_ref[...], bd_ref[...]) * v * g_ref[rows[c], :]))
    quad = [jnp.where(keep, _dot3(jnp.concatenate([s["a2"], s["r2"]], axis=0),
                                  jnp.concatenate([s["b2"], s["k2"]], axis=0), _NT), 0.0) for s in st]
    pw = [q[:C2, :C2] for q in quad]
    t_inv = [eye2 + p for p in pw]
    x = [_dot3(q[:C2, C2:], s["v2"]) for q, s in zip(quad, st)]
    for _ in range(int(math.log2(C)) - 1):
        pw = [_dot3(p, p) for p in pw]
        t_inv = [t + _dot3(t, p) for t, p in zip(t_inv, pw)]
    au = [_dot3(t, jnp.concatenate([s["a2"], xx], axis=1)) for t, s, xx in zip(t_inv, st, x)]
    my = [_dot3(q[C2:, :], jnp.concatenate([a, jnp.concatenate([zeros2, s["v2"]], axis=1)], axis=0))
          for q, a, s in zip(quad, au, st)]
    gm = [eye_l * s["decay"] + _dot3(a[:, :LANES], s["bh2"], _TN) for a, s in zip(au, st)]
    dm = [_dot3(jnp.concatenate([a[:, LANES:], s["v2"]], axis=0),
                jnp.concatenate([s["bh2"], s["kh2"]], axis=0), _TN) for a, s in zip(au, st)]
    rhat_ref[...] = jnp.concatenate([s["r_t"] + m[:C, :LANES] + m[C:, :LANES] for s, m in zip(st, my)], axis=0)
    y0_ref[...] = jnp.concatenate([m[:C, LANES:] + m[C:, LANES:] for m in my], axis=0)
    e_ref[...] = jnp.concatenate([s["e"] for s in st], axis=0)
    g_out_ref[...] = jnp.stack(gm)
    d_out_ref[...] = jnp.stack(dm)


def rwkv_chunk(r, lw, k, v, kn, b, g, r_k, bd, batch, chunks=4):
    T, W = r.shape
    S = T // batch
    C = RWKV_CHUNK
    nc = S // C
    pairs = W // LANES
    steps = nc // chunks
    tile = pl.BlockSpec((chunks * C, LANES), lambda bi, h, c: (bi * steps + c, h))
    mat = pl.BlockSpec((None, None, chunks, LANES, LANES), lambda bi, h, c: (bi, h, c, 0, 0))
    tw = jax.ShapeDtypeStruct((T, W), jnp.float32)
    gd = jax.ShapeDtypeStruct((batch, pairs, nc, LANES, LANES), jnp.float32)
    return pl.pallas_call(
        functools.partial(_rwkv_chunk_kernel, chunks=chunks),
        grid=(batch, pairs, steps),
        in_specs=[tile] * 7 + [pl.BlockSpec((1, LANES), lambda bi, h, c: (0, h)),
                               pl.BlockSpec((LANES, LANES), lambda bi, h, c: (0, 0))],
        out_specs=[tile, tile, tile, mat, mat],
        out_shape=[tw, tw, tw, gd, gd],
        compiler_params=_params(("parallel", "parallel", "parallel")),
        name="rwkv_chunk",
    )(r, lw, k, v, kn, b, g, r_k, bd)


def _rwkv_state_kernel(rhat_ref, y0_ref, g_ref, e_ref, gm_ref, dm_ref, lng_ref, lnb_ref, bd_ref,
                       o_ref, state_ref, *, chunks, group):
    C = RWKV_CHUNK

    @pl.when(pl.program_id(2) == 0)
    def _():
        state_ref[...] = jnp.zeros_like(state_ref)

    bd = bd_ref[...]
    inv_n = 1.0 / HEAD_DIM

    def body(c, carry):
        rows = pl.ds(pl.multiple_of(c * C, C), C)
        pairs = range(group)
        state = [state_ref[p] for p in pairs]
        y = [_dot3(rhat_ref[rows, p * LANES:(p + 1) * LANES], state[p], _NT) for p in pairs]
        new_state = [_dot3(state[p], gm_ref[p, c]) + dm_ref[p, c] for p in pairs]
        y = jnp.concatenate(y, axis=1) + y0_ref[rows, :]
        state_ref[...] = jnp.stack(new_state)
        mean = jnp.concatenate([_dot_sel(y[:, p * LANES:(p + 1) * LANES], bd) for p in pairs], axis=1) * inv_n
        d = y - mean
        dd = d * d
        var = jnp.concatenate([_dot_sel(dd[:, p * LANES:(p + 1) * LANES], bd) for p in pairs], axis=1) * inv_n
        yn = d * lax.rsqrt(var + RWKV_GN_EPS) * lng_ref[...] + lnb_ref[...]
        o_ref[rows, :] = yn * g_ref[rows, :] + e_ref[rows, :]
        return carry

    lax.fori_loop(0, chunks, body, 0)


def rwkv_state(rhat, y0, g, e, gmat, dmat, ln_g, ln_b, bd, chunks=16, group=3):
    T, W = rhat.shape
    batch, pairs, nc = gmat.shape[:3]
    C = RWKV_CHUNK
    steps = nc // chunks
    gw = group * LANES
    tile = pl.BlockSpec((chunks * C, gw), lambda bi, h, c: (bi * steps + c, h))
    mat = pl.BlockSpec((None, group, chunks, LANES, LANES), lambda bi, h, c: (bi, h, c, 0, 0))
    row = pl.BlockSpec((1, gw), lambda bi, h, c: (0, h))
    return pl.pallas_call(
        functools.partial(_rwkv_state_kernel, chunks=chunks, group=group),
        grid=(batch, pairs // group, steps),
        in_specs=[tile] * 4 + [mat, mat, row, row, pl.BlockSpec((LANES, LANES), lambda bi, h, c: (0, 0))],
        out_specs=tile,
        out_shape=jax.ShapeDtypeStruct((T, W), jnp.float32),
        scratch_shapes=[pltpu.VMEM((group, LANES, LANES), jnp.float32)],
        compiler_params=_params(("parallel", "parallel", "arbitrary")),
        name="rwkv_state",
    )(rhat, y0, g, e, gmat, dmat, ln_g, ln_b, bd)


def _merge_kernel(oa_ref, ob_ref, acc_ref, m_ref, l_ref, ga_ref, gb_ref, gc_ref,
                  pa_ref, pb_ref, pc_ref, o_ref):
    m = m_ref[...]
    mx = jnp.max(m, axis=0)
    wgt = jnp.exp(m - mx[None])
    o_c = jnp.sum(wgt * acc_ref[...], axis=0) / jnp.sum(wgt * l_ref[...], axis=0)
    bf = jnp.bfloat16
    merged = (_sigmoid(ga_ref[...]) * _dot(oa_ref[...].astype(bf), pa_ref[...])
              + _sigmoid(gb_ref[...]) * _dot(ob_ref[...].astype(bf), pb_ref[...])
              + _sigmoid(gc_ref[...]) * _dot(o_c.astype(bf), pc_ref[...]))
    o_ref[...] = merged.astype(o_ref.dtype)


def merge_branches(o_a, o_b, acc, m, l, proj, gate_off, p_a, p_b, p_c, tm, tn):
    T = o_a.shape[0]
    D = p_a.shape[1]
    nj = D // tn
    Wc = acc.shape[-1]
    grp = pl.BlockSpec((3, tm, Wc), lambda i, j: (0, i, 0))
    gate = lambda n: pl.BlockSpec((tm, tn), lambda i, j: (i, gate_off + n * nj + j))
    wspec = lambda a: pl.BlockSpec((a.shape[0], tn), lambda i, j: (0, j))
    return pl.pallas_call(
        _merge_kernel,
        grid=(T // tm, nj),
        in_specs=[pl.BlockSpec((tm, o_a.shape[1]), lambda i, j: (i, 0)),
                  pl.BlockSpec((tm, o_b.shape[1]), lambda i, j: (i, 0)),
                  grp, grp, grp, gate(0), gate(1), gate(2), wspec(p_a), wspec(p_b), wspec(p_c)],
        out_specs=pl.BlockSpec((tm, tn), lambda i, j: (i, j)),
        out_shape=jax.ShapeDtypeStruct((T, D), jnp.bfloat16),
        compiler_params=_params(("parallel", "arbitrary")),
        name="merge",
    )(o_a, o_b, acc, m, l, proj, proj, proj, p_a, p_b, p_c)


def _out_ln_kernel(mg_ref, w_ref, x_ref, g1_ref, lg_ref, lb_ref, o_ref, *, alpha):
    mix = _dot(mg_ref[...], w_ref[...])
    y = alpha * x_ref[...] + (1.0 + g1_ref[...]) * mix
    o_ref[...] = _layer_norm(y, lg_ref[...], lb_ref[...])


def out_ln(merged, w_o, x2, g1, ln_g, ln_b, seq, tm, alpha):
    T, D = x2.shape
    per = seq // tm
    tile = pl.BlockSpec((tm, D), lambda i: (i, 0))
    row = pl.BlockSpec((1, D), lambda i: (0, 0))
    return pl.pallas_call(
        functools.partial(_out_ln_kernel, alpha=alpha),
        grid=(T // tm,),
        in_specs=[tile, pl.BlockSpec((D, D), lambda i: (0, 0)), tile,
                  pl.BlockSpec((None, 1, D), lambda i: (i // per, 0, 0)), row, row],
        out_specs=tile,
        out_shape=jax.ShapeDtypeStruct((T, D), jnp.float32),
        compiler_params=_params(("parallel",)),
        name="out_ln",
    )(merged, w_o, x2, g1, ln_g, ln_b)


def _router_kernel(x_ref, sc_ref, sh_ref, w_ref, b_ref, id_ref, gw_ref, h_ref):
    h = x_ref[...] * (1.0 + sc_ref[...]) + sh_ref[...]
    h_ref[...] = h.astype(h_ref.dtype)
    lg = _dot(h, w_ref[...], precision=HI) + b_ref[...]
    G, EPG = MOE_GROUPS, MOE_EXPERTS_PER_GROUP
    lane = lax.broadcasted_iota(jnp.int32, (1, LANES), 1).astype(jnp.float32)
    first = lambda hit: jnp.min(jnp.where(hit, lane, float(LANES)), axis=-1, keepdims=True)
    is_grp = lane < G
    gmax = jnp.max(jnp.where(is_grp, lg, -jnp.inf), axis=-1, keepdims=True)
    ge = jnp.where(is_grp, jnp.exp(jnp.where(is_grp, lg, gmax) - gmax), 0.0)
    prob = ge / jnp.sum(ge, axis=-1, keepdims=True)
    grp_p = jnp.max(prob, axis=-1, keepdims=True)
    grp_i = first(is_grp & (prob == grp_p))
    lo = G + grp_i * EPG
    el = jnp.where((lane >= lo) & (lane < lo + EPG), lg, -jnp.inf)
    l1 = jnp.max(el, axis=-1, keepdims=True)
    i1 = first(el == l1)
    el = jnp.where(lane == i1, -jnp.inf, el)
    l2 = jnp.max(el, axis=-1, keepdims=True)
    i2 = first(el == l2)
    t = jnp.exp(l2 - l1)
    w1 = grp_p / (1.0 + t)
    id_ref[...] = jnp.where(lane == 0, i1 - G, jnp.where(lane == 1, i2 - G, 0.0)).astype(jnp.int32)
    gw_ref[...] = jnp.where(lane == 0, w1, jnp.where(lane == 1, w1 * t, 0.0))


def router(x2, sc, sh, w_r, b_r, seq, tm):
    T, D = x2.shape
    per = seq // tm
    mod = pl.BlockSpec((None, 1, D), lambda i: (i // per, 0, 0))
    narrow = pl.BlockSpec((tm, LANES), lambda i: (i, 0))
    return pl.pallas_call(
        _router_kernel,
        grid=(T // tm,),
        in_specs=[pl.BlockSpec((tm, D), lambda i: (i, 0)), mod, mod,
                  pl.BlockSpec((D, LANES), lambda i: (0, 0)), pl.BlockSpec((1, LANES), lambda i: (0, 0))],
        out_specs=[narrow, narrow, pl.BlockSpec((tm, D), lambda i: (i, 0))],
        out_shape=[jax.ShapeDtypeStruct((T, LANES), jnp.int32), jax.ShapeDtypeStruct((T, LANES), jnp.float32),
                   jax.ShapeDtypeStruct((T, D), jnp.float32)],
        compiler_params=_params(("parallel",)),
        name="router",
    )(x2, sc, sh, w_r, b_r)


def _experts_kernel(be_ref, nu_ref, x_ref, sw_ref, wg_ref, wu_ref, wd_ref, o_ref, wg_s, wu_s, wd_s):
    i = pl.program_id(0)
    used = i < nu_ref[0]
    new_expert = (i == 0) | (be_ref[i] != be_ref[jnp.maximum(i - 1, 0)])

    @pl.when(used & new_expert)
    def _():
        wg_s[...] = wg_ref[...].astype(wg_s.dtype)
        wu_s[...] = wu_ref[...].astype(wu_s.dtype)
        wd_s[...] = wd_ref[...].astype(wd_s.dtype)

    @pl.when(used)
    def _():
        x = x_ref[...].astype(jnp.bfloat16)
        gate = _dot(x, wg_s[...])
        hid = gate * _sigmoid(gate) * _dot(x, wu_s[...])
        o_ref[...] = _dot(hid.astype(jnp.bfloat16), wd_s[...]) * sw_ref[...]

    @pl.when(jnp.logical_not(used))
    def _():
        o_ref[...] = jnp.zeros_like(o_ref)


def experts(xs, slot_w, block_e, n_used, w_gate, w_up, w_down, layer):
    R, D = xs.shape
    F = w_gate.shape[-1]
    n_blocks = R // MOE_BLOCK
    grid_spec = pltpu.PrefetchScalarGridSpec(
        num_scalar_prefetch=2,
        grid=(n_blocks,),
        in_specs=[pl.BlockSpec((MOE_BLOCK, D), lambda i, be, nu: (i, 0)),
                  pl.BlockSpec((MOE_BLOCK, 1), lambda i, be, nu: (i, 0)),
                  pl.BlockSpec((None, None, D, F), lambda i, be, nu: (layer, be[i], 0, 0)),
                  pl.BlockSpec((None, None, D, F), lambda i, be, nu: (layer, be[i], 0, 0)),
                  pl.BlockSpec((None, None, F, D), lambda i, be, nu: (layer, be[i], 0, 0))],
        out_specs=pl.BlockSpec((MOE_BLOCK, D), lambda i, be, nu: (i, 0)),
        scratch_shapes=[pltpu.VMEM((D, F), jnp.bfloat16), pltpu.VMEM((D, F), jnp.bfloat16),
                        pltpu.VMEM((F, D), jnp.bfloat16)],
    )
    return pl.pallas_call(
        _experts_kernel,
        grid_spec=grid_spec,
        out_shape=jax.ShapeDtypeStruct((R, D), jnp.float32),
        compiler_params=_params(("arbitrary",)),
        name="experts",
    )(block_e, n_used, xs, slot_w, w_gate, w_up, w_down)


def _combine_ln_kernel(x_ref, f0_ref, f1_ref, g2_ref, lg_ref, lb_ref, o_ref, *, alpha):
    y = alpha * x_ref[...] + (1.0 + g2_ref[...]) * (f0_ref[...] + f1_ref[...])
    o_ref[...] = _layer_norm(y, lg_ref[...], lb_ref[...])


def combine_ln(x2, f0, f1, g2, ln_g, ln_b, seq, tm, alpha):
    T, D = x2.shape
    per = seq // tm
    tile = pl.BlockSpec((tm, D), lambda i: (i, 0))
    row = pl.BlockSpec((1, D), lambda i: (0, 0))
    return pl.pallas_call(
        functools.partial(_combine_ln_kernel, alpha=alpha),
        grid=(T // tm,),
        in_specs=[tile, tile, tile, pl.BlockSpec((None, 1, D), lambda i: (i // per, 0, 0)), row, row],
        out_specs=tile,
        out_shape=jax.ShapeDtypeStruct((T, D), jnp.float32),
        compiler_params=_params(("parallel",)),
        name="combine_ln",
    )(x2, f0, f1, g2, ln_g, ln_b)


def _to_residue(t, dilation):
    B, S, W = t.shape
    return t.reshape(B, S // dilation, dilation, W).transpose(0, 2, 1, 3).reshape(B, S, W)


def _from_residue(t, dilation):
    B, S, W = t.shape
    return t.reshape(B, dilation, S // dilation, W).transpose(0, 2, 1, 3).reshape(B, S, W)


def _route(expert_id, gate_w):
    T = expert_id.shape[0]
    E = MOE_GROUPS * MOE_EXPERTS_PER_GROUP
    A = T * MOE_TOPK
    n_blocks = (A + E * (MOE_BLOCK - 1) + MOE_BLOCK - 1) // MOE_BLOCK
    flat_e = expert_id.reshape(A)
    flat_w = gate_w.reshape(A)
    e_s, order = lax.sort_key_val(flat_e, jnp.arange(A, dtype=jnp.int32))
    bounds = jnp.searchsorted(e_s, jnp.arange(E + 1, dtype=jnp.int32), side='left').astype(jnp.int32)
    start = bounds[:E]
    counts = bounds[1:] - start
    padded = (counts + MOE_BLOCK - 1) // MOE_BLOCK * MOE_BLOCK
    pad_end = jnp.cumsum(padded)
    pad_start = pad_end - padded
    block_e = jnp.minimum(jnp.sum(pad_end[None, :] <= (jnp.arange(n_blocks) * MOE_BLOCK)[:, None], axis=1),
                          E - 1).astype(jnp.int32)
    blk_rank = jnp.arange(n_blocks, dtype=jnp.int32) * MOE_BLOCK - pad_start[block_e]
    rank = blk_rank[:, None] + jnp.arange(MOE_BLOCK, dtype=jnp.int32)[None, :]
    valid = (rank < counts[block_e][:, None]).reshape(-1)
    src = order[jnp.clip(start[block_e][:, None] + rank, 0, A - 1).reshape(-1)]
    slot_tok = jnp.where(valid, src // MOE_TOPK, 0).astype(jnp.int32)
    slot_w = jnp.where(valid, flat_w[src], 0.0)
    pos = jnp.arange(A, dtype=jnp.int32)[None, :]
    in_e = (pos >= start[:, None]) & (pos < bounds[1:, None])
    dest = pos[0] + jnp.sum(jnp.where(in_e, (pad_start - start)[:, None], 0), axis=0).astype(jnp.int32)
    _, slot_of = lax.sort_key_val(order, dest)
    n_used = (pad_end[-1] // MOE_BLOCK).astype(jnp.int32).reshape(1)
    return slot_tok, slot_w, slot_of.reshape(T, MOE_TOPK), block_e, n_used


def kernel(x, c, rel_bias, w_in, p_a, p_b, p_c, w_o, rwkv_mu, rwkv_w0, rwkv_w_up, rwkv_a0, rwkv_a_up,
           rwkv_g_up, rwkv_k_k, rwkv_k_a, rwkv_r_k, rwkv_ln_g, rwkv_ln_b, rwkv_v0, rwkv_mv_down,
           rwkv_mv_up, w_ada, b_ada, ln1_g, ln1_b, ln2_g, ln2_b, router_grp_w, router_grp_b,
           router_exp_w, router_exp_b, exp_w_gate, exp_w_up, exp_w_down):
    B, S, D = x.shape
    depth = w_in.shape[0]
    T = B * S
    bf = jnp.bfloat16
    W = RWKV_HEADS * HEAD_DIM
    wa_w = MOBA_HEADS * HEAD_DIM
    wc_w = len(DIL_GROUPS) * DIL_HEADS_PER_GROUP * HEAD_DIM
    rw_cols = rwkv_mu.shape[-1]
    alpha = (2 * depth) ** 0.25
    off_c = 3 * wa_w
    off_b = off_c + 3 * wc_w
    off_g = off_b + rw_cols
    new_a = rw_cols
    new_c = new_a + 3 * wa_w
    new_g = new_c + 3 * wc_w
    w_in_p = jnp.concatenate([w_in[:, :, off_b:off_g], w_in[:, :, :off_b], w_in[:, :, off_g:]], axis=-1).astype(bf)
    p_a_b, p_b_b, p_c_b, w_o_b = p_a.astype(bf), p_b.astype(bf), p_c.astype(bf), w_o.astype(bf)

    bias_h = rel_bias.T.astype(jnp.float32)
    moba_tab = moba_bias_table(bias_h[:MOBA_HEADS], S // MOBA_BLOCK)
    dil_tab = dil_bias_table(bias_h[MOBA_HEADS:])

    c8 = jnp.zeros((8, D), jnp.float32).at[:B].set(c)
    mod = ada_mod(c8, w_ada, b_ada)[:, :B]

    hd_idx = jnp.arange(LANES) // HEAD_DIM
    bd = (hd_idx[:, None] == hd_idx[None, :]).astype(jnp.float32)
    zeros_w = jnp.zeros((RWKV_DECAY_LORA, W), jnp.float32)
    pad_lora = LANES - RWKV_MV_LORA

    x2 = x.reshape(T, D)
    v_first = None
    for l in range(depth):
        sh1, sc1, g1, sh2, sc2, g2 = [m.reshape(B, 1, D) for m in jnp.split(mod[l], 6, axis=-1)]
        proj = in_proj(x2, sc1, sh1, w_in_p, l, S, 1024, 1024)
        proj3 = proj.reshape(B, S, -1)
        o_a = moba_attention(proj3, moba_tab, new_a // LANES, (new_a + wa_w) // LANES,
                             (new_a + 2 * wa_w) // LANES).reshape(T, wa_w)
        gw = DIL_HEADS_PER_GROUP * HEAD_DIM
        qs, ks, vs = [], [], []
        for g, (_, dil) in enumerate(DIL_GROUPS):
            qs.append(_to_residue(proj3[:, :, new_c + g * gw:new_c + (g + 1) * gw], dil))
            ks.append(_to_residue(proj3[:, :, new_c + wc_w + g * gw:new_c + wc_w + (g + 1) * gw], dil))
            vs.append(_to_residue(proj3[:, :, new_c + 2 * wc_w + g * gw:new_c + 2 * wc_w + (g + 1) * gw], dil))
        acc, mm, ll = dilated_attention(jnp.stack(qs, 1), jnp.stack(ks, 1), jnp.stack(vs, 1), dil_tab)
        unperm = lambda t: jnp.stack([_from_residue(t[:, g], dil).reshape(T, gw)
                                      for g, (_, dil) in enumerate(DIL_GROUPS)])
        acc, mm, ll = unperm(acc), unperm(mm), unperm(ll)
        wa_up = jnp.concatenate([jnp.concatenate([rwkv_w_up[l], zeros_w], axis=1),
                                 jnp.concatenate([zeros_w, rwkv_a_up[l]], axis=1)], axis=0)
        res = None
        if l > 0:
            res = (v_first, rwkv_v0[l - 1][None],
                   jnp.pad(rwkv_mv_down[l - 1], ((0, 0), (0, pad_lora))),
                   jnp.pad(rwkv_mv_up[l - 1], ((0, pad_lora), (0, 0))))
        r_, lw_, k_, v_, kn_, b_, g_ = rwkv_prep(proj, S, 256, rwkv_mu[l][None], rwkv_w0[l][None],
                                                 rwkv_a0[l][None], wa_up, rwkv_g_up[l], rwkv_k_k[l][None],
                                                 rwkv_k_a[l][None], bd, res)
        if l == 0:
            v_first = v_
        rhat, y0, e_, gmat, dmat = rwkv_chunk(r_, lw_, k_, v_, kn_, b_, g_, rwkv_r_k[l][None], bd, B)
        o_b = rwkv_state(rhat, y0, g_, e_, gmat, dmat, rwkv_ln_g[l][None], rwkv_ln_b[l][None], bd)
        merged = merge_branches(o_a, o_b, acc, mm, ll, proj, new_g // 1024, p_a_b[l], p_b_b[l], p_c_b[l],
                                256, 1024)
        x2 = out_ln(merged, w_o_b[l], x2, g1, ln1_g[l][None], ln1_b[l][None], S, 256, alpha)
        w_r = jnp.zeros((D, LANES), jnp.float32)
        w_r = w_r.at[:, :MOE_GROUPS].set(router_grp_w[l]).at[:, MOE_GROUPS:MOE_GROUPS + router_exp_w.shape[-1]].set(
            router_exp_w[l])
        b_r = jnp.zeros((1, LANES), jnp.float32)
        b_r = b_r.at[0, :MOE_GROUPS].set(router_grp_b[l]).at[0, MOE_GROUPS:MOE_GROUPS + router_exp_b.shape[-1]].set(
            router_exp_b[l])
        ids, gws, h2 = router(x2, sc2, sh2, w_r, b_r, S, 512)
        slot_tok, slot_w, slot_of, block_e, n_used = _route(ids[:, :MOE_TOPK], gws[:, :MOE_TOPK])
        y = experts(h2[slot_tok], slot_w[:, None], block_e, n_used, exp_w_gate, exp_w_up, exp_w_down, l)
        x2 = combine_ln(x2, y[slot_of[:, 0]], y[slot_of[:, 1]], g2, ln2_g[l][None], ln2_b[l][None], S, 512, alpha)
    return x2.reshape(B, S, D)
```

```python
import functools
import math

import jax
import jax.numpy as jnp
import numpy as np
from jax import lax
from jax.experimental import pallas as pl
from jax.experimental.pallas import tpu as pltpu

HEAD_DIM = 64
LANES = 128
MOBA_HEADS = 12
MOBA_BLOCK = 256
MOBA_TOPK = 3
RWKV_HEADS = 12
RWKV_DECAY_LORA = 64
RWKV_A_LORA = 64
RWKV_MV_LORA = 32
RWKV_GATE_LORA = 128
RWKV_GN_EPS = 64e-5
RWKV_CHUNK = 64
DIL_GROUPS = ((128, 1), (512, 4), (2048, 16))
DIL_HEADS_PER_GROUP = 4
DIL_SPAN = 128
REL_BUCKETS = 32
REL_MAX_DISTANCE = 2048
MOE_GROUPS = 8
MOE_EXPERTS_PER_GROUP = 8
MOE_TOPK = 2
MOE_BLOCK = 256
LN_EPS = 1e-5
NEG = -1e30
VMEM_LIMIT = 56 * 1024 * 1024
HI = lax.Precision.HIGHEST


def _params(sem):
    return pltpu.CompilerParams(dimension_semantics=sem, vmem_limit_bytes=VMEM_LIMIT)


def _sigmoid(x):
    return 1.0 / (1.0 + jnp.exp(-x))


def _dot(a, b, precision=None):
    return jnp.dot(a, b, preferred_element_type=jnp.float32, precision=precision)


def _dot_nt(a, b, precision=None):
    return lax.dot_general(a, b, (((1,), (1,)), ((), ())), preferred_element_type=jnp.float32,
                           precision=precision)


def _dot_tn(a, b, precision=None):
    return lax.dot_general(a, b, (((0,), (0,)), ((), ())), preferred_element_type=jnp.float32,
                           precision=precision)


def _split2(x):
    hi = x.astype(jnp.bfloat16)
    return hi, (x - hi.astype(jnp.float32)).astype(jnp.bfloat16)


def _dot3(a, b, dims=((1,), (0,))):
    (ca,), (cb,) = dims
    ah, al = _split2(a)
    bh, bl = _split2(b)
    return lax.dot_general(jnp.concatenate([ah, ah, al], axis=ca), jnp.concatenate([bh, bl, bh], axis=cb),
                           (dims, ((), ())), preferred_element_type=jnp.float32)


def _dot_sel(x, sel):
    xh, xl = _split2(x)
    sb = sel.astype(jnp.bfloat16)
    return _dot(jnp.concatenate([xh, xl], axis=1), jnp.concatenate([sb, sb], axis=0))


_NT = ((1,), (1,))
_TN = ((0,), (0,))


def _layer_norm(y, g, b):
    mu = jnp.mean(y, axis=-1, keepdims=True)
    d = y - mu
    var = jnp.mean(d * d, axis=-1, keepdims=True)
    return d * lax.rsqrt(var + LN_EPS) * g + b


def _ada_kernel(c_ref, w_ref, b_ref, o_ref):
    c = c_ref[...]
    cond = c * _sigmoid(c)
    o_ref[...] = _dot(cond, w_ref[...]) + b_ref[...]


def ada_mod(c8, w_ada, b_ada):
    L, D, N = w_ada.shape
    tn = 1024
    return pl.pallas_call(
        _ada_kernel,
        grid=(L, N // tn),
        in_specs=[pl.BlockSpec((8, D), lambda l, j: (0, 0)),
                  pl.BlockSpec((None, D, tn), lambda l, j: (l, 0, j)),
                  pl.BlockSpec((None, 1, tn), lambda l, j: (l, 0, j))],
        out_specs=pl.BlockSpec((None, 8, tn), lambda l, j: (l, 0, j)),
        out_shape=jax.ShapeDtypeStruct((L, 8, N), jnp.float32),
        compiler_params=_params(("parallel", "parallel")),
        name="ada_mod",
    )(c8, w_ada, b_ada.reshape(L, 1, N))


def _in_proj_kernel(x_ref, sc_ref, sh_ref, w_ref, o_ref, h_ref):
    @pl.when(pl.program_id(1) == 0)
    def _():
        h_ref[...] = (x_ref[...] * (1.0 + sc_ref[...]) + sh_ref[...]).astype(h_ref.dtype)

    o_ref[...] = _dot(h_ref[...], w_ref[...])


def in_proj(x2, sc, sh, w, layer, seq, tm, tn):
    T, D = x2.shape
    N = w.shape[-1]
    per = seq // tm
    return pl.pallas_call(
        _in_proj_kernel,
        grid=(T // tm, N // tn),
        in_specs=[pl.BlockSpec((tm, D), lambda i, j: (i, 0)),
                  pl.BlockSpec((None, 1, D), lambda i, j: (i // per, 0, 0)),
                  pl.BlockSpec((None, 1, D), lambda i, j: (i // per, 0, 0)),
                  pl.BlockSpec((None, D, tn), lambda i, j: (layer, 0, j))],
        out_specs=pl.BlockSpec((tm, tn), lambda i, j: (i, j)),
        out_shape=jax.ShapeDtypeStruct((T, N), jnp.float32),
        scratch_shapes=[pltpu.VMEM((tm, D), jnp.bfloat16)],
        compiler_params=_params(("parallel", "arbitrary")),
        name="in_proj",
    )(x2, sc, sh, w)


def _t5_bucket(dist):
    n = jnp.maximum(dist, 0)
    max_exact = REL_BUCKETS // 2
    nf = jnp.maximum(n, 1).astype(jnp.float32)
    large = max_exact + (jnp.log(nf / max_exact) / math.log(REL_MAX_DISTANCE / max_exact)
                         * (REL_BUCKETS - max_exact)).astype(jnp.int32)
    large = jnp.minimum(large, REL_BUCKETS - 1)
    return jnp.where(n < max_exact, n, large)


def _moba_n_delta(nb):
    last_start = 1
    d = np.arange(1, nb * MOBA_BLOCK + 1)
    large = 16 + (np.log(d / 16.0) / math.log(REL_MAX_DISTANCE / 16.0) * 16).astype(np.int64)
    bucket = np.where(d < 16, d, np.minimum(large, REL_BUCKETS - 1))
    last_start = int(d[bucket < REL_BUCKETS - 1].max()) + 1 if (bucket < REL_BUCKETS - 1).any() else 1
    delta = 1
    while delta * MOBA_BLOCK - (MOBA_BLOCK - 1) < last_start + 2:
        delta += 1
    return min(delta + 1, nb)


def moba_bias_table(bias_a, nb):
    nd = _moba_n_delta(nb)
    key = jnp.arange(MOBA_BLOCK)[:, None]
    qry = jnp.arange(MOBA_BLOCK)[None, :]
    dist = jnp.arange(nd)[:, None, None] * MOBA_BLOCK + (qry - key)[None]
    bucket = _t5_bucket(dist)[None]
    tab = jnp.zeros((bias_a.shape[0],) + dist.shape, jnp.float32)
    for b in range(REL_BUCKETS):
        tab = jnp.where(bucket == b, bias_a[:, b][:, None, None, None], tab)
    return jnp.where((dist >= 0)[None], tab, NEG)


def dil_bias_table(bias_c):
    span = DIL_SPAN
    rel = span + jnp.arange(span)[:, None] - jnp.arange(2 * span)[None, :]
    valid = (rel >= 0) & (rel <= span)
    tabs = []
    for g, (_, dilation) in enumerate(DIL_GROUPS):
        bh = bias_c[g * DIL_HEADS_PER_GROUP:(g + 1) * DIL_HEADS_PER_GROUP]
        tabs.append(jnp.where(valid[None], bh[:, _t5_bucket(rel * dilation)], NEG))
    return jnp.stack(tabs)


def _moba_kernel(q_ref, k_ref, v_ref, bias_ref, o_ref,
                 kaug_ref, vt_ref, km_ref, acc_ref, *, nb, n_delta):
    qb = pl.program_id(2)
    bs = MOBA_BLOCK
    S = nb * bs
    half = bs // 2
    lane = lax.broadcasted_iota(jnp.int32, (1, LANES), 1)
    f32 = jnp.float32

    @pl.when(qb == 0)
    def _():
        k = k_ref[...]
        rowblk = lax.broadcasted_iota(jnp.int32, (S, LANES), 0) // bs
        lanes = lax.broadcasted_iota(jnp.int32, (S, LANES), 1)
        km_ref[...] = jnp.mean(k.reshape(nb, bs, LANES), axis=1)
        kaug_ref[0] = jnp.where(lanes < HEAD_DIM, k, (lanes - HEAD_DIM == rowblk).astype(f32)).astype(kaug_ref.dtype)
        kaug_ref[1] = jnp.where(lanes >= HEAD_DIM, k, (lanes == rowblk).astype(f32)).astype(kaug_ref.dtype)
        for i in range(nb):
            vt_ref[i] = v_ref[i * bs:(i + 1) * bs, :].T.astype(vt_ref.dtype)

    q_t = q_ref[...].T
    scale = HEAD_DIM ** -0.5
    blk = lax.broadcasted_iota(jnp.int32, (nb, 1), 0)
    past = blk < qb
    rhs = []
    for t in range(2):
        hm = (lane >= t * HEAD_DIM) & (lane < (t + 1) * HEAD_DIM)
        gate = _dot(jnp.where(hm, km_ref[...], 0.0), q_t, precision=HI)
        gate = jnp.where(past, gate, -jnp.inf)
        cnt = jnp.zeros((nb, bs), jnp.int32)
        for m in range(nb):
            gm = gate[m:m + 1, :]
            ahead = (gm > gate) | ((gm == gate) & (m < blk))
            cnt = cnt + ahead.astype(jnp.int32)
        chosen = (past & (cnt < MOBA_TOPK)) | (blk == qb)
        pen = jnp.where(chosen, 0.0, NEG)
        qh = q_t[t * HEAD_DIM:(t + 1) * HEAD_DIM, :] * scale
        if t == 0:
            parts = [qh, pen, jnp.zeros((LANES - HEAD_DIM - nb, bs), f32)]
        else:
            parts = [pen, jnp.zeros((HEAD_DIM - nb, bs), f32), qh]
        rhs.append(jnp.concatenate(parts, axis=0).astype(jnp.bfloat16))
    acc_ref[...] = jnp.zeros((LANES, bs), f32)

    chains = [(t, hq) for t in range(2) for hq in range(2)]
    cols = [slice(hq * half, (hq + 1) * half) for _, hq in chains]

    def step(blocks, carry):
        rows = [pl.multiple_of(n * bs, bs) for n in blocks]
        delta = [jnp.minimum(qb - n, n_delta - 1) for n in blocks]
        s = [[_dot(kaug_ref[t, pl.ds(rows[j], bs), :], rhs[t][:, cols[c]]) + bias_ref[t, delta[j], :, cols[c]]
              for j in range(len(blocks))] for c, (t, _) in enumerate(chains)]
        m_new = [functools.reduce(jnp.maximum, [carry[c][0]] + [jnp.max(x, axis=0, keepdims=True) for x in s[c]])
                 for c in range(4)]
        alpha = [jnp.exp(carry[c][0] - m_new[c]) for c in range(4)]
        p = [[jnp.exp(x - m_new[c]) for x in s[c]] for c in range(4)]
        l_new = [alpha[c] * carry[c][1] + sum(jnp.sum(x, axis=0, keepdims=True) for x in p[c]) for c in range(4)]
        pv = [sum(_dot(vt_ref[n, t * HEAD_DIM:(t + 1) * HEAD_DIM, :], p[c][j].astype(jnp.bfloat16))
                  for j, n in enumerate(blocks)) for c, (t, _) in enumerate(chains)]
        pieces = [alpha[c] * acc_ref[t * HEAD_DIM:(t + 1) * HEAD_DIM, cols[c]] + pv[c]
                  for c, (t, _) in enumerate(chains)]
        acc_ref[...] = jnp.concatenate([jnp.concatenate(pieces[:2], axis=1),
                                        jnp.concatenate(pieces[2:], axis=1)], axis=0)
        return tuple((m_new[c], l_new[c]) for c in range(4))

    init = tuple((jnp.full((1, half), NEG, f32), jnp.zeros((1, half), f32)) for _ in range(4))
    n_blocks = qb + 1
    fin = lax.fori_loop(0, n_blocks // 2, lambda i, c: step([2 * i, 2 * i + 1], c), init)
    fin = lax.cond(n_blocks % 2 == 1, lambda c: step([qb], c), lambda c: c, fin)
    l_all = jnp.concatenate([jnp.broadcast_to(jnp.concatenate([fin[2 * t][1], fin[2 * t + 1][1]], axis=1),
                                              (HEAD_DIM, bs)) for t in range(2)], axis=0)
    o_ref[...] = (acc_ref[...] / l_all).T


def moba_attention(proj3, bias_tab, q_off, k_off, v_off):
    B, S, _ = proj3.shape
    nb = S // MOBA_BLOCK
    n_delta = bias_tab.shape[1]
    pairs = MOBA_HEADS // 2
    kern = functools.partial(_moba_kernel, nb=nb, n_delta=n_delta)
    return pl.pallas_call(
        kern,
        grid=(pairs, B, nb),
        in_specs=[pl.BlockSpec((None, MOBA_BLOCK, LANES), lambda h, b, i: (b, i, q_off + h)),
                  pl.BlockSpec((None, S, LANES), lambda h, b, i: (b, 0, k_off + h)),
                  pl.BlockSpec((None, S, LANES), lambda h, b, i: (b, 0, v_off + h)),
                  pl.BlockSpec((2, n_delta, MOBA_BLOCK, MOBA_BLOCK), lambda h, b, i: (h, 0, 0, 0))],
        out_specs=pl.BlockSpec((None, MOBA_BLOCK, LANES), lambda h, b, i: (b, i, h)),
        out_shape=jax.ShapeDtypeStruct((B, S, pairs * LANES), jnp.float32),
        scratch_shapes=[pltpu.VMEM((2, S, LANES), jnp.bfloat16),
                        pltpu.VMEM((nb, LANES, MOBA_BLOCK), jnp.bfloat16),
                        pltpu.VMEM((nb, LANES), jnp.float32),
                        pltpu.VMEM((LANES, MOBA_BLOCK), jnp.float32)],
        compiler_params=_params(("parallel", "parallel", "arbitrary")),
        name="moba",
    )(proj3, proj3, proj3, bias_tab)


def _dil_kernel(q_ref, kp_ref, kc_ref, vp_ref, vc_ref, bias_ref, acc_ref, m_ref, l_ref, *, blocks_per_seq0):
    g = pl.program_id(1)
    blk = pl.program_id(2)
    span = DIL_SPAN
    width = DIL_HEADS_PER_GROUP * HEAD_DIM
    bps = jnp.right_shift(blocks_per_seq0, 2 * g)
    first = (blk & (bps - 1)) == 0
    lane = lax.broadcasted_iota(jnp.int32, (1, width), 1)
    col = lax.broadcasted_iota(jnp.int32, (1, 2 * span), 1)
    no_prev = first & (col < span)
    q = q_ref[...] * (HEAD_DIM ** -0.5)
    kcat = jnp.concatenate([kp_ref[...], kc_ref[...]], axis=0).astype(jnp.bfloat16)
    vcat = jnp.concatenate([vp_ref[...], vc_ref[...]], axis=0).astype(jnp.bfloat16)
    heads = range(DIL_HEADS_PER_GROUP)
    hm = [(lane >= t * HEAD_DIM) & (lane < (t + 1) * HEAD_DIM) for t in heads]
    s = [jnp.where(no_prev, NEG, _dot_nt(jnp.where(hm[t], q, 0.0).astype(jnp.bfloat16), kcat) + bias_ref[t])
         for t in heads]
    m = [jnp.max(s[t], axis=-1, keepdims=True) for t in heads]
    e = [jnp.exp(s[t] - m[t]) for t in heads]
    l = [jnp.sum(e[t], axis=-1, keepdims=True) for t in heads]
    pv = [_dot(e[t].astype(jnp.bfloat16), vcat) for t in heads]
    acc = jnp.zeros((span, width), jnp.float32)
    mb = jnp.zeros((span, width), jnp.float32)
    lb = jnp.zeros((span, width), jnp.float32)
    for t in heads:
        acc = jnp.where(hm[t], pv[t], acc)
        mb = jnp.where(hm[t], m[t], mb)
        lb = jnp.where(hm[t], l[t], lb)
    acc_ref[...] = acc
    m_ref[...] = mb
    l_ref[...] = lb


def dilated_attention(qp, kp, vp, bias_tab):
    B, G, S, W = qp.shape
    span = DIL_SPAN
    nblk = S // span
    cur = pl.BlockSpec((None, None, span, W), lambda b, g, i: (b, g, i, 0))
    prev = pl.BlockSpec((None, None, span, W), lambda b, g, i: (b, g, jnp.maximum(i - 1, 0), 0))
    out = jax.ShapeDtypeStruct((B, G, S, W), jnp.float32)
    kern = functools.partial(_dil_kernel, blocks_per_seq0=nblk)
    return pl.pallas_call(
        kern,
        grid=(B, G, nblk),
        in_specs=[cur, prev, cur, prev, cur,
                  pl.BlockSpec((None, DIL_HEADS_PER_GROUP, span, 2 * span), lambda b, g, i: (g, 0, 0, 0))],
        out_specs=[cur, cur, cur],
        out_shape=[out, out, out],
        compiler_params=_params(("parallel", "parallel", "parallel")),
        name="dilated",
    )(qp, kp, kp, vp, vp, bias_tab)


def _rwkv_prep_kernel(*refs, width, has_res, rows_per_seq):
    if has_res:
        (z_ref, zl_ref, mu_ref, w0_ref, a0_ref, wa_ref, gup_ref, kk_ref, ka_ref, bd_ref,
         vf_ref, v0_ref, mvd_ref, mvu_ref,
         r_o, lw_o, k_o, v_o, kn_o, b_o, g_o) = refs
    else:
        (z_ref, zl_ref, mu_ref, w0_ref, a0_ref, wa_ref, gup_ref, kk_ref, ka_ref, bd_ref,
         r_o, lw_o, k_o, v_o, kn_o, b_o, g_o) = refs
    i = pl.program_id(0)
    W = width
    z = z_ref[...]
    tm = z.shape[0]
    row = lax.broadcasted_iota(jnp.int32, (tm, 1), 0)
    seq_start = (i % rows_per_seq) == 0
    last = jnp.where(seq_start, 0.0, zl_ref[7:8, :])
    zp = jnp.where(row == 0, last, pltpu.roll(z, 1, 0))
    zf = z + mu_ref[...] * (zp - z)
    lora = zf[:, 3 * W:3 * W + LANES]
    lane = lax.broadcasted_iota(jnp.int32, (1, LANES), 1)
    lora = jnp.where(lane < RWKV_DECAY_LORA, jnp.tanh(lora), lora)
    wa = _dot(lora, wa_ref[...], precision=HI)
    g = _dot(_sigmoid(zf[:, 3 * W + LANES:3 * W + 2 * LANES]), gup_ref[...], precision=HI)
    g_o[...] = g
    v_all = zf[:, 2 * W:3 * W]
    if has_res:
        mix = _dot(_dot(v_all, mvd_ref[...], precision=HI), mvu_ref[...], precision=HI)
    for c in range(W // LANES):
        sl = slice(c * LANES, (c + 1) * LANES)
        x = w0_ref[:, sl] + wa[:, sl]
        sp = jnp.maximum(-x, 0.0) + jnp.log(1.0 + jnp.exp(-jnp.abs(x)))
        lw_o[:, sl] = -jnp.exp(-sp - 0.5)
        a = _sigmoid(a0_ref[:, sl] + wa[:, W + c * LANES:W + (c + 1) * LANES])
        r_o[:, sl] = zf[:, sl]
        k = zf[:, W + c * LANES:W + (c + 1) * LANES]
        v = v_all[:, sl]
        if has_res:
            v = v + (vf_ref[:, sl] - v) * _sigmoid(v0_ref[:, sl] + mix[:, sl])
        v_o[:, sl] = v
        kk = k * kk_ref[:, sl]
        ss = _dot_sel(kk * kk, bd_ref[...])
        kn = kk / jnp.maximum(jnp.sqrt(ss), 1e-12)
        kn_o[:, sl] = kn
        b_o[:, sl] = kn * a
        k_o[:, sl] = k * (1.0 + (a - 1.0) * ka_ref[:, sl])


def rwkv_prep(proj, seq, tm, mu, w0, a0, wa_up, g_up, k_k, k_a, bd, res):
    T = proj.shape[0]
    W = RWKV_HEADS * HEAD_DIM
    cols = mu.shape[-1]
    row1 = lambda n: pl.BlockSpec((1, n), lambda i: (0, 0))
    full = lambda a: pl.BlockSpec(a.shape, lambda i: (0, 0))
    tile = pl.BlockSpec((tm, W), lambda i: (i, 0))
    in_specs = [pl.BlockSpec((tm, cols), lambda i: (i, 0)),
                pl.BlockSpec((8, cols), lambda i: (jnp.maximum(i * (tm // 8) - 1, 0), 0)),
                row1(cols), row1(W), row1(W), full(wa_up), full(g_up), row1(W), row1(W), full(bd)]
    args = [proj, proj, mu, w0, a0, wa_up, g_up, k_k, k_a, bd]
    if res is not None:
        v_first, v0, mvd, mvu = res
        in_specs += [tile, row1(W), full(mvd), full(mvu)]
        args += [v_first, v0, mvd, mvu]
    out = jax.ShapeDtypeStruct((T, W), jnp.float32)
    kern = functools.partial(_rwkv_prep_kernel, width=W, has_res=res is not None, rows_per_seq=seq // tm)
    return pl.pallas_call(
        kern,
        grid=(T // tm,),
        in_specs=in_specs,
        out_specs=[tile] * 7,
        out_shape=[out] * 7,
        compiler_params=_params(("parallel",)),
        name="rwkv_prep",
    )(*args)


def _stack_heads(x, lane):
    return jnp.concatenate([jnp.where(lane < HEAD_DIM, x, 0.0), jnp.where(lane >= HEAD_DIM, x, 0.0)], axis=0)


def _rwkv_chunk_kernel(r_ref, lw_ref, k_ref, v_ref, kn_ref, b_ref, g_ref, rk_ref, bd_ref,
                       rhat_ref, y0_ref, e_ref, g_out_ref, d_out_ref, *, chunks):
    C = RWKV_CHUNK
    C2, C4 = 2 * C, 4 * C
    f32 = jnp.float32
    ti = lax.broadcasted_iota(jnp.int32, (C, C), 0)
    si = lax.broadcasted_iota(jnp.int32, (C, C), 1)
    tri = (ti >= si).astype(f32)
    lane = lax.broadcasted_iota(jnp.int32, (1, LANES), 1)
    rho = lax.broadcasted_iota(jnp.int32, (C4, C4), 0)
    sig = lax.broadcasted_iota(jnp.int32, (C4, C4), 1)
    keep = jnp.where(rho >= C2, rho & (C - 1), (rho & (C - 1)) - 1) >= (sig & (C - 1))
    eye2 = (lax.broadcasted_iota(jnp.int32, (C2, C2), 0) == lax.broadcasted_iota(jnp.int32, (C2, C2), 1)).astype(f32)
    eye_l = (lax.broadcasted_iota(jnp.int32, (LANES, LANES), 0)
             == lax.broadcasted_iota(jnp.int32, (LANES, LANES), 1)).astype(f32)
    zeros2 = jnp.zeros((C2, LANES), f32)
    ch = range(chunks)
    rows = [slice(c * C, (c + 1) * C) for c in ch]
    cum = [_dot(tri, lw_ref[rows[c], :], precision=HI) for c in ch]
    st = []
    for c in ch:
        r, lw, k, v = r_ref[rows[c], :], lw_ref[rows[c], :], k_ref[rows[c], :], v_ref[rows[c], :]
        kn, bb = kn_ref[rows[c], :], b_ref[rows[c], :]
        cum_last = cum[c][C - 1:C, :]
        e_out = jnp.exp(-cum[c])
        e_tail = jnp.exp(cum_last - cum[c])
        r_t = r * jnp.exp(cum[c])
        st.append(dict(
            r_t=r_t, decay=jnp.exp(cum_last),
            a2=_stack_heads(-kn * jnp.exp(cum[c] - lw), lane), r2=_stack_heads(r_t, lane),
            b2=_stack_heads(bb * e_out, lane), k2=_stack_heads(k * e_out, lane), v2=_stack_heads(v, lane),
            bh2=_stack_heads(bb * e_tail, lane), kh2=_stack_heads(k * e_tail, lane),
            e=_dot_sel(r * k * rk_ref[...], bd_ref[...]) * v * g_ref[rows[c], :]))
    quad = [jnp.where(keep, _dot3(jnp.concatenate([s["a2"], s["r2"]], axis=0),
                                  jnp.concatenate([s["b2"], s["k2"]], axis=0), _NT), 0.0) for s in st]
    pw = [q[:C2, :C2] for q in quad]
    t_inv = [eye2 + p for p in pw]
    x = [_dot3(q[:C2, C2:], s["v2"]) for q, s in zip(quad, st)]
    for _ in range(int(math.log2(C)) - 1):
        pw = [_dot3(p, p) for p in pw]
        t_inv = [t + _dot3(t, p) for t, p in zip(t_inv, pw)]
    au = [_dot3(t, jnp.concatenate([s["a2"], xx], axis=1)) for t, s, xx in zip(t_inv, st, x)]
    my = [_dot3(q[C2:, :], jnp.concatenate([a, jnp.concatenate([zeros2, s["v2"]], axis=1)], axis=0))
          for q, a, s in zip(quad, au, st)]
    gm = [eye_l * s["decay"] + _dot3(a[:, :LANES], s["bh2"], _TN) for a, s in zip(au, st)]
    dm = [_dot3(jnp.concatenate([a[:, LANES:], s["v2"]], axis=0),
                jnp.concatenate([s["bh2"], s["kh2"]], axis=0), _TN) for a, s in zip(au, st)]
    rhat_ref[...] = jnp.concatenate([s["r_t"] + m[:C, :LANES] + m[C:, :LANES] for s, m in zip(st, my)], axis=0)
    y0_ref[...] = jnp.concatenate([m[:C, LANES:] + m[C:, LANES:] for m in my], axis=0)
    e_ref[...] = jnp.concatenate([s["e"] for s in st], axis=0)
    g_out_ref[...] = jnp.stack(gm)
    d_out_ref[...] = jnp.stack(dm)


def rwkv_chunk(r, lw, k, v, kn, b, g, r_k, bd, batch, chunks=4):
    T, W = r.shape
    S = T // batch
    C = RWKV_CHUNK
    nc = S // C
    pairs = W // LANES
    steps = nc // chunks
    tile = pl.BlockSpec((chunks * C, LANES), lambda bi, h, c: (bi * steps + c, h))
    mat = pl.BlockSpec((None, None, chunks, LANES, LANES), lambda bi, h, c: (bi, h, c, 0, 0))
    tw = jax.ShapeDtypeStruct((T, W), jnp.float32)
    gd = jax.ShapeDtypeStruct((batch, pairs, nc, LANES, LANES), jnp.float32)
    return pl.pallas_call(
        functools.partial(_rwkv_chunk_kernel, chunks=chunks),
        grid=(batch, pairs, steps),
        in_specs=[tile] * 7 + [pl.BlockSpec((1, LANES), lambda bi, h, c: (0, h)),
                               pl.BlockSpec((LANES, LANES), lambda bi, h, c: (0, 0))],
        out_specs=[tile, tile, tile, mat, mat],
        out_shape=[tw, tw, tw, gd, gd],
        compiler_params=_params(("parallel", "parallel", "parallel")),
        name="rwkv_chunk",
    )(r, lw, k, v, kn, b, g, r_k, bd)


def _rwkv_state_kernel(rhat_ref, y0_ref, g_ref, e_ref, gm_ref, dm_ref, lng_ref, lnb_ref, bd_ref,
                       o_ref, state_ref, *, chunks, group):
    C = RWKV_CHUNK

    @pl.when(pl.program_id(2) == 0)
    def _():
        state_ref[...] = jnp.zeros_like(state_ref)

    bd = bd_ref[...]
    inv_n = 1.0 / HEAD_DIM

    def body(c, carry):
        rows = pl.ds(pl.multiple_of(c * C, C), C)
        pairs = range(group)
        state = [state_ref[p] for p in pairs]
        y = [_dot3(rhat_ref[rows, p * LANES:(p + 1) * LANES], state[p], _NT) for p in pairs]
        new_state = [_dot3(state[p], gm_ref[p, c]) + dm_ref[p, c] for p in pairs]
        y = jnp.concatenate(y, axis=1) + y0_ref[rows, :]
        state_ref[...] = jnp.stack(new_state)
        mean = jnp.concatenate([_dot_sel(y[:, p * LANES:(p + 1) * LANES], bd) for p in pairs], axis=1) * inv_n
        d = y - mean
        dd = d * d
        var = jnp.concatenate([_dot_sel(dd[:, p * LANES:(p + 1) * LANES], bd) for p in pairs], axis=1) * inv_n
        yn = d * lax.rsqrt(var + RWKV_GN_EPS) * lng_ref[...] + lnb_ref[...]
        o_ref[rows, :] = yn * g_ref[rows, :] + e_ref[rows, :]
        return carry

    lax.fori_loop(0, chunks, body, 0)


def rwkv_state(rhat, y0, g, e, gmat, dmat, ln_g, ln_b, bd, chunks=16, group=3):
    T, W = rhat.shape
    batch, pairs, nc = gmat.shape[:3]
    C = RWKV_CHUNK
    steps = nc // chunks
    gw = group * LANES
    tile = pl.BlockSpec((chunks * C, gw), lambda bi, h, c: (bi * steps + c, h))
    mat = pl.BlockSpec((None, group, chunks, LANES, LANES), lambda bi, h, c: (bi, h, c, 0, 0))
    row = pl.BlockSpec((1, gw), lambda bi, h, c: (0, h))
    return pl.pallas_call(
        functools.partial(_rwkv_state_kernel, chunks=chunks, group=group),
        grid=(batch, pairs // group, steps),
        in_specs=[tile] * 4 + [mat, mat, row, row, pl.BlockSpec((LANES, LANES), lambda bi, h, c: (0, 0))],
        out_specs=tile,
        out_shape=jax.ShapeDtypeStruct((T, W), jnp.float32),
        scratch_shapes=[pltpu.VMEM((group, LANES, LANES), jnp.float32)],
        compiler_params=_params(("parallel", "parallel", "arbitrary")),
        name="rwkv_state",
    )(rhat, y0, g, e, gmat, dmat, ln_g, ln_b, bd)


def _merge_kernel(oa_ref, ob_ref, acc_ref, m_ref, l_ref, ga_ref, gb_ref, gc_ref,
                  pa_ref, pb_ref, pc_ref, o_ref):
    m = m_ref[...]
    mx = jnp.max(m, axis=0)
    wgt = jnp.exp(m - mx[None])
    o_c = jnp.sum(wgt * acc_ref[...], axis=0) / jnp.sum(wgt * l_ref[...], axis=0)
    bf = jnp.bfloat16
    merged = (_sigmoid(ga_ref[...]) * _dot(oa_ref[...].astype(bf), pa_ref[...])
              + _sigmoid(gb_ref[...]) * _dot(ob_ref[...].astype(bf), pb_ref[...])
              + _sigmoid(gc_ref[...]) * _dot(o_c.astype(bf), pc_ref[...]))
    o_ref[...] = merged.astype(o_ref.dtype)


def merge_branches(o_a, o_b, acc, m, l, proj, gate_off, p_a, p_b, p_c, tm, tn):
    T = o_a.shape[0]
    D = p_a.shape[1]
    nj = D // tn
    Wc = acc.shape[-1]
    grp = pl.BlockSpec((3, tm, Wc), lambda i, j: (0, i, 0))
    gate = lambda n: pl.BlockSpec((tm, tn), lambda i, j: (i, gate_off + n * nj + j))
    wspec = lambda a: pl.BlockSpec((a.shape[0], tn), lambda i, j: (0, j))
    return pl.pallas_call(
        _merge_kernel,
        grid=(T // tm, nj),
        in_specs=[pl.BlockSpec((tm, o_a.shape[1]), lambda i, j: (i, 0)),
                  pl.BlockSpec((tm, o_b.shape[1]), lambda i, j: (i, 0)),
                  grp, grp, grp, gate(0), gate(1), gate(2), wspec(p_a), wspec(p_b), wspec(p_c)],
        out_specs=pl.BlockSpec((tm, tn), lambda i, j: (i, j)),
        out_shape=jax.ShapeDtypeStruct((T, D), jnp.bfloat16),
        compiler_params=_params(("parallel", "arbitrary")),
        name="merge",
    )(o_a, o_b, acc, m, l, proj, proj, proj, p_a, p_b, p_c)


def _out_ln_kernel(mg_ref, w_ref, x_ref, g1_ref, lg_ref, lb_ref, o_ref, *, alpha):
    mix = _dot(mg_ref[...], w_ref[...])
    y = alpha * x_ref[...] + (1.0 + g1_ref[...]) * mix
    o_ref[...] = _layer_norm(y, lg_ref[...], lb_ref[...])


def out_ln(merged, w_o, x2, g1, ln_g, ln_b, seq, tm, alpha):
    T, D = x2.shape
    per = seq // tm
    tile = pl.BlockSpec((tm, D), lambda i: (i, 0))
    row = pl.BlockSpec((1, D), lambda i: (0, 0))
    return pl.pallas_call(
        functools.partial(_out_ln_kernel, alpha=alpha),
        grid=(T // tm,),
        in_specs=[tile, pl.BlockSpec((D, D), lambda i: (0, 0)), tile,
                  pl.BlockSpec((None, 1, D), lambda i: (i // per, 0, 0)), row, row],
        out_specs=tile,
        out_shape=jax.ShapeDtypeStruct((T, D), jnp.float32),
        compiler_params=_params(("parallel",)),
        name="out_ln",
    )(merged, w_o, x2, g1, ln_g, ln_b)


def _router_kernel(x_ref, sc_ref, sh_ref, w_ref, b_ref, id_ref, gw_ref, h_ref):
    h = x_ref[...] * (1.0 + sc_ref[...]) + sh_ref[...]
    h_ref[...] = h.astype(h_ref.dtype)
    lg = _dot(h, w_ref[...], precision=HI) + b_ref[...]
    G, EPG = MOE_GROUPS, MOE_EXPERTS_PER_GROUP
    lane = lax.broadcasted_iota(jnp.int32, (1, LANES), 1).astype(jnp.float32)
    first = lambda hit: jnp.min(jnp.where(hit, lane, float(LANES)), axis=-1, keepdims=True)
    is_grp = lane < G
    gmax = jnp.max(jnp.where(is_grp, lg, -jnp.inf), axis=-1, keepdims=True)
    ge = jnp.where(is_grp, jnp.exp(jnp.where(is_grp, lg, gmax) - gmax), 0.0)
    prob = ge / jnp.sum(ge, axis=-1, keepdims=True)
    grp_p = jnp.max(prob, axis=-1, keepdims=True)
    grp_i = first(is_grp & (prob == grp_p))
    lo = G + grp_i * EPG
    el = jnp.where((lane >= lo) & (lane < lo + EPG), lg, -jnp.inf)
    l1 = jnp.max(el, axis=-1, keepdims=True)
    i1 = first(el == l1)
    el = jnp.where(lane == i1, -jnp.inf, el)
    l2 = jnp.max(el, axis=-1, keepdims=True)
    i2 = first(el == l2)
    t = jnp.exp(l2 - l1)
    w1 = grp_p / (1.0 + t)
    id_ref[...] = jnp.where(lane == 0, i1 - G, jnp.where(lane == 1, i2 - G, 0.0)).astype(jnp.int32)
    gw_ref[...] = jnp.where(lane == 0, w1, jnp.where(lane == 1, w1 * t, 0.0))


def router(x2, sc, sh, w_r, b_r, seq, tm):
    T, D = x2.shape
    per = seq // tm
    mod = pl.BlockSpec((None, 1, D), lambda i: (i // per, 0, 0))
    narrow = pl.BlockSpec((tm, LANES), lambda i: (i, 0))
    return pl.pallas_call(
        _router_kernel,
        grid=(T // tm,),
        in_specs=[pl.BlockSpec((tm, D), lambda i: (i, 0)), mod, mod,
                  pl.BlockSpec((D, LANES), lambda i: (0, 0)), pl.BlockSpec((1, LANES), lambda i: (0, 0))],
        out_specs=[narrow, narrow, pl.BlockSpec((tm, D), lambda i: (i, 0))],
        out_shape=[jax.ShapeDtypeStruct((T, LANES), jnp.int32), jax.ShapeDtypeStruct((T, LANES), jnp.float32),
                   jax.ShapeDtypeStruct((T, D), jnp.float32)],
        compiler_params=_params(("parallel",)),
        name="router",
    )(x2, sc, sh, w_r, b_r)


def _experts_kernel(be_ref, nu_ref, x_ref, sw_ref, wg_ref, wu_ref, wd_ref, o_ref, wg_s, wu_s, wd_s):
    i = pl.program_id(0)
    used = i < nu_ref[0]
    new_expert = (i == 0) | (be_ref[i] != be_ref[jnp.maximum(i - 1, 0)])

    @pl.when(used & new_expert)
    def _():
        wg_s[...] = wg_ref[...].astype(wg_s.dtype)
        wu_s[...] = wu_ref[...].astype(wu_s.dtype)
        wd_s[...] = wd_ref[...].astype(wd_s.dtype)

    @pl.when(used)
    def _():
        x = x_ref[...].astype(jnp.bfloat16)
        gate = _dot(x, wg_s[...])
        hid = gate * _sigmoid(gate) * _dot(x, wu_s[...])
        o_ref[...] = _dot(hid.astype(jnp.bfloat16), wd_s[...]) * sw_ref[...]

    @pl.when(jnp.logical_not(used))
    def _():
        o_ref[...] = jnp.zeros_like(o_ref)


def experts(xs, slot_w, block_e, n_used, w_gate, w_up, w_down, layer):
    R, D = xs.shape
    F = w_gate.shape[-1]
    n_blocks = R // MOE_BLOCK
    grid_spec = pltpu.PrefetchScalarGridSpec(
        num_scalar_prefetch=2,
        grid=(n_blocks,),
        in_specs=[pl.BlockSpec((MOE_BLOCK, D), lambda i, be, nu: (i, 0)),
                  pl.BlockSpec((MOE_BLOCK, 1), lambda i, be, nu: (i, 0)),
                  pl.BlockSpec((None, None, D, F), lambda i, be, nu: (layer, be[i], 0, 0)),
                  pl.BlockSpec((None, None, D, F), lambda i, be, nu: (layer, be[i], 0, 0)),
                  pl.BlockSpec((None, None, F, D), lambda i, be, nu: (layer, be[i], 0, 0))],
        out_specs=pl.BlockSpec((MOE_BLOCK, D), lambda i, be, nu: (i, 0)),
        scratch_shapes=[pltpu.VMEM((D, F), jnp.bfloat16), pltpu.VMEM((D, F), jnp.bfloat16),
                        pltpu.VMEM((F, D), jnp.bfloat16)],
    )
    return pl.pallas_call(
        _experts_kernel,
        grid_spec=grid_spec,
        out_shape=jax.ShapeDtypeStruct((R, D), jnp.float32),
        compiler_params=_params(("arbitrary",)),
        name="experts",
    )(block_e, n_used, xs, slot_w, w_gate, w_up, w_down)


def _combine_ln_kernel(x_ref, f0_ref, f1_ref, g2_ref, lg_ref, lb_ref, o_ref, *, alpha):
    y = alpha * x_ref[...] + (1.0 + g2_ref[...]) * (f0_ref[...] + f1_ref[...])
    o_ref[...] = _layer_norm(y, lg_ref[...], lb_ref[...])


def combine_ln(x2, f0, f1, g2, ln_g, ln_b, seq, tm, alpha):
    T, D = x2.shape
    per = seq // tm
    tile = pl.BlockSpec((tm, D), lambda i: (i, 0))
    row = pl.BlockSpec((1, D), lambda i: (0, 0))
    return pl.pallas_call(
        functools.partial(_combine_ln_kernel, alpha=alpha),
        grid=(T // tm,),
        in_specs=[tile, tile, tile, pl.BlockSpec((None, 1, D), lambda i: (i // per, 0, 0)), row, row],
        out_specs=tile,
        out_shape=jax.ShapeDtypeStruct((T, D), jnp.float32),
        compiler_params=_params(("parallel",)),
        name="combine_ln",
    )(x2, f0, f1, g2, ln_g, ln_b)


def _to_residue(t, dilation):
    B, S, W = t.shape
    return t.reshape(B, S // dilation, dilation, W).transpose(0, 2, 1, 3).reshape(B, S, W)


def _from_residue(t, dilation):
    B, S, W = t.shape
    return t.reshape(B, dilation, S // dilation, W).transpose(0, 2, 1, 3).reshape(B, S, W)


def _route(expert_id, gate_w):
    T = expert_id.shape[0]
    E = MOE_GROUPS * MOE_EXPERTS_PER_GROUP
    A = T * MOE_TOPK
    n_blocks = (A + E * (MOE_BLOCK - 1) + MOE_BLOCK - 1) // MOE_BLOCK
    flat_e = expert_id.reshape(A)
    flat_w = gate_w.reshape(A)
    e_s, order = lax.sort_key_val(flat_e, jnp.arange(A, dtype=jnp.int32))
    bounds = jnp.searchsorted(e_s, jnp.arange(E + 1, dtype=jnp.int32), side='left').astype(jnp.int32)
    start = bounds[:E]
    counts = bounds[1:] - start
    padded = (counts + MOE_BLOCK - 1) // MOE_BLOCK * MOE_BLOCK
    pad_end = jnp.cumsum(padded)
    pad_start = pad_end - padded
    block_e = jnp.minimum(jnp.sum(pad_end[None, :] <= (jnp.arange(n_blocks) * MOE_BLOCK)[:, None], axis=1),
                          E - 1).astype(jnp.int32)
    blk_rank = jnp.arange(n_blocks, dtype=jnp.int32) * MOE_BLOCK - pad_start[block_e]
    rank = blk_rank[:, None] + jnp.arange(MOE_BLOCK, dtype=jnp.int32)[None, :]
    valid = (rank < counts[block_e][:, None]).reshape(-1)
    src = order[jnp.clip(start[block_e][:, None] + rank, 0, A - 1).reshape(-1)]
    spread = jnp.arange(n_blocks * MOE_BLOCK, dtype=jnp.int32) % T
    slot_tok = jnp.where(valid, src // MOE_TOPK, spread).astype(jnp.int32)
    slot_w = jnp.where(valid, flat_w[src], 0.0)
    pos = jnp.arange(A, dtype=jnp.int32)[None, :]
    in_e = (pos >= start[:, None]) & (pos < bounds[1:, None])
    dest = pos[0] + jnp.sum(jnp.where(in_e, (pad_start - start)[:, None], 0), axis=0).astype(jnp.int32)
    _, slot_of = lax.sort_key_val(order, dest)
    n_used = (pad_end[-1] // MOE_BLOCK).astype(jnp.int32).reshape(1)
    return slot_tok, slot_w, slot_of.reshape(T, MOE_TOPK), block_e, n_used


def kernel(x, c, rel_bias, w_in, p_a, p_b, p_c, w_o, rwkv_mu, rwkv_w0, rwkv_w_up, rwkv_a0, rwkv_a_up,
           rwkv_g_up, rwkv_k_k, rwkv_k_a, rwkv_r_k, rwkv_ln_g, rwkv_ln_b, rwkv_v0, rwkv_mv_down,
           rwkv_mv_up, w_ada, b_ada, ln1_g, ln1_b, ln2_g, ln2_b, router_grp_w, router_grp_b,
           router_exp_w, router_exp_b, exp_w_gate, exp_w_up, exp_w_down):
    B, S, D = x.shape
    depth = w_in.shape[0]
    T = B * S
    bf = jnp.bfloat16
    W = RWKV_HEADS * HEAD_DIM
    wa_w = MOBA_HEADS * HEAD_DIM
    wc_w = len(DIL_GROUPS) * DIL_HEADS_PER_GROUP * HEAD_DIM
    rw_cols = rwkv_mu.shape[-1]
    alpha = (2 * depth) ** 0.25
    off_c = 3 * wa_w
    off_b = off_c + 3 * wc_w
    off_g = off_b + rw_cols
    new_a = rw_cols
    new_c = new_a + 3 * wa_w
    new_g = new_c + 3 * wc_w
    w_in_p = jnp.concatenate([w_in[:, :, off_b:off_g], w_in[:, :, :off_b], w_in[:, :, off_g:]], axis=-1).astype(bf)
    p_a_b, p_b_b, p_c_b, w_o_b = p_a.astype(bf), p_b.astype(bf), p_c.astype(bf), w_o.astype(bf)

    bias_h = rel_bias.T.astype(jnp.float32)
    moba_tab = moba_bias_table(bias_h[:MOBA_HEADS], S // MOBA_BLOCK)
    dil_tab = dil_bias_table(bias_h[MOBA_HEADS:])

    c8 = jnp.zeros((8, D), jnp.float32).at[:B].set(c)
    mod = ada_mod(c8, w_ada, b_ada)[:, :B]

    hd_idx = jnp.arange(LANES) // HEAD_DIM
    bd = (hd_idx[:, None] == hd_idx[None, :]).astype(jnp.float32)
    zeros_w = jnp.zeros((RWKV_DECAY_LORA, W), jnp.float32)
    pad_lora = LANES - RWKV_MV_LORA

    x2 = x.reshape(T, D)
    v_first = None
    for l in range(depth):
        sh1, sc1, g1, sh2, sc2, g2 = [m.reshape(B, 1, D) for m in jnp.split(mod[l], 6, axis=-1)]
        proj = in_proj(x2, sc1, sh1, w_in_p, l, S, 1024, 1024)
        proj3 = proj.reshape(B, S, -1)
        o_a = moba_attention(proj3, moba_tab, new_a // LANES, (new_a + wa_w) // LANES,
                             (new_a + 2 * wa_w) // LANES).reshape(T, wa_w)
        gw = DIL_HEADS_PER_GROUP * HEAD_DIM
        qs, ks, vs = [], [], []
        for g, (_, dil) in enumerate(DIL_GROUPS):
            qs.append(_to_residue(proj3[:, :, new_c + g * gw:new_c + (g + 1) * gw], dil))
            ks.append(_to_residue(proj3[:, :, new_c + wc_w + g * gw:new_c + wc_w + (g + 1) * gw], dil))
            vs.append(_to_residue(proj3[:, :, new_c + 2 * wc_w + g * gw:new_c + 2 * wc_w + (g + 1) * gw], dil))
        acc, mm, ll = dilated_attention(jnp.stack(qs, 1), jnp.stack(ks, 1), jnp.stack(vs, 1), dil_tab)
        unperm = lambda t: jnp.stack([_from_residue(t[:, g], dil).reshape(T, gw)
                                      for g, (_, dil) in enumerate(DIL_GROUPS)])
        acc, mm, ll = unperm(acc), unperm(mm), unperm(ll)
        wa_up = jnp.concatenate([jnp.concatenate([rwkv_w_up[l], zeros_w], axis=1),
                                 jnp.concatenate([zeros_w, rwkv_a_up[l]], axis=1)], axis=0)
        res = None
        if l > 0:
            res = (v_first, rwkv_v0[l - 1][None],
                   jnp.pad(rwkv_mv_down[l - 1], ((0, 0), (0, pad_lora))),
                   jnp.pad(rwkv_mv_up[l - 1], ((0, pad_lora), (0, 0))))
        r_, lw_, k_, v_, kn_, b_, g_ = rwkv_prep(proj, S, 256, rwkv_mu[l][None], rwkv_w0[l][None],
                                                 rwkv_a0[l][None], wa_up, rwkv_g_up[l], rwkv_k_k[l][None],
                                                 rwkv_k_a[l][None], bd, res)
        if l == 0:
            v_first = v_
        rhat, y0, e_, gmat, dmat = rwkv_chunk(r_, lw_, k_, v_, kn_, b_, g_, rwkv_r_k[l][None], bd, B)
        o_b = rwkv_state(rhat, y0, g_, e_, gmat, dmat, rwkv_ln_g[l][None], rwkv_ln_b[l][None], bd)
        merged = merge_branches(o_a, o_b, acc, mm, ll, proj, new_g // 1024, p_a_b[l], p_b_b[l], p_c_b[l],
                                256, 1024)
        x2 = out_ln(merged, w_o_b[l], x2, g1, ln1_g[l][None], ln1_b[l][None], S, 256, alpha)
        w_r = jnp.zeros((D, LANES), jnp.float32)
        w_r = w_r.at[:, :MOE_GROUPS].set(router_grp_w[l]).at[:, MOE_GROUPS:MOE_GROUPS + router_exp_w.shape[-1]].set(
            router_exp_w[l])
        b_r = jnp.zeros((1, LANES), jnp.float32)
        b_r = b_r.at[0, :MOE_GROUPS].set(router_grp_b[l]).at[0, MOE_GROUPS:MOE_GROUPS + router_exp_b.shape[-1]].set(
            router_exp_b[l])
        ids, gws, h2 = router(x2, sc2, sh2, w_r, b_r, S, 512)
        slot_tok, slot_w, slot_of, block_e, n_used = _route(ids[:, :MOE_TOPK], gws[:, :MOE_TOPK])
        y = experts(h2[slot_tok], slot_w[:, None], block_e, n_used, exp_w_gate, exp_w_up, exp_w_down, l)
        x2 = combine_ln(x2, y[slot_of[:, 0]], y[slot_of[:, 1]], g2, ln2_g[l][None], ln2_b[l][None], S, 512, alpha)
    return x2.reshape(B, S, D)
```

```python
import functools
import math

import jax
import jax.numpy as jnp
import numpy as np
from jax import lax
from jax.experimental import pallas as pl
from jax.experimental.pallas import tpu as pltpu

HEAD_DIM = 64
LANES = 128
MOBA_HEADS = 12
MOBA_BLOCK = 256
MOBA_TOPK = 3
RWKV_HEADS = 12
RWKV_DECAY_LORA = 64
RWKV_A_LORA = 64
RWKV_MV_LORA = 32
RWKV_GATE_LORA = 128
RWKV_GN_EPS = 64e-5
RWKV_CHUNK = 64
DIL_GROUPS = ((128, 1), (512, 4), (2048, 16))
DIL_HEADS_PER_GROUP = 4
DIL_SPAN = 128
REL_BUCKETS = 32
REL_MAX_DISTANCE = 2048
MOE_GROUPS = 8
MOE_EXPERTS_PER_GROUP = 8
MOE_TOPK = 2
MOE_BLOCK = 256
LN_EPS = 1e-5
NEG = -1e30
VMEM_LIMIT = 56 * 1024 * 1024
HI = lax.Precision.HIGHEST


def _params(sem):
    return pltpu.CompilerParams(dimension_semantics=sem, vmem_limit_bytes=VMEM_LIMIT)


def _sigmoid(x):
    return 1.0 / (1.0 + jnp.exp(-x))


def _dot(a, b, precision=None):
    return jnp.dot(a, b, preferred_element_type=jnp.float32, precision=precision)


def _dot_nt(a, b, precision=None):
    return lax.dot_general(a, b, (((1,), (1,)), ((), ())), preferred_element_type=jnp.float32,
                           precision=precision)


def _dot_tn(a, b, precision=None):
    return lax.dot_general(a, b, (((0,), (0,)), ((), ())), preferred_element_type=jnp.float32,
                           precision=precision)


def _split2(x):
    hi = x.astype(jnp.bfloat16)
    return hi, (x - hi.astype(jnp.float32)).astype(jnp.bfloat16)


def _dot3(a, b, dims=((1,), (0,))):
    (ca,), (cb,) = dims
    ah, al = _split2(a)
    bh, bl = _split2(b)
    return lax.dot_general(jnp.concatenate([ah, ah, al], axis=ca), jnp.concatenate([bh, bl, bh], axis=cb),
                           (dims, ((), ())), preferred_element_type=jnp.float32)


def _dot_sel(x, sel):
    xh, xl = _split2(x)
    sb = sel.astype(jnp.bfloat16)
    return _dot(jnp.concatenate([xh, xl], axis=1), jnp.concatenate([sb, sb], axis=0))


_NT = ((1,), (1,))
_TN = ((0,), (0,))


def _layer_norm(y, g, b):
    mu = jnp.mean(y, axis=-1, keepdims=True)
    d = y - mu
    var = jnp.mean(d * d, axis=-1, keepdims=True)
    return d * lax.rsqrt(var + LN_EPS) * g + b


def _ada_kernel(c_ref, w_ref, b_ref, o_ref):
    c = c_ref[...]
    cond = c * _sigmoid(c)
    o_ref[...] = _dot(cond, w_ref[...]) + b_ref[...]


def ada_mod(c8, w_ada, b_ada):
    L, D, N = w_ada.shape
    tn = 1024
    return pl.pallas_call(
        _ada_kernel,
        grid=(L, N // tn),
        in_specs=[pl.BlockSpec((8, D), lambda l, j: (0, 0)),
                  pl.BlockSpec((None, D, tn), lambda l, j: (l, 0, j)),
                  pl.BlockSpec((None, 1, tn), lambda l, j: (l, 0, j))],
        out_specs=pl.BlockSpec((None, 8, tn), lambda l, j: (l, 0, j)),
        out_shape=jax.ShapeDtypeStruct((L, 8, N), jnp.float32),
        compiler_params=_params(("parallel", "parallel")),
        name="ada_mod",
    )(c8, w_ada, b_ada.reshape(L, 1, N))


def _in_proj_kernel(x_ref, sc_ref, sh_ref, w_ref, o_ref, h_ref):
    @pl.when(pl.program_id(1) == 0)
    def _():
        h_ref[...] = (x_ref[...] * (1.0 + sc_ref[...]) + sh_ref[...]).astype(h_ref.dtype)

    o_ref[...] = _dot(h_ref[...], w_ref[...])


def in_proj(x2, sc, sh, w, layer, seq, tm, tn):
    T, D = x2.shape
    N = w.shape[-1]
    per = seq // tm
    return pl.pallas_call(
        _in_proj_kernel,
        grid=(T // tm, N // tn),
        in_specs=[pl.BlockSpec((tm, D), lambda i, j: (i, 0)),
                  pl.BlockSpec((None, 1, D), lambda i, j: (i // per, 0, 0)),
                  pl.BlockSpec((None, 1, D), lambda i, j: (i // per, 0, 0)),
                  pl.BlockSpec((None, D, tn), lambda i, j: (layer, 0, j))],
        out_specs=pl.BlockSpec((tm, tn), lambda i, j: (i, j)),
        out_shape=jax.ShapeDtypeStruct((T, N), jnp.float32),
        scratch_shapes=[pltpu.VMEM((tm, D), jnp.bfloat16)],
        compiler_params=_params(("parallel", "arbitrary")),
        name="in_proj",
    )(x2, sc, sh, w)


def _t5_bucket(dist):
    n = jnp.maximum(dist, 0)
    max_exact = REL_BUCKETS // 2
    nf = jnp.maximum(n, 1).astype(jnp.float32)
    large = max_exact + (jnp.log(nf / max_exact) / math.log(REL_MAX_DISTANCE / max_exact)
                         * (REL_BUCKETS - max_exact)).astype(jnp.int32)
    large = jnp.minimum(large, REL_BUCKETS - 1)
    return jnp.where(n < max_exact, n, large)


def _moba_n_delta(nb):
    last_start = 1
    d = np.arange(1, nb * MOBA_BLOCK + 1)
    large = 16 + (np.log(d / 16.0) / math.log(REL_MAX_DISTANCE / 16.0) * 16).astype(np.int64)
    bucket = np.where(d < 16, d, np.minimum(large, REL_BUCKETS - 1))
    last_start = int(d[bucket < REL_BUCKETS - 1].max()) + 1 if (bucket < REL_BUCKETS - 1).any() else 1
    delta = 1
    while delta * MOBA_BLOCK - (MOBA_BLOCK - 1) < last_start + 2:
        delta += 1
    return min(delta + 1, nb)


def moba_bias_table(bias_a, nb):
    nd = _moba_n_delta(nb)
    key = jnp.arange(MOBA_BLOCK)[:, None]
    qry = jnp.arange(MOBA_BLOCK)[None, :]
    dist = jnp.arange(nd)[:, None, None] * MOBA_BLOCK + (qry - key)[None]
    bucket = _t5_bucket(dist)[None]
    tab = jnp.zeros((bias_a.shape[0],) + dist.shape, jnp.float32)
    for b in range(REL_BUCKETS):
        tab = jnp.where(bucket == b, bias_a[:, b][:, None, None, None], tab)
    return jnp.where((dist >= 0)[None], tab, NEG)


def dil_bias_table(bias_c):
    span = DIL_SPAN
    rel = span + jnp.arange(span)[:, None] - jnp.arange(2 * span)[None, :]
    valid = (rel >= 0) & (rel <= span)
    tabs = []
    for g, (_, dilation) in enumerate(DIL_GROUPS):
        bh = bias_c[g * DIL_HEADS_PER_GROUP:(g + 1) * DIL_HEADS_PER_GROUP]
        tabs.append(jnp.where(valid[None], bh[:, _t5_bucket(rel * dilation)], NEG))
    return jnp.stack(tabs)


def _moba_kernel(q_ref, k_ref, v_ref, bias_ref, o_ref,
                 kaug_ref, vt_ref, km_ref, acc_ref, *, nb, n_delta):
    qb = pl.program_id(2)
    bs = MOBA_BLOCK
    S = nb * bs
    half = bs // 2
    lane = lax.broadcasted_iota(jnp.int32, (1, LANES), 1)
    f32 = jnp.float32

    @pl.when(qb == 0)
    def _():
        k = k_ref[...]
        rowblk = lax.broadcasted_iota(jnp.int32, (S, LANES), 0) // bs
        lanes = lax.broadcasted_iota(jnp.int32, (S, LANES), 1)
        km_ref[...] = jnp.mean(k.reshape(nb, bs, LANES), axis=1)
        kaug_ref[0] = jnp.where(lanes < HEAD_DIM, k, (lanes - HEAD_DIM == rowblk).astype(f32)).astype(kaug_ref.dtype)
        kaug_ref[1] = jnp.where(lanes >= HEAD_DIM, k, (lanes == rowblk).astype(f32)).astype(kaug_ref.dtype)
        for i in range(nb):
            vt_ref[i] = v_ref[i * bs:(i + 1) * bs, :].T.astype(vt_ref.dtype)

    q_t = q_ref[...].T
    scale = HEAD_DIM ** -0.5
    blk = lax.broadcasted_iota(jnp.int32, (nb, 1), 0)
    past = blk < qb
    rhs = []
    for t in range(2):
        hm = (lane >= t * HEAD_DIM) & (lane < (t + 1) * HEAD_DIM)
        gate = _dot(jnp.where(hm, km_ref[...], 0.0), q_t, precision=HI)
        gate = jnp.where(past, gate, -jnp.inf)
        cnt = jnp.zeros((nb, bs), jnp.int32)
        for m in range(nb):
            gm = gate[m:m + 1, :]
            ahead = (gm > gate) | ((gm == gate) & (m < blk))
            cnt = cnt + ahead.astype(jnp.int32)
        chosen = (past & (cnt < MOBA_TOPK)) | (blk == qb)
        pen = jnp.where(chosen, 0.0, NEG)
        qh = q_t[t * HEAD_DIM:(t + 1) * HEAD_DIM, :] * scale
        if t == 0:
            parts = [qh, pen, jnp.zeros((LANES - HEAD_DIM - nb, bs), f32)]
        else:
            parts = [pen, jnp.zeros((HEAD_DIM - nb, bs), f32), qh]
        rhs.append(jnp.concatenate(parts, axis=0).astype(jnp.bfloat16))
    acc_ref[...] = jnp.zeros((LANES, bs), f32)

    chains = [(t, hq) for t in range(2) for hq in range(2)]
    cols = [slice(hq * half, (hq + 1) * half) for _, hq in chains]

    def step(blocks, carry):
        rows = [pl.multiple_of(n * bs, bs) for n in blocks]
        delta = [jnp.minimum(qb - n, n_delta - 1) for n in blocks]
        s = [[_dot(kaug_ref[t, pl.ds(rows[j], bs), :], rhs[t][:, cols[c]]) + bias_ref[t, delta[j], :, cols[c]]
              for j in range(len(blocks))] for c, (t, _) in enumerate(chains)]
        m_new = [functools.reduce(jnp.maximum, [carry[c][0]] + [jnp.max(x, axis=0, keepdims=True) for x in s[c]])
                 for c in range(4)]
        alpha = [jnp.exp(carry[c][0] - m_new[c]) for c in range(4)]
        p = [[jnp.exp(x - m_new[c]) for x in s[c]] for c in range(4)]
        l_new = [alpha[c] * carry[c][1] + sum(jnp.sum(x, axis=0, keepdims=True) for x in p[c]) for c in range(4)]
        pv = [sum(_dot(vt_ref[n, t * HEAD_DIM:(t + 1) * HEAD_DIM, :], p[c][j].astype(jnp.bfloat16))
                  for j, n in enumerate(blocks)) for c, (t, _) in enumerate(chains)]
        pieces = [alpha[c] * acc_ref[t * HEAD_DIM:(t + 1) * HEAD_DIM, cols[c]] + pv[c]
                  for c, (t, _) in enumerate(chains)]
        acc_ref[...] = jnp.concatenate([jnp.concatenate(pieces[:2], axis=1),
                                        jnp.concatenate(pieces[2:], axis=1)], axis=0)
        return tuple((m_new[c], l_new[c]) for c in range(4))

    init = tuple((jnp.full((1, half), NEG, f32), jnp.zeros((1, half), f32)) for _ in range(4))
    n_blocks = qb + 1
    fin = lax.fori_loop(0, n_blocks // 2, lambda i, c: step([2 * i, 2 * i + 1], c), init)
    fin = lax.cond(n_blocks % 2 == 1, lambda c: step([qb], c), lambda c: c, fin)
    l_all = jnp.concatenate([jnp.broadcast_to(jnp.concatenate([fin[2 * t][1], fin[2 * t + 1][1]], axis=1),
                                              (HEAD_DIM, bs)) for t in range(2)], axis=0)
    o_ref[...] = (acc_ref[...] / l_all).T


def moba_attention(proj3, bias_tab, q_off, k_off, v_off):
    B, S, _ = proj3.shape
    nb = S // MOBA_BLOCK
    n_delta = bias_tab.shape[1]
    pairs = MOBA_HEADS // 2
    kern = functools.partial(_moba_kernel, nb=nb, n_delta=n_delta)
    return pl.pallas_call(
        kern,
        grid=(pairs, B, nb),
        in_specs=[pl.BlockSpec((None, MOBA_BLOCK, LANES), lambda h, b, i: (b, i, q_off + h)),
                  pl.BlockSpec((None, S, LANES), lambda h, b, i: (b, 0, k_off + h)),
                  pl.BlockSpec((None, S, LANES), lambda h, b, i: (b, 0, v_off + h)),
                  pl.BlockSpec((2, n_delta, MOBA_BLOCK, MOBA_BLOCK), lambda h, b, i: (h, 0, 0, 0))],
        out_specs=pl.BlockSpec((None, MOBA_BLOCK, LANES), lambda h, b, i: (b, i, h)),
        out_shape=jax.ShapeDtypeStruct((B, S, pairs * LANES), jnp.float32),
        scratch_shapes=[pltpu.VMEM((2, S, LANES), jnp.bfloat16),
                        pltpu.VMEM((nb, LANES, MOBA_BLOCK), jnp.bfloat16),
                        pltpu.VMEM((nb, LANES), jnp.float32),
                        pltpu.VMEM((LANES, MOBA_BLOCK), jnp.float32)],
        compiler_params=_params(("parallel", "parallel", "arbitrary")),
        name="moba",
    )(proj3, proj3, proj3, bias_tab)


def _dil_kernel(q_ref, k_ref, v_ref, bias_ref, o_ref, lw_ref, *, dilation, seq):
    span = DIL_SPAN
    bps = seq // dilation // span
    scale = HEAD_DIM ** -0.5
    lane = lax.broadcasted_iota(jnp.int32, (1, LANES), 1)
    col = lax.broadcasted_iota(jnp.int32, (1, 2 * span), 1)
    hm = [lane < HEAD_DIM, lane >= HEAD_DIM]
    unroll = 2

    def body(it, carry):
        blocks = []
        for u in range(unroll):
            j = it * unroll + u
            r = j // bps
            i = j - r * bps
            start = r + i * (span * dilation)
            prev = jnp.maximum(start - span * dilation, r)
            rows = lambda s0: pl.ds(s0, span, stride=dilation)
            q = q_ref[rows(start), :] * scale
            kcat = jnp.concatenate([k_ref[rows(prev), :], k_ref[rows(start), :]], axis=0).astype(jnp.bfloat16)
            vcat = jnp.concatenate([v_ref[rows(prev), :], v_ref[rows(start), :]], axis=0).astype(jnp.bfloat16)
            blocks.append((start, q, kcat, vcat, (i == 0) & (col < span)))
        chains = [(u, t) for u in range(unroll) for t in range(2)]
        s = [jnp.where(blocks[u][4], NEG,
                       _dot_nt(jnp.where(hm[t], blocks[u][1], 0.0).astype(jnp.bfloat16), blocks[u][2]) + bias_ref[t])
             for u, t in chains]
        m = [jnp.max(x, axis=-1, keepdims=True) for x in s]
        e = [jnp.exp(x - mm) for x, mm in zip(s, m)]
        l = [jnp.sum(x, axis=-1, keepdims=True) for x in e]
        pv = [_dot(x.astype(jnp.bfloat16), blocks[u][3]) for x, (u, _) in zip(e, chains)]
        for u in range(unroll):
            a, b = 2 * u, 2 * u + 1
            rows = pl.ds(blocks[u][0], span, stride=dilation)
            o_ref[rows, :] = jnp.where(hm[0], pv[a] / l[a], pv[b] / l[b])
            lw_ref[rows, :] = jnp.where(hm[0], m[a] + jnp.log(l[a]), m[b] + jnp.log(l[b]))
        return carry

    lax.fori_loop(0, seq // span // unroll, body, 0)


def dilated_attention(proj3, bias_tab, q_off, k_off, v_off, group):
    B, S, _ = proj3.shape
    dilation = DIL_GROUPS[group][1]
    pairs = DIL_HEADS_PER_GROUP // 2
    slab = lambda off: pl.BlockSpec((None, S, LANES), lambda b, h: (b, 0, off + h))
    out = jax.ShapeDtypeStruct((B, S, pairs * LANES), jnp.float32)
    return pl.pallas_call(
        functools.partial(_dil_kernel, dilation=dilation, seq=S),
        grid=(B, pairs),
        in_specs=[slab(q_off), slab(k_off), slab(v_off),
                  pl.BlockSpec((None, 2, DIL_SPAN, 2 * DIL_SPAN), lambda b, h: (group, h, 0, 0))],
        out_specs=[slab(0), slab(0)],
        out_shape=[out, out],
        compiler_params=_params(("parallel", "parallel")),
        name="dilated",
    )(proj3, proj3, proj3, bias_tab)


def _rwkv_prep_kernel(*refs, width, has_res, rows_per_seq):
    if has_res:
        (z_ref, zl_ref, mu_ref, w0_ref, a0_ref, wa_ref, gup_ref, kk_ref, ka_ref, bd_ref,
         vf_ref, v0_ref, mvd_ref, mvu_ref,
         r_o, lw_o, k_o, v_o, kn_o, b_o, g_o) = refs
    else:
        (z_ref, zl_ref, mu_ref, w0_ref, a0_ref, wa_ref, gup_ref, kk_ref, ka_ref, bd_ref,
         r_o, lw_o, k_o, v_o, kn_o, b_o, g_o) = refs
    i = pl.program_id(0)
    W = width
    z = z_ref[...]
    tm = z.shape[0]
    row = lax.broadcasted_iota(jnp.int32, (tm, 1), 0)
    seq_start = (i % rows_per_seq) == 0
    last = jnp.where(seq_start, 0.0, zl_ref[7:8, :])
    zp = jnp.where(row == 0, last, pltpu.roll(z, 1, 0))
    zf = z + mu_ref[...] * (zp - z)
    lora = zf[:, 3 * W:3 * W + LANES]
    lane = lax.broadcasted_iota(jnp.int32, (1, LANES), 1)
    lora = jnp.where(lane < RWKV_DECAY_LORA, jnp.tanh(lora), lora)
    wa = _dot(lora, wa_ref[...], precision=HI)
    g = _dot(_sigmoid(zf[:, 3 * W + LANES:3 * W + 2 * LANES]), gup_ref[...], precision=HI)
    g_o[...] = g
    v_all = zf[:, 2 * W:3 * W]
    if has_res:
        mix = _dot(_dot(v_all, mvd_ref[...], precision=HI), mvu_ref[...], precision=HI)
    for c in range(W // LANES):
        sl = slice(c * LANES, (c + 1) * LANES)
        x = w0_ref[:, sl] + wa[:, sl]
        sp = jnp.maximum(-x, 0.0) + jnp.log(1.0 + jnp.exp(-jnp.abs(x)))
        lw_o[:, sl] = -jnp.exp(-sp - 0.5)
        a = _sigmoid(a0_ref[:, sl] + wa[:, W + c * LANES:W + (c + 1) * LANES])
        r_o[:, sl] = zf[:, sl]
        k = zf[:, W + c * LANES:W + (c + 1) * LANES]
        v = v_all[:, sl]
        if has_res:
            v = v + (vf_ref[:, sl] - v) * _sigmoid(v0_ref[:, sl] + mix[:, sl])
        v_o[:, sl] = v
        kk = k * kk_ref[:, sl]
        ss = _dot_sel(kk * kk, bd_ref[...])
        kn = kk / jnp.maximum(jnp.sqrt(ss), 1e-12)
        kn_o[:, sl] = kn
        b_o[:, sl] = kn * a
        k_o[:, sl] = k * (1.0 + (a - 1.0) * ka_ref[:, sl])


def rwkv_prep(proj, seq, tm, mu, w0, a0, wa_up, g_up, k_k, k_a, bd, res):
    T = proj.shape[0]
    W = RWKV_HEADS * HEAD_DIM
    cols = mu.shape[-1]
    row1 = lambda n: pl.BlockSpec((1, n), lambda i: (0, 0))
    full = lambda a: pl.BlockSpec(a.shape, lambda i: (0, 0))
    tile = pl.BlockSpec((tm, W), lambda i: (i, 0))
    in_specs = [pl.BlockSpec((tm, cols), lambda i: (i, 0)),
                pl.BlockSpec((8, cols), lambda i: (jnp.maximum(i * (tm // 8) - 1, 0), 0)),
                row1(cols), row1(W), row1(W), full(wa_up), full(g_up), row1(W), row1(W), full(bd)]
    args = [proj, proj, mu, w0, a0, wa_up, g_up, k_k, k_a, bd]
    if res is not None:
        v_first, v0, mvd, mvu = res
        in_specs += [tile, row1(W), full(mvd), full(mvu)]
        args += [v_first, v0, mvd, mvu]
    out = jax.ShapeDtypeStruct((T, W), jnp.float32)
    kern = functools.partial(_rwkv_prep_kernel, width=W, has_res=res is not None, rows_per_seq=seq // tm)
    return pl.pallas_call(
        kern,
        grid=(T // tm,),
        in_specs=in_specs,
        out_specs=[tile] * 7,
        out_shape=[out] * 7,
        compiler_params=_params(("parallel",)),
        name="rwkv_prep",
    )(*args)


def _stack_heads(x, lane):
    return jnp.concatenate([jnp.where(lane < HEAD_DIM, x, 0.0), jnp.where(lane >= HEAD_DIM, x, 0.0)], axis=0)


def _rwkv_chunk_kernel(r_ref, lw_ref, k_ref, v_ref, kn_ref, b_ref, g_ref, rk_ref, bd_ref,
                       rhat_ref, y0_ref, e_ref, g_out_ref, d_out_ref, *, chunks):
    C = RWKV_CHUNK
    C2, C4 = 2 * C, 4 * C
    f32 = jnp.float32
    ti = lax.broadcasted_iota(jnp.int32, (C, C), 0)
    si = lax.broadcasted_iota(jnp.int32, (C, C), 1)
    tri = (ti >= si).astype(f32)
    lane = lax.broadcasted_iota(jnp.int32, (1, LANES), 1)
    rho = lax.broadcasted_iota(jnp.int32, (C4, C4), 0)
    sig = lax.broadcasted_iota(jnp.int32, (C4, C4), 1)
    keep = jnp.where(rho >= C2, rho & (C - 1), (rho & (C - 1)) - 1) >= (sig & (C - 1))
    eye2 = (lax.broadcasted_iota(jnp.int32, (C2, C2), 0) == lax.broadcasted_iota(jnp.int32, (C2, C2), 1)).astype(f32)
    eye_l = (lax.broadcasted_iota(jnp.int32, (LANES, LANES), 0)
             == lax.broadcasted_iota(jnp.int32, (LANES, LANES), 1)).astype(f32)
    zeros2 = jnp.zeros((C2, LANES), f32)
    ch = range(chunks)
    rows = [slice(c * C, (c + 1) * C) for c in ch]
    cum = [_dot(tri, lw_ref[rows[c], :], precision=HI) for c in ch]
    st = []
    for c in ch:
        r, lw, k, v = r_ref[rows[c], :], lw_ref[rows[c], :], k_ref[rows[c], :], v_ref[rows[c], :]
        kn, bb = kn_ref[rows[c], :], b_ref[rows[c], :]
        cum_last = cum[c][C - 1:C, :]
        e_out = jnp.exp(-cum[c])
        e_tail = jnp.exp(cum_last - cum[c])
        r_t = r * jnp.exp(cum[c])
        st.append(dict(
            r_t=r_t, decay=jnp.exp(cum_last),
            a2=_stack_heads(-kn * jnp.exp(cum[c] - lw), lane), r2=_stack_heads(r_t, lane),
            b2=_stack_heads(bb * e_out, lane), k2=_stack_heads(k * e_out, lane), v2=_stack_heads(v, lane),
            bh2=_stack_heads(bb * e_tail, lane), kh2=_stack_heads(k * e_tail, lane),
            e=_dot_sel(r * k * rk_ref[...], bd_ref[...]) * v * g_ref[rows[c], :]))
    quad = [jnp.where(keep, _dot3(jnp.concatenate([s["a2"], s["r2"]], axis=0),
                                  jnp.concatenate([s["b2"], s["k2"]], axis=0), _NT), 0.0) for s in st]
    pw = [q[:C2, :C2] for q in quad]
    t_inv = [eye2 + p for p in pw]
    x = [_dot3(q[:C2, C2:], s["v2"]) for q, s in zip(quad, st)]
    for level in range(int(math.log2(C)) - 1):
        mm = _dot3 if level < 2 else (lambda a, b: _dot(a.astype(jnp.bfloat16), b.astype(jnp.bfloat16)))
        pw = [mm(p, p) for p in pw]
        t_inv = [t + mm(t, p) for t, p in zip(t_inv, pw)]
    au = [_dot3(t, jnp.concatenate([s["a2"], xx], axis=1)) for t, s, xx in zip(t_inv, st, x)]
    my = [_dot3(q[C2:, :], jnp.concatenate([a, jnp.concatenate([zeros2, s["v2"]], axis=1)], axis=0))
          for q, a, s in zip(quad, au, st)]
    gm = [eye_l * s["decay"] + _dot3(a[:, :LANES], s["bh2"], _TN) for a, s in zip(au, st)]
    dm = [_dot3(jnp.concatenate([a[:, LANES:], s["v2"]], axis=0),
                jnp.concatenate([s["bh2"], s["kh2"]], axis=0), _TN) for a, s in zip(au, st)]
    rhat_ref[...] = jnp.concatenate([s["r_t"] + m[:C, :LANES] + m[C:, :LANES] for s, m in zip(st, my)], axis=0)
    y0_ref[...] = jnp.concatenate([m[:C, LANES:] + m[C:, LANES:] for m in my], axis=0)
    e_ref[...] = jnp.concatenate([s["e"] for s in st], axis=0)
    g_out_ref[...] = jnp.stack(gm)
    d_out_ref[...] = jnp.stack(dm)


def rwkv_chunk(r, lw, k, v, kn, b, g, r_k, bd, batch, chunks=4):
    T, W = r.shape
    S = T // batch
    C = RWKV_CHUNK
    nc = S // C
    pairs = W // LANES
    steps = nc // chunks
    tile = pl.BlockSpec((chunks * C, LANES), lambda bi, h, c: (bi * steps + c, h))
    mat = pl.BlockSpec((None, None, chunks, LANES, LANES), lambda bi, h, c: (bi, h, c, 0, 0))
    tw = jax.ShapeDtypeStruct((T, W), jnp.float32)
    gd = jax.ShapeDtypeStruct((batch, pairs, nc, LANES, LANES), jnp.float32)
    return pl.pallas_call(
        functools.partial(_rwkv_chunk_kernel, chunks=chunks),
        grid=(batch, pairs, steps),
        in_specs=[tile] * 7 + [pl.BlockSpec((1, LANES), lambda bi, h, c: (0, h)),
                               pl.BlockSpec((LANES, LANES), lambda bi, h, c: (0, 0))],
        out_specs=[tile, tile, tile, mat, mat],
        out_shape=[tw, tw, tw, gd, gd],
        compiler_params=_params(("parallel", "parallel", "parallel")),
        name="rwkv_chunk",
    )(r, lw, k, v, kn, b, g, r_k, bd)


def _rwkv_state_kernel(rhat_ref, y0_ref, g_ref, e_ref, gm_ref, dm_ref, lng_ref, lnb_ref, bd_ref,
                       o_ref, state_ref, *, chunks, group):
    C = RWKV_CHUNK

    @pl.when(pl.program_id(2) == 0)
    def _():
        state_ref[...] = jnp.zeros_like(state_ref)

    bd = bd_ref[...]
    inv_n = 1.0 / HEAD_DIM

    def body(c, carry):
        rows = pl.ds(pl.multiple_of(c * C, C), C)
        pairs = range(group)
        state = [state_ref[p] for p in pairs]
        y = [_dot3(rhat_ref[rows, p * LANES:(p + 1) * LANES], state[p], _NT) for p in pairs]
        new_state = [_dot3(state[p], gm_ref[p, c]) + dm_ref[p, c] for p in pairs]
        y = jnp.concatenate(y, axis=1) + y0_ref[rows, :]
        state_ref[...] = jnp.stack(new_state)
        mean = jnp.concatenate([_dot_sel(y[:, p * LANES:(p + 1) * LANES], bd) for p in pairs], axis=1) * inv_n
        d = y - mean
        dd = d * d
        var = jnp.concatenate([_dot_sel(dd[:, p * LANES:(p + 1) * LANES], bd) for p in pairs], axis=1) * inv_n
        yn = d * lax.rsqrt(var + RWKV_GN_EPS) * lng_ref[...] + lnb_ref[...]
        o_ref[rows, :] = yn * g_ref[rows, :] + e_ref[rows, :]
        return carry

    lax.fori_loop(0, chunks, body, 0)


def rwkv_state(rhat, y0, g, e, gmat, dmat, ln_g, ln_b, bd, chunks=16, group=3):
    T, W = rhat.shape
    batch, pairs, nc = gmat.shape[:3]
    C = RWKV_CHUNK
    steps = nc // chunks
    gw = group * LANES
    tile = pl.BlockSpec((chunks * C, gw), lambda bi, h, c: (bi * steps + c, h))
    mat = pl.BlockSpec((None, group, chunks, LANES, LANES), lambda bi, h, c: (bi, h, c, 0, 0))
    row = pl.BlockSpec((1, gw), lambda bi, h, c: (0, h))
    return pl.pallas_call(
        functools.partial(_rwkv_state_kernel, chunks=chunks, group=group),
        grid=(batch, pairs // group, steps),
        in_specs=[tile] * 4 + [mat, mat, row, row, pl.BlockSpec((LANES, LANES), lambda bi, h, c: (0, 0))],
        out_specs=tile,
        out_shape=jax.ShapeDtypeStruct((T, W), jnp.float32),
        scratch_shapes=[pltpu.VMEM((group, LANES, LANES), jnp.float32)],
        compiler_params=_params(("parallel", "parallel", "arbitrary")),
        name="rwkv_state",
    )(rhat, y0, g, e, gmat, dmat, ln_g, ln_b, bd)


def _merge_kernel(oa_ref, ob_ref, oc0_ref, oc1_ref, oc2_ref, lw0_ref, lw1_ref, lw2_ref, ga_ref, gb_ref, gc_ref,
                  pa_ref, pb_ref, pc_ref, o_ref):
    lw = [lw0_ref[...], lw1_ref[...], lw2_ref[...]]
    mx = jnp.maximum(jnp.maximum(lw[0], lw[1]), lw[2])
    wgt = [jnp.exp(x - mx) for x in lw]
    o_c = ((wgt[0] * oc0_ref[...] + wgt[1] * oc1_ref[...] + wgt[2] * oc2_ref[...])
           / (wgt[0] + wgt[1] + wgt[2]))
    bf = jnp.bfloat16
    merged = (_sigmoid(ga_ref[...]) * _dot(oa_ref[...].astype(bf), pa_ref[...])
              + _sigmoid(gb_ref[...]) * _dot(ob_ref[...].astype(bf), pb_ref[...])
              + _sigmoid(gc_ref[...]) * _dot(o_c.astype(bf), pc_ref[...]))
    o_ref[...] = merged.astype(o_ref.dtype)


def merge_branches(o_a, o_b, o_c, lw_c, proj, gate_off, p_a, p_b, p_c, tm, tn):
    T = o_a.shape[0]
    D = p_a.shape[1]
    nj = D // tn
    grp = pl.BlockSpec((tm, o_c[0].shape[1]), lambda i, j: (i, 0))
    gate = lambda n: pl.BlockSpec((tm, tn), lambda i, j: (i, gate_off + n * nj + j))
    wspec = lambda a: pl.BlockSpec((a.shape[0], tn), lambda i, j: (0, j))
    return pl.pallas_call(
        _merge_kernel,
        grid=(T // tm, nj),
        in_specs=[pl.BlockSpec((tm, o_a.shape[1]), lambda i, j: (i, 0)),
                  pl.BlockSpec((tm, o_b.shape[1]), lambda i, j: (i, 0)),
                  grp, grp, grp, grp, grp, grp, gate(0), gate(1), gate(2), wspec(p_a), wspec(p_b), wspec(p_c)],
        out_specs=pl.BlockSpec((tm, tn), lambda i, j: (i, j)),
        out_shape=jax.ShapeDtypeStruct((T, D), jnp.bfloat16),
        compiler_params=_params(("parallel", "arbitrary")),
        name="merge",
    )(o_a, o_b, *o_c, *lw_c, proj, proj, proj, p_a, p_b, p_c)


def _out_ln_kernel(mg_ref, w_ref, x_ref, g1_ref, lg_ref, lb_ref, o_ref, *, alpha):
    mix = _dot(mg_ref[...], w_ref[...])
    y = alpha * x_ref[...] + (1.0 + g1_ref[...]) * mix
    o_ref[...] = _layer_norm(y, lg_ref[...], lb_ref[...])


def out_ln(merged, w_o, x2, g1, ln_g, ln_b, seq, tm, alpha):
    T, D = x2.shape
    per = seq // tm
    tile = pl.BlockSpec((tm, D), lambda i: (i, 0))
    row = pl.BlockSpec((1, D), lambda i: (0, 0))
    return pl.pallas_call(
        functools.partial(_out_ln_kernel, alpha=alpha),
        grid=(T // tm,),
        in_specs=[tile, pl.BlockSpec((D, D), lambda i: (0, 0)), tile,
                  pl.BlockSpec((None, 1, D), lambda i: (i // per, 0, 0)), row, row],
        out_specs=tile,
        out_shape=jax.ShapeDtypeStruct((T, D), jnp.float32),
        compiler_params=_params(("parallel",)),
        name="out_ln",
    )(merged, w_o, x2, g1, ln_g, ln_b)


def _router_kernel(x_ref, sc_ref, sh_ref, w_ref, b_ref, id_ref, gw_ref, h_ref):
    h = x_ref[...] * (1.0 + sc_ref[...]) + sh_ref[...]
    h_ref[...] = h.astype(h_ref.dtype)
    lg = _dot(h, w_ref[...], precision=HI) + b_ref[...]
    G, EPG = MOE_GROUPS, MOE_EXPERTS_PER_GROUP
    lane = lax.broadcasted_iota(jnp.int32, (1, LANES), 1).astype(jnp.float32)
    first = lambda hit: jnp.min(jnp.where(hit, lane, float(LANES)), axis=-1, keepdims=True)
    is_grp = lane < G
    gmax = jnp.max(jnp.where(is_grp, lg, -jnp.inf), axis=-1, keepdims=True)
    ge = jnp.where(is_grp, jnp.exp(jnp.where(is_grp, lg, gmax) - gmax), 0.0)
    prob = ge / jnp.sum(ge, axis=-1, keepdims=True)
    grp_p = jnp.max(prob, axis=-1, keepdims=True)
    grp_i = first(is_grp & (prob == grp_p))
    lo = G + grp_i * EPG
    el = jnp.where((lane >= lo) & (lane < lo + EPG), lg, -jnp.inf)
    l1 = jnp.max(el, axis=-1, keepdims=True)
    i1 = first(el == l1)
    el = jnp.where(lane == i1, -jnp.inf, el)
    l2 = jnp.max(el, axis=-1, keepdims=True)
    i2 = first(el == l2)
    t = jnp.exp(l2 - l1)
    w1 = grp_p / (1.0 + t)
    id_ref[...] = jnp.where(lane == 0, i1 - G, jnp.where(lane == 1, i2 - G, 0.0)).astype(jnp.int32)
    gw_ref[...] = jnp.where(lane == 0, w1, jnp.where(lane == 1, w1 * t, 0.0))


def router(x2, sc, sh, w_r, b_r, seq, tm):
    T, D = x2.shape
    per = seq // tm
    mod = pl.BlockSpec((None, 1, D), lambda i: (i // per, 0, 0))
    narrow = pl.BlockSpec((tm, LANES), lambda i: (i, 0))
    return pl.pallas_call(
        _router_kernel,
        grid=(T // tm,),
        in_specs=[pl.BlockSpec((tm, D), lambda i: (i, 0)), mod, mod,
                  pl.BlockSpec((D, LANES), lambda i: (0, 0)), pl.BlockSpec((1, LANES), lambda i: (0, 0))],
        out_specs=[narrow, narrow, pl.BlockSpec((tm, D), lambda i: (i, 0))],
        out_shape=[jax.ShapeDtypeStruct((T, LANES), jnp.int32), jax.ShapeDtypeStruct((T, LANES), jnp.float32),
                   jax.ShapeDtypeStruct((T, D), jnp.float32)],
        compiler_params=_params(("parallel",)),
        name="router",
    )(x2, sc, sh, w_r, b_r)


def _experts_kernel(be_ref, nu_ref, x_ref, sw_ref, wg_ref, wu_ref, wd_ref, o_ref, wg_s, wu_s, wd_s):
    i = pl.program_id(0)
    used = i < nu_ref[0]
    new_expert = (i == 0) | (be_ref[i] != be_ref[jnp.maximum(i - 1, 0)])

    @pl.when(used & new_expert)
    def _():
        wg_s[...] = wg_ref[...].astype(wg_s.dtype)
        wu_s[...] = wu_ref[...].astype(wu_s.dtype)
        wd_s[...] = wd_ref[...].astype(wd_s.dtype)

    @pl.when(used)
    def _():
        x = x_ref[...].astype(jnp.bfloat16)
        gate = _dot(x, wg_s[...])
        hid = gate * _sigmoid(gate) * _dot(x, wu_s[...])
        o_ref[...] = _dot(hid.astype(jnp.bfloat16), wd_s[...]) * sw_ref[...]

    @pl.when(jnp.logical_not(used))
    def _():
        o_ref[...] = jnp.zeros_like(o_ref)


def experts(xs, slot_w, block_e, n_used, w_gate, w_up, w_down, layer):
    R, D = xs.shape
    F = w_gate.shape[-1]
    n_blocks = R // MOE_BLOCK
    grid_spec = pltpu.PrefetchScalarGridSpec(
        num_scalar_prefetch=2,
        grid=(n_blocks,),
        in_specs=[pl.BlockSpec((MOE_BLOCK, D), lambda i, be, nu: (i, 0)),
                  pl.BlockSpec((MOE_BLOCK, 1), lambda i, be, nu: (i, 0)),
                  pl.BlockSpec((None, None, D, F), lambda i, be, nu: (layer, be[i], 0, 0)),
                  pl.BlockSpec((None, None, D, F), lambda i, be, nu: (layer, be[i], 0, 0)),
                  pl.BlockSpec((None, None, F, D), lambda i, be, nu: (layer, be[i], 0, 0))],
        out_specs=pl.BlockSpec((MOE_BLOCK, D), lambda i, be, nu: (i, 0)),
        scratch_shapes=[pltpu.VMEM((D, F), jnp.bfloat16), pltpu.VMEM((D, F), jnp.bfloat16),
                        pltpu.VMEM((F, D), jnp.bfloat16)],
    )
    return pl.pallas_call(
        _experts_kernel,
        grid_spec=grid_spec,
        out_shape=jax.ShapeDtypeStruct((R, D), jnp.float32),
        compiler_params=_params(("arbitrary",)),
        name="experts",
    )(block_e, n_used, xs, slot_w, w_gate, w_up, w_down)


def _combine_ln_kernel(x_ref, f0_ref, f1_ref, g2_ref, lg_ref, lb_ref, o_ref, *, alpha):
    y = alpha * x_ref[...] + (1.0 + g2_ref[...]) * (f0_ref[...] + f1_ref[...])
    o_ref[...] = _layer_norm(y, lg_ref[...], lb_ref[...])


def combine_ln(x2, f0, f1, g2, ln_g, ln_b, seq, tm, alpha):
    T, D = x2.shape
    per = seq // tm
    tile = pl.BlockSpec((tm, D), lambda i: (i, 0))
    row = pl.BlockSpec((1, D), lambda i: (0, 0))
    return pl.pallas_call(
        functools.partial(_combine_ln_kernel, alpha=alpha),
        grid=(T // tm,),
        in_specs=[tile, tile, tile, pl.BlockSpec((None, 1, D), lambda i: (i // per, 0, 0)), row, row],
        out_specs=tile,
        out_shape=jax.ShapeDtypeStruct((T, D), jnp.float32),
        compiler_params=_params(("parallel",)),
        name="combine_ln",
    )(x2, f0, f1, g2, ln_g, ln_b)


def _route(expert_id, gate_w):
    T = expert_id.shape[0]
    E = MOE_GROUPS * MOE_EXPERTS_PER_GROUP
    A = T * MOE_TOPK
    n_blocks = (A + E * (MOE_BLOCK - 1) + MOE_BLOCK - 1) // MOE_BLOCK
    flat_e = expert_id.reshape(A)
    flat_w = gate_w.reshape(A)
    e_s, order = lax.sort_key_val(flat_e, jnp.arange(A, dtype=jnp.int32))
    bounds = jnp.searchsorted(e_s, jnp.arange(E + 1, dtype=jnp.int32), side='left').astype(jnp.int32)
    start = bounds[:E]
    counts = bounds[1:] - start
    padded = (counts + MOE_BLOCK - 1) // MOE_BLOCK * MOE_BLOCK
    pad_end = jnp.cumsum(padded)
    pad_start = pad_end - padded
    block_e = jnp.minimum(jnp.sum(pad_end[None, :] <= (jnp.arange(n_blocks) * MOE_BLOCK)[:, None], axis=1),
                          E - 1).astype(jnp.int32)
    blk_rank = jnp.arange(n_blocks, dtype=jnp.int32) * MOE_BLOCK - pad_start[block_e]
    rank = blk_rank[:, None] + jnp.arange(MOE_BLOCK, dtype=jnp.int32)[None, :]
    valid = (rank < counts[block_e][:, None]).reshape(-1)
    src = order[jnp.clip(start[block_e][:, None] + rank, 0, A - 1).reshape(-1)]
    spread = jnp.arange(n_blocks * MOE_BLOCK, dtype=jnp.int32) % T
    slot_tok = jnp.where(valid, src // MOE_TOPK, spread).astype(jnp.int32)
    slot_w = jnp.where(valid, flat_w[src], 0.0)
    pos = jnp.arange(A, dtype=jnp.int32)[None, :]
    in_e = (pos >= start[:, None]) & (pos < bounds[1:, None])
    dest = pos[0] + jnp.sum(jnp.where(in_e, (pad_start - start)[:, None], 0), axis=0).astype(jnp.int32)
    _, slot_of = lax.sort_key_val(order, dest)
    n_used = (pad_end[-1] // MOE_BLOCK).astype(jnp.int32).reshape(1)
    return slot_tok, slot_w, slot_of.reshape(T, MOE_TOPK), block_e, n_used


def kernel(x, c, rel_bias, w_in, p_a, p_b, p_c, w_o, rwkv_mu, rwkv_w0, rwkv_w_up, rwkv_a0, rwkv_a_up,
           rwkv_g_up, rwkv_k_k, rwkv_k_a, rwkv_r_k, rwkv_ln_g, rwkv_ln_b, rwkv_v0, rwkv_mv_down,
           rwkv_mv_up, w_ada, b_ada, ln1_g, ln1_b, ln2_g, ln2_b, router_grp_w, router_grp_b,
           router_exp_w, router_exp_b, exp_w_gate, exp_w_up, exp_w_down):
    B, S, D = x.shape
    depth = w_in.shape[0]
    T = B * S
    bf = jnp.bfloat16
    W = RWKV_HEADS * HEAD_DIM
    wa_w = MOBA_HEADS * HEAD_DIM
    wc_w = len(DIL_GROUPS) * DIL_HEADS_PER_GROUP * HEAD_DIM
    rw_cols = rwkv_mu.shape[-1]
    alpha = (2 * depth) ** 0.25
    off_c = 3 * wa_w
    off_b = off_c + 3 * wc_w
    off_g = off_b + rw_cols
    new_a = rw_cols
    new_c = new_a + 3 * wa_w
    new_g = new_c + 3 * wc_w
    w_in_p = jnp.concatenate([w_in[:, :, off_b:off_g], w_in[:, :, :off_b], w_in[:, :, off_g:]], axis=-1).astype(bf)
    p_a_b, p_b_b, p_c_b, w_o_b = p_a.astype(bf), p_b.astype(bf), p_c.astype(bf), w_o.astype(bf)

    bias_h = rel_bias.T.astype(jnp.float32)
    moba_tab = moba_bias_table(bias_h[:MOBA_HEADS], S // MOBA_BLOCK)
    dil_tab = dil_bias_table(bias_h[MOBA_HEADS:])

    c8 = jnp.zeros((8, D), jnp.float32).at[:B].set(c)
    mod = ada_mod(c8, w_ada, b_ada)[:, :B]

    hd_idx = jnp.arange(LANES) // HEAD_DIM
    bd = (hd_idx[:, None] == hd_idx[None, :]).astype(jnp.float32)
    zeros_w = jnp.zeros((RWKV_DECAY_LORA, W), jnp.float32)
    pad_lora = LANES - RWKV_MV_LORA

    x2 = x.reshape(T, D)
    v_first = None
    for l in range(depth):
        sh1, sc1, g1, sh2, sc2, g2 = [m.reshape(B, 1, D) for m in jnp.split(mod[l], 6, axis=-1)]
        proj = in_proj(x2, sc1, sh1, w_in_p, l, S, 1024, 1024)
        proj3 = proj.reshape(B, S, -1)
        o_a = moba_attention(proj3, moba_tab, new_a // LANES, (new_a + wa_w) // LANES,
                             (new_a + 2 * wa_w) // LANES).reshape(T, wa_w)
        gw = DIL_HEADS_PER_GROUP * HEAD_DIM
        o_c, lw_c = [], []
        for g in range(len(DIL_GROUPS)):
            og, lwg = dilated_attention(proj3, dil_tab, (new_c + g * gw) // LANES,
                                        (new_c + wc_w + g * gw) // LANES, (new_c + 2 * wc_w + g * gw) // LANES, g)
            o_c.append(og.reshape(T, gw))
            lw_c.append(lwg.reshape(T, gw))
        wa_up = jnp.concatenate([jnp.concatenate([rwkv_w_up[l], zeros_w], axis=1),
                                 jnp.concatenate([zeros_w, rwkv_a_up[l]], axis=1)], axis=0)
        res = None
        if l > 0:
            res = (v_first, rwkv_v0[l - 1][None],
                   jnp.pad(rwkv_mv_down[l - 1], ((0, 0), (0, pad_lora))),
                   jnp.pad(rwkv_mv_up[l - 1], ((0, pad_lora), (0, 0))))
        r_, lw_, k_, v_, kn_, b_, g_ = rwkv_prep(proj, S, 256, rwkv_mu[l][None], rwkv_w0[l][None],
                                                 rwkv_a0[l][None], wa_up, rwkv_g_up[l], rwkv_k_k[l][None],
                                                 rwkv_k_a[l][None], bd, res)
        if l == 0:
            v_first = v_
        rhat, y0, e_, gmat, dmat = rwkv_chunk(r_, lw_, k_, v_, kn_, b_, g_, rwkv_r_k[l][None], bd, B)
        o_b = rwkv_state(rhat, y0, g_, e_, gmat, dmat, rwkv_ln_g[l][None], rwkv_ln_b[l][None], bd)
        merged = merge_branches(o_a, o_b, o_c, lw_c, proj, new_g // 1024, p_a_b[l], p_b_b[l], p_c_b[l],
                                256, 1024)
        x2 = out_ln(merged, w_o_b[l], x2, g1, ln1_g[l][None], ln1_b[l][None], S, 256, alpha)
        w_r = jnp.zeros((D, LANES), jnp.float32)
        w_r = w_r.at[:, :MOE_GROUPS].set(router_grp_w[l]).at[:, MOE_GROUPS:MOE_GROUPS + router_exp_w.shape[-1]].set(
            router_exp_w[l])
        b_r = jnp.zeros((1, LANES), jnp.float32)
        b_r = b_r.at[0, :MOE_GROUPS].set(router_grp_b[l]).at[0, MOE_GROUPS:MOE_GROUPS + router_exp_b.shape[-1]].set(
            router_exp_b[l])
        ids, gws, h2 = router(x2, sc2, sh2, w_r, b_r, S, 512)
        slot_tok, slot_w, slot_of, block_e, n_used = _route(ids[:, :MOE_TOPK], gws[:, :MOE_TOPK])
        y = experts(h2[slot_tok], slot_w[:, None], block_e, n_used, exp_w_gate, exp_w_up, exp_w_down, l)
        x2 = combine_ln(x2, y[slot_of[:, 0]], y[slot_of[:, 1]], g2, ln2_g[l][None], ln2_b[l][None], S, 512, alpha)
    return x2.reshape(B, S, D)
```

```python
import functools
import math

import jax
import jax.numpy as jnp
import numpy as np
from jax import lax
from jax.experimental import pallas as pl
from jax.experimental.pallas import tpu as pltpu

HEAD_DIM = 64
LANES = 128
MOBA_HEADS = 12
MOBA_BLOCK = 256
MOBA_TOPK = 3
MOBA_STEP = 2
RWKV_HEADS = 12
RWKV_DECAY_LORA = 64
RWKV_A_LORA = 64
RWKV_MV_LORA = 32
RWKV_GATE_LORA = 128
RWKV_GN_EPS = 64e-5
RWKV_CHUNK = 64
DIL_GROUPS = ((128, 1), (512, 4), (2048, 16))
DIL_HEADS_PER_GROUP = 4
DIL_SPAN = 128
REL_BUCKETS = 32
REL_MAX_DISTANCE = 2048
MOE_GROUPS = 8
MOE_EXPERTS_PER_GROUP = 8
MOE_TOPK = 2
MOE_BLOCK = 256
LN_EPS = 1e-5
NEG = -1e30
LOG2E = math.log2(math.e)
LN2 = math.log(2.0)
VMEM_LIMIT = 56 * 1024 * 1024
HI = lax.Precision.HIGHEST


def _params(sem):
    return pltpu.CompilerParams(dimension_semantics=sem, vmem_limit_bytes=VMEM_LIMIT)


def _sigmoid(x):
    return 0.5 * jnp.tanh(0.5 * x) + 0.5


def _dot(a, b, precision=None):
    return jnp.dot(a, b, preferred_element_type=jnp.float32, precision=precision)


def _dot_nt(a, b, precision=None):
    return lax.dot_general(a, b, (((1,), (1,)), ((), ())), preferred_element_type=jnp.float32,
                           precision=precision)


def _dot_tn(a, b, precision=None):
    return lax.dot_general(a, b, (((0,), (0,)), ((), ())), preferred_element_type=jnp.float32,
                           precision=precision)


def _split2(x):
    hi = x.astype(jnp.bfloat16)
    return hi, (x - hi.astype(jnp.float32)).astype(jnp.bfloat16)


def _dot3(a, b, dims=((1,), (0,))):
    (ca,), (cb,) = dims
    ah, al = _split2(a)
    bh, bl = _split2(b)
    return lax.dot_general(jnp.concatenate([ah, ah, al], axis=ca), jnp.concatenate([bh, bl, bh], axis=cb),
                           (dims, ((), ())), preferred_element_type=jnp.float32)


def _dot_sel(x, sel):
    xh, xl = _split2(x)
    sb = sel.astype(jnp.bfloat16)
    return _dot(jnp.concatenate([xh, xl], axis=1), jnp.concatenate([sb, sb], axis=0))


_NT = ((1,), (1,))
_TN = ((0,), (0,))


def _layer_norm(y, g, b):
    mu = jnp.mean(y, axis=-1, keepdims=True)
    d = y - mu
    var = jnp.mean(d * d, axis=-1, keepdims=True)
    return d * lax.rsqrt(var + LN_EPS) * g + b


def _ada_kernel(c_ref, w_ref, b_ref, o_ref):
    c = c_ref[...]
    cond = c * _sigmoid(c)
    o_ref[...] = _dot(cond, w_ref[...]) + b_ref[...]


def ada_mod(c8, w_ada, b_ada):
    L, D, N = w_ada.shape
    tn = 1024
    return pl.pallas_call(
        _ada_kernel,
        grid=(L, N // tn),
        in_specs=[pl.BlockSpec((8, D), lambda l, j: (0, 0)),
                  pl.BlockSpec((None, D, tn), lambda l, j: (l, 0, j)),
                  pl.BlockSpec((None, 1, tn), lambda l, j: (l, 0, j))],
        out_specs=pl.BlockSpec((None, 8, tn), lambda l, j: (l, 0, j)),
        out_shape=jax.ShapeDtypeStruct((L, 8, N), jnp.float32),
        compiler_params=_params(("parallel", "parallel")),
        name="ada_mod",
    )(c8, w_ada, b_ada.reshape(L, 1, N))


def _in_proj_kernel(x_ref, sc_ref, sh_ref, w_ref, o_ref, h_ref):
    @pl.when(pl.program_id(1) == 0)
    def _():
        h_ref[...] = (x_ref[...] * (1.0 + sc_ref[...]) + sh_ref[...]).astype(h_ref.dtype)

    o_ref[...] = _dot(h_ref[...], w_ref[...])


def in_proj(x2, sc, sh, w, layer, seq, tm, tn):
    T, D = x2.shape
    N = w.shape[-1]
    per = seq // tm
    return pl.pallas_call(
        _in_proj_kernel,
        grid=(T // tm, N // tn),
        in_specs=[pl.BlockSpec((tm, D), lambda i, j: (i, 0)),
                  pl.BlockSpec((None, 1, D), lambda i, j: (i // per, 0, 0)),
                  pl.BlockSpec((None, 1, D), lambda i, j: (i // per, 0, 0)),
                  pl.BlockSpec((None, D, tn), lambda i, j: (layer, 0, j))],
        out_specs=pl.BlockSpec((tm, tn), lambda i, j: (i, j)),
        out_shape=jax.ShapeDtypeStruct((T, N), jnp.float32),
        scratch_shapes=[pltpu.VMEM((tm, D), jnp.bfloat16)],
        compiler_params=_params(("parallel", "arbitrary")),
        name="in_proj",
    )(x2, sc, sh, w)


def _t5_bucket(dist):
    n = jnp.maximum(dist, 0)
    max_exact = REL_BUCKETS // 2
    nf = jnp.maximum(n, 1).astype(jnp.float32)
    large = max_exact + (jnp.log(nf / max_exact) / math.log(REL_MAX_DISTANCE / max_exact)
                         * (REL_BUCKETS - max_exact)).astype(jnp.int32)
    large = jnp.minimum(large, REL_BUCKETS - 1)
    return jnp.where(n < max_exact, n, large)


def _moba_n_delta(nb):
    last_start = 1
    d = np.arange(1, nb * MOBA_BLOCK + 1)
    large = 16 + (np.log(d / 16.0) / math.log(REL_MAX_DISTANCE / 16.0) * 16).astype(np.int64)
    bucket = np.where(d < 16, d, np.minimum(large, REL_BUCKETS - 1))
    last_start = int(d[bucket < REL_BUCKETS - 1].max()) + 1 if (bucket < REL_BUCKETS - 1).any() else 1
    delta = 1
    while delta * MOBA_BLOCK - (MOBA_BLOCK - 1) < last_start + 2:
        delta += 1
    return min(delta + 1, nb)


def moba_bias_table(bias_a, nb):
    nd = _moba_n_delta(nb)
    key = jnp.arange(MOBA_BLOCK)[:, None]
    qry = jnp.arange(MOBA_BLOCK)[None, :]
    dist = jnp.arange(nd)[:, None, None] * MOBA_BLOCK + (qry - key)[None]
    bucket = _t5_bucket(dist)[None]
    tab = jnp.zeros((bias_a.shape[0],) + dist.shape, jnp.float32)
    for b in range(REL_BUCKETS):
        tab = jnp.where(bucket == b, bias_a[:, b][:, None, None, None], tab)
    return jnp.where((dist >= 0)[None], tab * LOG2E, NEG)


def dil_bias_table(bias_c):
    span = DIL_SPAN
    rel = span + jnp.arange(span)[:, None] - jnp.arange(2 * span)[None, :]
    valid = (rel >= 0) & (rel <= span)
    tabs = []
    for g, (_, dilation) in enumerate(DIL_GROUPS):
        bh = bias_c[g * DIL_HEADS_PER_GROUP:(g + 1) * DIL_HEADS_PER_GROUP]
        tabs.append(jnp.where(valid[None], bh[:, _t5_bucket(rel * dilation)] * LOG2E, NEG))
    return jnp.stack(tabs)


def _moba_kernel(q_ref, k_ref, v_ref, bias_ref, o_ref,
                 kaug_ref, vt_ref, km_ref, acc_ref, *, nb, n_delta):
    qb = pl.program_id(2)
    bs = MOBA_BLOCK
    S = nb * bs
    half = bs // 2
    lane = lax.broadcasted_iota(jnp.int32, (1, LANES), 1)
    f32 = jnp.float32

    @pl.when(qb == 0)
    def _():
        k = k_ref[...]
        rowblk = lax.broadcasted_iota(jnp.int32, (S, LANES), 0) // bs
        lanes = lax.broadcasted_iota(jnp.int32, (S, LANES), 1)
        km_ref[...] = jnp.mean(k.reshape(nb, bs, LANES), axis=1)
        kaug_ref[0] = jnp.where(lanes < HEAD_DIM, k, (lanes - HEAD_DIM == rowblk).astype(f32)).astype(kaug_ref.dtype)
        kaug_ref[1] = jnp.where(lanes >= HEAD_DIM, k, (lanes == rowblk).astype(f32)).astype(kaug_ref.dtype)
        for i in range(nb):
            vt_ref[i] = v_ref[i * bs:(i + 1) * bs, :].T.astype(vt_ref.dtype)

    q_t = q_ref[...].T
    scale = HEAD_DIM ** -0.5 * LOG2E
    blk = lax.broadcasted_iota(jnp.int32, (nb, 1), 0)
    past = blk < qb
    rhs = []
    for t in range(2):
        hm = (lane >= t * HEAD_DIM) & (lane < (t + 1) * HEAD_DIM)
        gate = _dot(jnp.where(hm, km_ref[...], 0.0), q_t, precision=HI)
        gate = jnp.where(past, gate, -jnp.inf)
        cnt = jnp.zeros((nb, bs), jnp.int32)
        for m in range(nb):
            gm = gate[m:m + 1, :]
            ahead = (gm > gate) | ((gm == gate) & (m < blk))
            cnt = cnt + ahead.astype(jnp.int32)
        chosen = (past & (cnt < MOBA_TOPK)) | (blk == qb)
        pen = jnp.where(chosen, 0.0, NEG)
        qh = q_t[t * HEAD_DIM:(t + 1) * HEAD_DIM, :] * scale
        if t == 0:
            parts = [qh, pen, jnp.zeros((LANES - HEAD_DIM - nb, bs), f32)]
        else:
            parts = [pen, jnp.zeros((HEAD_DIM - nb, bs), f32), qh]
        rhs.append(jnp.concatenate(parts, axis=0).astype(jnp.bfloat16))
    acc_ref[...] = jnp.zeros((LANES, bs), f32)

    chains = [(t, hq) for t in range(2) for hq in range(2)]
    cols = [slice(hq * half, (hq + 1) * half) for _, hq in chains]

    def step(blocks, carry):
        rows = [pl.multiple_of(n * bs, bs) for n in blocks]
        delta = [jnp.clip(qb - n, 0, n_delta - 1) for n in blocks]
        s = [[_dot(kaug_ref[t, pl.ds(rows[j], bs), :], rhs[t][:, cols[c]]) + bias_ref[t, delta[j], :, cols[c]]
              for j in range(len(blocks))] for c, (t, _) in enumerate(chains)]
        m_new = [functools.reduce(jnp.maximum, [carry[c][0]] + [jnp.max(x, axis=0, keepdims=True) for x in s[c]])
                 for c in range(4)]
        alpha = [jnp.exp2(carry[c][0] - m_new[c]) for c in range(4)]
        p = [[jnp.exp2(x - m_new[c]) for x in s[c]] for c in range(4)]
        l_new = [alpha[c] * carry[c][1] + sum(jnp.sum(x, axis=0, keepdims=True) for x in p[c]) for c in range(4)]
        pv = [sum(_dot(vt_ref[n, t * HEAD_DIM:(t + 1) * HEAD_DIM, :], p[c][j].astype(jnp.bfloat16))
                  for j, n in enumerate(blocks)) for c, (t, _) in enumerate(chains)]
        pieces = [alpha[c] * acc_ref[t * HEAD_DIM:(t + 1) * HEAD_DIM, cols[c]] + pv[c]
                  for c, (t, _) in enumerate(chains)]
        acc_ref[...] = jnp.concatenate([jnp.concatenate(pieces[:2], axis=1),
                                        jnp.concatenate(pieces[2:], axis=1)], axis=0)
        return tuple((m_new[c], l_new[c]) for c in range(4))

    init = tuple((jnp.full((1, half), NEG, f32), jnp.zeros((1, half), f32)) for _ in range(4))
    fin = lax.fori_loop(0, qb // MOBA_STEP + 1,
                        lambda i, c: step([MOBA_STEP * i + j for j in range(MOBA_STEP)], c), init)
    l_all = jnp.concatenate([jnp.broadcast_to(jnp.concatenate([fin[2 * t][1], fin[2 * t + 1][1]], axis=1),
                                              (HEAD_DIM, bs)) for t in range(2)], axis=0)
    o_ref[...] = (acc_ref[...] / l_all).T


def moba_attention(proj3, bias_tab, q_off, k_off, v_off):
    B, S, _ = proj3.shape
    nb = S // MOBA_BLOCK
    n_delta = bias_tab.shape[1]
    pairs = MOBA_HEADS // 2
    kern = functools.partial(_moba_kernel, nb=nb, n_delta=n_delta)
    return pl.pallas_call(
        kern,
        grid=(pairs, B, nb),
        in_specs=[pl.BlockSpec((None, MOBA_BLOCK, LANES), lambda h, b, i: (b, i, q_off + h)),
                  pl.BlockSpec((None, S, LANES), lambda h, b, i: (b, 0, k_off + h)),
                  pl.BlockSpec((None, S, LANES), lambda h, b, i: (b, 0, v_off + h)),
                  pl.BlockSpec((2, n_delta, MOBA_BLOCK, MOBA_BLOCK), lambda h, b, i: (h, 0, 0, 0))],
        out_specs=pl.BlockSpec((None, MOBA_BLOCK, LANES), lambda h, b, i: (b, i, h)),
        out_shape=jax.ShapeDtypeStruct((B, S, pairs * LANES), jnp.float32),
        scratch_shapes=[pltpu.VMEM((2, S, LANES), jnp.bfloat16),
                        pltpu.VMEM((nb, LANES, MOBA_BLOCK), jnp.bfloat16),
                        pltpu.VMEM((nb, LANES), jnp.float32),
                        pltpu.VMEM((LANES, MOBA_BLOCK), jnp.float32)],
        compiler_params=_params(("parallel", "parallel", "arbitrary")),
        name="moba",
    )(proj3, proj3, proj3, bias_tab)


def _dil_kernel(q_ref, k_ref, v_ref, bias_ref, o_ref, lw_ref, *, dilation, seq):
    span = DIL_SPAN
    bps = seq // dilation // span
    scale = HEAD_DIM ** -0.5 * LOG2E
    lane = lax.broadcasted_iota(jnp.int32, (1, LANES), 1)
    col = lax.broadcasted_iota(jnp.int32, (1, 2 * span), 1)
    hm = [lane < HEAD_DIM, lane >= HEAD_DIM]
    unroll = 2

    def body(it, carry):
        blocks = []
        for u in range(unroll):
            j = it * unroll + u
            r = j // bps
            i = j - r * bps
            start = r + i * (span * dilation)
            prev = jnp.maximum(start - span * dilation, r)
            rows = lambda s0: pl.ds(s0, span, stride=dilation)
            q = q_ref[rows(start), :] * scale
            kcat = jnp.concatenate([k_ref[rows(prev), :], k_ref[rows(start), :]], axis=0).astype(jnp.bfloat16)
            vcat = jnp.concatenate([v_ref[rows(prev), :], v_ref[rows(start), :]], axis=0).astype(jnp.bfloat16)
            blocks.append((start, q, kcat, vcat, (i == 0) & (col < span)))
        chains = [(u, t) for u in range(unroll) for t in range(2)]
        s = [jnp.where(blocks[u][4], NEG,
                       _dot_nt(jnp.where(hm[t], blocks[u][1], 0.0).astype(jnp.bfloat16), blocks[u][2]) + bias_ref[t])
             for u, t in chains]
        m = [jnp.max(x, axis=-1, keepdims=True) for x in s]
        e = [jnp.exp2(x - mm) for x, mm in zip(s, m)]
        l = [jnp.sum(x, axis=-1, keepdims=True) for x in e]
        pv = [_dot(x.astype(jnp.bfloat16), blocks[u][3]) for x, (u, _) in zip(e, chains)]
        for u in range(unroll):
            a, b = 2 * u, 2 * u + 1
            rows = pl.ds(blocks[u][0], span, stride=dilation)
            o_ref[rows, :] = jnp.where(hm[0], pv[a] / l[a], pv[b] / l[b])
            lw_ref[rows, :] = jnp.where(hm[0], m[a] * LN2 + jnp.log(l[a]), m[b] * LN2 + jnp.log(l[b]))
        return carry

    lax.fori_loop(0, seq // span // unroll, body, 0)


def dilated_attention(proj3, bias_tab, q_off, k_off, v_off, group):
    B, S, _ = proj3.shape
    dilation = DIL_GROUPS[group][1]
    pairs = DIL_HEADS_PER_GROUP // 2
    slab = lambda off: pl.BlockSpec((None, S, LANES), lambda b, h: (b, 0, off + h))
    out = jax.ShapeDtypeStruct((B, S, pairs * LANES), jnp.float32)
    return pl.pallas_call(
        functools.partial(_dil_kernel, dilation=dilation, seq=S),
        grid=(B, pairs),
        in_specs=[slab(q_off), slab(k_off), slab(v_off),
                  pl.BlockSpec((None, 2, DIL_SPAN, 2 * DIL_SPAN), lambda b, h: (group, h, 0, 0))],
        out_specs=[slab(0), slab(0)],
        out_shape=[out, out],
        compiler_params=_params(("parallel", "parallel")),
        name="dilated",
    )(proj3, proj3, proj3, bias_tab)


def _rwkv_prep_kernel(*refs, width, has_res, rows_per_seq):
    if has_res:
        (z_ref, zl_ref, mu_ref, w0_ref, a0_ref, wa_ref, gup_ref, kk_ref, ka_ref, bd_ref,
         vf_ref, v0_ref, mvd_ref, mvu_ref,
         r_o, lw_o, k_o, v_o, kn_o, b_o, g_o) = refs
    else:
        (z_ref, zl_ref, mu_ref, w0_ref, a0_ref, wa_ref, gup_ref, kk_ref, ka_ref, bd_ref,
         r_o, lw_o, k_o, v_o, kn_o, b_o, g_o) = refs
    i = pl.program_id(0)
    W = width
    z = z_ref[...]
    tm = z.shape[0]
    row = lax.broadcasted_iota(jnp.int32, (tm, 1), 0)
    seq_start = (i % rows_per_seq) == 0
    last = jnp.where(seq_start, 0.0, zl_ref[7:8, :])
    zp = jnp.where(row == 0, last, pltpu.roll(z, 1, 0))
    zf = z + mu_ref[...] * (zp - z)
    lora = zf[:, 3 * W:3 * W + LANES]
    lane = lax.broadcasted_iota(jnp.int32, (1, LANES), 1)
    lora = jnp.where(lane < RWKV_DECAY_LORA, jnp.tanh(lora), lora)
    wa = _dot(lora, wa_ref[...], precision=HI)
    g = _dot(_sigmoid(zf[:, 3 * W + LANES:3 * W + 2 * LANES]), gup_ref[...], precision=HI)
    g_o[...] = g
    v_all = zf[:, 2 * W:3 * W]
    if has_res:
        mix = _dot(_dot(v_all, mvd_ref[...], precision=HI), mvu_ref[...], precision=HI)
    for c in range(W // LANES):
        sl = slice(c * LANES, (c + 1) * LANES)
        x = w0_ref[:, sl] + wa[:, sl]
        sp = jnp.maximum(-x, 0.0) + jnp.log(1.0 + jnp.exp(-jnp.abs(x)))
        lw_o[:, sl] = -jnp.exp(-sp - 0.5)
        a = _sigmoid(a0_ref[:, sl] + wa[:, W + c * LANES:W + (c + 1) * LANES])
        r_o[:, sl] = zf[:, sl]
        k = zf[:, W + c * LANES:W + (c + 1) * LANES]
        v = v_all[:, sl]
        if has_res:
            v = v + (vf_ref[:, sl] - v) * _sigmoid(v0_ref[:, sl] + mix[:, sl])
        v_o[:, sl] = v
        kk = k * kk_ref[:, sl]
        ss = _dot_sel(kk * kk, bd_ref[...])
        kn = kk / jnp.maximum(jnp.sqrt(ss), 1e-12)
        kn_o[:, sl] = kn
        b_o[:, sl] = kn * a
        k_o[:, sl] = k * (1.0 + (a - 1.0) * ka_ref[:, sl])


def rwkv_prep(proj, seq, tm, mu, w0, a0, wa_up, g_up, k_k, k_a, bd, res):
    T = proj.shape[0]
    W = RWKV_HEADS * HEAD_DIM
    cols = mu.shape[-1]
    row1 = lambda n: pl.BlockSpec((1, n), lambda i: (0, 0))
    full = lambda a: pl.BlockSpec(a.shape, lambda i: (0, 0))
    tile = pl.BlockSpec((tm, W), lambda i: (i, 0))
    in_specs = [pl.BlockSpec((tm, cols), lambda i: (i, 0)),
                pl.BlockSpec((8, cols), lambda i: (jnp.maximum(i * (tm // 8) - 1, 0), 0)),
                row1(cols), row1(W), row1(W), full(wa_up), full(g_up), row1(W), row1(W), full(bd)]
    args = [proj, proj, mu, w0, a0, wa_up, g_up, k_k, k_a, bd]
    if res is not None:
        v_first, v0, mvd, mvu = res
        in_specs += [tile, row1(W), full(mvd), full(mvu)]
        args += [v_first, v0, mvd, mvu]
    out = jax.ShapeDtypeStruct((T, W), jnp.float32)
    kern = functools.partial(_rwkv_prep_kernel, width=W, has_res=res is not None, rows_per_seq=seq // tm)
    return pl.pallas_call(
        kern,
        grid=(T // tm,),
        in_specs=in_specs,
        out_specs=[tile] * 7,
        out_shape=[out] * 7,
        compiler_params=_params(("parallel",)),
        name="rwkv_prep",
    )(*args)


def _stack_heads(x, lane):
    return jnp.concatenate([jnp.where(lane < HEAD_DIM, x, 0.0), jnp.where(lane >= HEAD_DIM, x, 0.0)], axis=0)


def _rwkv_chunk_kernel(r_ref, lw_ref, k_ref, v_ref, kn_ref, b_ref, g_ref, rk_ref, bd_ref,
                       rhat_ref, y0_ref, e_ref, g_out_ref, d_out_ref, *, chunks):
    C = RWKV_CHUNK
    C2, C4 = 2 * C, 4 * C
    f32 = jnp.float32
    ti = lax.broadcasted_iota(jnp.int32, (C, C), 0)
    si = lax.broadcasted_iota(jnp.int32, (C, C), 1)
    tri = (ti >= si).astype(f32)
    lane = lax.broadcasted_iota(jnp.int32, (1, LANES), 1)
    rho = lax.broadcasted_iota(jnp.int32, (C4, C4), 0)
    sig = lax.broadcasted_iota(jnp.int32, (C4, C4), 1)
    keep = jnp.where(rho >= C2, rho & (C - 1), (rho & (C - 1)) - 1) >= (sig & (C - 1))
    eye2 = (lax.broadcasted_iota(jnp.int32, (C2, C2), 0) == lax.broadcasted_iota(jnp.int32, (C2, C2), 1)).astype(f32)
    eye_l = (lax.broadcasted_iota(jnp.int32, (LANES, LANES), 0)
             == lax.broadcasted_iota(jnp.int32, (LANES, LANES), 1)).astype(f32)
    zeros2 = jnp.zeros((C2, LANES), f32)
    ch = range(chunks)
    rows = [slice(c * C, (c + 1) * C) for c in ch]
    cum = [_dot(tri, lw_ref[rows[c], :], precision=HI) for c in ch]
    st = []
    for c in ch:
        r, lw, k, v = r_ref[rows[c], :], lw_ref[rows[c], :], k_ref[rows[c], :], v_ref[rows[c], :]
        kn, bb = kn_ref[rows[c], :], b_ref[rows[c], :]
        cum_last = cum[c][C - 1:C, :]
        e_out = jnp.exp(-cum[c])
        e_tail = jnp.exp(cum_last - cum[c])
        r_t = r * jnp.exp(cum[c])
        st.append(dict(
            r_t=r_t, decay=jnp.exp(cum_last),
            a2=_stack_heads(-kn * jnp.exp(cum[c] - lw), lane), r2=_stack_heads(r_t, lane),
            b2=_stack_heads(bb * e_out, lane), k2=_stack_heads(k * e_out, lane), v2=_stack_heads(v, lane),
            bh2=_stack_heads(bb * e_tail, lane), kh2=_stack_heads(k * e_tail, lane),
            e=_dot_sel(r * k * rk_ref[...], bd_ref[...]) * v * g_ref[rows[c], :]))
    quad = [jnp.where(keep, _dot3(jnp.concatenate([s["a2"], s["r2"]], axis=0),
                                  jnp.concatenate([s["b2"], s["k2"]], axis=0), _NT), 0.0) for s in st]
    pw = [q[:C2, :C2] for q in quad]
    t_inv = [eye2 + p for p in pw]
    x = [_dot3(q[:C2, C2:], s["v2"]) for q, s in zip(quad, st)]
    for level in range(int(math.log2(C)) - 1):
        mm = _dot3 if level < 2 else (lambda a, b: _dot(a.astype(jnp.bfloat16), b.astype(jnp.bfloat16)))
        pw = [mm(p, p) for p in pw]
        t_inv = [t + mm(t, p) for t, p in zip(t_inv, pw)]
    au = [_dot3(t, jnp.concatenate([s["a2"], xx], axis=1)) for t, s, xx in zip(t_inv, st, x)]
    my = [_dot3(q[C2:, :], jnp.concatenate([a, jnp.concatenate([zeros2, s["v2"]], axis=1)], axis=0))
          for q, a, s in zip(quad, au, st)]
    gm = [eye_l * s["decay"] + _dot3(a[:, :LANES], s["bh2"], _TN) for a, s in zip(au, st)]
    dm = [_dot3(jnp.concatenate([a[:, LANES:], s["v2"]], axis=0),
                jnp.concatenate([s["bh2"], s["kh2"]], axis=0), _TN) for a, s in zip(au, st)]
    rhat_ref[...] = jnp.concatenate([s["r_t"] + m[:C, :LANES] + m[C:, :LANES] for s, m in zip(st, my)], axis=0)
    y0_ref[...] = jnp.concatenate([m[:C, LANES:] + m[C:, LANES:] for m in my], axis=0)
    e_ref[...] = jnp.concatenate([s["e"] for s in st], axis=0)
    g_out_ref[...] = jnp.stack(gm)
    d_out_ref[...] = jnp.stack(dm)


def rwkv_chunk(r, lw, k, v, kn, b, g, r_k, bd, batch, chunks=4):
    T, W = r.shape
    S = T // batch
    C = RWKV_CHUNK
    nc = S // C
    pairs = W // LANES
    steps = nc // chunks
    tile = pl.BlockSpec((chunks * C, LANES), lambda bi, h, c: (bi * steps + c, h))
    mat = pl.BlockSpec((None, None, chunks, LANES, LANES), lambda bi, h, c: (bi, h, c, 0, 0))
    tw = jax.ShapeDtypeStruct((T, W), jnp.float32)
    gd = jax.ShapeDtypeStruct((batch, pairs, nc, LANES, LANES), jnp.float32)
    return pl.pallas_call(
        functools.partial(_rwkv_chunk_kernel, chunks=chunks),
        grid=(batch, pairs, steps),
        in_specs=[tile] * 7 + [pl.BlockSpec((1, LANES), lambda bi, h, c: (0, h)),
                               pl.BlockSpec((LANES, LANES), lambda bi, h, c: (0, 0))],
        out_specs=[tile, tile, tile, mat, mat],
        out_shape=[tw, tw, tw, gd, gd],
        compiler_params=_params(("parallel", "parallel", "parallel")),
        name="rwkv_chunk",
    )(r, lw, k, v, kn, b, g, r_k, bd)


def _rwkv_state_kernel(rhat_ref, y0_ref, g_ref, e_ref, gm_ref, dm_ref, lng_ref, lnb_ref, bd_ref,
                       o_ref, state_ref, *, chunks, group):
    C = RWKV_CHUNK

    @pl.when(pl.program_id(2) == 0)
    def _():
        state_ref[...] = jnp.zeros_like(state_ref)

    bd = bd_ref[...]
    inv_n = 1.0 / HEAD_DIM

    def body(c, carry):
        rows = pl.ds(pl.multiple_of(c * C, C), C)
        pairs = range(group)
        state = [state_ref[p] for p in pairs]
        y = [_dot3(rhat_ref[rows, p * LANES:(p + 1) * LANES], state[p], _NT) for p in pairs]
        new_state = [_dot3(state[p], gm_ref[p, c]) + dm_ref[p, c] for p in pairs]
        y = jnp.concatenate(y, axis=1) + y0_ref[rows, :]
        state_ref[...] = jnp.stack(new_state)
        mean = jnp.concatenate([_dot_sel(y[:, p * LANES:(p + 1) * LANES], bd) for p in pairs], axis=1) * inv_n
        d = y - mean
        dd = d * d
        var = jnp.concatenate([_dot_sel(dd[:, p * LANES:(p + 1) * LANES], bd) for p in pairs], axis=1) * inv_n
        yn = d * lax.rsqrt(var + RWKV_GN_EPS) * lng_ref[...] + lnb_ref[...]
        o_ref[rows, :] = yn * g_ref[rows, :] + e_ref[rows, :]
        return carry

    lax.fori_loop(0, chunks, body, 0)


def rwkv_state(rhat, y0, g, e, gmat, dmat, ln_g, ln_b, bd, chunks=16, group=3):
    T, W = rhat.shape
    batch, pairs, nc = gmat.shape[:3]
    C = RWKV_CHUNK
    steps = nc // chunks
    gw = group * LANES
    tile = pl.BlockSpec((chunks * C, gw), lambda bi, h, c: (bi * steps + c, h))
    mat = pl.BlockSpec((None, group, chunks, LANES, LANES), lambda bi, h, c: (bi, h, c, 0, 0))
    row = pl.BlockSpec((1, gw), lambda bi, h, c: (0, h))
    return pl.pallas_call(
        functools.partial(_rwkv_state_kernel, chunks=chunks, group=group),
        grid=(batch, pairs // group, steps),
        in_specs=[tile] * 4 + [mat, mat, row, row, pl.BlockSpec((LANES, LANES), lambda bi, h, c: (0, 0))],
        out_specs=tile,
        out_shape=jax.ShapeDtypeStruct((T, W), jnp.float32),
        scratch_shapes=[pltpu.VMEM((group, LANES, LANES), jnp.float32)],
        compiler_params=_params(("parallel", "parallel", "arbitrary")),
        name="rwkv_state",
    )(rhat, y0, g, e, gmat, dmat, ln_g, ln_b, bd)


def _merge_kernel(oa_ref, ob_ref, oc0_ref, oc1_ref, oc2_ref, lw0_ref, lw1_ref, lw2_ref, ga_ref, gb_ref, gc_ref,
                  pa_ref, pb_ref, pc_ref, o_ref):
    lw = [lw0_ref[...], lw1_ref[...], lw2_ref[...]]
    mx = jnp.maximum(jnp.maximum(lw[0], lw[1]), lw[2])
    wgt = [jnp.exp(x - mx) for x in lw]
    o_c = ((wgt[0] * oc0_ref[...] + wgt[1] * oc1_ref[...] + wgt[2] * oc2_ref[...])
           / (wgt[0] + wgt[1] + wgt[2]))
    bf = jnp.bfloat16
    merged = (_sigmoid(ga_ref[...]) * _dot(oa_ref[...].astype(bf), pa_ref[...])
              + _sigmoid(gb_ref[...]) * _dot(ob_ref[...].astype(bf), pb_ref[...])
              + _sigmoid(gc_ref[...]) * _dot(o_c.astype(bf), pc_ref[...]))
    o_ref[...] = merged.astype(o_ref.dtype)


def merge_branches(o_a, o_b, o_c, lw_c, proj, gate_off, p_a, p_b, p_c, tm, tn):
    T = o_a.shape[0]
    D = p_a.shape[1]
    nj = D // tn
    grp = pl.BlockSpec((tm, o_c[0].shape[1]), lambda i, j: (i, 0))
    gate = lambda n: pl.BlockSpec((tm, tn), lambda i, j: (i, gate_off + n * nj + j))
    wspec = lambda a: pl.BlockSpec((a.shape[0], tn), lambda i, j: (0, j))
    return pl.pallas_call(
        _merge_kernel,
        grid=(T // tm, nj),
        in_specs=[pl.BlockSpec((tm, o_a.shape[1]), lambda i, j: (i, 0)),
                  pl.BlockSpec((tm, o_b.shape[1]), lambda i, j: (i, 0)),
                  grp, grp, grp, grp, grp, grp, gate(0), gate(1), gate(2), wspec(p_a), wspec(p_b), wspec(p_c)],
        out_specs=pl.BlockSpec((tm, tn), lambda i, j: (i, j)),
        out_shape=jax.ShapeDtypeStruct((T, D), jnp.bfloat16),
        compiler_params=_params(("parallel", "arbitrary")),
        name="merge",
    )(o_a, o_b, *o_c, *lw_c, proj, proj, proj, p_a, p_b, p_c)


def _out_ln_kernel(mg_ref, w_ref, x_ref, g1_ref, lg_ref, lb_ref, o_ref, *, alpha):
    mix = _dot(mg_ref[...], w_ref[...])
    y = alpha * x_ref[...] + (1.0 + g1_ref[...]) * mix
    o_ref[...] = _layer_norm(y, lg_ref[...], lb_ref[...])


def out_ln(merged, w_o, x2, g1, ln_g, ln_b, seq, tm, alpha):
    T, D = x2.shape
    per = seq // tm
    tile = pl.BlockSpec((tm, D), lambda i: (i, 0))
    row = pl.BlockSpec((1, D), lambda i: (0, 0))
    return pl.pallas_call(
        functools.partial(_out_ln_kernel, alpha=alpha),
        grid=(T // tm,),
        in_specs=[tile, pl.BlockSpec((D, D), lambda i: (0, 0)), tile,
                  pl.BlockSpec((None, 1, D), lambda i: (i // per, 0, 0)), row, row],
        out_specs=tile,
        out_shape=jax.ShapeDtypeStruct((T, D), jnp.float32),
        compiler_params=_params(("parallel",)),
        name="out_ln",
    )(merged, w_o, x2, g1, ln_g, ln_b)


def _router_kernel(x_ref, sc_ref, sh_ref, w_ref, b_ref, id_ref, gw_ref, h_ref):
    h = x_ref[...] * (1.0 + sc_ref[...]) + sh_ref[...]
    h_ref[...] = h.astype(h_ref.dtype)
    lg = _dot(h, w_ref[...], precision=HI) + b_ref[...]
    G, EPG = MOE_GROUPS, MOE_EXPERTS_PER_GROUP
    lane = lax.broadcasted_iota(jnp.int32, (1, LANES), 1).astype(jnp.float32)
    first = lambda hit: jnp.min(jnp.where(hit, lane, float(LANES)), axis=-1, keepdims=True)
    is_grp = lane < G
    gmax = jnp.max(jnp.where(is_grp, lg, -jnp.inf), axis=-1, keepdims=True)
    ge = jnp.where(is_grp, jnp.exp(jnp.where(is_grp, lg, gmax) - gmax), 0.0)
    prob = ge / jnp.sum(ge, axis=-1, keepdims=True)
    grp_p = jnp.max(prob, axis=-1, keepdims=True)
    grp_i = first(is_grp & (prob == grp_p))
    lo = G + grp_i * EPG
    el = jnp.where((lane >= lo) & (lane < lo + EPG), lg, -jnp.inf)
    l1 = jnp.max(el, axis=-1, keepdims=True)
    i1 = first(el == l1)
    el = jnp.where(lane == i1, -jnp.inf, el)
    l2 = jnp.max(el, axis=-1, keepdims=True)
    i2 = first(el == l2)
    t = jnp.exp(l2 - l1)
    w1 = grp_p / (1.0 + t)
    id_ref[...] = jnp.where(lane == 0, i1 - G, jnp.where(lane == 1, i2 - G, 0.0)).astype(jnp.int32)
    gw_ref[...] = jnp.where(lane == 0, w1, jnp.where(lane == 1, w1 * t, 0.0))


def router(x2, sc, sh, w_r, b_r, seq, tm):
    T, D = x2.shape
    per = seq // tm
    mod = pl.BlockSpec((None, 1, D), lambda i: (i // per, 0, 0))
    narrow = pl.BlockSpec((tm, LANES), lambda i: (i, 0))
    return pl.pallas_call(
        _router_kernel,
        grid=(T // tm,),
        in_specs=[pl.BlockSpec((tm, D), lambda i: (i, 0)), mod, mod,
                  pl.BlockSpec((D, LANES), lambda i: (0, 0)), pl.BlockSpec((1, LANES), lambda i: (0, 0))],
        out_specs=[narrow, narrow, pl.BlockSpec((tm, D), lambda i: (i, 0))],
        out_shape=[jax.ShapeDtypeStruct((T, LANES), jnp.int32), jax.ShapeDtypeStruct((T, LANES), jnp.float32),
                   jax.ShapeDtypeStruct((T, D), jnp.float32)],
        compiler_params=_params(("parallel",)),
        name="router",
    )(x2, sc, sh, w_r, b_r)


def _experts_kernel(be_ref, nu_ref, x_ref, sw_ref, wg_ref, wu_ref, wd_ref, o_ref, wg_s, wu_s, wd_s):
    i = pl.program_id(0)
    used = i < nu_ref[0]
    new_expert = (i == 0) | (be_ref[i] != be_ref[jnp.maximum(i - 1, 0)])

    @pl.when(used & new_expert)
    def _():
        wg_s[...] = wg_ref[...].astype(wg_s.dtype)
        wu_s[...] = wu_ref[...].astype(wu_s.dtype)
        wd_s[...] = wd_ref[...].astype(wd_s.dtype)

    @pl.when(used)
    def _():
        x = x_ref[...].astype(jnp.bfloat16)
        gate = _dot(x, wg_s[...])
        hid = gate * _sigmoid(gate) * _dot(x, wu_s[...])
        o_ref[...] = _dot(hid.astype(jnp.bfloat16), wd_s[...]) * sw_ref[...]

    @pl.when(jnp.logical_not(used))
    def _():
        o_ref[...] = jnp.zeros_like(o_ref)


def experts(xs, slot_w, block_e, n_used, w_gate, w_up, w_down, layer):
    R, D = xs.shape
    F = w_gate.shape[-1]
    n_blocks = R // MOE_BLOCK
    grid_spec = pltpu.PrefetchScalarGridSpec(
        num_scalar_prefetch=2,
        grid=(n_blocks,),
        in_specs=[pl.BlockSpec((MOE_BLOCK, D), lambda i, be, nu: (i, 0)),
                  pl.BlockSpec((MOE_BLOCK, 1), lambda i, be, nu: (i, 0)),
                  pl.BlockSpec((None, None, D, F), lambda i, be, nu: (layer, be[i], 0, 0)),
                  pl.BlockSpec((None, None, D, F), lambda i, be, nu: (layer, be[i], 0, 0)),
                  pl.BlockSpec((None, None, F, D), lambda i, be, nu: (layer, be[i], 0, 0))],
        out_specs=pl.BlockSpec((MOE_BLOCK, D), lambda i, be, nu: (i, 0)),
        scratch_shapes=[pltpu.VMEM((D, F), jnp.bfloat16), pltpu.VMEM((D, F), jnp.bfloat16),
                        pltpu.VMEM((F, D), jnp.bfloat16)],
    )
    return pl.pallas_call(
        _experts_kernel,
        grid_spec=grid_spec,
        out_shape=jax.ShapeDtypeStruct((R, D), jnp.float32),
        compiler_params=_params(("arbitrary",)),
        name="experts",
    )(block_e, n_used, xs, slot_w, w_gate, w_up, w_down)


def _combine_ln_kernel(x_ref, f0_ref, f1_ref, g2_ref, lg_ref, lb_ref, o_ref, *, alpha):
    y = alpha * x_ref[...] + (1.0 + g2_ref[...]) * (f0_ref[...] + f1_ref[...])
    o_ref[...] = _layer_norm(y, lg_ref[...], lb_ref[...])


def combine_ln(x2, f0, f1, g2, ln_g, ln_b, seq, tm, alpha):
    T, D = x2.shape
    per = seq // tm
    tile = pl.BlockSpec((tm, D), lambda i: (i, 0))
    row = pl.BlockSpec((1, D), lambda i: (0, 0))
    return pl.pallas_call(
        functools.partial(_combine_ln_kernel, alpha=alpha),
        grid=(T // tm,),
        in_specs=[tile, tile, tile, pl.BlockSpec((None, 1, D), lambda i: (i // per, 0, 0)), row, row],
        out_specs=tile,
        out_shape=jax.ShapeDtypeStruct((T, D), jnp.float32),
        compiler_params=_params(("parallel",)),
        name="combine_ln",
    )(x2, f0, f1, g2, ln_g, ln_b)


def _route(expert_id, gate_w):
    T = expert_id.shape[0]
    E = MOE_GROUPS * MOE_EXPERTS_PER_GROUP
    A = T * MOE_TOPK
    n_blocks = (A + E * (MOE_BLOCK - 1) + MOE_BLOCK - 1) // MOE_BLOCK
    flat_e = expert_id.reshape(A)
    flat_w = gate_w.reshape(A)
    e_s, order = lax.sort_key_val(flat_e, jnp.arange(A, dtype=jnp.int32))
    bounds = jnp.searchsorted(e_s, jnp.arange(E + 1, dtype=jnp.int32), side='left').astype(jnp.int32)
    start = bounds[:E]
    counts = bounds[1:] - start
    padded = (counts + MOE_BLOCK - 1) // MOE_BLOCK * MOE_BLOCK
    pad_end = jnp.cumsum(padded)
    pad_start = pad_end - padded
    block_e = jnp.minimum(jnp.sum(pad_end[None, :] <= (jnp.arange(n_blocks) * MOE_BLOCK)[:, None], axis=1),
                          E - 1).astype(jnp.int32)
    blk_rank = jnp.arange(n_blocks, dtype=jnp.int32) * MOE_BLOCK - pad_start[block_e]
    rank = blk_rank[:, None] + jnp.arange(MOE_BLOCK, dtype=jnp.int32)[None, :]
    valid = (rank < counts[block_e][:, None]).reshape(-1)
    src = order[jnp.clip(start[block_e][:, None] + rank, 0, A - 1).reshape(-1)]
    spread = jnp.arange(n_blocks * MOE_BLOCK, dtype=jnp.int32) % T
    slot_tok = jnp.where(valid, src // MOE_TOPK, spread).astype(jnp.int32)
    slot_w = jnp.where(valid, flat_w[src], 0.0)
    pos = jnp.arange(A, dtype=jnp.int32)[None, :]
    in_e = (pos >= start[:, None]) & (pos < bounds[1:, None])
    dest = pos[0] + jnp.sum(jnp.where(in_e, (pad_start - start)[:, None], 0), axis=0).astype(jnp.int32)
    _, slot_of = lax.sort_key_val(order, dest)
    n_used = (pad_end[-1] // MOE_BLOCK).astype(jnp.int32).reshape(1)
    return slot_tok, slot_w, slot_of.reshape(T, MOE_TOPK), block_e, n_used


def kernel(x, c, rel_bias, w_in, p_a, p_b, p_c, w_o, rwkv_mu, rwkv_w0, rwkv_w_up, rwkv_a0, rwkv_a_up,
           rwkv_g_up, rwkv_k_k, rwkv_k_a, rwkv_r_k, rwkv_ln_g, rwkv_ln_b, rwkv_v0, rwkv_mv_down,
           rwkv_mv_up, w_ada, b_ada, ln1_g, ln1_b, ln2_g, ln2_b, router_grp_w, router_grp_b,
           router_exp_w, router_exp_b, exp_w_gate, exp_w_up, exp_w_down):
    B, S, D = x.shape
    depth = w_in.shape[0]
    T = B * S
    bf = jnp.bfloat16
    W = RWKV_HEADS * HEAD_DIM
    wa_w = MOBA_HEADS * HEAD_DIM
    wc_w = len(DIL_GROUPS) * DIL_HEADS_PER_GROUP * HEAD_DIM
    rw_cols = rwkv_mu.shape[-1]
    alpha = (2 * depth) ** 0.25
    off_c = 3 * wa_w
    off_b = off_c + 3 * wc_w
    off_g = off_b + rw_cols
    new_a = rw_cols
    new_c = new_a + 3 * wa_w
    new_g = new_c + 3 * wc_w
    w_in_p = jnp.concatenate([w_in[:, :, off_b:off_g], w_in[:, :, :off_b], w_in[:, :, off_g:]], axis=-1).astype(bf)
    p_a_b, p_b_b, p_c_b, w_o_b = p_a.astype(bf), p_b.astype(bf), p_c.astype(bf), w_o.astype(bf)

    bias_h = rel_bias.T.astype(jnp.float32)
    moba_tab = moba_bias_table(bias_h[:MOBA_HEADS], S // MOBA_BLOCK)
    dil_tab = dil_bias_table(bias_h[MOBA_HEADS:])

    c8 = jnp.zeros((8, D), jnp.float32).at[:B].set(c)
    mod = ada_mod(c8, w_ada, b_ada)[:, :B]

    hd_idx = jnp.arange(LANES) // HEAD_DIM
    bd = (hd_idx[:, None] == hd_idx[None, :]).astype(jnp.float32)
    zeros_w = jnp.zeros((RWKV_DECAY_LORA, W), jnp.float32)
    pad_lora = LANES - RWKV_MV_LORA

    x2 = x.reshape(T, D)
    v_first = None
    for l in range(depth):
        sh1, sc1, g1, sh2, sc2, g2 = [m.reshape(B, 1, D) for m in jnp.split(mod[l], 6, axis=-1)]
        proj = in_proj(x2, sc1, sh1, w_in_p, l, S, 1024, 1024)
        proj3 = proj.reshape(B, S, -1)
        o_a = moba_attention(proj3, moba_tab, new_a // LANES, (new_a + wa_w) // LANES,
                             (new_a + 2 * wa_w) // LANES).reshape(T, wa_w)
        gw = DIL_HEADS_PER_GROUP * HEAD_DIM
        o_c, lw_c = [], []
        for g in range(len(DIL_GROUPS)):
            og, lwg = dilated_attention(proj3, dil_tab, (new_c + g * gw) // LANES,
                                        (new_c + wc_w + g * gw) // LANES, (new_c + 2 * wc_w + g * gw) // LANES, g)
            o_c.append(og.reshape(T, gw))
            lw_c.append(lwg.reshape(T, gw))
        wa_up = jnp.concatenate([jnp.concatenate([rwkv_w_up[l], zeros_w], axis=1),
                                 jnp.concatenate([zeros_w, rwkv_a_up[l]], axis=1)], axis=0)
        res = None
        if l > 0:
            res = (v_first, rwkv_v0[l - 1][None],
                   jnp.pad(rwkv_mv_down[l - 1], ((0, 0), (0, pad_lora))),
                   jnp.pad(rwkv_mv_up[l - 1], ((0, pad_lora), (0, 0))))
        r_, lw_, k_, v_, kn_, b_, g_ = rwkv_prep(proj, S, 256, rwkv_mu[l][None], rwkv_w0[l][None],
                                                 rwkv_a0[l][None], wa_up, rwkv_g_up[l], rwkv_k_k[l][None],
                                                 rwkv_k_a[l][None], bd, res)
        if l == 0:
            v_first = v_
        rhat, y0, e_, gmat, dmat = rwkv_chunk(r_, lw_, k_, v_, kn_, b_, g_, rwkv_r_k[l][None], bd, B)
        o_b = rwkv_state(rhat, y0, g_, e_, gmat, dmat, rwkv_ln_g[l][None], rwkv_ln_b[l][None], bd)
        merged = merge_branches(o_a, o_b, o_c, lw_c, proj, new_g // 1024, p_a_b[l], p_b_b[l], p_c_b[l],
                                256, 1024)
        x2 = out_ln(merged, w_o_b[l], x2, g1, ln1_g[l][None], ln1_b[l][None], S, 256, alpha)
        w_r = jnp.zeros((D, LANES), jnp.float32)
        w_r = w_r.at[:, :MOE_GROUPS].set(router_grp_w[l]).at[:, MOE_GROUPS:MOE_GROUPS + router_exp_w.shape[-1]].set(
            router_exp_w[l])
        b_r = jnp.zeros((1, LANES), jnp.float32)
        b_r = b_r.at[0, :MOE_GROUPS].set(router_grp_b[l]).at[0, MOE_GROUPS:MOE_GROUPS + router_exp_b.shape[-1]].set(
            router_exp_b[l])
        ids, gws, h2 = router(x2, sc2, sh2, w_r, b_r, S, 512)
        slot_tok, slot_w, slot_of, block_e, n_used = _route(ids[:, :MOE_TOPK], gws[:, :MOE_TOPK])
        y = experts(h2[slot_tok], slot_w[:, None], block_e, n_used, exp_w_gate, exp_w_up, exp_w_down, l)
        x2 = combine_ln(x2, y[slot_of[:, 0]], y[slot_of[:, 1]], g2, ln2_g[l][None], ln2_b[l][None], S, 512, alpha)
    return x2.reshape(B, S, D)
```

```python
import functools
import math

import jax
import jax.numpy as jnp
import numpy as np
from jax import lax
from jax.experimental import pallas as pl
from jax.experimental.pallas import tpu as pltpu

HEAD_DIM = 64
LANES = 128
MOBA_HEADS = 12
MOBA_BLOCK = 256
MOBA_TOPK = 3
MOBA_STEP = 2
RWKV_HEADS = 12
RWKV_DECAY_LORA = 64
RWKV_A_LORA = 64
RWKV_MV_LORA = 32
RWKV_GATE_LORA = 128
RWKV_GN_EPS = 64e-5
RWKV_CHUNK = 64
DIL_GROUPS = ((128, 1), (512, 4), (2048, 16))
DIL_HEADS_PER_GROUP = 4
DIL_SPAN = 128
REL_BUCKETS = 32
REL_MAX_DISTANCE = 2048
MOE_GROUPS = 8
MOE_EXPERTS_PER_GROUP = 8
MOE_TOPK = 2
MOE_BLOCK = 256
LN_EPS = 1e-5
NEG = -1e30
LOG2E = math.log2(math.e)
LN2 = math.log(2.0)
VMEM_LIMIT = 56 * 1024 * 1024
HI = lax.Precision.HIGHEST


def _params(sem):
    return pltpu.CompilerParams(dimension_semantics=sem, vmem_limit_bytes=VMEM_LIMIT)


def _sigmoid(x):
    return 0.5 * jnp.tanh(0.5 * x) + 0.5


def _dot(a, b, precision=None):
    return jnp.dot(a, b, preferred_element_type=jnp.float32, precision=precision)


def _dot_nt(a, b, precision=None):
    return lax.dot_general(a, b, (((1,), (1,)), ((), ())), preferred_element_type=jnp.float32,
                           precision=precision)


def _dot_tn(a, b, precision=None):
    return lax.dot_general(a, b, (((0,), (0,)), ((), ())), preferred_element_type=jnp.float32,
                           precision=precision)


def _split2(x):
    hi = x.astype(jnp.bfloat16)
    return hi, (x - hi.astype(jnp.float32)).astype(jnp.bfloat16)


def _dot3(a, b, dims=((1,), (0,))):
    (ca,), (cb,) = dims
    ah, al = _split2(a)
    bh, bl = _split2(b)
    return lax.dot_general(jnp.concatenate([ah, ah, al], axis=ca), jnp.concatenate([bh, bl, bh], axis=cb),
                           (dims, ((), ())), preferred_element_type=jnp.float32)


def _dot_sel(x, sel):
    xh, xl = _split2(x)
    sb = sel.astype(jnp.bfloat16)
    return _dot(jnp.concatenate([xh, xl], axis=1), jnp.concatenate([sb, sb], axis=0))


_NT = ((1,), (1,))
_TN = ((0,), (0,))


def _layer_norm(y, g, b):
    mu = jnp.mean(y, axis=-1, keepdims=True)
    d = y - mu
    var = jnp.mean(d * d, axis=-1, keepdims=True)
    return d * lax.rsqrt(var + LN_EPS) * g + b


def _ada_kernel(c_ref, w_ref, b_ref, o_ref):
    c = c_ref[...]
    cond = c * _sigmoid(c)
    o_ref[...] = _dot(cond, w_ref[...]) + b_ref[...]


def ada_mod(c8, w_ada, b_ada):
    L, D, N = w_ada.shape
    tn = 1024
    return pl.pallas_call(
        _ada_kernel,
        grid=(L, N // tn),
        in_specs=[pl.BlockSpec((8, D), lambda l, j: (0, 0)),
                  pl.BlockSpec((None, D, tn), lambda l, j: (l, 0, j)),
                  pl.BlockSpec((None, 1, tn), lambda l, j: (l, 0, j))],
        out_specs=pl.BlockSpec((None, 8, tn), lambda l, j: (l, 0, j)),
        out_shape=jax.ShapeDtypeStruct((L, 8, N), jnp.float32),
        compiler_params=_params(("parallel", "parallel")),
        name="ada_mod",
    )(c8, w_ada, b_ada.reshape(L, 1, N))


def _in_proj_kernel(x_ref, sc_ref, sh_ref, w_ref, o_ref, h_ref):
    @pl.when(pl.program_id(1) == 0)
    def _():
        h_ref[...] = (x_ref[...] * (1.0 + sc_ref[...]) + sh_ref[...]).astype(h_ref.dtype)

    o_ref[...] = _dot(h_ref[...], w_ref[...])


def in_proj(x2, sc, sh, w, layer, seq, tm, tn):
    T, D = x2.shape
    N = w.shape[-1]
    per = seq // tm
    return pl.pallas_call(
        _in_proj_kernel,
        grid=(T // tm, N // tn),
        in_specs=[pl.BlockSpec((tm, D), lambda i, j: (i, 0)),
                  pl.BlockSpec((None, 1, D), lambda i, j: (i // per, 0, 0)),
                  pl.BlockSpec((None, 1, D), lambda i, j: (i // per, 0, 0)),
                  pl.BlockSpec((None, D, tn), lambda i, j: (layer, 0, j))],
        out_specs=pl.BlockSpec((tm, tn), lambda i, j: (i, j)),
        out_shape=jax.ShapeDtypeStruct((T, N), jnp.float32),
        scratch_shapes=[pltpu.VMEM((tm, D), jnp.bfloat16)],
        compiler_params=_params(("parallel", "arbitrary")),
        name="in_proj",
    )(x2, sc, sh, w)


def _t5_bucket(dist):
    n = jnp.maximum(dist, 0)
    max_exact = REL_BUCKETS // 2
    nf = jnp.maximum(n, 1).astype(jnp.float32)
    large = max_exact + (jnp.log(nf / max_exact) / math.log(REL_MAX_DISTANCE / max_exact)
                         * (REL_BUCKETS - max_exact)).astype(jnp.int32)
    large = jnp.minimum(large, REL_BUCKETS - 1)
    return jnp.where(n < max_exact, n, large)


def _moba_n_delta(nb):
    last_start = 1
    d = np.arange(1, nb * MOBA_BLOCK + 1)
    large = 16 + (np.log(d / 16.0) / math.log(REL_MAX_DISTANCE / 16.0) * 16).astype(np.int64)
    bucket = np.where(d < 16, d, np.minimum(large, REL_BUCKETS - 1))
    last_start = int(d[bucket < REL_BUCKETS - 1].max()) + 1 if (bucket < REL_BUCKETS - 1).any() else 1
    delta = 1
    while delta * MOBA_BLOCK - (MOBA_BLOCK - 1) < last_start + 2:
        delta += 1
    return min(delta + 1, nb)


def moba_bias_table(bias_a, nb):
    nd = _moba_n_delta(nb)
    key = jnp.arange(MOBA_BLOCK)[:, None]
    qry = jnp.arange(MOBA_BLOCK)[None, :]
    dist = jnp.arange(nd)[:, None, None] * MOBA_BLOCK + (qry - key)[None]
    bucket = _t5_bucket(dist)[None]
    tab = jnp.zeros((bias_a.shape[0],) + dist.shape, jnp.float32)
    for b in range(REL_BUCKETS):
        tab = jnp.where(bucket == b, bias_a[:, b][:, None, None, None], tab)
    return jnp.where((dist >= 0)[None], tab * LOG2E, NEG)


def dil_bias_table(bias_c):
    span = DIL_SPAN
    rel = span + jnp.arange(span)[:, None] - jnp.arange(2 * span)[None, :]
    valid = (rel >= 0) & (rel <= span)
    tabs = []
    for g, (_, dilation) in enumerate(DIL_GROUPS):
        bh = bias_c[g * DIL_HEADS_PER_GROUP:(g + 1) * DIL_HEADS_PER_GROUP]
        tabs.append(jnp.where(valid[None], bh[:, _t5_bucket(rel * dilation)] * LOG2E, NEG))
    return jnp.stack(tabs)


def _moba_kernel(q_ref, k_ref, v_ref, bias_ref, o_ref,
                 kaug_ref, vt_ref, km_ref, acc_ref, s_ref, *, nb, n_delta):
    qb = pl.program_id(2)
    bs = MOBA_BLOCK
    S = nb * bs
    half = bs // 2
    lane = lax.broadcasted_iota(jnp.int32, (1, LANES), 1)
    f32 = jnp.float32

    @pl.when(qb == 0)
    def _():
        k = k_ref[...]
        rowblk = lax.broadcasted_iota(jnp.int32, (S, LANES), 0) // bs
        lanes = lax.broadcasted_iota(jnp.int32, (S, LANES), 1)
        km_ref[...] = jnp.mean(k.reshape(nb, bs, LANES), axis=1)
        kaug_ref[0] = jnp.where(lanes < HEAD_DIM, k, (lanes - HEAD_DIM == rowblk).astype(f32)).astype(kaug_ref.dtype)
        kaug_ref[1] = jnp.where(lanes >= HEAD_DIM, k, (lanes == rowblk).astype(f32)).astype(kaug_ref.dtype)
        for i in range(nb):
            vt_ref[i] = v_ref[i * bs:(i + 1) * bs, :].T.astype(vt_ref.dtype)

    q_t = q_ref[...].T
    scale = HEAD_DIM ** -0.5 * LOG2E
    blk = lax.broadcasted_iota(jnp.int32, (nb, 1), 0)
    past = blk < qb
    rhs = []
    for t in range(2):
        hm = (lane >= t * HEAD_DIM) & (lane < (t + 1) * HEAD_DIM)
        gate = _dot(jnp.where(hm, km_ref[...], 0.0), q_t, precision=HI)
        gate = jnp.where(past, gate, -jnp.inf)
        cnt = jnp.zeros((nb, bs), jnp.int32)
        for m in range(nb):
            gm = gate[m:m + 1, :]
            ahead = (gm > gate) | ((gm == gate) & (m < blk))
            cnt = cnt + ahead.astype(jnp.int32)
        chosen = (past & (cnt < MOBA_TOPK)) | (blk == qb)
        pen = jnp.where(chosen, 0.0, NEG)
        qh = q_t[t * HEAD_DIM:(t + 1) * HEAD_DIM, :] * scale
        if t == 0:
            parts = [qh, pen, jnp.zeros((LANES - HEAD_DIM - nb, bs), f32)]
        else:
            parts = [pen, jnp.zeros((HEAD_DIM - nb, bs), f32), qh]
        rhs.append(jnp.concatenate(parts, axis=0).astype(jnp.bfloat16))
    acc_ref[...] = jnp.zeros((LANES, bs), f32)

    chains = [(t, hq) for t in range(2) for hq in range(2)]
    cols = [slice(hq * half, (hq + 1) * half) for _, hq in chains]

    def logits(slot, k):
        blocks = [jnp.minimum(MOBA_STEP * k + j, nb - 1) for j in range(MOBA_STEP)]
        rows = [pl.multiple_of(n * bs, bs) for n in blocks]
        delta = [jnp.clip(qb - n, 0, n_delta - 1) for n in blocks]
        for c, (t, _) in enumerate(chains):
            for j in range(MOBA_STEP):
                s_ref[slot, MOBA_STEP * c + j] = (_dot(kaug_ref[t, pl.ds(rows[j], bs), :], rhs[t][:, cols[c]])
                                                  + bias_ref[t, delta[j], :, cols[c]])

    def step(slot, k, carry):
        blocks = [MOBA_STEP * k + j for j in range(MOBA_STEP)]
        s = [[s_ref[slot, MOBA_STEP * c + j] for j in range(MOBA_STEP)] for c in range(4)]
        m_new = [functools.reduce(jnp.maximum, [carry[c][0]] + [jnp.max(x, axis=0, keepdims=True) for x in s[c]])
                 for c in range(4)]
        alpha = [jnp.exp2(carry[c][0] - m_new[c]) for c in range(4)]
        p = [[jnp.exp2(x - m_new[c]) for x in s[c]] for c in range(4)]
        l_new = [alpha[c] * carry[c][1] + sum(jnp.sum(x, axis=0, keepdims=True) for x in p[c]) for c in range(4)]
        pv = [sum(_dot(vt_ref[n, t * HEAD_DIM:(t + 1) * HEAD_DIM, :], p[c][j].astype(jnp.bfloat16))
                  for j, n in enumerate(blocks)) for c, (t, _) in enumerate(chains)]
        pieces = [alpha[c] * acc_ref[t * HEAD_DIM:(t + 1) * HEAD_DIM, cols[c]] + pv[c]
                  for c, (t, _) in enumerate(chains)]
        acc_ref[...] = jnp.concatenate([jnp.concatenate(pieces[:2], axis=1),
                                        jnp.concatenate(pieces[2:], axis=1)], axis=0)
        return tuple((m_new[c], l_new[c]) for c in range(4))

    init = tuple((jnp.full((1, half), NEG, f32), jnp.zeros((1, half), f32)) for _ in range(4))
    n_steps = qb // MOBA_STEP + 1
    logits(0, 0)

    def two_steps(i, carry):
        logits(1, 2 * i + 1)
        carry = step(0, 2 * i, carry)
        logits(0, 2 * i + 2)
        return step(1, 2 * i + 1, carry)

    fin = lax.fori_loop(0, (n_steps + 1) // 2, two_steps, init)
    l_all = jnp.concatenate([jnp.broadcast_to(jnp.concatenate([fin[2 * t][1], fin[2 * t + 1][1]], axis=1),
                                              (HEAD_DIM, bs)) for t in range(2)], axis=0)
    o_ref[...] = (acc_ref[...] / l_all).T


def moba_attention(proj3, bias_tab, q_off, k_off, v_off):
    B, S, _ = proj3.shape
    nb = S // MOBA_BLOCK
    assert nb % (2 * MOBA_STEP) == 0, "the two-step loop needs an even number of key-block steps"
    n_delta = bias_tab.shape[1]
    pairs = MOBA_HEADS // 2
    kern = functools.partial(_moba_kernel, nb=nb, n_delta=n_delta)
    return pl.pallas_call(
        kern,
        grid=(pairs, B, nb),
        in_specs=[pl.BlockSpec((None, MOBA_BLOCK, LANES), lambda h, b, i: (b, i, q_off + h)),
                  pl.BlockSpec((None, S, LANES), lambda h, b, i: (b, 0, k_off + h)),
                  pl.BlockSpec((None, S, LANES), lambda h, b, i: (b, 0, v_off + h)),
                  pl.BlockSpec((2, n_delta, MOBA_BLOCK, MOBA_BLOCK), lambda h, b, i: (h, 0, 0, 0))],
        out_specs=pl.BlockSpec((None, MOBA_BLOCK, LANES), lambda h, b, i: (b, i, h)),
        out_shape=jax.ShapeDtypeStruct((B, S, pairs * LANES), jnp.float32),
        scratch_shapes=[pltpu.VMEM((2, S, LANES), jnp.bfloat16),
                        pltpu.VMEM((nb, LANES, MOBA_BLOCK), jnp.bfloat16),
                        pltpu.VMEM((nb, LANES), jnp.float32),
                        pltpu.VMEM((LANES, MOBA_BLOCK), jnp.float32),
                        pltpu.VMEM((2, 4 * MOBA_STEP, MOBA_BLOCK, MOBA_BLOCK // 2), jnp.float32)],
        compiler_params=_params(("parallel", "parallel", "arbitrary")),
        name="moba",
    )(proj3, proj3, proj3, bias_tab)


def _dil_kernel(q_ref, k_ref, v_ref, bias_ref, o_ref, lw_ref, *, dilation, seq):
    span = DIL_SPAN
    bps = seq // dilation // span
    scale = HEAD_DIM ** -0.5 * LOG2E
    lane = lax.broadcasted_iota(jnp.int32, (1, LANES), 1)
    col = lax.broadcasted_iota(jnp.int32, (1, 2 * span), 1)
    hm = [lane < HEAD_DIM, lane >= HEAD_DIM]
    unroll = 2

    def body(it, carry):
        blocks = []
        for u in range(unroll):
            j = it * unroll + u
            r = j // bps
            i = j - r * bps
            start = r + i * (span * dilation)
            prev = jnp.maximum(start - span * dilation, r)
            rows = lambda s0: pl.ds(s0, span, stride=dilation)
            q = q_ref[rows(start), :] * scale
            kcat = jnp.concatenate([k_ref[rows(prev), :], k_ref[rows(start), :]], axis=0).astype(jnp.bfloat16)
            vcat = jnp.concatenate([v_ref[rows(prev), :], v_ref[rows(start), :]], axis=0).astype(jnp.bfloat16)
            blocks.append((start, q, kcat, vcat, (i == 0) & (col < span)))
        chains = [(u, t) for u in range(unroll) for t in range(2)]
        s = [jnp.where(blocks[u][4], NEG,
                       _dot_nt(jnp.where(hm[t], blocks[u][1], 0.0).astype(jnp.bfloat16), blocks[u][2]) + bias_ref[t])
             for u, t in chains]
        m = [jnp.max(x, axis=-1, keepdims=True) for x in s]
        e = [jnp.exp2(x - mm) for x, mm in zip(s, m)]
        l = [jnp.sum(x, axis=-1, keepdims=True) for x in e]
        pv = [_dot(x.astype(jnp.bfloat16), blocks[u][3]) for x, (u, _) in zip(e, chains)]
        for u in range(unroll):
            a, b = 2 * u, 2 * u + 1
            rows = pl.ds(blocks[u][0], span, stride=dilation)
            o_ref[rows, :] = jnp.where(hm[0], pv[a] / l[a], pv[b] / l[b])
            lw_ref[rows, :] = jnp.where(hm[0], m[a] * LN2 + jnp.log(l[a]), m[b] * LN2 + jnp.log(l[b]))
        return carry

    lax.fori_loop(0, seq // span // unroll, body, 0)


def dilated_attention(proj3, bias_tab, q_off, k_off, v_off, group):
    B, S, _ = proj3.shape
    dilation = DIL_GROUPS[group][1]
    pairs = DIL_HEADS_PER_GROUP // 2
    slab = lambda off: pl.BlockSpec((None, S, LANES), lambda b, h: (b, 0, off + h))
    out = jax.ShapeDtypeStruct((B, S, pairs * LANES), jnp.float32)
    return pl.pallas_call(
        functools.partial(_dil_kernel, dilation=dilation, seq=S),
        grid=(B, pairs),
        in_specs=[slab(q_off), slab(k_off), slab(v_off),
                  pl.BlockSpec((None, 2, DIL_SPAN, 2 * DIL_SPAN), lambda b, h: (group, h, 0, 0))],
        out_specs=[slab(0), slab(0)],
        out_shape=[out, out],
        compiler_params=_params(("parallel", "parallel")),
        name="dilated",
    )(proj3, proj3, proj3, bias_tab)


def _rwkv_prep_kernel(*refs, width, has_res, rows_per_seq):
    if has_res:
        (z_ref, zl_ref, mu_ref, w0_ref, a0_ref, wa_ref, gup_ref, kk_ref, ka_ref, bd_ref,
         vf_ref, v0_ref, mvd_ref, mvu_ref,
         r_o, lw_o, k_o, v_o, kn_o, b_o, g_o) = refs
    else:
        (z_ref, zl_ref, mu_ref, w0_ref, a0_ref, wa_ref, gup_ref, kk_ref, ka_ref, bd_ref,
         r_o, lw_o, k_o, v_o, kn_o, b_o, g_o) = refs
    i = pl.program_id(0)
    W = width
    z = z_ref[...]
    tm = z.shape[0]
    row = lax.broadcasted_iota(jnp.int32, (tm, 1), 0)
    seq_start = (i % rows_per_seq) == 0
    last = jnp.where(seq_start, 0.0, zl_ref[7:8, :])
    zp = jnp.where(row == 0, last, pltpu.roll(z, 1, 0))
    zf = z + mu_ref[...] * (zp - z)
    lora = zf[:, 3 * W:3 * W + LANES]
    lane = lax.broadcasted_iota(jnp.int32, (1, LANES), 1)
    lora = jnp.where(lane < RWKV_DECAY_LORA, jnp.tanh(lora), lora)
    wa = _dot(lora, wa_ref[...], precision=HI)
    g = _dot(_sigmoid(zf[:, 3 * W + LANES:3 * W + 2 * LANES]), gup_ref[...], precision=HI)
    g_o[...] = g
    v_all = zf[:, 2 * W:3 * W]
    if has_res:
        mix = _dot(_dot(v_all, mvd_ref[...], precision=HI), mvu_ref[...], precision=HI)
    for c in range(W // LANES):
        sl = slice(c * LANES, (c + 1) * LANES)
        x = w0_ref[:, sl] + wa[:, sl]
        sp = jnp.maximum(-x, 0.0) + jnp.log(1.0 + jnp.exp(-jnp.abs(x)))
        lw_o[:, sl] = -jnp.exp(-sp - 0.5)
        a = _sigmoid(a0_ref[:, sl] + wa[:, W + c * LANES:W + (c + 1) * LANES])
        r_o[:, sl] = zf[:, sl]
        k = zf[:, W + c * LANES:W + (c + 1) * LANES]
        v = v_all[:, sl]
        if has_res:
            v = v + (vf_ref[:, sl] - v) * _sigmoid(v0_ref[:, sl] + mix[:, sl])
        v_o[:, sl] = v
        kk = k * kk_ref[:, sl]
        ss = _dot_sel(kk * kk, bd_ref[...])
        kn = kk / jnp.maximum(jnp.sqrt(ss), 1e-12)
        kn_o[:, sl] = kn
        b_o[:, sl] = kn * a
        k_o[:, sl] = k * (1.0 + (a - 1.0) * ka_ref[:, sl])


def rwkv_prep(proj, seq, tm, mu, w0, a0, wa_up, g_up, k_k, k_a, bd, res):
    T = proj.shape[0]
    W = RWKV_HEADS * HEAD_DIM
    cols = mu.shape[-1]
    row1 = lambda n: pl.BlockSpec((1, n), lambda i: (0, 0))
    full = lambda a: pl.BlockSpec(a.shape, lambda i: (0, 0))
    tile = pl.BlockSpec((tm, W), lambda i: (i, 0))
    in_specs = [pl.BlockSpec((tm, cols), lambda i: (i, 0)),
                pl.BlockSpec((8, cols), lambda i: (jnp.maximum(i * (tm // 8) - 1, 0), 0)),
                row1(cols), row1(W), row1(W), full(wa_up), full(g_up), row1(W), row1(W), full(bd)]
    args = [proj, proj, mu, w0, a0, wa_up, g_up, k_k, k_a, bd]
    if res is not None:
        v_first, v0, mvd, mvu = res
        in_specs += [tile, row1(W), full(mvd), full(mvu)]
        args += [v_first, v0, mvd, mvu]
    out = jax.ShapeDtypeStruct((T, W), jnp.float32)
    kern = functools.partial(_rwkv_prep_kernel, width=W, has_res=res is not None, rows_per_seq=seq // tm)
    return pl.pallas_call(
        kern,
        grid=(T // tm,),
        in_specs=in_specs,
        out_specs=[tile] * 7,
        out_shape=[out] * 7,
        compiler_params=_params(("parallel",)),
        name="rwkv_prep",
    )(*args)


def _stack_heads(x, lane):
    return jnp.concatenate([jnp.where(lane < HEAD_DIM, x, 0.0), jnp.where(lane >= HEAD_DIM, x, 0.0)], axis=0)


def _rwkv_chunk_kernel(r_ref, lw_ref, k_ref, v_ref, kn_ref, b_ref, g_ref, rk_ref, bd_ref,
                       rhat_ref, y0_ref, e_ref, g_out_ref, d_out_ref, *, chunks):
    C = RWKV_CHUNK
    C2, C4 = 2 * C, 4 * C
    f32 = jnp.float32
    ti = lax.broadcasted_iota(jnp.int32, (C, C), 0)
    si = lax.broadcasted_iota(jnp.int32, (C, C), 1)
    tri = (ti >= si).astype(f32)
    lane = lax.broadcasted_iota(jnp.int32, (1, LANES), 1)
    rho = lax.broadcasted_iota(jnp.int32, (C4, C4), 0)
    sig = lax.broadcasted_iota(jnp.int32, (C4, C4), 1)
    keep = jnp.where(rho >= C2, rho & (C - 1), (rho & (C - 1)) - 1) >= (sig & (C - 1))
    eye2 = (lax.broadcasted_iota(jnp.int32, (C2, C2), 0) == lax.broadcasted_iota(jnp.int32, (C2, C2), 1)).astype(f32)
    eye_l = (lax.broadcasted_iota(jnp.int32, (LANES, LANES), 0)
             == lax.broadcasted_iota(jnp.int32, (LANES, LANES), 1)).astype(f32)
    zeros2 = jnp.zeros((C2, LANES), f32)
    ch = range(chunks)
    rows = [slice(c * C, (c + 1) * C) for c in ch]
    cum = [_dot(tri, lw_ref[rows[c], :], precision=HI) for c in ch]
    st = []
    for c in ch:
        r, lw, k, v = r_ref[rows[c], :], lw_ref[rows[c], :], k_ref[rows[c], :], v_ref[rows[c], :]
        kn, bb = kn_ref[rows[c], :], b_ref[rows[c], :]
        cum_last = cum[c][C - 1:C, :]
        e_out = jnp.exp(-cum[c])
        e_tail = jnp.exp(cum_last - cum[c])
        r_t = r * jnp.exp(cum[c])
        st.append(dict(
            r_t=r_t, decay=jnp.exp(cum_last),
            a2=_stack_heads(-kn * jnp.exp(cum[c] - lw), lane), r2=_stack_heads(r_t, lane),
            b2=_stack_heads(bb * e_out, lane), k2=_stack_heads(k * e_out, lane), v2=_stack_heads(v, lane),
            bh2=_stack_heads(bb * e_tail, lane), kh2=_stack_heads(k * e_tail, lane),
            e=_dot_sel(r * k * rk_ref[...], bd_ref[...]) * v * g_ref[rows[c], :]))
    quad = [jnp.where(keep, _dot3(jnp.concatenate([s["a2"], s["r2"]], axis=0),
                                  jnp.concatenate([s["b2"], s["k2"]], axis=0), _NT), 0.0) for s in st]
    pw = [q[:C2, :C2] for q in quad]
    t_inv = [eye2 + p for p in pw]
    x = [_dot3(q[:C2, C2:], s["v2"]) for q, s in zip(quad, st)]
    for level in range(int(math.log2(C)) - 1):
        mm = _dot3 if level < 2 else (lambda a, b: _dot(a.astype(jnp.bfloat16), b.astype(jnp.bfloat16)))
        pw = [mm(p, p) for p in pw]
        t_inv = [t + mm(t, p) for t, p in zip(t_inv, pw)]
    au = [_dot3(t, jnp.concatenate([s["a2"], xx], axis=1)) for t, s, xx in zip(t_inv, st, x)]
    my = [_dot3(q[C2:, :], jnp.concatenate([a, jnp.concatenate([zeros2, s["v2"]], axis=1)], axis=0))
          for q, a, s in zip(quad, au, st)]
    gm = [eye_l * s["decay"] + _dot3(a[:, :LANES], s["bh2"], _TN) for a, s in zip(au, st)]
    dm = [_dot3(jnp.concatenate([a[:, LANES:], s["v2"]], axis=0),
                jnp.concatenate([s["bh2"], s["kh2"]], axis=0), _TN) for a, s in zip(au, st)]
    rhat_ref[...] = jnp.concatenate([s["r_t"] + m[:C, :LANES] + m[C:, :LANES] for s, m in zip(st, my)], axis=0)
    y0_ref[...] = jnp.concatenate([m[:C, LANES:] + m[C:, LANES:] for m in my], axis=0)
    e_ref[...] = jnp.concatenate([s["e"] for s in st], axis=0)
    g_out_ref[...] = jnp.stack(gm)
    d_out_ref[...] = jnp.stack(dm)


def rwkv_chunk(r, lw, k, v, kn, b, g, r_k, bd, batch, chunks=4):
    T, W = r.shape
    S = T // batch
    C = RWKV_CHUNK
    nc = S // C
    pairs = W // LANES
    steps = nc // chunks
    tile = pl.BlockSpec((chunks * C, LANES), lambda bi, h, c: (bi * steps + c, h))
    mat = pl.BlockSpec((None, None, chunks, LANES, LANES), lambda bi, h, c: (bi, h, c, 0, 0))
    tw = jax.ShapeDtypeStruct((T, W), jnp.float32)
    gd = jax.ShapeDtypeStruct((batch, pairs, nc, LANES, LANES), jnp.float32)
    return pl.pallas_call(
        functools.partial(_rwkv_chunk_kernel, chunks=chunks),
        grid=(batch, pairs, steps),
        in_specs=[tile] * 7 + [pl.BlockSpec((1, LANES), lambda bi, h, c: (0, h)),
                               pl.BlockSpec((LANES, LANES), lambda bi, h, c: (0, 0))],
        out_specs=[tile, tile, tile, mat, mat],
        out_shape=[tw, tw, tw, gd, gd],
        compiler_params=_params(("parallel", "parallel", "parallel")),
        name="rwkv_chunk",
    )(r, lw, k, v, kn, b, g, r_k, bd)


def _rwkv_state_kernel(rhat_ref, y0_ref, g_ref, e_ref, gm_ref, dm_ref, lng_ref, lnb_ref, bd_ref,
                       o_ref, state_ref, *, chunks, group):
    C = RWKV_CHUNK

    @pl.when(pl.program_id(2) == 0)
    def _():
        state_ref[...] = jnp.zeros_like(state_ref)

    bd = bd_ref[...]
    inv_n = 1.0 / HEAD_DIM

    def body(c, carry):
        rows = pl.ds(pl.multiple_of(c * C, C), C)
        pairs = range(group)
        state = [state_ref[p] for p in pairs]
        y = [_dot3(rhat_ref[rows, p * LANES:(p + 1) * LANES], state[p], _NT) for p in pairs]
        new_state = [_dot3(state[p], gm_ref[p, c]) + dm_ref[p, c] for p in pairs]
        y = jnp.concatenate(y, axis=1) + y0_ref[rows, :]
        state_ref[...] = jnp.stack(new_state)
        mean = jnp.concatenate([_dot_sel(y[:, p * LANES:(p + 1) * LANES], bd) for p in pairs], axis=1) * inv_n
        d = y - mean
        dd = d * d
        var = jnp.concatenate([_dot_sel(dd[:, p * LANES:(p + 1) * LANES], bd) for p in pairs], axis=1) * inv_n
        yn = d * lax.rsqrt(var + RWKV_GN_EPS) * lng_ref[...] + lnb_ref[...]
        o_ref[rows, :] = yn * g_ref[rows, :] + e_ref[rows, :]
        return carry

    lax.fori_loop(0, chunks, body, 0)


def rwkv_state(rhat, y0, g, e, gmat, dmat, ln_g, ln_b, bd, chunks=16, group=3):
    T, W = rhat.shape
    batch, pairs, nc = gmat.shape[:3]
    C = RWKV_CHUNK
    steps = nc // chunks
    gw = group * LANES
    tile = pl.BlockSpec((chunks * C, gw), lambda bi, h, c: (bi * steps + c, h))
    mat = pl.BlockSpec((None, group, chunks, LANES, LANES), lambda bi, h, c: (bi, h, c, 0, 0))
    row = pl.BlockSpec((1, gw), lambda bi, h, c: (0, h))
    return pl.pallas_call(
        functools.partial(_rwkv_state_kernel, chunks=chunks, group=group),
        grid=(batch, pairs // group, steps),
        in_specs=[tile] * 4 + [mat, mat, row, row, pl.BlockSpec((LANES, LANES), lambda bi, h, c: (0, 0))],
        out_specs=tile,
        out_shape=jax.ShapeDtypeStruct((T, W), jnp.float32),
        scratch_shapes=[pltpu.VMEM((group, LANES, LANES), jnp.float32)],
        compiler_params=_params(("parallel", "parallel", "arbitrary")),
        name="rwkv_state",
    )(rhat, y0, g, e, gmat, dmat, ln_g, ln_b, bd)


def _merge_kernel(oa_ref, ob_ref, oc0_ref, oc1_ref, oc2_ref, lw0_ref, lw1_ref, lw2_ref, ga_ref, gb_ref, gc_ref,
                  pa_ref, pb_ref, pc_ref, o_ref):
    lw = [lw0_ref[...], lw1_ref[...], lw2_ref[...]]
    mx = jnp.maximum(jnp.maximum(lw[0], lw[1]), lw[2])
    wgt = [jnp.exp(x - mx) for x in lw]
    o_c = ((wgt[0] * oc0_ref[...] + wgt[1] * oc1_ref[...] + wgt[2] * oc2_ref[...])
           / (wgt[0] + wgt[1] + wgt[2]))
    bf = jnp.bfloat16
    merged = (_sigmoid(ga_ref[...]) * _dot(oa_ref[...].astype(bf), pa_ref[...])
              + _sigmoid(gb_ref[...]) * _dot(ob_ref[...].astype(bf), pb_ref[...])
              + _sigmoid(gc_ref[...]) * _dot(o_c.astype(bf), pc_ref[...]))
    o_ref[...] = merged.astype(o_ref.dtype)


def merge_branches(o_a, o_b, o_c, lw_c, proj, gate_off, p_a, p_b, p_c, tm, tn):
    T = o_a.shape[0]
    D = p_a.shape[1]
    nj = D // tn
    grp = pl.BlockSpec((tm, o_c[0].shape[1]), lambda i, j: (i, 0))
    gate = lambda n: pl.BlockSpec((tm, tn), lambda i, j: (i, gate_off + n * nj + j))
    wspec = lambda a: pl.BlockSpec((a.shape[0], tn), lambda i, j: (0, j))
    return pl.pallas_call(
        _merge_kernel,
        grid=(T // tm, nj),
        in_specs=[pl.BlockSpec((tm, o_a.shape[1]), lambda i, j: (i, 0)),
                  pl.BlockSpec((tm, o_b.shape[1]), lambda i, j: (i, 0)),
                  grp, grp, grp, grp, grp, grp, gate(0), gate(1), gate(2), wspec(p_a), wspec(p_b), wspec(p_c)],
        out_specs=pl.BlockSpec((tm, tn), lambda i, j: (i, j)),
        out_shape=jax.ShapeDtypeStruct((T, D), jnp.bfloat16),
        compiler_params=_params(("parallel", "arbitrary")),
        name="merge",
    )(o_a, o_b, *o_c, *lw_c, proj, proj, proj, p_a, p_b, p_c)


def _out_ln_kernel(mg_ref, w_ref, x_ref, g1_ref, lg_ref, lb_ref, o_ref, *, alpha):
    mix = _dot(mg_ref[...], w_ref[...])
    y = alpha * x_ref[...] + (1.0 + g1_ref[...]) * mix
    o_ref[...] = _layer_norm(y, lg_ref[...], lb_ref[...])


def out_ln(merged, w_o, x2, g1, ln_g, ln_b, seq, tm, alpha):
    T, D = x2.shape
    per = seq // tm
    tile = pl.BlockSpec((tm, D), lambda i: (i, 0))
    row = pl.BlockSpec((1, D), lambda i: (0, 0))
    return pl.pallas_call(
        functools.partial(_out_ln_kernel, alpha=alpha),
        grid=(T // tm,),
        in_specs=[tile, pl.BlockSpec((D, D), lambda i: (0, 0)), tile,
                  pl.BlockSpec((None, 1, D), lambda i: (i // per, 0, 0)), row, row],
        out_specs=tile,
        out_shape=jax.ShapeDtypeStruct((T, D), jnp.float32),
        compiler_params=_params(("parallel",)),
        name="out_ln",
    )(merged, w_o, x2, g1, ln_g, ln_b)


def _router_kernel(x_ref, sc_ref, sh_ref, w_ref, b_ref, id_ref, gw_ref, h_ref):
    h = x_ref[...] * (1.0 + sc_ref[...]) + sh_ref[...]
    h_ref[...] = h.astype(h_ref.dtype)
    lg = _dot(h, w_ref[...], precision=HI) + b_ref[...]
    G, EPG = MOE_GROUPS, MOE_EXPERTS_PER_GROUP
    lane = lax.broadcasted_iota(jnp.int32, (1, LANES), 1).astype(jnp.float32)
    first = lambda hit: jnp.min(jnp.where(hit, lane, float(LANES)), axis=-1, keepdims=True)
    is_grp = lane < G
    gmax = jnp.max(jnp.where(is_grp, lg, -jnp.inf), axis=-1, keepdims=True)
    ge = jnp.where(is_grp, jnp.exp(jnp.where(is_grp, lg, gmax) - gmax), 0.0)
    prob = ge / jnp.sum(ge, axis=-1, keepdims=True)
    grp_p = jnp.max(prob, axis=-1, keepdims=True)
    grp_i = first(is_grp & (prob == grp_p))
    lo = G + grp_i * EPG
    el = jnp.where((lane >= lo) & (lane < lo + EPG), lg, -jnp.inf)
    l1 = jnp.max(el, axis=-1, keepdims=True)
    i1 = first(el == l1)
    el = jnp.where(lane == i1, -jnp.inf, el)
    l2 = jnp.max(el, axis=-1, keepdims=True)
    i2 = first(el == l2)
    t = jnp.exp(l2 - l1)
    w1 = grp_p / (1.0 + t)
    id_ref[...] = jnp.where(lane == 0, i1 - G, jnp.where(lane == 1, i2 - G, 0.0)).astype(jnp.int32)
    gw_ref[...] = jnp.where(lane == 0, w1, jnp.where(lane == 1, w1 * t, 0.0))


def router(x2, sc, sh, w_r, b_r, seq, tm):
    T, D = x2.shape
    per = seq // tm
    mod = pl.BlockSpec((None, 1, D), lambda i: (i // per, 0, 0))
    narrow = pl.BlockSpec((tm, LANES), lambda i: (i, 0))
    return pl.pallas_call(
        _router_kernel,
        grid=(T // tm,),
        in_specs=[pl.BlockSpec((tm, D), lambda i: (i, 0)), mod, mod,
                  pl.BlockSpec((D, LANES), lambda i: (0, 0)), pl.BlockSpec((1, LANES), lambda i: (0, 0))],
        out_specs=[narrow, narrow, pl.BlockSpec((tm, D), lambda i: (i, 0))],
        out_shape=[jax.ShapeDtypeStruct((T, LANES), jnp.int32), jax.ShapeDtypeStruct((T, LANES), jnp.float32),
                   jax.ShapeDtypeStruct((T, D), jnp.float32)],
        compiler_params=_params(("parallel",)),
        name="router",
    )(x2, sc, sh, w_r, b_r)


def _experts_kernel(be_ref, nu_ref, x_ref, sw_ref, wg_ref, wu_ref, wd_ref, o_ref, wg_s, wu_s, wd_s):
    i = pl.program_id(0)
    used = i < nu_ref[0]
    new_expert = (i == 0) | (be_ref[i] != be_ref[jnp.maximum(i - 1, 0)])

    @pl.when(used & new_expert)
    def _():
        wg_s[...] = wg_ref[...].astype(wg_s.dtype)
        wu_s[...] = wu_ref[...].astype(wu_s.dtype)
        wd_s[...] = wd_ref[...].astype(wd_s.dtype)

    @pl.when(used)
    def _():
        x = x_ref[...].astype(jnp.bfloat16)
        gate = _dot(x, wg_s[...])
        hid = gate * _sigmoid(gate) * _dot(x, wu_s[...])
        o_ref[...] = _dot(hid.astype(jnp.bfloat16), wd_s[...]) * sw_ref[...]

    @pl.when(jnp.logical_not(used))
    def _():
        o_ref[...] = jnp.zeros_like(o_ref)


def experts(xs, slot_w, block_e, n_used, w_gate, w_up, w_down, layer):
    R, D = xs.shape
    F = w_gate.shape[-1]
    n_blocks = R // MOE_BLOCK
    grid_spec = pltpu.PrefetchScalarGridSpec(
        num_scalar_prefetch=2,
        grid=(n_blocks,),
        in_specs=[pl.BlockSpec((MOE_BLOCK, D), lambda i, be, nu: (i, 0)),
                  pl.BlockSpec((MOE_BLOCK, 1), lambda i, be, nu: (i, 0)),
                  pl.BlockSpec((None, None, D, F), lambda i, be, nu: (layer, be[i], 0, 0)),
                  pl.BlockSpec((None, None, D, F), lambda i, be, nu: (layer, be[i], 0, 0)),
                  pl.BlockSpec((None, None, F, D), lambda i, be, nu: (layer, be[i], 0, 0))],
        out_specs=pl.BlockSpec((MOE_BLOCK, D), lambda i, be, nu: (i, 0)),
        scratch_shapes=[pltpu.VMEM((D, F), jnp.bfloat16), pltpu.VMEM((D, F), jnp.bfloat16),
                        pltpu.VMEM((F, D), jnp.bfloat16)],
    )
    return pl.pallas_call(
        _experts_kernel,
        grid_spec=grid_spec,
        out_shape=jax.ShapeDtypeStruct((R, D), jnp.float32),
        compiler_params=_params(("arbitrary",)),
        name="experts",
    )(block_e, n_used, xs, slot_w, w_gate, w_up, w_down)


def _combine_ln_kernel(x_ref, f0_ref, f1_ref, g2_ref, lg_ref, lb_ref, o_ref, *, alpha):
    y = alpha * x_ref[...] + (1.0 + g2_ref[...]) * (f0_ref[...] + f1_ref[...])
    o_ref[...] = _layer_norm(y, lg_ref[...], lb_ref[...])


def combine_ln(x2, f0, f1, g2, ln_g, ln_b, seq, tm, alpha):
    T, D = x2.shape
    per = seq // tm
    tile = pl.BlockSpec((tm, D), lambda i: (i, 0))
    row = pl.BlockSpec((1, D), lambda i: (0, 0))
    return pl.pallas_call(
        functools.partial(_combine_ln_kernel, alpha=alpha),
        grid=(T // tm,),
        in_specs=[tile, tile, tile, pl.BlockSpec((None, 1, D), lambda i: (i // per, 0, 0)), row, row],
        out_specs=tile,
        out_shape=jax.ShapeDtypeStruct((T, D), jnp.float32),
        compiler_params=_params(("parallel",)),
        name="combine_ln",
    )(x2, f0, f1, g2, ln_g, ln_b)


def _route(expert_id, gate_w):
    T = expert_id.shape[0]
    E = MOE_GROUPS * MOE_EXPERTS_PER_GROUP
    A = T * MOE_TOPK
    n_blocks = (A + E * (MOE_BLOCK - 1) + MOE_BLOCK - 1) // MOE_BLOCK
    flat_e = expert_id.reshape(A)
    flat_w = gate_w.reshape(A)
    e_s, order = lax.sort_key_val(flat_e, jnp.arange(A, dtype=jnp.int32))
    bounds = jnp.searchsorted(e_s, jnp.arange(E + 1, dtype=jnp.int32), side='left').astype(jnp.int32)
    start = bounds[:E]
    counts = bounds[1:] - start
    padded = (counts + MOE_BLOCK - 1) // MOE_BLOCK * MOE_BLOCK
    pad_end = jnp.cumsum(padded)
    pad_start = pad_end - padded
    block_e = jnp.minimum(jnp.sum(pad_end[None, :] <= (jnp.arange(n_blocks) * MOE_BLOCK)[:, None], axis=1),
                          E - 1).astype(jnp.int32)
    blk_rank = jnp.arange(n_blocks, dtype=jnp.int32) * MOE_BLOCK - pad_start[block_e]
    rank = blk_rank[:, None] + jnp.arange(MOE_BLOCK, dtype=jnp.int32)[None, :]
    valid = (rank < counts[block_e][:, None]).reshape(-1)
    src = order[jnp.clip(start[block_e][:, None] + rank, 0, A - 1).reshape(-1)]
    spread = jnp.arange(n_blocks * MOE_BLOCK, dtype=jnp.int32) % T
    slot_tok = jnp.where(valid, src // MOE_TOPK, spread).astype(jnp.int32)
    slot_w = jnp.where(valid, flat_w[src], 0.0)
    pos = jnp.arange(A, dtype=jnp.int32)[None, :]
    in_e = (pos >= start[:, None]) & (pos < bounds[1:, None])
    dest = pos[0] + jnp.sum(jnp.where(in_e, (pad_start - start)[:, None], 0), axis=0).astype(jnp.int32)
    _, slot_of = lax.sort_key_val(order, dest)
    n_used = (pad_end[-1] // MOE_BLOCK).astype(jnp.int32).reshape(1)
    return slot_tok, slot_w, slot_of.reshape(T, MOE_TOPK), block_e, n_used


def kernel(x, c, rel_bias, w_in, p_a, p_b, p_c, w_o, rwkv_mu, rwkv_w0, rwkv_w_up, rwkv_a0, rwkv_a_up,
           rwkv_g_up, rwkv_k_k, rwkv_k_a, rwkv_r_k, rwkv_ln_g, rwkv_ln_b, rwkv_v0, rwkv_mv_down,
           rwkv_mv_up, w_ada, b_ada, ln1_g, ln1_b, ln2_g, ln2_b, router_grp_w, router_grp_b,
           router_exp_w, router_exp_b, exp_w_gate, exp_w_up, exp_w_down):
    B, S, D = x.shape
    depth = w_in.shape[0]
    T = B * S
    bf = jnp.bfloat16
    W = RWKV_HEADS * HEAD_DIM
    wa_w = MOBA_HEADS * HEAD_DIM
    wc_w = len(DIL_GROUPS) * DIL_HEADS_PER_GROUP * HEAD_DIM
    rw_cols = rwkv_mu.shape[-1]
    alpha = (2 * depth) ** 0.25
    off_c = 3 * wa_w
    off_b = off_c + 3 * wc_w
    off_g = off_b + rw_cols
    new_a = rw_cols
    new_c = new_a + 3 * wa_w
    new_g = new_c + 3 * wc_w
    w_in_p = jnp.concatenate([w_in[:, :, off_b:off_g], w_in[:, :, :off_b], w_in[:, :, off_g:]], axis=-1).astype(bf)
    p_a_b, p_b_b, p_c_b, w_o_b = p_a.astype(bf), p_b.astype(bf), p_c.astype(bf), w_o.astype(bf)

    bias_h = rel_bias.T.astype(jnp.float32)
    moba_tab = moba_bias_table(bias_h[:MOBA_HEADS], S // MOBA_BLOCK)
    dil_tab = dil_bias_table(bias_h[MOBA_HEADS:])

    c8 = jnp.zeros((8, D), jnp.float32).at[:B].set(c)
    mod = ada_mod(c8, w_ada, b_ada)[:, :B]

    hd_idx = jnp.arange(LANES) // HEAD_DIM
    bd = (hd_idx[:, None] == hd_idx[None, :]).astype(jnp.float32)
    zeros_w = jnp.zeros((RWKV_DECAY_LORA, W), jnp.float32)
    pad_lora = LANES - RWKV_MV_LORA

    x2 = x.reshape(T, D)
    v_first = None
    for l in range(depth):
        sh1, sc1, g1, sh2, sc2, g2 = [m.reshape(B, 1, D) for m in jnp.split(mod[l], 6, axis=-1)]
        proj = in_proj(x2, sc1, sh1, w_in_p, l, S, 1024, 1024)
        proj3 = proj.reshape(B, S, -1)
        o_a = moba_attention(proj3, moba_tab, new_a // LANES, (new_a + wa_w) // LANES,
                             (new_a + 2 * wa_w) // LANES).reshape(T, wa_w)
        gw = DIL_HEADS_PER_GROUP * HEAD_DIM
        o_c, lw_c = [], []
        for g in range(len(DIL_GROUPS)):
            og, lwg = dilated_attention(proj3, dil_tab, (new_c + g * gw) // LANES,
                                        (new_c + wc_w + g * gw) // LANES, (new_c + 2 * wc_w + g * gw) // LANES, g)
            o_c.append(og.reshape(T, gw))
            lw_c.append(lwg.reshape(T, gw))
        wa_up = jnp.concatenate([jnp.concatenate([rwkv_w_up[l], zeros_w], axis=1),
                                 jnp.concatenate([zeros_w, rwkv_a_up[l]], axis=1)], axis=0)
        res = None
        if l > 0:
            res = (v_first, rwkv_v0[l - 1][None],
                   jnp.pad(rwkv_mv_down[l - 1], ((0, 0), (0, pad_lora))),
                   jnp.pad(rwkv_mv_up[l - 1], ((0, pad_lora), (0, 0))))
        r_, lw_, k_, v_, kn_, b_, g_ = rwkv_prep(proj, S, 256, rwkv_mu[l][None], rwkv_w0[l][None],
                                                 rwkv_a0[l][None], wa_up, rwkv_g_up[l], rwkv_k_k[l][None],
                                                 rwkv_k_a[l][None], bd, res)
        if l == 0:
            v_first = v_
        rhat, y0, e_, gmat, dmat = rwkv_chunk(r_, lw_, k_, v_, kn_, b_, g_, rwkv_r_k[l][None], bd, B)
        o_b = rwkv_state(rhat, y0, g_, e_, gmat, dmat, rwkv_ln_g[l][None], rwkv_ln_b[l][None], bd)
        merged = merge_branches(o_a, o_b, o_c, lw_c, proj, new_g // 1024, p_a_b[l], p_b_b[l], p_c_b[l],
                                256, 1024)
        x2 = out_ln(merged, w_o_b[l], x2, g1, ln1_g[l][None], ln1_b[l][None], S, 256, alpha)
        w_r = jnp.zeros((D, LANES), jnp.float32)
        w_r = w_r.at[:, :MOE_GROUPS].set(router_grp_w[l]).at[:, MOE_GROUPS:MOE_GROUPS + router_exp_w.shape[-1]].set(
            router_exp_w[l])
        b_r = jnp.zeros((1, LANES), jnp.float32)
        b_r = b_r.at[0, :MOE_GROUPS].set(router_grp_b[l]).at[0, MOE_GROUPS:MOE_GROUPS + router_exp_b.shape[-1]].set(
            router_exp_b[l])
        ids, gws, h2 = router(x2, sc2, sh2, w_r, b_r, S, 512)
        slot_tok, slot_w, slot_of, block_e, n_used = _route(ids[:, :MOE_TOPK], gws[:, :MOE_TOPK])
        y = experts(h2[slot_tok], slot_w[:, None], block_e, n_used, exp_w_gate, exp_w_up, exp_w_down, l)
        x2 = combine_ln(x2, y[slot_of[:, 0]], y[slot_of[:, 1]], g2, ln2_g[l][None], ln2_b[l][None], S, 512, alpha)
    return x2.reshape(B, S, D)
```

```python
import functools
import math

import jax
import jax.numpy as jnp
import numpy as np
from jax import lax
from jax.experimental import pallas as pl
from jax.experimental.pallas import tpu as pltpu

HEAD_DIM = 64
LANES = 128
MOBA_HEADS = 12
MOBA_BLOCK = 256
MOBA_TOPK = 3
MOBA_STEP = 2
RWKV_HEADS = 12
RWKV_DECAY_LORA = 64
RWKV_A_LORA = 64
RWKV_MV_LORA = 32
RWKV_GATE_LORA = 128
RWKV_GN_EPS = 64e-5
RWKV_CHUNK = 64
DIL_GROUPS = ((128, 1), (512, 4), (2048, 16))
DIL_HEADS_PER_GROUP = 4
DIL_SPAN = 128
REL_BUCKETS = 32
REL_MAX_DISTANCE = 2048
MOE_GROUPS = 8
MOE_EXPERTS_PER_GROUP = 8
MOE_TOPK = 2
MOE_BLOCK = 256
LN_EPS = 1e-5
NEG = -1e30
LOG2E = math.log2(math.e)
LN2 = math.log(2.0)
VMEM_LIMIT = 56 * 1024 * 1024
HI = lax.Precision.HIGHEST


def _params(sem):
    return pltpu.CompilerParams(dimension_semantics=sem, vmem_limit_bytes=VMEM_LIMIT)


def _sigmoid(x):
    return 0.5 * jnp.tanh(0.5 * x) + 0.5


def _dot(a, b, precision=None):
    return jnp.dot(a, b, preferred_element_type=jnp.float32, precision=precision)


def _dot_nt(a, b, precision=None):
    return lax.dot_general(a, b, (((1,), (1,)), ((), ())), preferred_element_type=jnp.float32,
                           precision=precision)


def _dot_tn(a, b, precision=None):
    return lax.dot_general(a, b, (((0,), (0,)), ((), ())), preferred_element_type=jnp.float32,
                           precision=precision)


def _split2(x):
    hi = x.astype(jnp.bfloat16)
    return hi, (x - hi.astype(jnp.float32)).astype(jnp.bfloat16)


def _dot3(a, b, dims=((1,), (0,))):
    (ca,), (cb,) = dims
    ah, al = _split2(a)
    bh, bl = _split2(b)
    return lax.dot_general(jnp.concatenate([ah, ah, al], axis=ca), jnp.concatenate([bh, bl, bh], axis=cb),
                           (dims, ((), ())), preferred_element_type=jnp.float32)


def _dot_sel(x, sel):
    xh, xl = _split2(x)
    sb = sel.astype(jnp.bfloat16)
    return _dot(jnp.concatenate([xh, xl], axis=1), jnp.concatenate([sb, sb], axis=0))


_NT = ((1,), (1,))
_TN = ((0,), (0,))


def _layer_norm(y, g, b):
    mu = jnp.mean(y, axis=-1, keepdims=True)
    d = y - mu
    var = jnp.mean(d * d, axis=-1, keepdims=True)
    return d * lax.rsqrt(var + LN_EPS) * g + b


def _ada_kernel(c_ref, w_ref, b_ref, o_ref):
    c = c_ref[...]
    cond = c * _sigmoid(c)
    o_ref[...] = _dot(cond, w_ref[...]) + b_ref[...]


def ada_mod(c8, w_ada, b_ada):
    L, D, N = w_ada.shape
    tn = 1024
    return pl.pallas_call(
        _ada_kernel,
        grid=(L, N // tn),
        in_specs=[pl.BlockSpec((8, D), lambda l, j: (0, 0)),
                  pl.BlockSpec((None, D, tn), lambda l, j: (l, 0, j)),
                  pl.BlockSpec((None, 1, tn), lambda l, j: (l, 0, j))],
        out_specs=pl.BlockSpec((None, 8, tn), lambda l, j: (l, 0, j)),
        out_shape=jax.ShapeDtypeStruct((L, 8, N), jnp.float32),
        compiler_params=_params(("parallel", "parallel")),
        name="ada_mod",
    )(c8, w_ada, b_ada.reshape(L, 1, N))


def _in_proj_kernel(x_ref, sc_ref, sh_ref, w_ref, o_ref, h_ref):
    @pl.when(pl.program_id(1) == 0)
    def _():
        h_ref[...] = (x_ref[...] * (1.0 + sc_ref[...]) + sh_ref[...]).astype(h_ref.dtype)

    o_ref[...] = _dot(h_ref[...], w_ref[...])


def in_proj(x2, sc, sh, w, layer, seq, tm, tn):
    T, D = x2.shape
    N = w.shape[-1]
    per = seq // tm
    return pl.pallas_call(
        _in_proj_kernel,
        grid=(T // tm, N // tn),
        in_specs=[pl.BlockSpec((tm, D), lambda i, j: (i, 0)),
                  pl.BlockSpec((None, 1, D), lambda i, j: (i // per, 0, 0)),
                  pl.BlockSpec((None, 1, D), lambda i, j: (i // per, 0, 0)),
                  pl.BlockSpec((None, D, tn), lambda i, j: (layer, 0, j))],
        out_specs=pl.BlockSpec((tm, tn), lambda i, j: (i, j)),
        out_shape=jax.ShapeDtypeStruct((T, N), jnp.float32),
        scratch_shapes=[pltpu.VMEM((tm, D), jnp.bfloat16)],
        compiler_params=_params(("parallel", "arbitrary")),
        name="in_proj",
    )(x2, sc, sh, w)


def _t5_bucket(dist):
    n = jnp.maximum(dist, 0)
    max_exact = REL_BUCKETS // 2
    nf = jnp.maximum(n, 1).astype(jnp.float32)
    large = max_exact + (jnp.log(nf / max_exact) / math.log(REL_MAX_DISTANCE / max_exact)
                         * (REL_BUCKETS - max_exact)).astype(jnp.int32)
    large = jnp.minimum(large, REL_BUCKETS - 1)
    return jnp.where(n < max_exact, n, large)


def _moba_n_delta(nb):
    last_start = 1
    d = np.arange(1, nb * MOBA_BLOCK + 1)
    large = 16 + (np.log(d / 16.0) / math.log(REL_MAX_DISTANCE / 16.0) * 16).astype(np.int64)
    bucket = np.where(d < 16, d, np.minimum(large, REL_BUCKETS - 1))
    last_start = int(d[bucket < REL_BUCKETS - 1].max()) + 1 if (bucket < REL_BUCKETS - 1).any() else 1
    delta = 1
    while delta * MOBA_BLOCK - (MOBA_BLOCK - 1) < last_start + 2:
        delta += 1
    return min(delta + 1, nb)


def moba_bias_table(bias_a, nb):
    nd = _moba_n_delta(nb)
    key = jnp.arange(MOBA_BLOCK)[:, None]
    qry = jnp.arange(MOBA_BLOCK)[None, :]
    dist = jnp.arange(nd)[:, None, None] * MOBA_BLOCK + (qry - key)[None]
    bucket = _t5_bucket(dist)[None]
    tab = jnp.zeros((bias_a.shape[0],) + dist.shape, jnp.float32)
    for b in range(REL_BUCKETS):
        tab = jnp.where(bucket == b, bias_a[:, b][:, None, None, None], tab)
    return jnp.where((dist >= 0)[None], tab * LOG2E, NEG)


def dil_bias_table(bias_c):
    span = DIL_SPAN
    rel = span + jnp.arange(span)[:, None] - jnp.arange(2 * span)[None, :]
    valid = (rel >= 0) & (rel <= span)
    tabs = []
    for g, (_, dilation) in enumerate(DIL_GROUPS):
        bh = bias_c[g * DIL_HEADS_PER_GROUP:(g + 1) * DIL_HEADS_PER_GROUP]
        tabs.append(jnp.where(valid[None], bh[:, _t5_bucket(rel * dilation)] * LOG2E, NEG))
    return jnp.stack(tabs)


def _moba_kernel(q_ref, k_ref, v_ref, bias_ref, o_ref,
                 kaug_ref, vt_ref, km_ref, acc_ref, s_ref, *, nb, n_delta):
    qb = pl.program_id(2)
    bs = MOBA_BLOCK
    S = nb * bs
    half = bs // 2
    lane = lax.broadcasted_iota(jnp.int32, (1, LANES), 1)
    f32 = jnp.float32

    @pl.when(qb == 0)
    def _():
        k = k_ref[...]
        rowblk = lax.broadcasted_iota(jnp.int32, (S, LANES), 0) // bs
        lanes = lax.broadcasted_iota(jnp.int32, (S, LANES), 1)
        km_ref[...] = jnp.mean(k.reshape(nb, bs, LANES), axis=1)
        kaug_ref[0] = jnp.where(lanes < HEAD_DIM, k, (lanes - HEAD_DIM == rowblk).astype(f32)).astype(kaug_ref.dtype)
        kaug_ref[1] = jnp.where(lanes >= HEAD_DIM, k, (lanes == rowblk).astype(f32)).astype(kaug_ref.dtype)
        for i in range(nb):
            vt_ref[i] = v_ref[i * bs:(i + 1) * bs, :].T.astype(vt_ref.dtype)

    q_t = q_ref[...].T
    scale = HEAD_DIM ** -0.5 * LOG2E
    blk = lax.broadcasted_iota(jnp.int32, (nb, 1), 0)
    past = blk < qb
    rhs = []
    for t in range(2):
        hm = (lane >= t * HEAD_DIM) & (lane < (t + 1) * HEAD_DIM)
        gate = _dot(jnp.where(hm, km_ref[...], 0.0), q_t, precision=HI)
        gate = jnp.where(past, gate, -jnp.inf)
        cnt = jnp.zeros((nb, bs), jnp.int32)
        for m in range(nb):
            gm = gate[m:m + 1, :]
            ahead = (gm > gate) | ((gm == gate) & (m < blk))
            cnt = cnt + ahead.astype(jnp.int32)
        chosen = (past & (cnt < MOBA_TOPK)) | (blk == qb)
        pen = jnp.where(chosen, 0.0, NEG)
        qh = q_t[t * HEAD_DIM:(t + 1) * HEAD_DIM, :] * scale
        if t == 0:
            parts = [qh, pen, jnp.zeros((LANES - HEAD_DIM - nb, bs), f32)]
        else:
            parts = [pen, jnp.zeros((HEAD_DIM - nb, bs), f32), qh]
        rhs.append(jnp.concatenate(parts, axis=0).astype(jnp.bfloat16))
    acc_ref[...] = jnp.zeros((LANES, bs), f32)

    chains = [(t, hq) for t in range(2) for hq in range(2)]
    cols = [slice(hq * half, (hq + 1) * half) for _, hq in chains]

    def logits(slot, k):
        blocks = [jnp.minimum(MOBA_STEP * k + j, nb - 1) for j in range(MOBA_STEP)]
        rows = [pl.multiple_of(n * bs, bs) for n in blocks]
        delta = [jnp.clip(qb - n, 0, n_delta - 1) for n in blocks]
        for c, (t, _) in enumerate(chains):
            for j in range(MOBA_STEP):
                s_ref[slot, MOBA_STEP * c + j] = (_dot(kaug_ref[t, pl.ds(rows[j], bs), :], rhs[t][:, cols[c]])
                                                  + bias_ref[t, delta[j], :, cols[c]])

    def step(slot, k, carry):
        blocks = [MOBA_STEP * k + j for j in range(MOBA_STEP)]
        s = [[s_ref[slot, MOBA_STEP * c + j] for j in range(MOBA_STEP)] for c in range(4)]
        m_new = [functools.reduce(jnp.maximum, [carry[c][0]] + [jnp.max(x, axis=0, keepdims=True) for x in s[c]])
                 for c in range(4)]
        alpha = [jnp.exp2(carry[c][0] - m_new[c]) for c in range(4)]
        p = [[jnp.exp2(x - m_new[c]) for x in s[c]] for c in range(4)]
        l_new = [alpha[c] * carry[c][1] + sum(jnp.sum(x, axis=0, keepdims=True) for x in p[c]) for c in range(4)]
        pv = [sum(_dot(vt_ref[n, t * HEAD_DIM:(t + 1) * HEAD_DIM, :], p[c][j].astype(jnp.bfloat16))
                  for j, n in enumerate(blocks)) for c, (t, _) in enumerate(chains)]
        pieces = [alpha[c] * acc_ref[t * HEAD_DIM:(t + 1) * HEAD_DIM, cols[c]] + pv[c]
                  for c, (t, _) in enumerate(chains)]
        acc_ref[...] = jnp.concatenate([jnp.concatenate(pieces[:2], axis=1),
                                        jnp.concatenate(pieces[2:], axis=1)], axis=0)
        return tuple((m_new[c], l_new[c]) for c in range(4))

    init = tuple((jnp.full((1, half), NEG, f32), jnp.zeros((1, half), f32)) for _ in range(4))
    n_steps = qb // MOBA_STEP + 1
    logits(0, 0)

    def two_steps(i, carry):
        logits(1, 2 * i + 1)
        carry = step(0, 2 * i, carry)
        logits(0, 2 * i + 2)
        return step(1, 2 * i + 1, carry)

    fin = lax.fori_loop(0, (n_steps + 1) // 2, two_steps, init)
    l_all = jnp.concatenate([jnp.broadcast_to(jnp.concatenate([fin[2 * t][1], fin[2 * t + 1][1]], axis=1),
                                              (HEAD_DIM, bs)) for t in range(2)], axis=0)
    o_ref[...] = (acc_ref[...] / l_all).T


def moba_attention(proj3, bias_tab, q_off, k_off, v_off):
    B, S, _ = proj3.shape
    nb = S // MOBA_BLOCK
    assert nb % (2 * MOBA_STEP) == 0, "the two-step loop needs an even number of key-block steps"
    n_delta = bias_tab.shape[1]
    pairs = MOBA_HEADS // 2
    kern = functools.partial(_moba_kernel, nb=nb, n_delta=n_delta)
    return pl.pallas_call(
        kern,
        grid=(pairs, B, nb),
        in_specs=[pl.BlockSpec((None, MOBA_BLOCK, LANES), lambda h, b, i: (b, i, q_off + h)),
                  pl.BlockSpec((None, S, LANES), lambda h, b, i: (b, 0, k_off + h)),
                  pl.BlockSpec((None, S, LANES), lambda h, b, i: (b, 0, v_off + h)),
                  pl.BlockSpec((2, n_delta, MOBA_BLOCK, MOBA_BLOCK), lambda h, b, i: (h, 0, 0, 0))],
        out_specs=pl.BlockSpec((None, MOBA_BLOCK, LANES), lambda h, b, i: (b, i, h)),
        out_shape=jax.ShapeDtypeStruct((B, S, pairs * LANES), jnp.float32),
        scratch_shapes=[pltpu.VMEM((2, S, LANES), jnp.bfloat16),
                        pltpu.VMEM((nb, LANES, MOBA_BLOCK), jnp.bfloat16),
                        pltpu.VMEM((nb, LANES), jnp.float32),
                        pltpu.VMEM((LANES, MOBA_BLOCK), jnp.float32),
                        pltpu.VMEM((2, 4 * MOBA_STEP, MOBA_BLOCK, MOBA_BLOCK // 2), jnp.float32)],
        compiler_params=_params(("parallel", "parallel", "arbitrary")),
        name="moba",
    )(proj3, proj3, proj3, bias_tab)


def _dil_kernel(q_ref, k_ref, v_ref, bias_ref, o_ref, lw_ref, *, dilation, seq):
    span = DIL_SPAN
    bps = seq // dilation // span
    scale = HEAD_DIM ** -0.5 * LOG2E
    lane = lax.broadcasted_iota(jnp.int32, (1, LANES), 1)
    col = lax.broadcasted_iota(jnp.int32, (1, 2 * span), 1)
    hm = [lane < HEAD_DIM, lane >= HEAD_DIM]
    unroll = 2

    def body(it, carry):
        blocks = []
        for u in range(unroll):
            j = it * unroll + u
            r = j // bps
            i = j - r * bps
            start = r + i * (span * dilation)
            prev = jnp.maximum(start - span * dilation, r)
            rows = lambda s0: pl.ds(s0, span, stride=dilation)
            q = q_ref[rows(start), :] * scale
            kcat = jnp.concatenate([k_ref[rows(prev), :], k_ref[rows(start), :]], axis=0).astype(jnp.bfloat16)
            vcat = jnp.concatenate([v_ref[rows(prev), :], v_ref[rows(start), :]], axis=0).astype(jnp.bfloat16)
            blocks.append((start, q, kcat, vcat, (i == 0) & (col < span)))
        chains = [(u, t) for u in range(unroll) for t in range(2)]
        s = [jnp.where(blocks[u][4], NEG,
                       _dot_nt(jnp.where(hm[t], blocks[u][1], 0.0).astype(jnp.bfloat16), blocks[u][2]) + bias_ref[t])
             for u, t in chains]
        m = [jnp.max(x, axis=-1, keepdims=True) for x in s]
        e = [jnp.exp2(x - mm) for x, mm in zip(s, m)]
        l = [jnp.sum(x, axis=-1, keepdims=True) for x in e]
        pv = [_dot(x.astype(jnp.bfloat16), blocks[u][3]) for x, (u, _) in zip(e, chains)]
        for u in range(unroll):
            a, b = 2 * u, 2 * u + 1
            rows = pl.ds(blocks[u][0], span, stride=dilation)
            o_ref[rows, :] = jnp.where(hm[0], pv[a] / l[a], pv[b] / l[b])
            lw_ref[rows, :] = jnp.where(hm[0], m[a] * LN2 + jnp.log(l[a]), m[b] * LN2 + jnp.log(l[b]))
        return carry

    lax.fori_loop(0, seq // span // unroll, body, 0)


def dilated_attention(proj3, bias_tab, q_off, k_off, v_off, group):
    B, S, _ = proj3.shape
    dilation = DIL_GROUPS[group][1]
    pairs = DIL_HEADS_PER_GROUP // 2
    slab = lambda off: pl.BlockSpec((None, S, LANES), lambda b, h: (b, 0, off + h))
    out = jax.ShapeDtypeStruct((B, S, pairs * LANES), jnp.float32)
    return pl.pallas_call(
        functools.partial(_dil_kernel, dilation=dilation, seq=S),
        grid=(B, pairs),
        in_specs=[slab(q_off), slab(k_off), slab(v_off),
                  pl.BlockSpec((None, 2, DIL_SPAN, 2 * DIL_SPAN), lambda b, h: (group, h, 0, 0))],
        out_specs=[slab(0), slab(0)],
        out_shape=[out, out],
        compiler_params=_params(("parallel", "parallel")),
        name="dilated",
    )(proj3, proj3, proj3, bias_tab)


def _rwkv_prep_kernel(*refs, width, has_res, rows_per_seq):
    if has_res:
        (z_ref, zl_ref, mu_ref, w0_ref, a0_ref, wa_ref, gup_ref, kk_ref, ka_ref, bd_ref,
         vf_ref, v0_ref, mvd_ref, mvu_ref,
         r_o, lw_o, k_o, v_o, kn_o, b_o, g_o) = refs
    else:
        (z_ref, zl_ref, mu_ref, w0_ref, a0_ref, wa_ref, gup_ref, kk_ref, ka_ref, bd_ref,
         r_o, lw_o, k_o, v_o, kn_o, b_o, g_o) = refs
    i = pl.program_id(0)
    W = width
    z = z_ref[...]
    tm = z.shape[0]
    row = lax.broadcasted_iota(jnp.int32, (tm, 1), 0)
    seq_start = (i % rows_per_seq) == 0
    last = jnp.where(seq_start, 0.0, zl_ref[7:8, :])
    zp = jnp.where(row == 0, last, pltpu.roll(z, 1, 0))
    zf = z + mu_ref[...] * (zp - z)
    lora = zf[:, 3 * W:3 * W + LANES]
    lane = lax.broadcasted_iota(jnp.int32, (1, LANES), 1)
    lora = jnp.where(lane < RWKV_DECAY_LORA, jnp.tanh(lora), lora)
    wa = _dot3(lora, wa_ref[...])
    g = _dot3(_sigmoid(zf[:, 3 * W + LANES:3 * W + 2 * LANES]), gup_ref[...])
    g_o[...] = g
    v_all = zf[:, 2 * W:3 * W]
    if has_res:
        mix = _dot3(_dot3(v_all, mvd_ref[...]), mvu_ref[...])
    for c in range(W // LANES):
        sl = slice(c * LANES, (c + 1) * LANES)
        x = w0_ref[:, sl] + wa[:, sl]
        sp = jnp.maximum(-x, 0.0) + jnp.log(1.0 + jnp.exp(-jnp.abs(x)))
        lw_o[:, sl] = -jnp.exp(-sp - 0.5)
        a = _sigmoid(a0_ref[:, sl] + wa[:, W + c * LANES:W + (c + 1) * LANES])
        r_o[:, sl] = zf[:, sl]
        k = zf[:, W + c * LANES:W + (c + 1) * LANES]
        v = v_all[:, sl]
        if has_res:
            v = v + (vf_ref[:, sl] - v) * _sigmoid(v0_ref[:, sl] + mix[:, sl])
        v_o[:, sl] = v
        kk = k * kk_ref[:, sl]
        ss = _dot_sel(kk * kk, bd_ref[...])
        kn = kk / jnp.maximum(jnp.sqrt(ss), 1e-12)
        kn_o[:, sl] = kn
        b_o[:, sl] = kn * a
        k_o[:, sl] = k * (1.0 + (a - 1.0) * ka_ref[:, sl])


def rwkv_prep(proj, seq, tm, mu, w0, a0, wa_up, g_up, k_k, k_a, bd, res):
    T = proj.shape[0]
    W = RWKV_HEADS * HEAD_DIM
    cols = mu.shape[-1]
    row1 = lambda n: pl.BlockSpec((1, n), lambda i: (0, 0))
    full = lambda a: pl.BlockSpec(a.shape, lambda i: (0, 0))
    tile = pl.BlockSpec((tm, W), lambda i: (i, 0))
    in_specs = [pl.BlockSpec((tm, cols), lambda i: (i, 0)),
                pl.BlockSpec((8, cols), lambda i: (jnp.maximum(i * (tm // 8) - 1, 0), 0)),
                row1(cols), row1(W), row1(W), full(wa_up), full(g_up), row1(W), row1(W), full(bd)]
    args = [proj, proj, mu, w0, a0, wa_up, g_up, k_k, k_a, bd]
    if res is not None:
        v_first, v0, mvd, mvu = res
        in_specs += [tile, row1(W), full(mvd), full(mvu)]
        args += [v_first, v0, mvd, mvu]
    out = jax.ShapeDtypeStruct((T, W), jnp.float32)
    kern = functools.partial(_rwkv_prep_kernel, width=W, has_res=res is not None, rows_per_seq=seq // tm)
    return pl.pallas_call(
        kern,
        grid=(T // tm,),
        in_specs=in_specs,
        out_specs=[tile] * 7,
        out_shape=[out] * 7,
        compiler_params=_params(("parallel",)),
        name="rwkv_prep",
    )(*args)


def _stack_heads(x, lane):
    return jnp.concatenate([jnp.where(lane < HEAD_DIM, x, 0.0), jnp.where(lane >= HEAD_DIM, x, 0.0)], axis=0)


def _rwkv_chunk_kernel(r_ref, lw_ref, k_ref, v_ref, kn_ref, b_ref, g_ref, rk_ref, bd_ref,
                       rhat_ref, y0_ref, e_ref, g_out_ref, d_out_ref, *, chunks):
    C = RWKV_CHUNK
    C2, C4 = 2 * C, 4 * C
    f32 = jnp.float32
    ti = lax.broadcasted_iota(jnp.int32, (C, C), 0)
    si = lax.broadcasted_iota(jnp.int32, (C, C), 1)
    tri = (ti >= si).astype(f32)
    lane = lax.broadcasted_iota(jnp.int32, (1, LANES), 1)
    rho = lax.broadcasted_iota(jnp.int32, (C4, C4), 0)
    sig = lax.broadcasted_iota(jnp.int32, (C4, C4), 1)
    keep = jnp.where(rho >= C2, rho & (C - 1), (rho & (C - 1)) - 1) >= (sig & (C - 1))
    eye2 = (lax.broadcasted_iota(jnp.int32, (C2, C2), 0) == lax.broadcasted_iota(jnp.int32, (C2, C2), 1)).astype(f32)
    eye_l = (lax.broadcasted_iota(jnp.int32, (LANES, LANES), 0)
             == lax.broadcasted_iota(jnp.int32, (LANES, LANES), 1)).astype(f32)
    zeros2 = jnp.zeros((C2, LANES), f32)
    ch = range(chunks)
    rows = [slice(c * C, (c + 1) * C) for c in ch]
    cum = [_dot(tri, lw_ref[rows[c], :], precision=HI) for c in ch]
    st = []
    for c in ch:
        r, lw, k, v = r_ref[rows[c], :], lw_ref[rows[c], :], k_ref[rows[c], :], v_ref[rows[c], :]
        kn, bb = kn_ref[rows[c], :], b_ref[rows[c], :]
        cum_last = cum[c][C - 1:C, :]
        e_out = jnp.exp(-cum[c])
        e_tail = jnp.exp(cum_last - cum[c])
        r_t = r * jnp.exp(cum[c])
        st.append(dict(
            r_t=r_t, decay=jnp.exp(cum_last),
            a2=_stack_heads(-kn * jnp.exp(cum[c] - lw), lane), r2=_stack_heads(r_t, lane),
            b2=_stack_heads(bb * e_out, lane), k2=_stack_heads(k * e_out, lane), v2=_stack_heads(v, lane),
            bh2=_stack_heads(bb * e_tail, lane), kh2=_stack_heads(k * e_tail, lane),
            e=_dot_sel(r * k * rk_ref[...], bd_ref[...]) * v * g_ref[rows[c], :]))
    quad = [jnp.where(keep, _dot3(jnp.concatenate([s["a2"], s["r2"]], axis=0),
                                  jnp.concatenate([s["b2"], s["k2"]], axis=0), _NT), 0.0) for s in st]
    pw = [q[:C2, :C2] for q in quad]
    t_inv = [eye2 + p for p in pw]
    x = [_dot3(q[:C2, C2:], s["v2"]) for q, s in zip(quad, st)]
    for level in range(int(math.log2(C)) - 1):
        mm = _dot3 if level < 2 else (lambda a, b: _dot(a.astype(jnp.bfloat16), b.astype(jnp.bfloat16)))
        pw = [mm(p, p) for p in pw]
        t_inv = [t + mm(t, p) for t, p in zip(t_inv, pw)]
    au = [_dot3(t, jnp.concatenate([s["a2"], xx], axis=1)) for t, s, xx in zip(t_inv, st, x)]
    my = [_dot3(q[C2:, :], jnp.concatenate([a, jnp.concatenate([zeros2, s["v2"]], axis=1)], axis=0))
          for q, a, s in zip(quad, au, st)]
    gm = [eye_l * s["decay"] + _dot3(a[:, :LANES], s["bh2"], _TN) for a, s in zip(au, st)]
    dm = [_dot3(jnp.concatenate([a[:, LANES:], s["v2"]], axis=0),
                jnp.concatenate([s["bh2"], s["kh2"]], axis=0), _TN) for a, s in zip(au, st)]
    rhat_ref[...] = jnp.concatenate([s["r_t"] + m[:C, :LANES] + m[C:, :LANES] for s, m in zip(st, my)], axis=0)
    y0_ref[...] = jnp.concatenate([m[:C, LANES:] + m[C:, LANES:] for m in my], axis=0)
    e_ref[...] = jnp.concatenate([s["e"] for s in st], axis=0)
    g_out_ref[...] = jnp.stack(gm)
    d_out_ref[...] = jnp.stack(dm)


def rwkv_chunk(r, lw, k, v, kn, b, g, r_k, bd, batch, chunks=4):
    T, W = r.shape
    S = T // batch
    C = RWKV_CHUNK
    nc = S // C
    pairs = W // LANES
    steps = nc // chunks
    tile = pl.BlockSpec((chunks * C, LANES), lambda bi, h, c: (bi * steps + c, h))
    mat = pl.BlockSpec((None, None, chunks, LANES, LANES), lambda bi, h, c: (bi, h, c, 0, 0))
    tw = jax.ShapeDtypeStruct((T, W), jnp.float32)
    gd = jax.ShapeDtypeStruct((batch, pairs, nc, LANES, LANES), jnp.float32)
    return pl.pallas_call(
        functools.partial(_rwkv_chunk_kernel, chunks=chunks),
        grid=(batch, pairs, steps),
        in_specs=[tile] * 7 + [pl.BlockSpec((1, LANES), lambda bi, h, c: (0, h)),
                               pl.BlockSpec((LANES, LANES), lambda bi, h, c: (0, 0))],
        out_specs=[tile, tile, tile, mat, mat],
        out_shape=[tw, tw, tw, gd, gd],
        compiler_params=_params(("parallel", "parallel", "parallel")),
        name="rwkv_chunk",
    )(r, lw, k, v, kn, b, g, r_k, bd)


def _rwkv_state_kernel(rhat_ref, y0_ref, g_ref, e_ref, gm_ref, dm_ref, lng_ref, lnb_ref, bd_ref,
                       o_ref, state_ref, *, chunks, group):
    C = RWKV_CHUNK

    @pl.when(pl.program_id(2) == 0)
    def _():
        state_ref[...] = jnp.zeros_like(state_ref)

    bd = bd_ref[...]
    inv_n = 1.0 / HEAD_DIM

    def body(c, carry):
        rows = pl.ds(pl.multiple_of(c * C, C), C)
        pairs = range(group)
        state = [state_ref[p] for p in pairs]
        y = [_dot3(rhat_ref[rows, p * LANES:(p + 1) * LANES], state[p], _NT) for p in pairs]
        new_state = [_dot3(state[p], gm_ref[p, c]) + dm_ref[p, c] for p in pairs]
        y = jnp.concatenate(y, axis=1) + y0_ref[rows, :]
        state_ref[...] = jnp.stack(new_state)
        mean = jnp.concatenate([_dot_sel(y[:, p * LANES:(p + 1) * LANES], bd) for p in pairs], axis=1) * inv_n
        d = y - mean
        dd = d * d
        var = jnp.concatenate([_dot_sel(dd[:, p * LANES:(p + 1) * LANES], bd) for p in pairs], axis=1) * inv_n
        yn = d * lax.rsqrt(var + RWKV_GN_EPS) * lng_ref[...] + lnb_ref[...]
        o_ref[rows, :] = yn * g_ref[rows, :] + e_ref[rows, :]
        return carry

    lax.fori_loop(0, chunks, body, 0)


def rwkv_state(rhat, y0, g, e, gmat, dmat, ln_g, ln_b, bd, chunks=16, group=3):
    T, W = rhat.shape
    batch, pairs, nc = gmat.shape[:3]
    C = RWKV_CHUNK
    steps = nc // chunks
    gw = group * LANES
    tile = pl.BlockSpec((chunks * C, gw), lambda bi, h, c: (bi * steps + c, h))
    mat = pl.BlockSpec((None, group, chunks, LANES, LANES), lambda bi, h, c: (bi, h, c, 0, 0))
    row = pl.BlockSpec((1, gw), lambda bi, h, c: (0, h))
    return pl.pallas_call(
        functools.partial(_rwkv_state_kernel, chunks=chunks, group=group),
        grid=(batch, pairs // group, steps),
        in_specs=[tile] * 4 + [mat, mat, row, row, pl.BlockSpec((LANES, LANES), lambda bi, h, c: (0, 0))],
        out_specs=tile,
        out_shape=jax.ShapeDtypeStruct((T, W), jnp.float32),
        scratch_shapes=[pltpu.VMEM((group, LANES, LANES), jnp.float32)],
        compiler_params=_params(("parallel", "parallel", "arbitrary")),
        name="rwkv_state",
    )(rhat, y0, g, e, gmat, dmat, ln_g, ln_b, bd)


def _merge_kernel(oa_ref, ob_ref, oc0_ref, oc1_ref, oc2_ref, lw0_ref, lw1_ref, lw2_ref, ga_ref, gb_ref, gc_ref,
                  pa_ref, pb_ref, pc_ref, o_ref):
    lw = [lw0_ref[...], lw1_ref[...], lw2_ref[...]]
    mx = jnp.maximum(jnp.maximum(lw[0], lw[1]), lw[2])
    wgt = [jnp.exp(x - mx) for x in lw]
    o_c = ((wgt[0] * oc0_ref[...] + wgt[1] * oc1_ref[...] + wgt[2] * oc2_ref[...])
           / (wgt[0] + wgt[1] + wgt[2]))
    bf = jnp.bfloat16
    merged = (_sigmoid(ga_ref[...]) * _dot(oa_ref[...].astype(bf), pa_ref[...])
              + _sigmoid(gb_ref[...]) * _dot(ob_ref[...].astype(bf), pb_ref[...])
              + _sigmoid(gc_ref[...]) * _dot(o_c.astype(bf), pc_ref[...]))
    o_ref[...] = merged.astype(o_ref.dtype)


def merge_branches(o_a, o_b, o_c, lw_c, proj, gate_off, p_a, p_b, p_c, tm, tn):
    T = o_a.shape[0]
    D = p_a.shape[1]
    nj = D // tn
    grp = pl.BlockSpec((tm, o_c[0].shape[1]), lambda i, j: (i, 0))
    gate = lambda n: pl.BlockSpec((tm, tn), lambda i, j: (i, gate_off + n * nj + j))
    wspec = lambda a: pl.BlockSpec((a.shape[0], tn), lambda i, j: (0, j))
    return pl.pallas_call(
        _merge_kernel,
        grid=(T // tm, nj),
        in_specs=[pl.BlockSpec((tm, o_a.shape[1]), lambda i, j: (i, 0)),
                  pl.BlockSpec((tm, o_b.shape[1]), lambda i, j: (i, 0)),
                  grp, grp, grp, grp, grp, grp, gate(0), gate(1), gate(2), wspec(p_a), wspec(p_b), wspec(p_c)],
        out_specs=pl.BlockSpec((tm, tn), lambda i, j: (i, j)),
        out_shape=jax.ShapeDtypeStruct((T, D), jnp.bfloat16),
        compiler_params=_params(("parallel", "arbitrary")),
        name="merge",
    )(o_a, o_b, *o_c, *lw_c, proj, proj, proj, p_a, p_b, p_c)


def _out_ln_kernel(mg_ref, w_ref, x_ref, g1_ref, lg_ref, lb_ref, o_ref, *, alpha):
    mix = _dot(mg_ref[...], w_ref[...])
    y = alpha * x_ref[...] + (1.0 + g1_ref[...]) * mix
    o_ref[...] = _layer_norm(y, lg_ref[...], lb_ref[...])


def out_ln(merged, w_o, x2, g1, ln_g, ln_b, seq, tm, alpha):
    T, D = x2.shape
    per = seq // tm
    tile = pl.BlockSpec((tm, D), lambda i: (i, 0))
    row = pl.BlockSpec((1, D), lambda i: (0, 0))
    return pl.pallas_call(
        functools.partial(_out_ln_kernel, alpha=alpha),
        grid=(T // tm,),
        in_specs=[tile, pl.BlockSpec((D, D), lambda i: (0, 0)), tile,
                  pl.BlockSpec((None, 1, D), lambda i: (i // per, 0, 0)), row, row],
        out_specs=tile,
        out_shape=jax.ShapeDtypeStruct((T, D), jnp.float32),
        compiler_params=_params(("parallel",)),
        name="out_ln",
    )(merged, w_o, x2, g1, ln_g, ln_b)


def _router_kernel(x_ref, sc_ref, sh_ref, w_ref, b_ref, id_ref, gw_ref, h_ref):
    h = x_ref[...] * (1.0 + sc_ref[...]) + sh_ref[...]
    h_ref[...] = h.astype(h_ref.dtype)
    lg = _dot3(h, w_ref[...]) + b_ref[...]
    G, EPG = MOE_GROUPS, MOE_EXPERTS_PER_GROUP
    lane = lax.broadcasted_iota(jnp.int32, (1, LANES), 1).astype(jnp.float32)
    first = lambda hit: jnp.min(jnp.where(hit, lane, float(LANES)), axis=-1, keepdims=True)
    is_grp = lane < G
    gmax = jnp.max(jnp.where(is_grp, lg, -jnp.inf), axis=-1, keepdims=True)
    ge = jnp.where(is_grp, jnp.exp(jnp.where(is_grp, lg, gmax) - gmax), 0.0)
    prob = ge / jnp.sum(ge, axis=-1, keepdims=True)
    grp_p = jnp.max(prob, axis=-1, keepdims=True)
    grp_i = first(is_grp & (prob == grp_p))
    lo = G + grp_i * EPG
    el = jnp.where((lane >= lo) & (lane < lo + EPG), lg, -jnp.inf)
    l1 = jnp.max(el, axis=-1, keepdims=True)
    i1 = first(el == l1)
    el = jnp.where(lane == i1, -jnp.inf, el)
    l2 = jnp.max(el, axis=-1, keepdims=True)
    i2 = first(el == l2)
    t = jnp.exp(l2 - l1)
    w1 = grp_p / (1.0 + t)
    id_ref[...] = jnp.where(lane == 0, i1 - G, jnp.where(lane == 1, i2 - G, 0.0)).astype(jnp.int32)
    gw_ref[...] = jnp.where(lane == 0, w1, jnp.where(lane == 1, w1 * t, 0.0))


def router(x2, sc, sh, w_r, b_r, seq, tm):
    T, D = x2.shape
    per = seq // tm
    mod = pl.BlockSpec((None, 1, D), lambda i: (i // per, 0, 0))
    narrow = pl.BlockSpec((tm, LANES), lambda i: (i, 0))
    return pl.pallas_call(
        _router_kernel,
        grid=(T // tm,),
        in_specs=[pl.BlockSpec((tm, D), lambda i: (i, 0)), mod, mod,
                  pl.BlockSpec((D, LANES), lambda i: (0, 0)), pl.BlockSpec((1, LANES), lambda i: (0, 0))],
        out_specs=[narrow, narrow, pl.BlockSpec((tm, D), lambda i: (i, 0))],
        out_shape=[jax.ShapeDtypeStruct((T, LANES), jnp.int32), jax.ShapeDtypeStruct((T, LANES), jnp.float32),
                   jax.ShapeDtypeStruct((T, D), jnp.bfloat16)],
        compiler_params=_params(("parallel",)),
        name="router",
    )(x2, sc, sh, w_r, b_r)


def _experts_kernel(be_ref, nu_ref, x_ref, sw_ref, wg_ref, wu_ref, wd_ref, o_ref, wg_s, wu_s, wd_s):
    i = pl.program_id(0)
    used = i < nu_ref[0]
    new_expert = (i == 0) | (be_ref[i] != be_ref[jnp.maximum(i - 1, 0)])

    @pl.when(used & new_expert)
    def _():
        wg_s[...] = wg_ref[...].astype(wg_s.dtype)
        wu_s[...] = wu_ref[...].astype(wu_s.dtype)
        wd_s[...] = wd_ref[...].astype(wd_s.dtype)

    @pl.when(used)
    def _():
        x = x_ref[...]
        gate = _dot(x, wg_s[...])
        hid = gate * _sigmoid(gate) * _dot(x, wu_s[...])
        o_ref[...] = _dot(hid.astype(jnp.bfloat16), wd_s[...]) * sw_ref[...]

    @pl.when(jnp.logical_not(used))
    def _():
        o_ref[...] = jnp.zeros_like(o_ref)


def experts(xs, slot_w, block_e, n_used, w_gate, w_up, w_down, layer):
    R, D = xs.shape
    F = w_gate.shape[-1]
    n_blocks = R // MOE_BLOCK
    grid_spec = pltpu.PrefetchScalarGridSpec(
        num_scalar_prefetch=2,
        grid=(n_blocks,),
        in_specs=[pl.BlockSpec((MOE_BLOCK, D), lambda i, be, nu: (i, 0)),
                  pl.BlockSpec((MOE_BLOCK, 1), lambda i, be, nu: (i, 0)),
                  pl.BlockSpec((None, None, D, F), lambda i, be, nu: (layer, be[i], 0, 0)),
                  pl.BlockSpec((None, None, D, F), lambda i, be, nu: (layer, be[i], 0, 0)),
                  pl.BlockSpec((None, None, F, D), lambda i, be, nu: (layer, be[i], 0, 0))],
        out_specs=pl.BlockSpec((MOE_BLOCK, D), lambda i, be, nu: (i, 0)),
        scratch_shapes=[pltpu.VMEM((D, F), jnp.bfloat16), pltpu.VMEM((D, F), jnp.bfloat16),
                        pltpu.VMEM((F, D), jnp.bfloat16)],
    )
    return pl.pallas_call(
        _experts_kernel,
        grid_spec=grid_spec,
        out_shape=jax.ShapeDtypeStruct((R, D), jnp.float32),
        compiler_params=_params(("arbitrary",)),
        name="experts",
    )(block_e, n_used, xs, slot_w, w_gate, w_up, w_down)


def _combine_ln_kernel(x_ref, f0_ref, f1_ref, g2_ref, lg_ref, lb_ref, o_ref, *, alpha):
    y = alpha * x_ref[...] + (1.0 + g2_ref[...]) * (f0_ref[...] + f1_ref[...])
    o_ref[...] = _layer_norm(y, lg_ref[...], lb_ref[...])


def combine_ln(x2, f0, f1, g2, ln_g, ln_b, seq, tm, alpha):
    T, D = x2.shape
    per = seq // tm
    tile = pl.BlockSpec((tm, D), lambda i: (i, 0))
    row = pl.BlockSpec((1, D), lambda i: (0, 0))
    return pl.pallas_call(
        functools.partial(_combine_ln_kernel, alpha=alpha),
        grid=(T // tm,),
        in_specs=[tile, tile, tile, pl.BlockSpec((None, 1, D), lambda i: (i // per, 0, 0)), row, row],
        out_specs=tile,
        out_shape=jax.ShapeDtypeStruct((T, D), jnp.float32),
        compiler_params=_params(("parallel",)),
        name="combine_ln",
    )(x2, f0, f1, g2, ln_g, ln_b)


def _route(expert_id, gate_w):
    T = expert_id.shape[0]
    E = MOE_GROUPS * MOE_EXPERTS_PER_GROUP
    A = T * MOE_TOPK
    n_blocks = (A + E * (MOE_BLOCK - 1) + MOE_BLOCK - 1) // MOE_BLOCK
    flat_e = expert_id.reshape(A)
    flat_w = gate_w.reshape(A)
    e_s, order = lax.sort_key_val(flat_e, jnp.arange(A, dtype=jnp.int32))
    bounds = jnp.searchsorted(e_s, jnp.arange(E + 1, dtype=jnp.int32), side='left').astype(jnp.int32)
    start = bounds[:E]
    counts = bounds[1:] - start
    padded = (counts + MOE_BLOCK - 1) // MOE_BLOCK * MOE_BLOCK
    pad_end = jnp.cumsum(padded)
    pad_start = pad_end - padded
    block_e = jnp.minimum(jnp.sum(pad_end[None, :] <= (jnp.arange(n_blocks) * MOE_BLOCK)[:, None], axis=1),
                          E - 1).astype(jnp.int32)
    blk_rank = jnp.arange(n_blocks, dtype=jnp.int32) * MOE_BLOCK - pad_start[block_e]
    rank = blk_rank[:, None] + jnp.arange(MOE_BLOCK, dtype=jnp.int32)[None, :]
    valid = (rank < counts[block_e][:, None]).reshape(-1)
    src = order[jnp.clip(start[block_e][:, None] + rank, 0, A - 1).reshape(-1)]
    spread = jnp.arange(n_blocks * MOE_BLOCK, dtype=jnp.int32) % T
    slot_tok = jnp.where(valid, src // MOE_TOPK, spread).astype(jnp.int32)
    slot_w = jnp.where(valid, flat_w[src], 0.0)
    pos = jnp.arange(A, dtype=jnp.int32)[None, :]
    in_e = (pos >= start[:, None]) & (pos < bounds[1:, None])
    dest = pos[0] + jnp.sum(jnp.where(in_e, (pad_start - start)[:, None], 0), axis=0).astype(jnp.int32)
    _, slot_of = lax.sort_key_val(order, dest)
    n_used = (pad_end[-1] // MOE_BLOCK).astype(jnp.int32).reshape(1)
    return slot_tok, slot_w, slot_of.reshape(T, MOE_TOPK), block_e, n_used


def kernel(x, c, rel_bias, w_in, p_a, p_b, p_c, w_o, rwkv_mu, rwkv_w0, rwkv_w_up, rwkv_a0, rwkv_a_up,
           rwkv_g_up, rwkv_k_k, rwkv_k_a, rwkv_r_k, rwkv_ln_g, rwkv_ln_b, rwkv_v0, rwkv_mv_down,
           rwkv_mv_up, w_ada, b_ada, ln1_g, ln1_b, ln2_g, ln2_b, router_grp_w, router_grp_b,
           router_exp_w, router_exp_b, exp_w_gate, exp_w_up, exp_w_down):
    B, S, D = x.shape
    depth = w_in.shape[0]
    T = B * S
    bf = jnp.bfloat16
    W = RWKV_HEADS * HEAD_DIM
    wa_w = MOBA_HEADS * HEAD_DIM
    wc_w = len(DIL_GROUPS) * DIL_HEADS_PER_GROUP * HEAD_DIM
    rw_cols = rwkv_mu.shape[-1]
    alpha = (2 * depth) ** 0.25
    off_c = 3 * wa_w
    off_b = off_c + 3 * wc_w
    off_g = off_b + rw_cols
    new_a = rw_cols
    new_c = new_a + 3 * wa_w
    new_g = new_c + 3 * wc_w
    w_in_p = jnp.concatenate([w_in[:, :, off_b:off_g], w_in[:, :, :off_b], w_in[:, :, off_g:]], axis=-1).astype(bf)
    p_a_b, p_b_b, p_c_b, w_o_b = p_a.astype(bf), p_b.astype(bf), p_c.astype(bf), w_o.astype(bf)

    bias_h = rel_bias.T.astype(jnp.float32)
    moba_tab = moba_bias_table(bias_h[:MOBA_HEADS], S // MOBA_BLOCK)
    dil_tab = dil_bias_table(bias_h[MOBA_HEADS:])

    c8 = jnp.zeros((8, D), jnp.float32).at[:B].set(c)
    mod = ada_mod(c8, w_ada, b_ada)[:, :B]

    hd_idx = jnp.arange(LANES) // HEAD_DIM
    bd = (hd_idx[:, None] == hd_idx[None, :]).astype(jnp.float32)
    zeros_w = jnp.zeros((RWKV_DECAY_LORA, W), jnp.float32)
    pad_lora = LANES - RWKV_MV_LORA

    x2 = x.reshape(T, D)
    v_first = None
    for l in range(depth):
        sh1, sc1, g1, sh2, sc2, g2 = [m.reshape(B, 1, D) for m in jnp.split(mod[l], 6, axis=-1)]
        proj = in_proj(x2, sc1, sh1, w_in_p, l, S, 1024, 1024)
        proj3 = proj.reshape(B, S, -1)
        o_a = moba_attention(proj3, moba_tab, new_a // LANES, (new_a + wa_w) // LANES,
                             (new_a + 2 * wa_w) // LANES).reshape(T, wa_w)
        gw = DIL_HEADS_PER_GROUP * HEAD_DIM
        o_c, lw_c = [], []
        for g in range(len(DIL_GROUPS)):
            og, lwg = dilated_attention(proj3, dil_tab, (new_c + g * gw) // LANES,
                                        (new_c + wc_w + g * gw) // LANES, (new_c + 2 * wc_w + g * gw) // LANES, g)
            o_c.append(og.reshape(T, gw))
            lw_c.append(lwg.reshape(T, gw))
        wa_up = jnp.concatenate([jnp.concatenate([rwkv_w_up[l], zeros_w], axis=1),
                                 jnp.concatenate([zeros_w, rwkv_a_up[l]], axis=1)], axis=0)
        res = None
        if l > 0:
            res = (v_first, rwkv_v0[l - 1][None],
                   jnp.pad(rwkv_mv_down[l - 1], ((0, 0), (0, pad_lora))),
                   jnp.pad(rwkv_mv_up[l - 1], ((0, pad_lora), (0, 0))))
        r_, lw_, k_, v_, kn_, b_, g_ = rwkv_prep(proj, S, 512, rwkv_mu[l][None], rwkv_w0[l][None],
                                                 rwkv_a0[l][None], wa_up, rwkv_g_up[l], rwkv_k_k[l][None],
                                                 rwkv_k_a[l][None], bd, res)
        if l == 0:
            v_first = v_
        rhat, y0, e_, gmat, dmat = rwkv_chunk(r_, lw_, k_, v_, kn_, b_, g_, rwkv_r_k[l][None], bd, B)
        o_b = rwkv_state(rhat, y0, g_, e_, gmat, dmat, rwkv_ln_g[l][None], rwkv_ln_b[l][None], bd)
        merged = merge_branches(o_a, o_b, o_c, lw_c, proj, new_g // 1024, p_a_b[l], p_b_b[l], p_c_b[l],
                                512, 1024)
        x2 = out_ln(merged, w_o_b[l], x2, g1, ln1_g[l][None], ln1_b[l][None], S, 512, alpha)
        w_r = jnp.zeros((D, LANES), jnp.float32)
        w_r = w_r.at[:, :MOE_GROUPS].set(router_grp_w[l]).at[:, MOE_GROUPS:MOE_GROUPS + router_exp_w.shape[-1]].set(
            router_exp_w[l])
        b_r = jnp.zeros((1, LANES), jnp.float32)
        b_r = b_r.at[0, :MOE_GROUPS].set(router_grp_b[l]).at[0, MOE_GROUPS:MOE_GROUPS + router_exp_b.shape[-1]].set(
            router_exp_b[l])
        ids, gws, h2 = router(x2, sc2, sh2, w_r, b_r, S, 512)
        slot_tok, slot_w, slot_of, block_e, n_used = _route(ids[:, :MOE_TOPK], gws[:, :MOE_TOPK])
        y = experts(h2[slot_tok], slot_w[:, None], block_e, n_used, exp_w_gate, exp_w_up, exp_w_down, l)
        x2 = combine_ln(x2, y[slot_of[:, 0]], y[slot_of[:, 1]], g2, ln2_g[l][None], ln2_b[l][None], S, 512, alpha)
    return x2.reshape(B, S, D)
```

```python
import functools
import math

import jax
import jax.numpy as jnp
import numpy as np
from jax import lax
from jax.experimental import pallas as pl
from jax.experimental.pallas import tpu as pltpu

HEAD_DIM = 64
LANES = 128
MOBA_HEADS = 12
MOBA_BLOCK = 256
MOBA_TOPK = 3
MOBA_STEP = 2
RWKV_HEADS = 12
RWKV_DECAY_LORA = 64
RWKV_A_LORA = 64
RWKV_MV_LORA = 32
RWKV_GATE_LORA = 128
RWKV_GN_EPS = 64e-5
RWKV_CHUNK = 64
DIL_GROUPS = ((128, 1), (512, 4), (2048, 16))
DIL_HEADS_PER_GROUP = 4
DIL_SPAN = 128
REL_BUCKETS = 32
REL_MAX_DISTANCE = 2048
MOE_GROUPS = 8
MOE_EXPERTS_PER_GROUP = 8
MOE_TOPK = 2
MOE_BLOCK = 256
LN_EPS = 1e-5
NEG = -1e30
LOG2E = math.log2(math.e)
LN2 = math.log(2.0)
VMEM_LIMIT = 56 * 1024 * 1024
HI = lax.Precision.HIGHEST


def _params(sem):
    return pltpu.CompilerParams(dimension_semantics=sem, vmem_limit_bytes=VMEM_LIMIT)


def _sigmoid(x):
    return 0.5 * jnp.tanh(0.5 * x) + 0.5


def _dot(a, b, precision=None):
    return jnp.dot(a, b, preferred_element_type=jnp.float32, precision=precision)


def _dot_nt(a, b, precision=None):
    return lax.dot_general(a, b, (((1,), (1,)), ((), ())), preferred_element_type=jnp.float32,
                           precision=precision)


def _dot_tn(a, b, precision=None):
    return lax.dot_general(a, b, (((0,), (0,)), ((), ())), preferred_element_type=jnp.float32,
                           precision=precision)


def _split2(x):
    hi = x.astype(jnp.bfloat16)
    return hi, (x - hi.astype(jnp.float32)).astype(jnp.bfloat16)


def _dot3(a, b, dims=((1,), (0,))):
    (ca,), (cb,) = dims
    ah, al = _split2(a)
    bh, bl = _split2(b)
    return lax.dot_general(jnp.concatenate([ah, ah, al], axis=ca), jnp.concatenate([bh, bl, bh], axis=cb),
                           (dims, ((), ())), preferred_element_type=jnp.float32)


def _dot_sel(x, sel):
    xh, xl = _split2(x)
    sb = sel.astype(jnp.bfloat16)
    return _dot(jnp.concatenate([xh, xl], axis=1), jnp.concatenate([sb, sb], axis=0))


_NT = ((1,), (1,))
_TN = ((0,), (0,))


def _layer_norm(y, g, b):
    mu = jnp.mean(y, axis=-1, keepdims=True)
    d = y - mu
    var = jnp.mean(d * d, axis=-1, keepdims=True)
    return d * lax.rsqrt(var + LN_EPS) * g + b


def _ada_kernel(c_ref, w_ref, b_ref, o_ref):
    c = c_ref[...]
    cond = c * _sigmoid(c)
    o_ref[...] = _dot(cond, w_ref[...]) + b_ref[...]


def ada_mod(c8, w_ada, b_ada):
    L, D, N = w_ada.shape
    tn = 1024
    return pl.pallas_call(
        _ada_kernel,
        grid=(L, N // tn),
        in_specs=[pl.BlockSpec((8, D), lambda l, j: (0, 0)),
                  pl.BlockSpec((None, D, tn), lambda l, j: (l, 0, j)),
                  pl.BlockSpec((None, 1, tn), lambda l, j: (l, 0, j))],
        out_specs=pl.BlockSpec((None, 8, tn), lambda l, j: (l, 0, j)),
        out_shape=jax.ShapeDtypeStruct((L, 8, N), jnp.float32),
        compiler_params=_params(("parallel", "parallel")),
        name="ada_mod",
    )(c8, w_ada, b_ada.reshape(L, 1, N))


def _in_proj_kernel(x_ref, sc_ref, sh_ref, w_ref, o_ref, h_ref):
    @pl.when(pl.program_id(1) == 0)
    def _():
        h_ref[...] = (x_ref[...] * (1.0 + sc_ref[...]) + sh_ref[...]).astype(h_ref.dtype)

    o_ref[...] = _dot(h_ref[...], w_ref[...])


def in_proj(x2, sc, sh, w, layer, seq, tm, tn):
    T, D = x2.shape
    N = w.shape[-1]
    per = seq // tm
    return pl.pallas_call(
        _in_proj_kernel,
        grid=(T // tm, N // tn),
        in_specs=[pl.BlockSpec((tm, D), lambda i, j: (i, 0)),
                  pl.BlockSpec((None, 1, D), lambda i, j: (i // per, 0, 0)),
                  pl.BlockSpec((None, 1, D), lambda i, j: (i // per, 0, 0)),
                  pl.BlockSpec((None, D, tn), lambda i, j: (layer, 0, j))],
        out_specs=pl.BlockSpec((tm, tn), lambda i, j: (i, j)),
        out_shape=jax.ShapeDtypeStruct((T, N), jnp.float32),
        scratch_shapes=[pltpu.VMEM((tm, D), jnp.bfloat16)],
        compiler_params=_params(("parallel", "arbitrary")),
        name="in_proj",
    )(x2, sc, sh, w)


def _t5_bucket(dist):
    n = jnp.maximum(dist, 0)
    max_exact = REL_BUCKETS // 2
    nf = jnp.maximum(n, 1).astype(jnp.float32)
    large = max_exact + (jnp.log(nf / max_exact) / math.log(REL_MAX_DISTANCE / max_exact)
                         * (REL_BUCKETS - max_exact)).astype(jnp.int32)
    large = jnp.minimum(large, REL_BUCKETS - 1)
    return jnp.where(n < max_exact, n, large)


def _moba_n_delta(nb):
    last_start = 1
    d = np.arange(1, nb * MOBA_BLOCK + 1)
    large = 16 + (np.log(d / 16.0) / math.log(REL_MAX_DISTANCE / 16.0) * 16).astype(np.int64)
    bucket = np.where(d < 16, d, np.minimum(large, REL_BUCKETS - 1))
    last_start = int(d[bucket < REL_BUCKETS - 1].max()) + 1 if (bucket < REL_BUCKETS - 1).any() else 1
    delta = 1
    while delta * MOBA_BLOCK - (MOBA_BLOCK - 1) < last_start + 2:
        delta += 1
    return min(delta + 1, nb)


def moba_bias_table(bias_a, nb):
    nd = _moba_n_delta(nb)
    key = jnp.arange(MOBA_BLOCK)[:, None]
    qry = jnp.arange(MOBA_BLOCK)[None, :]
    dist = jnp.arange(nd)[:, None, None] * MOBA_BLOCK + (qry - key)[None]
    bucket = _t5_bucket(dist)[None]
    tab = jnp.zeros((bias_a.shape[0],) + dist.shape, jnp.float32)
    for b in range(REL_BUCKETS):
        tab = jnp.where(bucket == b, bias_a[:, b][:, None, None, None], tab)
    return jnp.where((dist >= 0)[None], tab * LOG2E, NEG)


def dil_bias_table(bias_c):
    span = DIL_SPAN
    rel = span + jnp.arange(span)[:, None] - jnp.arange(2 * span)[None, :]
    valid = (rel >= 0) & (rel <= span)
    tabs = []
    for g, (_, dilation) in enumerate(DIL_GROUPS):
        bh = bias_c[g * DIL_HEADS_PER_GROUP:(g + 1) * DIL_HEADS_PER_GROUP]
        tabs.append(jnp.where(valid[None], bh[:, _t5_bucket(rel * dilation)] * LOG2E, NEG))
    return jnp.stack(tabs)


def _moba_kernel(q_ref, k_ref, v_ref, bias_ref, o_ref,
                 kaug_ref, vt_ref, km_ref, acc_ref, s_ref, *, nb, n_delta):
    qb = pl.program_id(2)
    bs = MOBA_BLOCK
    S = nb * bs
    half = bs // 2
    lane = lax.broadcasted_iota(jnp.int32, (1, LANES), 1)
    f32 = jnp.float32

    @pl.when(qb == 0)
    def _():
        k = k_ref[...]
        rowblk = lax.broadcasted_iota(jnp.int32, (S, LANES), 0) // bs
        lanes = lax.broadcasted_iota(jnp.int32, (S, LANES), 1)
        km_ref[...] = jnp.mean(k.reshape(nb, bs, LANES), axis=1)
        kaug_ref[0] = jnp.where(lanes < HEAD_DIM, k, (lanes - HEAD_DIM == rowblk).astype(f32)).astype(kaug_ref.dtype)
        kaug_ref[1] = jnp.where(lanes >= HEAD_DIM, k, (lanes == rowblk).astype(f32)).astype(kaug_ref.dtype)
        for i in range(nb):
            vt_ref[i] = v_ref[i * bs:(i + 1) * bs, :].T.astype(vt_ref.dtype)

    q_t = q_ref[...].T
    scale = HEAD_DIM ** -0.5 * LOG2E
    blk = lax.broadcasted_iota(jnp.int32, (nb, 1), 0)
    past = blk < qb
    rhs = []
    for t in range(2):
        hm = (lane >= t * HEAD_DIM) & (lane < (t + 1) * HEAD_DIM)
        gate = _dot(jnp.where(hm, km_ref[...], 0.0), q_t, precision=HI)
        gate = jnp.where(past, gate, -jnp.inf)
        cnt = jnp.zeros((nb, bs), jnp.int32)
        for m in range(nb):
            gm = gate[m:m + 1, :]
            ahead = (gm > gate) | ((gm == gate) & (m < blk))
            cnt = cnt + ahead.astype(jnp.int32)
        chosen = (past & (cnt < MOBA_TOPK)) | (blk == qb)
        pen = jnp.where(chosen, 0.0, NEG)
        qh = q_t[t * HEAD_DIM:(t + 1) * HEAD_DIM, :] * scale
        if t == 0:
            parts = [qh, pen, jnp.zeros((LANES - HEAD_DIM - nb, bs), f32)]
        else:
            parts = [pen, jnp.zeros((HEAD_DIM - nb, bs), f32), qh]
        rhs.append(jnp.concatenate(parts, axis=0).astype(jnp.bfloat16))
    acc_ref[...] = jnp.zeros((LANES, bs), f32)

    chains = [(t, hq) for t in range(2) for hq in range(2)]
    cols = [slice(hq * half, (hq + 1) * half) for _, hq in chains]

    def logits(slot, k):
        blocks = [jnp.minimum(MOBA_STEP * k + j, nb - 1) for j in range(MOBA_STEP)]
        rows = [pl.multiple_of(n * bs, bs) for n in blocks]
        delta = [jnp.clip(qb - n, 0, n_delta - 1) for n in blocks]
        for c, (t, _) in enumerate(chains):
            for j in range(MOBA_STEP):
                s_ref[slot, MOBA_STEP * c + j] = (_dot(kaug_ref[t, pl.ds(rows[j], bs), :], rhs[t][:, cols[c]])
                                                  + bias_ref[t, delta[j], :, cols[c]])

    def step(slot, k, carry):
        blocks = [MOBA_STEP * k + j for j in range(MOBA_STEP)]
        s = [[s_ref[slot, MOBA_STEP * c + j] for j in range(MOBA_STEP)] for c in range(4)]
        m_new = [functools.reduce(jnp.maximum, [carry[c][0]] + [jnp.max(x, axis=0, keepdims=True) for x in s[c]])
                 for c in range(4)]
        alpha = [jnp.exp2(carry[c][0] - m_new[c]) for c in range(4)]
        p = [[jnp.exp2(x - m_new[c]) for x in s[c]] for c in range(4)]
        l_new = [alpha[c] * carry[c][1] + sum(jnp.sum(x, axis=0, keepdims=True) for x in p[c]) for c in range(4)]
        pv = [sum(_dot(vt_ref[n, t * HEAD_DIM:(t + 1) * HEAD_DIM, :], p[c][j].astype(jnp.bfloat16))
                  for j, n in enumerate(blocks)) for c, (t, _) in enumerate(chains)]
        pieces = [alpha[c] * acc_ref[t * HEAD_DIM:(t + 1) * HEAD_DIM, cols[c]] + pv[c]
                  for c, (t, _) in enumerate(chains)]
        acc_ref[...] = jnp.concatenate([jnp.concatenate(pieces[:2], axis=1),
                                        jnp.concatenate(pieces[2:], axis=1)], axis=0)
        return tuple((m_new[c], l_new[c]) for c in range(4))

    init = tuple((jnp.full((1, half), NEG, f32), jnp.zeros((1, half), f32)) for _ in range(4))
    n_steps = qb // MOBA_STEP + 1
    logits(0, 0)

    def two_steps(i, carry):
        logits(1, 2 * i + 1)
        carry = step(0, 2 * i, carry)
        logits(0, 2 * i + 2)
        return step(1, 2 * i + 1, carry)

    fin = lax.fori_loop(0, (n_steps + 1) // 2, two_steps, init)
    l_all = jnp.concatenate([jnp.broadcast_to(jnp.concatenate([fin[2 * t][1], fin[2 * t + 1][1]], axis=1),
                                              (HEAD_DIM, bs)) for t in range(2)], axis=0)
    o_ref[...] = (acc_ref[...] / l_all).T


def moba_attention(proj3, bias_tab, q_off, k_off, v_off):
    B, S, _ = proj3.shape
    nb = S // MOBA_BLOCK
    assert nb % (2 * MOBA_STEP) == 0, "the two-step loop needs an even number of key-block steps"
    n_delta = bias_tab.shape[1]
    pairs = MOBA_HEADS // 2
    kern = functools.partial(_moba_kernel, nb=nb, n_delta=n_delta)
    return pl.pallas_call(
        kern,
        grid=(pairs, B, nb),
        in_specs=[pl.BlockSpec((None, MOBA_BLOCK, LANES), lambda h, b, i: (b, i, q_off + h)),
                  pl.BlockSpec((None, S, LANES), lambda h, b, i: (b, 0, k_off + h)),
                  pl.BlockSpec((None, S, LANES), lambda h, b, i: (b, 0, v_off + h)),
                  pl.BlockSpec((2, n_delta, MOBA_BLOCK, MOBA_BLOCK), lambda h, b, i: (h, 0, 0, 0))],
        out_specs=pl.BlockSpec((None, MOBA_BLOCK, LANES), lambda h, b, i: (b, i, h)),
        out_shape=jax.ShapeDtypeStruct((B, S, pairs * LANES), jnp.float32),
        scratch_shapes=[pltpu.VMEM((2, S, LANES), jnp.bfloat16),
                        pltpu.VMEM((nb, LANES, MOBA_BLOCK), jnp.bfloat16),
                        pltpu.VMEM((nb, LANES), jnp.float32),
                        pltpu.VMEM((LANES, MOBA_BLOCK), jnp.float32),
                        pltpu.VMEM((2, 4 * MOBA_STEP, MOBA_BLOCK, MOBA_BLOCK // 2), jnp.float32)],
        compiler_params=_params(("parallel", "parallel", "arbitrary")),
        name="moba",
    )(proj3, proj3, proj3, bias_tab)


def _dil_kernel(q_ref, k_ref, v_ref, bias_ref, o_ref, lw_ref, *, dilation, seq):
    span = DIL_SPAN
    bps = seq // dilation // span
    scale = HEAD_DIM ** -0.5 * LOG2E
    lane = lax.broadcasted_iota(jnp.int32, (1, LANES), 1)
    col = lax.broadcasted_iota(jnp.int32, (1, 2 * span), 1)
    hm = [lane < HEAD_DIM, lane >= HEAD_DIM]
    unroll = 2

    def body(it, carry):
        blocks = []
        for u in range(unroll):
            j = it * unroll + u
            r = j // bps
            i = j - r * bps
            start = r + i * (span * dilation)
            prev = jnp.maximum(start - span * dilation, r)
            rows = lambda s0: pl.ds(s0, span, stride=dilation)
            q = q_ref[rows(start), :] * scale
            kcat = jnp.concatenate([k_ref[rows(prev), :], k_ref[rows(start), :]], axis=0).astype(jnp.bfloat16)
            vcat = jnp.concatenate([v_ref[rows(prev), :], v_ref[rows(start), :]], axis=0).astype(jnp.bfloat16)
            blocks.append((start, q, kcat, vcat, (i == 0) & (col < span)))
        chains = [(u, t) for u in range(unroll) for t in range(2)]
        s = [jnp.where(blocks[u][4], NEG,
                       _dot_nt(jnp.where(hm[t], blocks[u][1], 0.0).astype(jnp.bfloat16), blocks[u][2]) + bias_ref[t])
             for u, t in chains]
        m = [jnp.max(x, axis=-1, keepdims=True) for x in s]
        e = [jnp.exp2(x - mm) for x, mm in zip(s, m)]
        l = [jnp.sum(x, axis=-1, keepdims=True) for x in e]
        pv = [_dot(x.astype(jnp.bfloat16), blocks[u][3]) for x, (u, _) in zip(e, chains)]
        for u in range(unroll):
            a, b = 2 * u, 2 * u + 1
            rows = pl.ds(blocks[u][0], span, stride=dilation)
            o_ref[rows, :] = jnp.where(hm[0], pv[a] / l[a], pv[b] / l[b])
            lw_ref[rows, :] = jnp.where(hm[0], m[a] * LN2 + jnp.log(l[a]), m[b] * LN2 + jnp.log(l[b]))
        return carry

    lax.fori_loop(0, seq // span // unroll, body, 0)


def dilated_attention(proj3, bias_tab, q_off, k_off, v_off, group):
    B, S, _ = proj3.shape
    dilation = DIL_GROUPS[group][1]
    pairs = DIL_HEADS_PER_GROUP // 2
    slab = lambda off: pl.BlockSpec((None, S, LANES), lambda b, h: (b, 0, off + h))
    out = jax.ShapeDtypeStruct((B, S, pairs * LANES), jnp.float32)
    return pl.pallas_call(
        functools.partial(_dil_kernel, dilation=dilation, seq=S),
        grid=(B, pairs),
        in_specs=[slab(q_off), slab(k_off), slab(v_off),
                  pl.BlockSpec((None, 2, DIL_SPAN, 2 * DIL_SPAN), lambda b, h: (group, h, 0, 0))],
        out_specs=[slab(0), slab(0)],
        out_shape=[out, out],
        compiler_params=_params(("parallel", "parallel")),
        name="dilated",
    )(proj3, proj3, proj3, bias_tab)


def _rwkv_prep_kernel(*refs, width, has_res, rows_per_seq):
    if has_res:
        (z_ref, zl_ref, mu_ref, w0_ref, a0_ref, wa_ref, gup_ref, kk_ref, ka_ref, bd_ref,
         vf_ref, v0_ref, mvd_ref, mvu_ref,
         r_o, lw_o, k_o, v_o, kn_o, b_o, g_o) = refs
    else:
        (z_ref, zl_ref, mu_ref, w0_ref, a0_ref, wa_ref, gup_ref, kk_ref, ka_ref, bd_ref,
         r_o, lw_o, k_o, v_o, kn_o, b_o, g_o) = refs
    i = pl.program_id(0)
    W = width
    z = z_ref[...]
    tm = z.shape[0]
    row = lax.broadcasted_iota(jnp.int32, (tm, 1), 0)
    seq_start = (i % rows_per_seq) == 0
    last = jnp.where(seq_start, 0.0, zl_ref[7:8, :])
    zp = jnp.where(row == 0, last, pltpu.roll(z, 1, 0))
    zf = z + mu_ref[...] * (zp - z)
    lora = zf[:, 3 * W:3 * W + LANES]
    lane = lax.broadcasted_iota(jnp.int32, (1, LANES), 1)
    lora = jnp.where(lane < RWKV_DECAY_LORA, jnp.tanh(lora), lora)
    wa = _dot3(lora, wa_ref[...])
    g = _dot3(_sigmoid(zf[:, 3 * W + LANES:3 * W + 2 * LANES]), gup_ref[...])
    g_o[...] = g
    v_all = zf[:, 2 * W:3 * W]
    if has_res:
        mix = _dot3(_dot3(v_all, mvd_ref[...]), mvu_ref[...])
    for c in range(W // LANES):
        sl = slice(c * LANES, (c + 1) * LANES)
        x = w0_ref[:, sl] + wa[:, sl]
        sp = jnp.maximum(-x, 0.0) + jnp.log(1.0 + jnp.exp(-jnp.abs(x)))
        lw_o[:, sl] = -jnp.exp(-sp - 0.5)
        a = _sigmoid(a0_ref[:, sl] + wa[:, W + c * LANES:W + (c + 1) * LANES])
        r_o[:, sl] = zf[:, sl]
        k = zf[:, W + c * LANES:W + (c + 1) * LANES]
        v = v_all[:, sl]
        if has_res:
            v = v + (vf_ref[:, sl] - v) * _sigmoid(v0_ref[:, sl] + mix[:, sl])
        v_o[:, sl] = v
        kk = k * kk_ref[:, sl]
        ss = _dot_sel(kk * kk, bd_ref[...])
        kn = kk / jnp.maximum(jnp.sqrt(ss), 1e-12)
        kn_o[:, sl] = kn
        b_o[:, sl] = kn * a
        k_o[:, sl] = k * (1.0 + (a - 1.0) * ka_ref[:, sl])


def rwkv_prep(proj, seq, tm, mu, w0, a0, wa_up, g_up, k_k, k_a, bd, res):
    T = proj.shape[0]
    W = RWKV_HEADS * HEAD_DIM
    cols = mu.shape[-1]
    row1 = lambda n: pl.BlockSpec((1, n), lambda i: (0, 0))
    full = lambda a: pl.BlockSpec(a.shape, lambda i: (0, 0))
    tile = pl.BlockSpec((tm, W), lambda i: (i, 0))
    in_specs = [pl.BlockSpec((tm, cols), lambda i: (i, 0)),
                pl.BlockSpec((8, cols), lambda i: (jnp.maximum(i * (tm // 8) - 1, 0), 0)),
                row1(cols), row1(W), row1(W), full(wa_up), full(g_up), row1(W), row1(W), full(bd)]
    args = [proj, proj, mu, w0, a0, wa_up, g_up, k_k, k_a, bd]
    if res is not None:
        v_first, v0, mvd, mvu = res
        in_specs += [tile, row1(W), full(mvd), full(mvu)]
        args += [v_first, v0, mvd, mvu]
    out = jax.ShapeDtypeStruct((T, W), jnp.float32)
    kern = functools.partial(_rwkv_prep_kernel, width=W, has_res=res is not None, rows_per_seq=seq // tm)
    return pl.pallas_call(
        kern,
        grid=(T // tm,),
        in_specs=in_specs,
        out_specs=[tile] * 7,
        out_shape=[out] * 7,
        compiler_params=_params(("parallel",)),
        name="rwkv_prep",
    )(*args)


def _stack_heads(x, lane):
    return jnp.concatenate([jnp.where(lane < HEAD_DIM, x, 0.0), jnp.where(lane >= HEAD_DIM, x, 0.0)], axis=0)


def _rwkv_chunk_kernel(r_ref, lw_ref, k_ref, v_ref, kn_ref, b_ref, g_ref, rk_ref, bd_ref,
                       rhat_ref, y0_ref, e_ref, g_out_ref, d_out_ref, *, chunks):
    C = RWKV_CHUNK
    C2, C4 = 2 * C, 4 * C
    f32 = jnp.float32
    tri = (lax.broadcasted_iota(jnp.int32, (C, C), 0) >= lax.broadcasted_iota(jnp.int32, (C, C), 1)).astype(f32)
    lane = lax.broadcasted_iota(jnp.int32, (1, LANES), 1)
    rho = lax.broadcasted_iota(jnp.int32, (C4, C4), 0)
    sig = lax.broadcasted_iota(jnp.int32, (C4, C4), 1)
    keep = jnp.where(rho >= C2, rho & (C - 1), (rho & (C - 1)) - 1) >= (sig & (C - 1))
    eye2 = (lax.broadcasted_iota(jnp.int32, (C2, C2), 0) == lax.broadcasted_iota(jnp.int32, (C2, C2), 1)).astype(f32)
    eye_l = (lax.broadcasted_iota(jnp.int32, (LANES, LANES), 0)
             == lax.broadcasted_iota(jnp.int32, (LANES, LANES), 1)).astype(f32)
    zeros2 = jnp.zeros((C2, LANES), f32)
    ch = range(chunks)
    rows = [slice(c * C, (c + 1) * C) for c in ch]
    cum = [_dot(tri, lw_ref[rows[c], :], precision=HI) for c in ch]
    st = []
    for c in ch:
        r, lw, k, v = r_ref[rows[c], :], lw_ref[rows[c], :], k_ref[rows[c], :], v_ref[rows[c], :]
        kn, bb = kn_ref[rows[c], :], b_ref[rows[c], :]
        cum_last = cum[c][C - 1:C, :]
        e_out = jnp.exp(-cum[c])
        e_tail = jnp.exp(cum_last - cum[c])
        r_t = r * jnp.exp(cum[c])
        st.append(dict(
            r_t=r_t, decay=jnp.exp(cum_last),
            a2=_stack_heads(-kn * jnp.exp(cum[c] - lw), lane), r2=_stack_heads(r_t, lane),
            b2=_stack_heads(bb * e_out, lane), k2=_stack_heads(k * e_out, lane), v2=_stack_heads(v, lane),
            bh2=_stack_heads(bb * e_tail, lane), kh2=_stack_heads(k * e_tail, lane),
            e=_dot_sel(r * k * rk_ref[...], bd_ref[...]) * v * g_ref[rows[c], :]))
    quad = [jnp.where(keep, _dot3(jnp.concatenate([s["a2"], s["r2"]], axis=0),
                                  jnp.concatenate([s["b2"], s["k2"]], axis=0), _NT), 0.0) for s in st]
    pw = [q[:C2, :C2] for q in quad]
    t_inv = [eye2 + p for p in pw]
    x = [_dot3(q[:C2, C2:], s["v2"]) for q, s in zip(quad, st)]
    for level in range(int(math.log2(C)) - 1):
        mm = _dot3 if level < 2 else (lambda a, b: _dot(a.astype(jnp.bfloat16), b.astype(jnp.bfloat16)))
        pw = [mm(p, p) for p in pw]
        t_inv = [t + mm(t, p) for t, p in zip(t_inv, pw)]
    au = [_dot3(t, jnp.concatenate([s["a2"], xx], axis=1)) for t, s, xx in zip(t_inv, st, x)]
    my = [_dot3(q[C2:, :], jnp.concatenate([a, jnp.concatenate([zeros2, s["v2"]], axis=1)], axis=0))
          for q, a, s in zip(quad, au, st)]
    gm = [eye_l * s["decay"] + _dot3(a[:, :LANES], s["bh2"], _TN) for a, s in zip(au, st)]
    dm = [_dot3(jnp.concatenate([a[:, LANES:], s["v2"]], axis=0),
                jnp.concatenate([s["bh2"], s["kh2"]], axis=0), _TN) for a, s in zip(au, st)]
    rhat_ref[...] = jnp.concatenate([s["r_t"] + m[:C, :LANES] + m[C:, :LANES] for s, m in zip(st, my)], axis=0)
    y0_ref[...] = jnp.concatenate([m[:C, LANES:] + m[C:, LANES:] for m in my], axis=0)
    e_ref[...] = jnp.concatenate([s["e"] for s in st], axis=0)
    g_out_ref[...] = jnp.stack(gm)
    d_out_ref[...] = jnp.stack(dm)


def rwkv_chunk(r, lw, k, v, kn, b, g, r_k, bd, batch, chunks=4):
    T, W = r.shape
    S = T // batch
    C = RWKV_CHUNK
    nc = S // C
    pairs = W // LANES
    steps = nc // chunks
    tile = pl.BlockSpec((chunks * C, LANES), lambda bi, h, c: (bi * steps + c, h))
    mat = pl.BlockSpec((None, None, chunks, LANES, LANES), lambda bi, h, c: (bi, h, c, 0, 0))
    tw = jax.ShapeDtypeStruct((T, W), jnp.float32)
    gd = jax.ShapeDtypeStruct((batch, pairs, nc, LANES, LANES), jnp.float32)
    return pl.pallas_call(
        functools.partial(_rwkv_chunk_kernel, chunks=chunks),
        grid=(batch, pairs, steps),
        in_specs=[tile] * 7 + [pl.BlockSpec((1, LANES), lambda bi, h, c: (0, h)),
                               pl.BlockSpec((LANES, LANES), lambda bi, h, c: (0, 0))],
        out_specs=[tile, tile, tile, mat, mat],
        out_shape=[tw, tw, tw, gd, gd],
        compiler_params=_params(("parallel", "parallel", "parallel")),
        name="rwkv_chunk",
    )(r, lw, k, v, kn, b, g, r_k, bd)


def _rwkv_state_kernel(rhat_ref, y0_ref, g_ref, e_ref, gm_ref, dm_ref, lng_ref, lnb_ref, bd_ref,
                       o_ref, state_ref, *, chunks, group):
    C = RWKV_CHUNK

    @pl.when(pl.program_id(2) == 0)
    def _():
        state_ref[...] = jnp.zeros_like(state_ref)

    bd = bd_ref[...]
    inv_n = 1.0 / HEAD_DIM

    def body(c, carry):
        rows = pl.ds(pl.multiple_of(c * C, C), C)
        pairs = range(group)
        state = [state_ref[p] for p in pairs]
        y = [_dot3(rhat_ref[rows, p * LANES:(p + 1) * LANES], state[p], _NT) for p in pairs]
        new_state = [_dot3(state[p], gm_ref[p, c]) + dm_ref[p, c] for p in pairs]
        y = jnp.concatenate(y, axis=1) + y0_ref[rows, :]
        state_ref[...] = jnp.stack(new_state)
        mean = jnp.concatenate([_dot_sel(y[:, p * LANES:(p + 1) * LANES], bd) for p in pairs], axis=1) * inv_n
        d = y - mean
        dd = d * d
        var = jnp.concatenate([_dot_sel(dd[:, p * LANES:(p + 1) * LANES], bd) for p in pairs], axis=1) * inv_n
        yn = d * lax.rsqrt(var + RWKV_GN_EPS) * lng_ref[...] + lnb_ref[...]
        o_ref[rows, :] = yn * g_ref[rows, :] + e_ref[rows, :]
        return carry

    lax.fori_loop(0, chunks, body, 0)


def rwkv_state(rhat, y0, g, e, gmat, dmat, ln_g, ln_b, bd, chunks=16, group=3):
    T, W = rhat.shape
    batch, pairs, nc = gmat.shape[:3]
    C = RWKV_CHUNK
    steps = nc // chunks
    gw = group * LANES
    tile = pl.BlockSpec((chunks * C, gw), lambda bi, h, c: (bi * steps + c, h))
    mat = pl.BlockSpec((None, group, chunks, LANES, LANES), lambda bi, h, c: (bi, h, c, 0, 0))
    row = pl.BlockSpec((1, gw), lambda bi, h, c: (0, h))
    return pl.pallas_call(
        functools.partial(_rwkv_state_kernel, chunks=chunks, group=group),
        grid=(batch, pairs // group, steps),
        in_specs=[tile] * 4 + [mat, mat, row, row, pl.BlockSpec((LANES, LANES), lambda bi, h, c: (0, 0))],
        out_specs=tile,
        out_shape=jax.ShapeDtypeStruct((T, W), jnp.float32),
        scratch_shapes=[pltpu.VMEM((group, LANES, LANES), jnp.float32)],
        compiler_params=_params(("parallel", "parallel", "arbitrary")),
        name="rwkv_state",
    )(rhat, y0, g, e, gmat, dmat, ln_g, ln_b, bd)


def _merge_kernel(oa_ref, ob_ref, oc0_ref, oc1_ref, oc2_ref, lw0_ref, lw1_ref, lw2_ref, ga_ref, gb_ref, gc_ref,
                  pa_ref, pb_ref, pc_ref, o_ref):
    lw = [lw0_ref[...], lw1_ref[...], lw2_ref[...]]
    mx = jnp.maximum(jnp.maximum(lw[0], lw[1]), lw[2])
    wgt = [jnp.exp(x - mx) for x in lw]
    o_c = ((wgt[0] * oc0_ref[...] + wgt[1] * oc1_ref[...] + wgt[2] * oc2_ref[...])
           / (wgt[0] + wgt[1] + wgt[2]))
    bf = jnp.bfloat16
    merged = (_sigmoid(ga_ref[...]) * _dot(oa_ref[...].astype(bf), pa_ref[...])
              + _sigmoid(gb_ref[...]) * _dot(ob_ref[...].astype(bf), pb_ref[...])
              + _sigmoid(gc_ref[...]) * _dot(o_c.astype(bf), pc_ref[...]))
    o_ref[...] = merged.astype(o_ref.dtype)


def merge_branches(o_a, o_b, o_c, lw_c, proj, gate_off, p_a, p_b, p_c, tm, tn):
    T = o_a.shape[0]
    D = p_a.shape[1]
    nj = D // tn
    grp = pl.BlockSpec((tm, o_c[0].shape[1]), lambda i, j: (i, 0))
    gate = lambda n: pl.BlockSpec((tm, tn), lambda i, j: (i, gate_off + n * nj + j))
    wspec = lambda a: pl.BlockSpec((a.shape[0], tn), lambda i, j: (0, j))
    return pl.pallas_call(
        _merge_kernel,
        grid=(T // tm, nj),
        in_specs=[pl.BlockSpec((tm, o_a.shape[1]), lambda i, j: (i, 0)),
                  pl.BlockSpec((tm, o_b.shape[1]), lambda i, j: (i, 0)),
                  grp, grp, grp, grp, grp, grp, gate(0), gate(1), gate(2), wspec(p_a), wspec(p_b), wspec(p_c)],
        out_specs=pl.BlockSpec((tm, tn), lambda i, j: (i, j)),
        out_shape=jax.ShapeDtypeStruct((T, D), jnp.bfloat16),
        compiler_params=_params(("parallel", "arbitrary")),
        name="merge",
    )(o_a, o_b, *o_c, *lw_c, proj, proj, proj, p_a, p_b, p_c)


def _out_ln_kernel(mg_ref, w_ref, x_ref, g1_ref, lg_ref, lb_ref, o_ref, *, alpha):
    mix = _dot(mg_ref[...], w_ref[...])
    y = alpha * x_ref[...] + (1.0 + g1_ref[...]) * mix
    o_ref[...] = _layer_norm(y, lg_ref[...], lb_ref[...])


def out_ln(merged, w_o, x2, g1, ln_g, ln_b, seq, tm, alpha):
    T, D = x2.shape
    per = seq // tm
    tile = pl.BlockSpec((tm, D), lambda i: (i, 0))
    row = pl.BlockSpec((1, D), lambda i: (0, 0))
    return pl.pallas_call(
        functools.partial(_out_ln_kernel, alpha=alpha),
        grid=(T // tm,),
        in_specs=[tile, pl.BlockSpec((D, D), lambda i: (0, 0)), tile,
                  pl.BlockSpec((None, 1, D), lambda i: (i // per, 0, 0)), row, row],
        out_specs=tile,
        out_shape=jax.ShapeDtypeStruct((T, D), jnp.float32),
        compiler_params=_params(("parallel",)),
        name="out_ln",
    )(merged, w_o, x2, g1, ln_g, ln_b)


def _router_kernel(x_ref, sc_ref, sh_ref, w_ref, b_ref, id_ref, gw_ref, h_ref):
    h = x_ref[...] * (1.0 + sc_ref[...]) + sh_ref[...]
    h_ref[...] = h.astype(h_ref.dtype)
    lg = _dot3(h, w_ref[...]) + b_ref[...]
    G, EPG = MOE_GROUPS, MOE_EXPERTS_PER_GROUP
    lane = lax.broadcasted_iota(jnp.int32, (1, LANES), 1).astype(jnp.float32)
    first = lambda hit: jnp.min(jnp.where(hit, lane, float(LANES)), axis=-1, keepdims=True)
    is_grp = lane < G
    gmax = jnp.max(jnp.where(is_grp, lg, -jnp.inf), axis=-1, keepdims=True)
    ge = jnp.where(is_grp, jnp.exp(jnp.where(is_grp, lg, gmax) - gmax), 0.0)
    prob = ge / jnp.sum(ge, axis=-1, keepdims=True)
    grp_p = jnp.max(prob, axis=-1, keepdims=True)
    grp_i = first(is_grp & (prob == grp_p))
    lo = G + grp_i * EPG
    el = jnp.where((lane >= lo) & (lane < lo + EPG), lg, -jnp.inf)
    l1 = jnp.max(el, axis=-1, keepdims=True)
    i1 = first(el == l1)
    el = jnp.where(lane == i1, -jnp.inf, el)
    l2 = jnp.max(el, axis=-1, keepdims=True)
    i2 = first(el == l2)
    t = jnp.exp(l2 - l1)
    w1 = grp_p / (1.0 + t)
    id_ref[...] = jnp.where(lane == 0, i1 - G, jnp.where(lane == 1, i2 - G, 0.0)).astype(jnp.int32)
    gw_ref[...] = jnp.where(lane == 0, w1, jnp.where(lane == 1, w1 * t, 0.0))


def router(x2, sc, sh, w_r, b_r, seq, tm):
    T, D = x2.shape
    per = seq // tm
    mod = pl.BlockSpec((None, 1, D), lambda i: (i // per, 0, 0))
    narrow = pl.BlockSpec((tm, LANES), lambda i: (i, 0))
    return pl.pallas_call(
        _router_kernel,
        grid=(T // tm,),
        in_specs=[pl.BlockSpec((tm, D), lambda i: (i, 0)), mod, mod,
                  pl.BlockSpec((D, LANES), lambda i: (0, 0)), pl.BlockSpec((1, LANES), lambda i: (0, 0))],
        out_specs=[narrow, narrow, pl.BlockSpec((tm, D), lambda i: (i, 0))],
        out_shape=[jax.ShapeDtypeStruct((T, LANES), jnp.int32), jax.ShapeDtypeStruct((T, LANES), jnp.float32),
                   jax.ShapeDtypeStruct((T, D), jnp.bfloat16)],
        compiler_params=_params(("parallel",)),
        name="router",
    )(x2, sc, sh, w_r, b_r)


def _experts_kernel(be_ref, nu_ref, x_ref, wg_ref, wu_ref, wd_ref, o_ref, wg_s, wu_s, wd_s):
    i = pl.program_id(0)
    used = i < nu_ref[0]
    new_expert = (i == 0) | (be_ref[i] != be_ref[jnp.maximum(i - 1, 0)])

    @pl.when(used & new_expert)
    def _():
        wg_s[...] = wg_ref[...].astype(wg_s.dtype)
        wu_s[...] = wu_ref[...].astype(wu_s.dtype)
        wd_s[...] = wd_ref[...].astype(wd_s.dtype)

    @pl.when(used)
    def _():
        x = x_ref[...]
        gate = _dot(x, wg_s[...])
        hid = gate * _sigmoid(gate) * _dot(x, wu_s[...])
        o_ref[...] = _dot(hid.astype(jnp.bfloat16), wd_s[...])

    @pl.when(jnp.logical_not(used))
    def _():
        o_ref[...] = jnp.zeros_like(o_ref)


def experts(xs, block_e, n_used, w_gate, w_up, w_down, layer):
    R, D = xs.shape
    F = w_gate.shape[-1]
    n_blocks = R // MOE_BLOCK
    grid_spec = pltpu.PrefetchScalarGridSpec(
        num_scalar_prefetch=2,
        grid=(n_blocks,),
        in_specs=[pl.BlockSpec((MOE_BLOCK, D), lambda i, be, nu: (i, 0)),
                  pl.BlockSpec((None, None, D, F), lambda i, be, nu: (layer, be[i], 0, 0)),
                  pl.BlockSpec((None, None, D, F), lambda i, be, nu: (layer, be[i], 0, 0)),
                  pl.BlockSpec((None, None, F, D), lambda i, be, nu: (layer, be[i], 0, 0))],
        out_specs=pl.BlockSpec((MOE_BLOCK, D), lambda i, be, nu: (i, 0)),
        scratch_shapes=[pltpu.VMEM((D, F), jnp.bfloat16), pltpu.VMEM((D, F), jnp.bfloat16),
                        pltpu.VMEM((F, D), jnp.bfloat16)],
    )
    return pl.pallas_call(
        _experts_kernel,
        grid_spec=grid_spec,
        out_shape=jax.ShapeDtypeStruct((R, D), jnp.float32),
        compiler_params=_params(("arbitrary",)),
        name="experts",
    )(block_e, n_used, xs, w_gate, w_up, w_down)


def _combine_ln_kernel(x_ref, f0_ref, f1_ref, gw_ref, g2_ref, lg_ref, lb_ref, o_ref, *, alpha):
    gw = gw_ref[...]
    ffn = gw[:, 0:1] * f0_ref[...] + gw[:, 1:2] * f1_ref[...]
    y = alpha * x_ref[...] + (1.0 + g2_ref[...]) * ffn
    o_ref[...] = _layer_norm(y, lg_ref[...], lb_ref[...])


def combine_ln(x2, f0, f1, gws, g2, ln_g, ln_b, seq, tm, alpha):
    T, D = x2.shape
    per = seq // tm
    tile = pl.BlockSpec((tm, D), lambda i: (i, 0))
    row = pl.BlockSpec((1, D), lambda i: (0, 0))
    return pl.pallas_call(
        functools.partial(_combine_ln_kernel, alpha=alpha),
        grid=(T // tm,),
        in_specs=[tile, tile, tile, pl.BlockSpec((tm, LANES), lambda i: (i, 0)),
                  pl.BlockSpec((None, 1, D), lambda i: (i // per, 0, 0)), row, row],
        out_specs=tile,
        out_shape=jax.ShapeDtypeStruct((T, D), jnp.float32),
        compiler_params=_params(("parallel",)),
        name="combine_ln",
    )(x2, f0, f1, gws, g2, ln_g, ln_b)


def _route(expert_id):
    T = expert_id.shape[0]
    E = MOE_GROUPS * MOE_EXPERTS_PER_GROUP
    A = T * MOE_TOPK
    n_blocks = (A + E * (MOE_BLOCK - 1) + MOE_BLOCK - 1) // MOE_BLOCK
    flat_e = expert_id.reshape(A)
    e_s, order = lax.sort_key_val(flat_e, jnp.arange(A, dtype=jnp.int32))
    bounds = jnp.searchsorted(e_s, jnp.arange(E + 1, dtype=jnp.int32), side='left').astype(jnp.int32)
    start = bounds[:E]
    counts = bounds[1:] - start
    padded = (counts + MOE_BLOCK - 1) // MOE_BLOCK * MOE_BLOCK
    pad_end = jnp.cumsum(padded)
    pad_start = pad_end - padded
    block_e = jnp.minimum(jnp.sum(pad_end[None, :] <= (jnp.arange(n_blocks) * MOE_BLOCK)[:, None], axis=1),
                          E - 1).astype(jnp.int32)
    blk_rank = jnp.arange(n_blocks, dtype=jnp.int32) * MOE_BLOCK - pad_start[block_e]
    rank = blk_rank[:, None] + jnp.arange(MOE_BLOCK, dtype=jnp.int32)[None, :]
    valid = (rank < counts[block_e][:, None]).reshape(-1)
    src = order[jnp.clip(start[block_e][:, None] + rank, 0, A - 1).reshape(-1)]
    spread = jnp.arange(n_blocks * MOE_BLOCK, dtype=jnp.int32) % T
    slot_tok = jnp.where(valid, src // MOE_TOPK, spread).astype(jnp.int32)
    pos = jnp.arange(A, dtype=jnp.int32)[None, :]
    in_e = (pos >= start[:, None]) & (pos < bounds[1:, None])
    dest = pos[0] + jnp.sum(jnp.where(in_e, (pad_start - start)[:, None], 0), axis=0).astype(jnp.int32)
    _, slot_of = lax.sort_key_val(order, dest)
    n_used = (pad_end[-1] // MOE_BLOCK).astype(jnp.int32).reshape(1)
    return slot_tok, slot_of.reshape(T, MOE_TOPK), block_e, n_used


def kernel(x, c, rel_bias, w_in, p_a, p_b, p_c, w_o, rwkv_mu, rwkv_w0, rwkv_w_up, rwkv_a0, rwkv_a_up,
           rwkv_g_up, rwkv_k_k, rwkv_k_a, rwkv_r_k, rwkv_ln_g, rwkv_ln_b, rwkv_v0, rwkv_mv_down,
           rwkv_mv_up, w_ada, b_ada, ln1_g, ln1_b, ln2_g, ln2_b, router_grp_w, router_grp_b,
           router_exp_w, router_exp_b, exp_w_gate, exp_w_up, exp_w_down):
    B, S, D = x.shape
    depth = w_in.shape[0]
    T = B * S
    bf = jnp.bfloat16
    W = RWKV_HEADS * HEAD_DIM
    wa_w = MOBA_HEADS * HEAD_DIM
    wc_w = len(DIL_GROUPS) * DIL_HEADS_PER_GROUP * HEAD_DIM
    rw_cols = rwkv_mu.shape[-1]
    alpha = (2 * depth) ** 0.25
    off_c = 3 * wa_w
    off_b = off_c + 3 * wc_w
    off_g = off_b + rw_cols
    new_a = rw_cols
    new_c = new_a + 3 * wa_w
    new_g = new_c + 3 * wc_w
    w_in_p = jnp.concatenate([w_in[:, :, off_b:off_g], w_in[:, :, :off_b], w_in[:, :, off_g:]], axis=-1).astype(bf)
    p_a_b, p_b_b, p_c_b, w_o_b = p_a.astype(bf), p_b.astype(bf), p_c.astype(bf), w_o.astype(bf)

    bias_h = rel_bias.T.astype(jnp.float32)
    moba_tab = moba_bias_table(bias_h[:MOBA_HEADS], S // MOBA_BLOCK)
    dil_tab = dil_bias_table(bias_h[MOBA_HEADS:])

    c8 = jnp.zeros((8, D), jnp.float32).at[:B].set(c)
    mod = ada_mod(c8, w_ada, b_ada)[:, :B]

    hd_idx = jnp.arange(LANES) // HEAD_DIM
    bd = (hd_idx[:, None] == hd_idx[None, :]).astype(jnp.float32)
    zeros_w = jnp.zeros((RWKV_DECAY_LORA, W), jnp.float32)
    pad_lora = LANES - RWKV_MV_LORA

    x2 = x.reshape(T, D)
    v_first = None
    for l in range(depth):
        sh1, sc1, g1, sh2, sc2, g2 = [m.reshape(B, 1, D) for m in jnp.split(mod[l], 6, axis=-1)]
        proj = in_proj(x2, sc1, sh1, w_in_p, l, S, 1024, 1024)
        proj3 = proj.reshape(B, S, -1)
        o_a = moba_attention(proj3, moba_tab, new_a // LANES, (new_a + wa_w) // LANES,
                             (new_a + 2 * wa_w) // LANES).reshape(T, wa_w)
        gw = DIL_HEADS_PER_GROUP * HEAD_DIM
        o_c, lw_c = [], []
        for g in range(len(DIL_GROUPS)):
            og, lwg = dilated_attention(proj3, dil_tab, (new_c + g * gw) // LANES,
                                        (new_c + wc_w + g * gw) // LANES, (new_c + 2 * wc_w + g * gw) // LANES, g)
            o_c.append(og.reshape(T, gw))
            lw_c.append(lwg.reshape(T, gw))
        wa_up = jnp.concatenate([jnp.concatenate([rwkv_w_up[l], zeros_w], axis=1),
                                 jnp.concatenate([zeros_w, rwkv_a_up[l]], axis=1)], axis=0)
        res = None
        if l > 0:
            res = (v_first, rwkv_v0[l - 1][None],
                   jnp.pad(rwkv_mv_down[l - 1], ((0, 0), (0, pad_lora))),
                   jnp.pad(rwkv_mv_up[l - 1], ((0, pad_lora), (0, 0))))
        r_, lw_, k_, v_, kn_, b_, g_ = rwkv_prep(proj, S, 512, rwkv_mu[l][None], rwkv_w0[l][None],
                                                 rwkv_a0[l][None], wa_up, rwkv_g_up[l], rwkv_k_k[l][None],
                                                 rwkv_k_a[l][None], bd, res)
        if l == 0:
            v_first = v_
        rhat, y0, e_, gmat, dmat = rwkv_chunk(r_, lw_, k_, v_, kn_, b_, g_, rwkv_r_k[l][None], bd, B)
        o_b = rwkv_state(rhat, y0, g_, e_, gmat, dmat, rwkv_ln_g[l][None], rwkv_ln_b[l][None], bd)
        merged = merge_branches(o_a, o_b, o_c, lw_c, proj, new_g // 1024, p_a_b[l], p_b_b[l], p_c_b[l],
                                512, 1024)
        x2 = out_ln(merged, w_o_b[l], x2, g1, ln1_g[l][None], ln1_b[l][None], S, 512, alpha)
        w_r = jnp.zeros((D, LANES), jnp.float32)
        w_r = w_r.at[:, :MOE_GROUPS].set(router_grp_w[l]).at[:, MOE_GROUPS:MOE_GROUPS + router_exp_w.shape[-1]].set(
            router_exp_w[l])
        b_r = jnp.zeros((1, LANES), jnp.float32)
        b_r = b_r.at[0, :MOE_GROUPS].set(router_grp_b[l]).at[0, MOE_GROUPS:MOE_GROUPS + router_exp_b.shape[-1]].set(
            router_exp_b[l])
        ids, gws, h2 = router(x2, sc2, sh2, w_r, b_r, S, 512)
        slot_tok, slot_of, block_e, n_used = _route(ids[:, :MOE_TOPK])
        y = experts(h2[slot_tok], block_e, n_used, exp_w_gate, exp_w_up, exp_w_down, l)
        x2 = combine_ln(x2, y[slot_of[:, 0]], y[slot_of[:, 1]], gws, g2, ln2_g[l][None], ln2_b[l][None], S, 512,
                        alpha)
    return x2.reshape(B, S, D)
```

```python
import functools
import math

import jax
import jax.numpy as jnp
import numpy as np
from jax import lax
from jax.experimental import pallas as pl
from jax.experimental.pallas import tpu as pltpu

HEAD_DIM = 64
LANES = 128
MOBA_HEADS = 12
MOBA_BLOCK = 256
MOBA_TOPK = 3
MOBA_STEP = 2
RWKV_HEADS = 12
RWKV_DECAY_LORA = 64
RWKV_A_LORA = 64
RWKV_MV_LORA = 32
RWKV_GATE_LORA = 128
RWKV_GN_EPS = 64e-5
RWKV_CHUNK = 64
DIL_GROUPS = ((128, 1), (512, 4), (2048, 16))
DIL_HEADS_PER_GROUP = 4
DIL_SPAN = 128
REL_BUCKETS = 32
REL_MAX_DISTANCE = 2048
MOE_GROUPS = 8
MOE_EXPERTS_PER_GROUP = 8
MOE_TOPK = 2
MOE_BLOCK = 256
LN_EPS = 1e-5
NEG = -1e30
LOG2E = math.log2(math.e)
LN2 = math.log(2.0)
VMEM_LIMIT = 56 * 1024 * 1024
HI = lax.Precision.HIGHEST


def _params(sem):
    return pltpu.CompilerParams(dimension_semantics=sem, vmem_limit_bytes=VMEM_LIMIT)


def _sigmoid(x):
    return 0.5 * jnp.tanh(0.5 * x) + 0.5


def _dot(a, b, precision=None):
    return jnp.dot(a, b, preferred_element_type=jnp.float32, precision=precision)


def _dot_nt(a, b, precision=None):
    return lax.dot_general(a, b, (((1,), (1,)), ((), ())), preferred_element_type=jnp.float32,
                           precision=precision)


def _dot_tn(a, b, precision=None):
    return lax.dot_general(a, b, (((0,), (0,)), ((), ())), preferred_element_type=jnp.float32,
                           precision=precision)


def _split2(x):
    hi = x.astype(jnp.bfloat16)
    return hi, (x - hi.astype(jnp.float32)).astype(jnp.bfloat16)


def _dot3(a, b, dims=((1,), (0,))):
    (ca,), (cb,) = dims
    ah, al = _split2(a)
    bh, bl = _split2(b)
    return lax.dot_general(jnp.concatenate([ah, ah, al], axis=ca), jnp.concatenate([bh, bl, bh], axis=cb),
                           (dims, ((), ())), preferred_element_type=jnp.float32)


def _dot_sel(x, sel):
    xh, xl = _split2(x)
    sb = sel.astype(jnp.bfloat16)
    return _dot(jnp.concatenate([xh, xl], axis=1), jnp.concatenate([sb, sb], axis=0))


_NT = ((1,), (1,))
_TN = ((0,), (0,))


def _layer_norm(y, g, b):
    mu = jnp.mean(y, axis=-1, keepdims=True)
    d = y - mu
    var = jnp.mean(d * d, axis=-1, keepdims=True)
    return d * lax.rsqrt(var + LN_EPS) * g + b


def _ada_kernel(c_ref, w_ref, b_ref, o_ref):
    c = c_ref[...]
    cond = c * _sigmoid(c)
    o_ref[...] = _dot(cond, w_ref[...]) + b_ref[...]


def ada_mod(c8, w_ada, b_ada):
    L, D, N = w_ada.shape
    tn = 1024
    return pl.pallas_call(
        _ada_kernel,
        grid=(L, N // tn),
        in_specs=[pl.BlockSpec((8, D), lambda l, j: (0, 0)),
                  pl.BlockSpec((None, D, tn), lambda l, j: (l, 0, j)),
                  pl.BlockSpec((None, 1, tn), lambda l, j: (l, 0, j))],
        out_specs=pl.BlockSpec((None, 8, tn), lambda l, j: (l, 0, j)),
        out_shape=jax.ShapeDtypeStruct((L, 8, N), jnp.float32),
        compiler_params=_params(("parallel", "parallel")),
        name="ada_mod",
    )(c8, w_ada, b_ada.reshape(L, 1, N))


def _in_proj_kernel(x_ref, sc_ref, sh_ref, w_ref, o_ref, h_ref):
    @pl.when(pl.program_id(1) == 0)
    def _():
        h_ref[...] = (x_ref[...] * (1.0 + sc_ref[...]) + sh_ref[...]).astype(h_ref.dtype)

    o_ref[...] = _dot(h_ref[...], w_ref[...])


def in_proj(x2, sc, sh, w, layer, seq, tm, tn):
    T, D = x2.shape
    N = w.shape[-1]
    per = seq // tm
    return pl.pallas_call(
        _in_proj_kernel,
        grid=(T // tm, N // tn),
        in_specs=[pl.BlockSpec((tm, D), lambda i, j: (i, 0)),
                  pl.BlockSpec((None, 1, D), lambda i, j: (i // per, 0, 0)),
                  pl.BlockSpec((None, 1, D), lambda i, j: (i // per, 0, 0)),
                  pl.BlockSpec((None, D, tn), lambda i, j: (layer, 0, j))],
        out_specs=pl.BlockSpec((tm, tn), lambda i, j: (i, j)),
        out_shape=jax.ShapeDtypeStruct((T, N), jnp.float32),
        scratch_shapes=[pltpu.VMEM((tm, D), jnp.bfloat16)],
        compiler_params=_params(("parallel", "arbitrary")),
        name="in_proj",
    )(x2, sc, sh, w)


def _t5_bucket(dist):
    n = jnp.maximum(dist, 0)
    max_exact = REL_BUCKETS // 2
    nf = jnp.maximum(n, 1).astype(jnp.float32)
    large = max_exact + (jnp.log(nf / max_exact) / math.log(REL_MAX_DISTANCE / max_exact)
                         * (REL_BUCKETS - max_exact)).astype(jnp.int32)
    large = jnp.minimum(large, REL_BUCKETS - 1)
    return jnp.where(n < max_exact, n, large)


def _moba_n_delta(nb):
    last_start = 1
    d = np.arange(1, nb * MOBA_BLOCK + 1)
    large = 16 + (np.log(d / 16.0) / math.log(REL_MAX_DISTANCE / 16.0) * 16).astype(np.int64)
    bucket = np.where(d < 16, d, np.minimum(large, REL_BUCKETS - 1))
    last_start = int(d[bucket < REL_BUCKETS - 1].max()) + 1 if (bucket < REL_BUCKETS - 1).any() else 1
    delta = 1
    while delta * MOBA_BLOCK - (MOBA_BLOCK - 1) < last_start + 2:
        delta += 1
    return min(delta + 1, nb)


def moba_bias_table(bias_a, nb):
    nd = _moba_n_delta(nb)
    key = jnp.arange(MOBA_BLOCK)[:, None]
    qry = jnp.arange(MOBA_BLOCK)[None, :]
    dist = jnp.arange(nd)[:, None, None] * MOBA_BLOCK + (qry - key)[None]
    bucket = _t5_bucket(dist)[None]
    tab = jnp.zeros((bias_a.shape[0],) + dist.shape, jnp.float32)
    for b in range(REL_BUCKETS):
        tab = jnp.where(bucket == b, bias_a[:, b][:, None, None, None], tab)
    return jnp.where((dist >= 0)[None], tab * LOG2E, NEG)


def dil_bias_table(bias_c):
    span = DIL_SPAN
    rel = span + jnp.arange(span)[:, None] - jnp.arange(2 * span)[None, :]
    valid = (rel >= 0) & (rel <= span)
    tabs = []
    for g, (_, dilation) in enumerate(DIL_GROUPS):
        bh = bias_c[g * DIL_HEADS_PER_GROUP:(g + 1) * DIL_HEADS_PER_GROUP]
        tabs.append(jnp.where(valid[None], bh[:, _t5_bucket(rel * dilation)] * LOG2E, NEG))
    return jnp.stack(tabs)


def _moba_kernel(q_ref, k_ref, v_ref, bias_ref, o_ref,
                 kaug_ref, vt_ref, km_ref, acc_ref, s_ref, *, nb, n_delta):
    qb = pl.program_id(2)
    bs = MOBA_BLOCK
    S = nb * bs
    half = bs // 2
    lane = lax.broadcasted_iota(jnp.int32, (1, LANES), 1)
    f32 = jnp.float32

    @pl.when(qb == 0)
    def _():
        k = k_ref[...]
        rowblk = lax.broadcasted_iota(jnp.int32, (S, LANES), 0) // bs
        lanes = lax.broadcasted_iota(jnp.int32, (S, LANES), 1)
        km_ref[...] = jnp.mean(k.reshape(nb, bs, LANES), axis=1)
        kaug_ref[0] = jnp.where(lanes < HEAD_DIM, k, (lanes - HEAD_DIM == rowblk).astype(f32)).astype(kaug_ref.dtype)
        kaug_ref[1] = jnp.where(lanes >= HEAD_DIM, k, (lanes == rowblk).astype(f32)).astype(kaug_ref.dtype)
        for i in range(nb):
            vt_ref[i] = v_ref[i * bs:(i + 1) * bs, :].T.astype(vt_ref.dtype)

    q_t = q_ref[...].T
    scale = HEAD_DIM ** -0.5 * LOG2E
    blk = lax.broadcasted_iota(jnp.int32, (nb, 1), 0)
    past = blk < qb
    rhs = []
    for t in range(2):
        hm = (lane >= t * HEAD_DIM) & (lane < (t + 1) * HEAD_DIM)
        gate = _dot(jnp.where(hm, km_ref[...], 0.0), q_t, precision=HI)
        gate = jnp.where(past, gate, -jnp.inf)
        cnt = jnp.zeros((nb, bs), jnp.int32)
        for m in range(nb):
            gm = gate[m:m + 1, :]
            ahead = (gm > gate) | ((gm == gate) & (m < blk))
            cnt = cnt + ahead.astype(jnp.int32)
        chosen = (past & (cnt < MOBA_TOPK)) | (blk == qb)
        pen = jnp.where(chosen, 0.0, NEG)
        qh = q_t[t * HEAD_DIM:(t + 1) * HEAD_DIM, :] * scale
        if t == 0:
            parts = [qh, pen, jnp.zeros((LANES - HEAD_DIM - nb, bs), f32)]
        else:
            parts = [pen, jnp.zeros((HEAD_DIM - nb, bs), f32), qh]
        rhs.append(jnp.concatenate(parts, axis=0).astype(jnp.bfloat16))
    acc_ref[...] = jnp.zeros((LANES, bs), f32)

    chains = [(t, hq) for t in range(2) for hq in range(2)]
    cols = [slice(hq * half, (hq + 1) * half) for _, hq in chains]

    def logits(slot, k):
        blocks = [jnp.minimum(MOBA_STEP * k + j, nb - 1) for j in range(MOBA_STEP)]
        rows = [pl.multiple_of(n * bs, bs) for n in blocks]
        delta = [jnp.clip(qb - n, 0, n_delta - 1) for n in blocks]
        for c, (t, _) in enumerate(chains):
            for j in range(MOBA_STEP):
                s_ref[slot, MOBA_STEP * c + j] = (_dot(kaug_ref[t, pl.ds(rows[j], bs), :], rhs[t][:, cols[c]])
                                                  + bias_ref[t, delta[j], :, cols[c]])

    def step(slot, k, carry):
        blocks = [MOBA_STEP * k + j for j in range(MOBA_STEP)]
        s = [[s_ref[slot, MOBA_STEP * c + j] for j in range(MOBA_STEP)] for c in range(4)]
        m_new = [functools.reduce(jnp.maximum, [carry[c][0]] + [jnp.max(x, axis=0, keepdims=True) for x in s[c]])
                 for c in range(4)]
        alpha = [jnp.exp2(carry[c][0] - m_new[c]) for c in range(4)]
        p = [[jnp.exp2(x - m_new[c]) for x in s[c]] for c in range(4)]
        l_new = [alpha[c] * carry[c][1] + sum(jnp.sum(x, axis=0, keepdims=True) for x in p[c]) for c in range(4)]
        pv = [sum(_dot(vt_ref[n, t * HEAD_DIM:(t + 1) * HEAD_DIM, :], p[c][j].astype(jnp.bfloat16))
                  for j, n in enumerate(blocks)) for c, (t, _) in enumerate(chains)]
        pieces = [alpha[c] * acc_ref[t * HEAD_DIM:(t + 1) * HEAD_DIM, cols[c]] + pv[c]
                  for c, (t, _) in enumerate(chains)]
        acc_ref[...] = jnp.concatenate([jnp.concatenate(pieces[:2], axis=1),
                                        jnp.concatenate(pieces[2:], axis=1)], axis=0)
        return tuple((m_new[c], l_new[c]) for c in range(4))

    init = tuple((jnp.full((1, half), NEG, f32), jnp.zeros((1, half), f32)) for _ in range(4))
    n_steps = qb // MOBA_STEP + 1
    logits(0, 0)

    def two_steps(i, carry):
        logits(1, 2 * i + 1)
        carry = step(0, 2 * i, carry)
        logits(0, 2 * i + 2)
        return step(1, 2 * i + 1, carry)

    fin = lax.fori_loop(0, (n_steps + 1) // 2, two_steps, init)
    l_all = jnp.concatenate([jnp.broadcast_to(jnp.concatenate([fin[2 * t][1], fin[2 * t + 1][1]], axis=1),
                                              (HEAD_DIM, bs)) for t in range(2)], axis=0)
    o_ref[...] = (acc_ref[...] / l_all).T


def moba_attention(proj3, bias_tab, q_off, k_off, v_off):
    B, S, _ = proj3.shape
    nb = S // MOBA_BLOCK
    assert nb % (2 * MOBA_STEP) == 0, "the two-step loop needs an even number of key-block steps"
    n_delta = bias_tab.shape[1]
    pairs = MOBA_HEADS // 2
    kern = functools.partial(_moba_kernel, nb=nb, n_delta=n_delta)
    return pl.pallas_call(
        kern,
        grid=(pairs, B, nb),
        in_specs=[pl.BlockSpec((None, MOBA_BLOCK, LANES), lambda h, b, i: (b, i, q_off + h)),
                  pl.BlockSpec((None, S, LANES), lambda h, b, i: (b, 0, k_off + h)),
                  pl.BlockSpec((None, S, LANES), lambda h, b, i: (b, 0, v_off + h)),
                  pl.BlockSpec((2, n_delta, MOBA_BLOCK, MOBA_BLOCK), lambda h, b, i: (h, 0, 0, 0))],
        out_specs=pl.BlockSpec((None, MOBA_BLOCK, LANES), lambda h, b, i: (b, i, h)),
        out_shape=jax.ShapeDtypeStruct((B, S, pairs * LANES), jnp.float32),
        scratch_shapes=[pltpu.VMEM((2, S, LANES), jnp.bfloat16),
                        pltpu.VMEM((nb, LANES, MOBA_BLOCK), jnp.bfloat16),
                        pltpu.VMEM((nb, LANES), jnp.float32),
                        pltpu.VMEM((LANES, MOBA_BLOCK), jnp.float32),
                        pltpu.VMEM((2, 4 * MOBA_STEP, MOBA_BLOCK, MOBA_BLOCK // 2), jnp.float32)],
        compiler_params=_params(("parallel", "parallel", "arbitrary")),
        name="moba",
    )(proj3, proj3, proj3, bias_tab)


def _dil_kernel(q_ref, k_ref, v_ref, bias_ref, o_ref, lw_ref, *, dilation, seq):
    span = DIL_SPAN
    bps = seq // dilation // span
    scale = HEAD_DIM ** -0.5 * LOG2E
    lane = lax.broadcasted_iota(jnp.int32, (1, LANES), 1)
    col = lax.broadcasted_iota(jnp.int32, (1, 2 * span), 1)
    hm = [lane < HEAD_DIM, lane >= HEAD_DIM]
    unroll = 2

    def body(it, carry):
        blocks = []
        for u in range(unroll):
            j = it * unroll + u
            r = j // bps
            i = j - r * bps
            start = r + i * (span * dilation)
            prev = jnp.maximum(start - span * dilation, r)
            rows = lambda s0: pl.ds(s0, span, stride=dilation)
            q = q_ref[rows(start), :] * scale
            kcat = jnp.concatenate([k_ref[rows(prev), :], k_ref[rows(start), :]], axis=0).astype(jnp.bfloat16)
            vcat = jnp.concatenate([v_ref[rows(prev), :], v_ref[rows(start), :]], axis=0).astype(jnp.bfloat16)
            blocks.append((start, q, kcat, vcat, (i == 0) & (col < span)))
        chains = [(u, t) for u in range(unroll) for t in range(2)]
        s = [jnp.where(blocks[u][4], NEG,
                       _dot_nt(jnp.where(hm[t], blocks[u][1], 0.0).astype(jnp.bfloat16), blocks[u][2]) + bias_ref[t])
             for u, t in chains]
        m = [jnp.max(x, axis=-1, keepdims=True) for x in s]
        e = [jnp.exp2(x - mm) for x, mm in zip(s, m)]
        l = [jnp.sum(x, axis=-1, keepdims=True) for x in e]
        pv = [_dot(x.astype(jnp.bfloat16), blocks[u][3]) for x, (u, _) in zip(e, chains)]
        for u in range(unroll):
            a, b = 2 * u, 2 * u + 1
            rows = pl.ds(blocks[u][0], span, stride=dilation)
            o_ref[rows, :] = jnp.where(hm[0], pv[a] / l[a], pv[b] / l[b])
            lw_ref[rows, :] = jnp.where(hm[0], m[a] * LN2 + jnp.log(l[a]), m[b] * LN2 + jnp.log(l[b]))
        return carry

    lax.fori_loop(0, seq // span // unroll, body, 0)


def dilated_attention(proj3, bias_tab, q_off, k_off, v_off, group):
    B, S, _ = proj3.shape
    dilation = DIL_GROUPS[group][1]
    pairs = DIL_HEADS_PER_GROUP // 2
    slab = lambda off: pl.BlockSpec((None, S, LANES), lambda b, h: (b, 0, off + h))
    out = jax.ShapeDtypeStruct((B, S, pairs * LANES), jnp.float32)
    return pl.pallas_call(
        functools.partial(_dil_kernel, dilation=dilation, seq=S),
        grid=(B, pairs),
        in_specs=[slab(q_off), slab(k_off), slab(v_off),
                  pl.BlockSpec((None, 2, DIL_SPAN, 2 * DIL_SPAN), lambda b, h: (group, h, 0, 0))],
        out_specs=[slab(0), slab(0)],
        out_shape=[out, out],
        compiler_params=_params(("parallel", "parallel")),
        name="dilated",
    )(proj3, proj3, proj3, bias_tab)


def _rwkv_prep_kernel(*refs, width, has_res, rows_per_seq):
    if has_res:
        (z_ref, zl_ref, mu_ref, w0_ref, a0_ref, wa_ref, gup_ref, kk_ref, ka_ref, bd_ref,
         vf_ref, v0_ref, mvd_ref, mvu_ref,
         r_o, lw_o, k_o, v_o, kn_o, b_o, g_o) = refs
    else:
        (z_ref, zl_ref, mu_ref, w0_ref, a0_ref, wa_ref, gup_ref, kk_ref, ka_ref, bd_ref,
         r_o, lw_o, k_o, v_o, kn_o, b_o, g_o) = refs
    i = pl.program_id(0)
    W = width
    z = z_ref[...]
    tm = z.shape[0]
    row = lax.broadcasted_iota(jnp.int32, (tm, 1), 0)
    seq_start = (i % rows_per_seq) == 0
    last = jnp.where(seq_start, 0.0, zl_ref[7:8, :])
    zp = jnp.where(row == 0, last, pltpu.roll(z, 1, 0))
    zf = z + mu_ref[...] * (zp - z)
    lora = zf[:, 3 * W:3 * W + LANES]
    lane = lax.broadcasted_iota(jnp.int32, (1, LANES), 1)
    lora = jnp.where(lane < RWKV_DECAY_LORA, jnp.tanh(lora), lora)
    wa = _dot3(lora, wa_ref[...])
    g = _dot3(_sigmoid(zf[:, 3 * W + LANES:3 * W + 2 * LANES]), gup_ref[...])
    g_o[...] = g
    v_all = zf[:, 2 * W:3 * W]
    if has_res:
        mix = _dot3(_dot3(v_all, mvd_ref[...]), mvu_ref[...])
    for c in range(W // LANES):
        sl = slice(c * LANES, (c + 1) * LANES)
        x = w0_ref[:, sl] + wa[:, sl]
        sp = jnp.maximum(-x, 0.0) + jnp.log(1.0 + jnp.exp(-jnp.abs(x)))
        lw_o[:, sl] = -jnp.exp(-sp - 0.5)
        a = _sigmoid(a0_ref[:, sl] + wa[:, W + c * LANES:W + (c + 1) * LANES])
        r_o[:, sl] = zf[:, sl]
        k = zf[:, W + c * LANES:W + (c + 1) * LANES]
        v = v_all[:, sl]
        if has_res:
            v = v + (vf_ref[:, sl] - v) * _sigmoid(v0_ref[:, sl] + mix[:, sl])
        v_o[:, sl] = v
        kk = k * kk_ref[:, sl]
        ss = _dot_sel(kk * kk, bd_ref[...])
        kn = kk / jnp.maximum(jnp.sqrt(ss), 1e-12)
        kn_o[:, sl] = kn
        b_o[:, sl] = kn * a
        k_o[:, sl] = k * (1.0 + (a - 1.0) * ka_ref[:, sl])


def rwkv_prep(proj, seq, tm, mu, w0, a0, wa_up, g_up, k_k, k_a, bd, res):
    T = proj.shape[0]
    W = RWKV_HEADS * HEAD_DIM
    cols = mu.shape[-1]
    row1 = lambda n: pl.BlockSpec((1, n), lambda i: (0, 0))
    full = lambda a: pl.BlockSpec(a.shape, lambda i: (0, 0))
    tile = pl.BlockSpec((tm, W), lambda i: (i, 0))
    in_specs = [pl.BlockSpec((tm, cols), lambda i: (i, 0)),
                pl.BlockSpec((8, cols), lambda i: (jnp.maximum(i * (tm // 8) - 1, 0), 0)),
                row1(cols), row1(W), row1(W), full(wa_up), full(g_up), row1(W), row1(W), full(bd)]
    args = [proj, proj, mu, w0, a0, wa_up, g_up, k_k, k_a, bd]
    if res is not None:
        v_first, v0, mvd, mvu = res
        in_specs += [tile, row1(W), full(mvd), full(mvu)]
        args += [v_first, v0, mvd, mvu]
    out = jax.ShapeDtypeStruct((T, W), jnp.float32)
    kern = functools.partial(_rwkv_prep_kernel, width=W, has_res=res is not None, rows_per_seq=seq // tm)
    return pl.pallas_call(
        kern,
        grid=(T // tm,),
        in_specs=in_specs,
        out_specs=[tile] * 7,
        out_shape=[out] * 7,
        compiler_params=_params(("parallel",)),
        name="rwkv_prep",
    )(*args)


def _stack_heads(x, lane):
    return jnp.concatenate([jnp.where(lane < HEAD_DIM, x, 0.0), jnp.where(lane >= HEAD_DIM, x, 0.0)], axis=0)


def _rwkv_chunk_kernel(r_ref, lw_ref, k_ref, v_ref, kn_ref, b_ref, g_ref, rk_ref, bd_ref,
                       rhat_ref, y0_ref, e_ref, g_out_ref, d_out_ref, *, chunks):
    C = RWKV_CHUNK
    C2, C4 = 2 * C, 4 * C
    f32 = jnp.float32
    tri = (lax.broadcasted_iota(jnp.int32, (C, C), 0) >= lax.broadcasted_iota(jnp.int32, (C, C), 1)).astype(f32)
    lane = lax.broadcasted_iota(jnp.int32, (1, LANES), 1)
    rho = lax.broadcasted_iota(jnp.int32, (C4, C4), 0)
    sig = lax.broadcasted_iota(jnp.int32, (C4, C4), 1)
    keep = jnp.where(rho >= C2, rho & (C - 1), (rho & (C - 1)) - 1) >= (sig & (C - 1))
    eye2 = (lax.broadcasted_iota(jnp.int32, (C2, C2), 0) == lax.broadcasted_iota(jnp.int32, (C2, C2), 1)).astype(f32)
    eye_l = (lax.broadcasted_iota(jnp.int32, (LANES, LANES), 0)
             == lax.broadcasted_iota(jnp.int32, (LANES, LANES), 1)).astype(f32)
    zeros2 = jnp.zeros((C2, LANES), f32)
    ch = range(chunks)
    rows = [slice(c * C, (c + 1) * C) for c in ch]
    cum = [_dot(tri, lw_ref[rows[c], :], precision=HI) for c in ch]
    st = []
    for c in ch:
        r, lw, k, v = r_ref[rows[c], :], lw_ref[rows[c], :], k_ref[rows[c], :], v_ref[rows[c], :]
        kn, bb = kn_ref[rows[c], :], b_ref[rows[c], :]
        cum_last = cum[c][C - 1:C, :]
        e_out = jnp.exp(-cum[c])
        e_tail = jnp.exp(cum_last - cum[c])
        r_t = r * jnp.exp(cum[c])
        st.append(dict(
            r_t=r_t, decay=jnp.exp(cum_last),
            a2=_stack_heads(-kn * jnp.exp(cum[c] - lw), lane), r2=_stack_heads(r_t, lane),
            b2=_stack_heads(bb * e_out, lane), k2=_stack_heads(k * e_out, lane), v2=_stack_heads(v, lane),
            bh2=_stack_heads(bb * e_tail, lane), kh2=_stack_heads(k * e_tail, lane),
            e=_dot_sel(r * k * rk_ref[...], bd_ref[...]) * v * g_ref[rows[c], :]))
    quad = [jnp.where(keep, _dot3(jnp.concatenate([s["a2"], s["r2"]], axis=0),
                                  jnp.concatenate([s["b2"], s["k2"]], axis=0), _NT), 0.0) for s in st]
    pw = [q[:C2, :C2] for q in quad]
    t_inv = [eye2 + p for p in pw]
    x = [_dot3(q[:C2, C2:], s["v2"]) for q, s in zip(quad, st)]
    for level in range(int(math.log2(C)) - 1):
        mm = _dot3 if level < 2 else (lambda a, b: _dot(a.astype(jnp.bfloat16), b.astype(jnp.bfloat16)))
        pw = [mm(p, p) for p in pw]
        t_inv = [t + mm(t, p) for t, p in zip(t_inv, pw)]
    au = [_dot3(t, jnp.concatenate([s["a2"], xx], axis=1)) for t, s, xx in zip(t_inv, st, x)]
    my = [_dot3(q[C2:, :], jnp.concatenate([a, jnp.concatenate([zeros2, s["v2"]], axis=1)], axis=0))
          for q, a, s in zip(quad, au, st)]
    gm = [eye_l * s["decay"] + _dot3(a[:, :LANES], s["bh2"], _TN) for a, s in zip(au, st)]
    dm = [_dot3(jnp.concatenate([a[:, LANES:], s["v2"]], axis=0),
                jnp.concatenate([s["bh2"], s["kh2"]], axis=0), _TN) for a, s in zip(au, st)]
    rhat_ref[...] = jnp.concatenate([s["r_t"] + m[:C, :LANES] + m[C:, :LANES] for s, m in zip(st, my)], axis=0)
    y0_ref[...] = jnp.concatenate([m[:C, LANES:] + m[C:, LANES:] for m in my], axis=0)
    e_ref[...] = jnp.concatenate([s["e"] for s in st], axis=0)
    g_out_ref[...] = jnp.stack(gm)
    d_out_ref[...] = jnp.stack(dm)


def rwkv_chunk(r, lw, k, v, kn, b, g, r_k, bd, batch, chunks=8):
    T, W = r.shape
    S = T // batch
    C = RWKV_CHUNK
    nc = S // C
    pairs = W // LANES
    steps = nc // chunks
    tile = pl.BlockSpec((chunks * C, LANES), lambda bi, h, c: (bi * steps + c, h))
    mat = pl.BlockSpec((None, None, chunks, LANES, LANES), lambda bi, h, c: (bi, h, c, 0, 0))
    tw = jax.ShapeDtypeStruct((T, W), jnp.float32)
    gd = jax.ShapeDtypeStruct((batch, pairs, nc, LANES, LANES), jnp.float32)
    return pl.pallas_call(
        functools.partial(_rwkv_chunk_kernel, chunks=chunks),
        grid=(batch, pairs, steps),
        in_specs=[tile] * 7 + [pl.BlockSpec((1, LANES), lambda bi, h, c: (0, h)),
                               pl.BlockSpec((LANES, LANES), lambda bi, h, c: (0, 0))],
        out_specs=[tile, tile, tile, mat, mat],
        out_shape=[tw, tw, tw, gd, gd],
        compiler_params=_params(("parallel", "parallel", "parallel")),
        name="rwkv_chunk",
    )(r, lw, k, v, kn, b, g, r_k, bd)


def _rwkv_state_kernel(rhat_ref, y0_ref, g_ref, e_ref, gm_ref, dm_ref, lng_ref, lnb_ref, bd_ref,
                       o_ref, state_ref, *, chunks, group):
    C = RWKV_CHUNK

    @pl.when(pl.program_id(2) == 0)
    def _():
        state_ref[...] = jnp.zeros_like(state_ref)

    bd = bd_ref[...]
    inv_n = 1.0 / HEAD_DIM

    def body(c, carry):
        rows = pl.ds(pl.multiple_of(c * C, C), C)
        pairs = range(group)
        state = [state_ref[p] for p in pairs]
        y = [_dot3(rhat_ref[rows, p * LANES:(p + 1) * LANES], state[p], _NT) for p in pairs]
        new_state = [_dot3(state[p], gm_ref[p, c]) + dm_ref[p, c] for p in pairs]
        y = jnp.concatenate(y, axis=1) + y0_ref[rows, :]
        state_ref[...] = jnp.stack(new_state)
        mean = jnp.concatenate([_dot_sel(y[:, p * LANES:(p + 1) * LANES], bd) for p in pairs], axis=1) * inv_n
        d = y - mean
        dd = d * d
        var = jnp.concatenate([_dot_sel(dd[:, p * LANES:(p + 1) * LANES], bd) for p in pairs], axis=1) * inv_n
        yn = d * lax.rsqrt(var + RWKV_GN_EPS) * lng_ref[...] + lnb_ref[...]
        o_ref[rows, :] = yn * g_ref[rows, :] + e_ref[rows, :]
        return carry

    lax.fori_loop(0, chunks, body, 0)


def rwkv_state(rhat, y0, g, e, gmat, dmat, ln_g, ln_b, bd, chunks=16, group=3):
    T, W = rhat.shape
    batch, pairs, nc = gmat.shape[:3]
    C = RWKV_CHUNK
    steps = nc // chunks
    gw = group * LANES
    tile = pl.BlockSpec((chunks * C, gw), lambda bi, h, c: (bi * steps + c, h))
    mat = pl.BlockSpec((None, group, chunks, LANES, LANES), lambda bi, h, c: (bi, h, c, 0, 0))
    row = pl.BlockSpec((1, gw), lambda bi, h, c: (0, h))
    return pl.pallas_call(
        functools.partial(_rwkv_state_kernel, chunks=chunks, group=group),
        grid=(batch, pairs // group, steps),
        in_specs=[tile] * 4 + [mat, mat, row, row, pl.BlockSpec((LANES, LANES), lambda bi, h, c: (0, 0))],
        out_specs=tile,
        out_shape=jax.ShapeDtypeStruct((T, W), jnp.float32),
        scratch_shapes=[pltpu.VMEM((group, LANES, LANES), jnp.float32)],
        compiler_params=_params(("parallel", "parallel", "arbitrary")),
        name="rwkv_state",
    )(rhat, y0, g, e, gmat, dmat, ln_g, ln_b, bd)


def _merge_kernel(oa_ref, ob_ref, oc0_ref, oc1_ref, oc2_ref, lw0_ref, lw1_ref, lw2_ref, ga_ref, gb_ref, gc_ref,
                  pa_ref, pb_ref, pc_ref, o_ref):
    lw = [lw0_ref[...], lw1_ref[...], lw2_ref[...]]
    mx = jnp.maximum(jnp.maximum(lw[0], lw[1]), lw[2])
    wgt = [jnp.exp(x - mx) for x in lw]
    o_c = ((wgt[0] * oc0_ref[...] + wgt[1] * oc1_ref[...] + wgt[2] * oc2_ref[...])
           / (wgt[0] + wgt[1] + wgt[2]))
    bf = jnp.bfloat16
    merged = (_sigmoid(ga_ref[...]) * _dot(oa_ref[...].astype(bf), pa_ref[...])
              + _sigmoid(gb_ref[...]) * _dot(ob_ref[...].astype(bf), pb_ref[...])
              + _sigmoid(gc_ref[...]) * _dot(o_c.astype(bf), pc_ref[...]))
    o_ref[...] = merged.astype(o_ref.dtype)


def merge_branches(o_a, o_b, o_c, lw_c, proj, gate_off, p_a, p_b, p_c, tm, tn):
    T = o_a.shape[0]
    D = p_a.shape[1]
    nj = D // tn
    grp = pl.BlockSpec((tm, o_c[0].shape[1]), lambda i, j: (i, 0))
    gate = lambda n: pl.BlockSpec((tm, tn), lambda i, j: (i, gate_off + n * nj + j))
    wspec = lambda a: pl.BlockSpec((a.shape[0], tn), lambda i, j: (0, j))
    return pl.pallas_call(
        _merge_kernel,
        grid=(T // tm, nj),
        in_specs=[pl.BlockSpec((tm, o_a.shape[1]), lambda i, j: (i, 0)),
                  pl.BlockSpec((tm, o_b.shape[1]), lambda i, j: (i, 0)),
                  grp, grp, grp, grp, grp, grp, gate(0), gate(1), gate(2), wspec(p_a), wspec(p_b), wspec(p_c)],
        out_specs=pl.BlockSpec((tm, tn), lambda i, j: (i, j)),
        out_shape=jax.ShapeDtypeStruct((T, D), jnp.bfloat16),
        compiler_params=_params(("parallel", "arbitrary")),
        name="merge",
    )(o_a, o_b, *o_c, *lw_c, proj, proj, proj, p_a, p_b, p_c)


def _out_ln_kernel(mg_ref, w_ref, x_ref, g1_ref, lg_ref, lb_ref, o_ref, *, alpha):
    mix = _dot(mg_ref[...], w_ref[...])
    y = alpha * x_ref[...] + (1.0 + g1_ref[...]) * mix
    o_ref[...] = _layer_norm(y, lg_ref[...], lb_ref[...])


def out_ln(merged, w_o, x2, g1, ln_g, ln_b, seq, tm, alpha):
    T, D = x2.shape
    per = seq // tm
    tile = pl.BlockSpec((tm, D), lambda i: (i, 0))
    row = pl.BlockSpec((1, D), lambda i: (0, 0))
    return pl.pallas_call(
        functools.partial(_out_ln_kernel, alpha=alpha),
        grid=(T // tm,),
        in_specs=[tile, pl.BlockSpec((D, D), lambda i: (0, 0)), tile,
                  pl.BlockSpec((None, 1, D), lambda i: (i // per, 0, 0)), row, row],
        out_specs=tile,
        out_shape=jax.ShapeDtypeStruct((T, D), jnp.float32),
        compiler_params=_params(("parallel",)),
        name="out_ln",
    )(merged, w_o, x2, g1, ln_g, ln_b)


def _router_kernel(x_ref, sc_ref, sh_ref, w_ref, b_ref, id_ref, gw_ref, h_ref):
    h = x_ref[...] * (1.0 + sc_ref[...]) + sh_ref[...]
    h_ref[...] = h.astype(h_ref.dtype)
    lg = _dot3(h, w_ref[...]) + b_ref[...]
    G, EPG = MOE_GROUPS, MOE_EXPERTS_PER_GROUP
    lane = lax.broadcasted_iota(jnp.int32, (1, LANES), 1).astype(jnp.float32)
    first = lambda hit: jnp.min(jnp.where(hit, lane, float(LANES)), axis=-1, keepdims=True)
    is_grp = lane < G
    gmax = jnp.max(jnp.where(is_grp, lg, -jnp.inf), axis=-1, keepdims=True)
    ge = jnp.where(is_grp, jnp.exp(jnp.where(is_grp, lg, gmax) - gmax), 0.0)
    prob = ge / jnp.sum(ge, axis=-1, keepdims=True)
    grp_p = jnp.max(prob, axis=-1, keepdims=True)
    grp_i = first(is_grp & (prob == grp_p))
    lo = G + grp_i * EPG
    el = jnp.where((lane >= lo) & (lane < lo + EPG), lg, -jnp.inf)
    l1 = jnp.max(el, axis=-1, keepdims=True)
    i1 = first(el == l1)
    el = jnp.where(lane == i1, -jnp.inf, el)
    l2 = jnp.max(el, axis=-1, keepdims=True)
    i2 = first(el == l2)
    t = jnp.exp(l2 - l1)
    w1 = grp_p / (1.0 + t)
    id_ref[...] = jnp.where(lane == 0, i1 - G, jnp.where(lane == 1, i2 - G, 0.0)).astype(jnp.int32)
    gw_ref[...] = jnp.where(lane == 0, w1, jnp.where(lane == 1, w1 * t, 0.0))


def router(x2, sc, sh, w_r, b_r, seq, tm):
    T, D = x2.shape
    per = seq // tm
    mod = pl.BlockSpec((None, 1, D), lambda i: (i // per, 0, 0))
    narrow = pl.BlockSpec((tm, LANES), lambda i: (i, 0))
    return pl.pallas_call(
        _router_kernel,
        grid=(T // tm,),
        in_specs=[pl.BlockSpec((tm, D), lambda i: (i, 0)), mod, mod,
                  pl.BlockSpec((D, LANES), lambda i: (0, 0)), pl.BlockSpec((1, LANES), lambda i: (0, 0))],
        out_specs=[narrow, narrow, pl.BlockSpec((tm, D), lambda i: (i, 0))],
        out_shape=[jax.ShapeDtypeStruct((T, LANES), jnp.int32), jax.ShapeDtypeStruct((T, LANES), jnp.float32),
                   jax.ShapeDtypeStruct((T, D), jnp.bfloat16)],
        compiler_params=_params(("parallel",)),
        name="router",
    )(x2, sc, sh, w_r, b_r)


def _experts_kernel(be_ref, nu_ref, x_ref, wg_ref, wu_ref, wd_ref, o_ref, wg_s, wu_s, wd_s):
    i = pl.program_id(0)
    used = i < nu_ref[0]
    new_expert = (i == 0) | (be_ref[i] != be_ref[jnp.maximum(i - 1, 0)])

    @pl.when(used & new_expert)
    def _():
        wg_s[...] = wg_ref[...].astype(wg_s.dtype)
        wu_s[...] = wu_ref[...].astype(wu_s.dtype)
        wd_s[...] = wd_ref[...].astype(wd_s.dtype)

    @pl.when(used)
    def _():
        x = x_ref[...]
        gate = _dot(x, wg_s[...])
        hid = gate * _sigmoid(gate) * _dot(x, wu_s[...])
        o_ref[...] = _dot(hid.astype(jnp.bfloat16), wd_s[...])

    @pl.when(jnp.logical_not(used))
    def _():
        o_ref[...] = jnp.zeros_like(o_ref)


def experts(xs, block_e, n_used, w_gate, w_up, w_down, layer):
    R, D = xs.shape
    F = w_gate.shape[-1]
    n_blocks = R // MOE_BLOCK
    grid_spec = pltpu.PrefetchScalarGridSpec(
        num_scalar_prefetch=2,
        grid=(n_blocks,),
        in_specs=[pl.BlockSpec((MOE_BLOCK, D), lambda i, be, nu: (i, 0)),
                  pl.BlockSpec((None, None, D, F), lambda i, be, nu: (layer, be[i], 0, 0)),
                  pl.BlockSpec((None, None, D, F), lambda i, be, nu: (layer, be[i], 0, 0)),
                  pl.BlockSpec((None, None, F, D), lambda i, be, nu: (layer, be[i], 0, 0))],
        out_specs=pl.BlockSpec((MOE_BLOCK, D), lambda i, be, nu: (i, 0)),
        scratch_shapes=[pltpu.VMEM((D, F), jnp.bfloat16), pltpu.VMEM((D, F), jnp.bfloat16),
                        pltpu.VMEM((F, D), jnp.bfloat16)],
    )
    return pl.pallas_call(
        _experts_kernel,
        grid_spec=grid_spec,
        out_shape=jax.ShapeDtypeStruct((R, D), jnp.float32),
        compiler_params=_params(("arbitrary",)),
        name="experts",
    )(block_e, n_used, xs, w_gate, w_up, w_down)


def _combine_ln_kernel(x_ref, f0_ref, f1_ref, gw_ref, g2_ref, lg_ref, lb_ref, o_ref, *, alpha):
    gw = gw_ref[...]
    ffn = gw[:, 0:1] * f0_ref[...] + gw[:, 1:2] * f1_ref[...]
    y = alpha * x_ref[...] + (1.0 + g2_ref[...]) * ffn
    o_ref[...] = _layer_norm(y, lg_ref[...], lb_ref[...])


def combine_ln(x2, f0, f1, gws, g2, ln_g, ln_b, seq, tm, alpha):
    T, D = x2.shape
    per = seq // tm
    tile = pl.BlockSpec((tm, D), lambda i: (i, 0))
    row = pl.BlockSpec((1, D), lambda i: (0, 0))
    return pl.pallas_call(
        functools.partial(_combine_ln_kernel, alpha=alpha),
        grid=(T // tm,),
        in_specs=[tile, tile, tile, pl.BlockSpec((tm, LANES), lambda i: (i, 0)),
                  pl.BlockSpec((None, 1, D), lambda i: (i // per, 0, 0)), row, row],
        out_specs=tile,
        out_shape=jax.ShapeDtypeStruct((T, D), jnp.float32),
        compiler_params=_params(("parallel",)),
        name="combine_ln",
    )(x2, f0, f1, gws, g2, ln_g, ln_b)


def _route(expert_id):
    T = expert_id.shape[0]
    E = MOE_GROUPS * MOE_EXPERTS_PER_GROUP
    A = T * MOE_TOPK
    n_blocks = (A + E * (MOE_BLOCK - 1) + MOE_BLOCK - 1) // MOE_BLOCK
    flat_e = expert_id.reshape(A)
    e_s, order = lax.sort_key_val(flat_e, jnp.arange(A, dtype=jnp.int32))
    bounds = jnp.searchsorted(e_s, jnp.arange(E + 1, dtype=jnp.int32), side='left').astype(jnp.int32)
    start = bounds[:E]
    counts = bounds[1:] - start
    padded = (counts + MOE_BLOCK - 1) // MOE_BLOCK * MOE_BLOCK
    pad_end = jnp.cumsum(padded)
    pad_start = pad_end - padded
    block_e = jnp.minimum(jnp.sum(pad_end[None, :] <= (jnp.arange(n_blocks) * MOE_BLOCK)[:, None], axis=1),
                          E - 1).astype(jnp.int32)
    blk_rank = jnp.arange(n_blocks, dtype=jnp.int32) * MOE_BLOCK - pad_start[block_e]
    rank = blk_rank[:, None] + jnp.arange(MOE_BLOCK, dtype=jnp.int32)[None, :]
    valid = (rank < counts[block_e][:, None]).reshape(-1)
    src = order[jnp.clip(start[block_e][:, None] + rank, 0, A - 1).reshape(-1)]
    spread = jnp.arange(n_blocks * MOE_BLOCK, dtype=jnp.int32) % T
    slot_tok = jnp.where(valid, src // MOE_TOPK, spread).astype(jnp.int32)
    pos = jnp.arange(A, dtype=jnp.int32)[None, :]
    in_e = (pos >= start[:, None]) & (pos < bounds[1:, None])
    dest = pos[0] + jnp.sum(jnp.where(in_e, (pad_start - start)[:, None], 0), axis=0).astype(jnp.int32)
    _, slot_of = lax.sort_key_val(order, dest)
    n_used = (pad_end[-1] // MOE_BLOCK).astype(jnp.int32).reshape(1)
    return slot_tok, slot_of.reshape(T, MOE_TOPK), block_e, n_used


def kernel(x, c, rel_bias, w_in, p_a, p_b, p_c, w_o, rwkv_mu, rwkv_w0, rwkv_w_up, rwkv_a0, rwkv_a_up,
           rwkv_g_up, rwkv_k_k, rwkv_k_a, rwkv_r_k, rwkv_ln_g, rwkv_ln_b, rwkv_v0, rwkv_mv_down,
           rwkv_mv_up, w_ada, b_ada, ln1_g, ln1_b, ln2_g, ln2_b, router_grp_w, router_grp_b,
           router_exp_w, router_exp_b, exp_w_gate, exp_w_up, exp_w_down):
    B, S, D = x.shape
    depth = w_in.shape[0]
    T = B * S
    bf = jnp.bfloat16
    W = RWKV_HEADS * HEAD_DIM
    wa_w = MOBA_HEADS * HEAD_DIM
    wc_w = len(DIL_GROUPS) * DIL_HEADS_PER_GROUP * HEAD_DIM
    rw_cols = rwkv_mu.shape[-1]
    alpha = (2 * depth) ** 0.25
    off_c = 3 * wa_w
    off_b = off_c + 3 * wc_w
    off_g = off_b + rw_cols
    new_a = rw_cols
    new_c = new_a + 3 * wa_w
    new_g = new_c + 3 * wc_w
    w_in_p = jnp.concatenate([w_in[:, :, off_b:off_g], w_in[:, :, :off_b], w_in[:, :, off_g:]], axis=-1).astype(bf)
    p_a_b, p_b_b, p_c_b, w_o_b = p_a.astype(bf), p_b.astype(bf), p_c.astype(bf), w_o.astype(bf)

    bias_h = rel_bias.T.astype(jnp.float32)
    moba_tab = moba_bias_table(bias_h[:MOBA_HEADS], S // MOBA_BLOCK)
    dil_tab = dil_bias_table(bias_h[MOBA_HEADS:])

    c8 = jnp.zeros((8, D), jnp.float32).at[:B].set(c)
    mod = ada_mod(c8, w_ada, b_ada)[:, :B]

    hd_idx = jnp.arange(LANES) // HEAD_DIM
    bd = (hd_idx[:, None] == hd_idx[None, :]).astype(jnp.float32)
    zeros_w = jnp.zeros((RWKV_DECAY_LORA, W), jnp.float32)
    pad_lora = LANES - RWKV_MV_LORA

    x2 = x.reshape(T, D)
    v_first = None
    for l in range(depth):
        sh1, sc1, g1, sh2, sc2, g2 = [m.reshape(B, 1, D) for m in jnp.split(mod[l], 6, axis=-1)]
        proj = in_proj(x2, sc1, sh1, w_in_p, l, S, 1024, 1024)
        proj3 = proj.reshape(B, S, -1)
        o_a = moba_attention(proj3, moba_tab, new_a // LANES, (new_a + wa_w) // LANES,
                             (new_a + 2 * wa_w) // LANES).reshape(T, wa_w)
        gw = DIL_HEADS_PER_GROUP * HEAD_DIM
        o_c, lw_c = [], []
        for g in range(len(DIL_GROUPS)):
            og, lwg = dilated_attention(proj3, dil_tab, (new_c + g * gw) // LANES,
                                        (new_c + wc_w + g * gw) // LANES, (new_c + 2 * wc_w + g * gw) // LANES, g)
            o_c.append(og.reshape(T, gw))
            lw_c.append(lwg.reshape(T, gw))
        wa_up = jnp.concatenate([jnp.concatenate([rwkv_w_up[l], zeros_w], axis=1),
                                 jnp.concatenate([zeros_w, rwkv_a_up[l]], axis=1)], axis=0)
        res = None
        if l > 0:
            res = (v_first, rwkv_v0[l - 1][None],
                   jnp.pad(rwkv_mv_down[l - 1], ((0, 0), (0, pad_lora))),
                   jnp.pad(rwkv_mv_up[l - 1], ((0, pad_lora), (0, 0))))
        r_, lw_, k_, v_, kn_, b_, g_ = rwkv_prep(proj, S, 512, rwkv_mu[l][None], rwkv_w0[l][None],
                                                 rwkv_a0[l][None], wa_up, rwkv_g_up[l], rwkv_k_k[l][None],
                                                 rwkv_k_a[l][None], bd, res)
        if l == 0:
            v_first = v_
        rhat, y0, e_, gmat, dmat = rwkv_chunk(r_, lw_, k_, v_, kn_, b_, g_, rwkv_r_k[l][None], bd, B)
        o_b = rwkv_state(rhat, y0, g_, e_, gmat, dmat, rwkv_ln_g[l][None], rwkv_ln_b[l][None], bd)
        merged = merge_branches(o_a, o_b, o_c, lw_c, proj, new_g // 1024, p_a_b[l], p_b_b[l], p_c_b[l],
                                512, 1024)
        x2 = out_ln(merged, w_o_b[l], x2, g1, ln1_g[l][None], ln1_b[l][None], S, 512, alpha)
        w_r = jnp.zeros((D, LANES), jnp.float32)
        w_r = w_r.at[:, :MOE_GROUPS].set(router_grp_w[l]).at[:, MOE_GROUPS:MOE_GROUPS + router_exp_w.shape[-1]].set(
            router_exp_w[l])
        b_r = jnp.zeros((1, LANES), jnp.float32)
        b_r = b_r.at[0, :MOE_GROUPS].set(router_grp_b[l]).at[0, MOE_GROUPS:MOE_GROUPS + router_exp_b.shape[-1]].set(
            router_exp_b[l])
        ids, gws, h2 = router(x2, sc2, sh2, w_r, b_r, S, 512)
        slot_tok, slot_of, block_e, n_used = _route(ids[:, :MOE_TOPK])
        y = experts(h2[slot_tok], block_e, n_used, exp_w_gate, exp_w_up, exp_w_down, l)
        x2 = combine_ln(x2, y[slot_of[:, 0]], y[slot_of[:, 1]], gws, g2, ln2_g[l][None], ln2_b[l][None], S, 512,
                        alpha)
    return x2.reshape(B, S, D)
```

```python
import functools
import math

import jax
import jax.numpy as jnp
import numpy as np
from jax import lax
from jax.experimental import pallas as pl
from jax.experimental.pallas import tpu as pltpu

HEAD_DIM = 64
LANES = 128
MOBA_HEADS = 12
MOBA_BLOCK = 256
MOBA_TOPK = 3
MOBA_STEP = 2
RWKV_HEADS = 12
RWKV_DECAY_LORA = 64
RWKV_A_LORA = 64
RWKV_MV_LORA = 32
RWKV_GATE_LORA = 128
RWKV_GN_EPS = 64e-5
RWKV_CHUNK = 64
DIL_GROUPS = ((128, 1), (512, 4), (2048, 16))
DIL_HEADS_PER_GROUP = 4
DIL_SPAN = 128
REL_BUCKETS = 32
REL_MAX_DISTANCE = 2048
MOE_GROUPS = 8
MOE_EXPERTS_PER_GROUP = 8
MOE_TOPK = 2
MOE_BLOCK = 256
LN_EPS = 1e-5
NEG = -1e30
LOG2E = math.log2(math.e)
LN2 = math.log(2.0)
VMEM_LIMIT = 56 * 1024 * 1024
HI = lax.Precision.HIGHEST


def _params(sem):
    return pltpu.CompilerParams(dimension_semantics=sem, vmem_limit_bytes=VMEM_LIMIT)


def _sigmoid(x):
    return 0.5 * jnp.tanh(0.5 * x) + 0.5


def _dot(a, b, precision=None):
    return jnp.dot(a, b, preferred_element_type=jnp.float32, precision=precision)


def _dot_nt(a, b, precision=None):
    return lax.dot_general(a, b, (((1,), (1,)), ((), ())), preferred_element_type=jnp.float32,
                           precision=precision)


def _dot_tn(a, b, precision=None):
    return lax.dot_general(a, b, (((0,), (0,)), ((), ())), preferred_element_type=jnp.float32,
                           precision=precision)


def _split2(x):
    hi = x.astype(jnp.bfloat16)
    return hi, (x - hi.astype(jnp.float32)).astype(jnp.bfloat16)


def _dot3(a, b, dims=((1,), (0,))):
    (ca,), (cb,) = dims
    ah, al = _split2(a)
    bh, bl = _split2(b)
    return lax.dot_general(jnp.concatenate([ah, ah, al], axis=ca), jnp.concatenate([bh, bl, bh], axis=cb),
                           (dims, ((), ())), preferred_element_type=jnp.float32)


def _dot_sel(x, sel):
    xh, xl = _split2(x)
    sb = sel.astype(jnp.bfloat16)
    return _dot(jnp.concatenate([xh, xl], axis=1), jnp.concatenate([sb, sb], axis=0))


_NT = ((1,), (1,))
_TN = ((0,), (0,))


def _layer_norm(y, g, b):
    mu = jnp.mean(y, axis=-1, keepdims=True)
    d = y - mu
    var = jnp.mean(d * d, axis=-1, keepdims=True)
    return d * lax.rsqrt(var + LN_EPS) * g + b


def _ada_kernel(c_ref, w_ref, b_ref, o_ref):
    c = c_ref[...]
    cond = c * _sigmoid(c)
    o_ref[...] = _dot(cond, w_ref[...]) + b_ref[...]


def ada_mod(c8, w_ada, b_ada):
    L, D, N = w_ada.shape
    tn = 1024
    return pl.pallas_call(
        _ada_kernel,
        grid=(L, N // tn),
        in_specs=[pl.BlockSpec((8, D), lambda l, j: (0, 0)),
                  pl.BlockSpec((None, D, tn), lambda l, j: (l, 0, j)),
                  pl.BlockSpec((None, 1, tn), lambda l, j: (l, 0, j))],
        out_specs=pl.BlockSpec((None, 8, tn), lambda l, j: (l, 0, j)),
        out_shape=jax.ShapeDtypeStruct((L, 8, N), jnp.float32),
        compiler_params=_params(("parallel", "parallel")),
        name="ada_mod",
    )(c8, w_ada, b_ada.reshape(L, 1, N))


def _in_proj_kernel(x_ref, sc_ref, sh_ref, w_ref, o_ref, h_ref):
    @pl.when(pl.program_id(1) == 0)
    def _():
        h_ref[...] = (x_ref[...] * (1.0 + sc_ref[...]) + sh_ref[...]).astype(h_ref.dtype)

    o_ref[...] = _dot(h_ref[...], w_ref[...])


def in_proj(x2, sc, sh, w, layer, seq, tm, tn):
    T, D = x2.shape
    N = w.shape[-1]
    per = seq // tm
    return pl.pallas_call(
        _in_proj_kernel,
        grid=(T // tm, N // tn),
        in_specs=[pl.BlockSpec((tm, D), lambda i, j: (i, 0)),
                  pl.BlockSpec((None, 1, D), lambda i, j: (i // per, 0, 0)),
                  pl.BlockSpec((None, 1, D), lambda i, j: (i // per, 0, 0)),
                  pl.BlockSpec((None, D, tn), lambda i, j: (layer, 0, j))],
        out_specs=pl.BlockSpec((tm, tn), lambda i, j: (i, j)),
        out_shape=jax.ShapeDtypeStruct((T, N), jnp.float32),
        scratch_shapes=[pltpu.VMEM((tm, D), jnp.bfloat16)],
        compiler_params=_params(("parallel", "arbitrary")),
        name="in_proj",
    )(x2, sc, sh, w)


def _t5_bucket(dist):
    n = jnp.maximum(dist, 0)
    max_exact = REL_BUCKETS // 2
    nf = jnp.maximum(n, 1).astype(jnp.float32)
    large = max_exact + (jnp.log(nf / max_exact) / math.log(REL_MAX_DISTANCE / max_exact)
                         * (REL_BUCKETS - max_exact)).astype(jnp.int32)
    large = jnp.minimum(large, REL_BUCKETS - 1)
    return jnp.where(n < max_exact, n, large)


def _moba_n_delta(nb):
    last_start = 1
    d = np.arange(1, nb * MOBA_BLOCK + 1)
    large = 16 + (np.log(d / 16.0) / math.log(REL_MAX_DISTANCE / 16.0) * 16).astype(np.int64)
    bucket = np.where(d < 16, d, np.minimum(large, REL_BUCKETS - 1))
    last_start = int(d[bucket < REL_BUCKETS - 1].max()) + 1 if (bucket < REL_BUCKETS - 1).any() else 1
    delta = 1
    while delta * MOBA_BLOCK - (MOBA_BLOCK - 1) < last_start + 2:
        delta += 1
    return min(delta + 1, nb)


def moba_bias_table(bias_a, nb):
    nd = _moba_n_delta(nb)
    key = jnp.arange(MOBA_BLOCK)[:, None]
    qry = jnp.arange(MOBA_BLOCK)[None, :]
    dist = jnp.arange(nd)[:, None, None] * MOBA_BLOCK + (qry - key)[None]
    bucket = _t5_bucket(dist)[None]
    tab = jnp.zeros((bias_a.shape[0],) + dist.shape, jnp.float32)
    for b in range(REL_BUCKETS):
        tab = jnp.where(bucket == b, bias_a[:, b][:, None, None, None], tab)
    return jnp.where((dist >= 0)[None], tab * LOG2E, NEG)


def dil_bias_table(bias_c):
    span = DIL_SPAN
    rel = span + jnp.arange(span)[:, None] - jnp.arange(2 * span)[None, :]
    valid = (rel >= 0) & (rel <= span)
    tabs = []
    for g, (_, dilation) in enumerate(DIL_GROUPS):
        bh = bias_c[g * DIL_HEADS_PER_GROUP:(g + 1) * DIL_HEADS_PER_GROUP]
        tabs.append(jnp.where(valid[None], bh[:, _t5_bucket(rel * dilation)] * LOG2E, NEG))
    return jnp.stack(tabs)


def _moba_kernel(q_ref, k_ref, v_ref, bias_ref, o_ref,
                 kaug_ref, vt_ref, km_ref, acc_ref, s_ref, *, nb, n_delta):
    qb = pl.program_id(2)
    bs = MOBA_BLOCK
    S = nb * bs
    half = bs // 2
    lane = lax.broadcasted_iota(jnp.int32, (1, LANES), 1)
    f32 = jnp.float32

    @pl.when(qb == 0)
    def _():
        k = k_ref[...]
        rowblk = lax.broadcasted_iota(jnp.int32, (S, LANES), 0) // bs
        lanes = lax.broadcasted_iota(jnp.int32, (S, LANES), 1)
        km_ref[...] = jnp.mean(k.reshape(nb, bs, LANES), axis=1)
        kaug_ref[0] = jnp.where(lanes < HEAD_DIM, k, (lanes - HEAD_DIM == rowblk).astype(f32)).astype(kaug_ref.dtype)
        kaug_ref[1] = jnp.where(lanes >= HEAD_DIM, k, (lanes == rowblk).astype(f32)).astype(kaug_ref.dtype)
        for i in range(nb):
            vt_ref[i] = v_ref[i * bs:(i + 1) * bs, :].T.astype(vt_ref.dtype)

    q_t = q_ref[...].T
    scale = HEAD_DIM ** -0.5 * LOG2E
    blk = lax.broadcasted_iota(jnp.int32, (nb, 1), 0)
    past = blk < qb
    rhs = []
    for t in range(2):
        hm = (lane >= t * HEAD_DIM) & (lane < (t + 1) * HEAD_DIM)
        gate = _dot(jnp.where(hm, km_ref[...], 0.0), q_t, precision=HI)
        gate = jnp.where(past, gate, -jnp.inf)
        cnt = jnp.zeros((nb, bs), jnp.int32)
        for m in range(nb):
            gm = gate[m:m + 1, :]
            ahead = (gm > gate) | ((gm == gate) & (m < blk))
            cnt = cnt + ahead.astype(jnp.int32)
        chosen = (past & (cnt < MOBA_TOPK)) | (blk == qb)
        pen = jnp.where(chosen, 0.0, NEG)
        qh = q_t[t * HEAD_DIM:(t + 1) * HEAD_DIM, :] * scale
        if t == 0:
            parts = [qh, pen, jnp.zeros((LANES - HEAD_DIM - nb, bs), f32)]
        else:
            parts = [pen, jnp.zeros((HEAD_DIM - nb, bs), f32), qh]
        rhs.append(jnp.concatenate(parts, axis=0).astype(jnp.bfloat16))
    acc_ref[...] = jnp.zeros((LANES, bs), f32)

    chains = [(t, hq) for t in range(2) for hq in range(2)]
    cols = [slice(hq * half, (hq + 1) * half) for _, hq in chains]

    def logits(slot, k):
        blocks = [jnp.minimum(MOBA_STEP * k + j, nb - 1) for j in range(MOBA_STEP)]
        rows = [pl.multiple_of(n * bs, bs) for n in blocks]
        delta = [jnp.clip(qb - n, 0, n_delta - 1) for n in blocks]
        for c, (t, _) in enumerate(chains):
            for j in range(MOBA_STEP):
                s_ref[slot, MOBA_STEP * c + j] = (_dot(kaug_ref[t, pl.ds(rows[j], bs), :], rhs[t][:, cols[c]])
                                                  + bias_ref[t, delta[j], :, cols[c]])

    def step(slot, k, carry):
        blocks = [MOBA_STEP * k + j for j in range(MOBA_STEP)]
        s = [[s_ref[slot, MOBA_STEP * c + j] for j in range(MOBA_STEP)] for c in range(4)]
        m_new = [functools.reduce(jnp.maximum, [carry[c][0]] + [jnp.max(x, axis=0, keepdims=True) for x in s[c]])
                 for c in range(4)]
        alpha = [jnp.exp2(carry[c][0] - m_new[c]) for c in range(4)]
        p = [[jnp.exp2(x - m_new[c]) for x in s[c]] for c in range(4)]
        l_new = [alpha[c] * carry[c][1] + sum(jnp.sum(x, axis=0, keepdims=True) for x in p[c]) for c in range(4)]
        pv = [sum(_dot(vt_ref[n, t * HEAD_DIM:(t + 1) * HEAD_DIM, :], p[c][j].astype(jnp.bfloat16))
                  for j, n in enumerate(blocks)) for c, (t, _) in enumerate(chains)]
        pieces = [alpha[c] * acc_ref[t * HEAD_DIM:(t + 1) * HEAD_DIM, cols[c]] + pv[c]
                  for c, (t, _) in enumerate(chains)]
        acc_ref[...] = jnp.concatenate([jnp.concatenate(pieces[:2], axis=1),
                                        jnp.concatenate(pieces[2:], axis=1)], axis=0)
        return tuple((m_new[c], l_new[c]) for c in range(4))

    init = tuple((jnp.full((1, half), NEG, f32), jnp.zeros((1, half), f32)) for _ in range(4))
    n_steps = qb // MOBA_STEP + 1
    logits(0, 0)

    def two_steps(i, carry):
        logits(1, 2 * i + 1)
        carry = step(0, 2 * i, carry)
        logits(0, 2 * i + 2)
        return step(1, 2 * i + 1, carry)

    fin = lax.fori_loop(0, (n_steps + 1) // 2, two_steps, init)
    l_all = jnp.concatenate([jnp.broadcast_to(jnp.concatenate([fin[2 * t][1], fin[2 * t + 1][1]], axis=1),
                                              (HEAD_DIM, bs)) for t in range(2)], axis=0)
    o_ref[...] = (acc_ref[...] / l_all).T


def moba_attention(proj3, bias_tab, q_off, k_off, v_off):
    B, S, _ = proj3.shape
    nb = S // MOBA_BLOCK
    assert nb % (2 * MOBA_STEP) == 0, "the two-step loop needs an even number of key-block steps"
    n_delta = bias_tab.shape[1]
    pairs = MOBA_HEADS // 2
    kern = functools.partial(_moba_kernel, nb=nb, n_delta=n_delta)
    return pl.pallas_call(
        kern,
        grid=(pairs, B, nb),
        in_specs=[pl.BlockSpec((None, MOBA_BLOCK, LANES), lambda h, b, i: (b, i, q_off + h)),
                  pl.BlockSpec((None, S, LANES), lambda h, b, i: (b, 0, k_off + h)),
                  pl.BlockSpec((None, S, LANES), lambda h, b, i: (b, 0, v_off + h)),
                  pl.BlockSpec((2, n_delta, MOBA_BLOCK, MOBA_BLOCK), lambda h, b, i: (h, 0, 0, 0))],
        out_specs=pl.BlockSpec((None, MOBA_BLOCK, LANES), lambda h, b, i: (b, i, h)),
        out_shape=jax.ShapeDtypeStruct((B, S, pairs * LANES), jnp.float32),
        scratch_shapes=[pltpu.VMEM((2, S, LANES), jnp.bfloat16),
                        pltpu.VMEM((nb, LANES, MOBA_BLOCK), jnp.bfloat16),
                        pltpu.VMEM((nb, LANES), jnp.float32),
                        pltpu.VMEM((LANES, MOBA_BLOCK), jnp.float32),
                        pltpu.VMEM((2, 4 * MOBA_STEP, MOBA_BLOCK, MOBA_BLOCK // 2), jnp.float32)],
        compiler_params=_params(("parallel", "parallel", "arbitrary")),
        name="moba",
    )(proj3, proj3, proj3, bias_tab)


def _dil_kernel(q_ref, k_ref, v_ref, bias_ref, o_ref, lw_ref, *, dilation, seq):
    span = DIL_SPAN
    bps = seq // dilation // span
    scale = HEAD_DIM ** -0.5 * LOG2E
    lane = lax.broadcasted_iota(jnp.int32, (1, LANES), 1)
    col = lax.broadcasted_iota(jnp.int32, (1, 2 * span), 1)
    hm = [lane < HEAD_DIM, lane >= HEAD_DIM]
    unroll = 4

    def body(it, carry):
        blocks = []
        for u in range(unroll):
            j = it * unroll + u
            r = j // bps
            i = j - r * bps
            start = r + i * (span * dilation)
            prev = jnp.maximum(start - span * dilation, r)
            rows = lambda s0: pl.ds(s0, span, stride=dilation)
            q = q_ref[rows(start), :] * scale
            kcat = jnp.concatenate([k_ref[rows(prev), :], k_ref[rows(start), :]], axis=0).astype(jnp.bfloat16)
            vcat = jnp.concatenate([v_ref[rows(prev), :], v_ref[rows(start), :]], axis=0).astype(jnp.bfloat16)
            blocks.append((start, q, kcat, vcat, (i == 0) & (col < span)))
        chains = [(u, t) for u in range(unroll) for t in range(2)]
        s = [jnp.where(blocks[u][4], NEG,
                       _dot_nt(jnp.where(hm[t], blocks[u][1], 0.0).astype(jnp.bfloat16), blocks[u][2]) + bias_ref[t])
             for u, t in chains]
        m = [jnp.max(x, axis=-1, keepdims=True) for x in s]
        e = [jnp.exp2(x - mm) for x, mm in zip(s, m)]
        l = [jnp.sum(x, axis=-1, keepdims=True) for x in e]
        pv = [_dot(x.astype(jnp.bfloat16), blocks[u][3]) for x, (u, _) in zip(e, chains)]
        for u in range(unroll):
            a, b = 2 * u, 2 * u + 1
            rows = pl.ds(blocks[u][0], span, stride=dilation)
            o_ref[rows, :] = jnp.where(hm[0], pv[a] / l[a], pv[b] / l[b])
            lw_ref[rows, :] = jnp.where(hm[0], m[a] * LN2 + jnp.log(l[a]), m[b] * LN2 + jnp.log(l[b]))
        return carry

    lax.fori_loop(0, seq // span // unroll, body, 0)


def dilated_attention(proj3, bias_tab, q_off, k_off, v_off, group):
    B, S, _ = proj3.shape
    dilation = DIL_GROUPS[group][1]
    pairs = DIL_HEADS_PER_GROUP // 2
    slab = lambda off: pl.BlockSpec((None, S, LANES), lambda b, h: (b, 0, off + h))
    out = jax.ShapeDtypeStruct((B, S, pairs * LANES), jnp.float32)
    return pl.pallas_call(
        functools.partial(_dil_kernel, dilation=dilation, seq=S),
        grid=(B, pairs),
        in_specs=[slab(q_off), slab(k_off), slab(v_off),
                  pl.BlockSpec((None, 2, DIL_SPAN, 2 * DIL_SPAN), lambda b, h: (group, h, 0, 0))],
        out_specs=[slab(0), slab(0)],
        out_shape=[out, out],
        compiler_params=_params(("parallel", "parallel")),
        name="dilated",
    )(proj3, proj3, proj3, bias_tab)


def _rwkv_prep_kernel(*refs, width, has_res, rows_per_seq):
    if has_res:
        (z_ref, zl_ref, mu_ref, w0_ref, a0_ref, wa_ref, gup_ref, kk_ref, ka_ref, bd_ref,
         vf_ref, v0_ref, mvd_ref, mvu_ref,
         r_o, lw_o, k_o, v_o, kn_o, b_o, g_o) = refs
    else:
        (z_ref, zl_ref, mu_ref, w0_ref, a0_ref, wa_ref, gup_ref, kk_ref, ka_ref, bd_ref,
         r_o, lw_o, k_o, v_o, kn_o, b_o, g_o) = refs
    i = pl.program_id(0)
    W = width
    z = z_ref[...]
    tm = z.shape[0]
    row = lax.broadcasted_iota(jnp.int32, (tm, 1), 0)
    seq_start = (i % rows_per_seq) == 0
    last = jnp.where(seq_start, 0.0, zl_ref[7:8, :])
    zp = jnp.where(row == 0, last, pltpu.roll(z, 1, 0))
    zf = z + mu_ref[...] * (zp - z)
    lora = zf[:, 3 * W:3 * W + LANES]
    lane = lax.broadcasted_iota(jnp.int32, (1, LANES), 1)
    lora = jnp.where(lane < RWKV_DECAY_LORA, jnp.tanh(lora), lora)
    wa = _dot3(lora, wa_ref[...])
    g = _dot3(_sigmoid(zf[:, 3 * W + LANES:3 * W + 2 * LANES]), gup_ref[...])
    g_o[...] = g
    v_all = zf[:, 2 * W:3 * W]
    if has_res:
        mix = _dot3(_dot3(v_all, mvd_ref[...]), mvu_ref[...])
    for c in range(W // LANES):
        sl = slice(c * LANES, (c + 1) * LANES)
        x = w0_ref[:, sl] + wa[:, sl]
        sp = jnp.maximum(-x, 0.0) + jnp.log(1.0 + jnp.exp(-jnp.abs(x)))
        lw_o[:, sl] = -jnp.exp(-sp - 0.5)
        a = _sigmoid(a0_ref[:, sl] + wa[:, W + c * LANES:W + (c + 1) * LANES])
        r_o[:, sl] = zf[:, sl]
        k = zf[:, W + c * LANES:W + (c + 1) * LANES]
        v = v_all[:, sl]
        if has_res:
            v = v + (vf_ref[:, sl] - v) * _sigmoid(v0_ref[:, sl] + mix[:, sl])
        v_o[:, sl] = v
        kk = k * kk_ref[:, sl]
        ss = _dot_sel(kk * kk, bd_ref[...])
        kn = kk / jnp.maximum(jnp.sqrt(ss), 1e-12)
        kn_o[:, sl] = kn
        b_o[:, sl] = kn * a
        k_o[:, sl] = k * (1.0 + (a - 1.0) * ka_ref[:, sl])


def rwkv_prep(proj, seq, tm, mu, w0, a0, wa_up, g_up, k_k, k_a, bd, res):
    T = proj.shape[0]
    W = RWKV_HEADS * HEAD_DIM
    cols = mu.shape[-1]
    row1 = lambda n: pl.BlockSpec((1, n), lambda i: (0, 0))
    full = lambda a: pl.BlockSpec(a.shape, lambda i: (0, 0))
    tile = pl.BlockSpec((tm, W), lambda i: (i, 0))
    in_specs = [pl.BlockSpec((tm, cols), lambda i: (i, 0)),
                pl.BlockSpec((8, cols), lambda i: (jnp.maximum(i * (tm // 8) - 1, 0), 0)),
                row1(cols), row1(W), row1(W), full(wa_up), full(g_up), row1(W), row1(W), full(bd)]
    args = [proj, proj, mu, w0, a0, wa_up, g_up, k_k, k_a, bd]
    if res is not None:
        v_first, v0, mvd, mvu = res
        in_specs += [tile, row1(W), full(mvd), full(mvu)]
        args += [v_first, v0, mvd, mvu]
    out = jax.ShapeDtypeStruct((T, W), jnp.float32)
    kern = functools.partial(_rwkv_prep_kernel, width=W, has_res=res is not None, rows_per_seq=seq // tm)
    return pl.pallas_call(
        kern,
        grid=(T // tm,),
        in_specs=in_specs,
        out_specs=[tile] * 7,
        out_shape=[out] * 7,
        compiler_params=_params(("parallel",)),
        name="rwkv_prep",
    )(*args)


def _stack_heads(x, lane):
    return jnp.concatenate([jnp.where(lane < HEAD_DIM, x, 0.0), jnp.where(lane >= HEAD_DIM, x, 0.0)], axis=0)


def _rwkv_chunk_kernel(r_ref, lw_ref, k_ref, v_ref, kn_ref, b_ref, g_ref, rk_ref, bd_ref,
                       rhat_ref, y0_ref, e_ref, g_out_ref, d_out_ref, *, chunks):
    C = RWKV_CHUNK
    C2, C4 = 2 * C, 4 * C
    f32 = jnp.float32
    tri = (lax.broadcasted_iota(jnp.int32, (C, C), 0) >= lax.broadcasted_iota(jnp.int32, (C, C), 1)).astype(f32)
    lane = lax.broadcasted_iota(jnp.int32, (1, LANES), 1)
    rho = lax.broadcasted_iota(jnp.int32, (C4, C4), 0)
    sig = lax.broadcasted_iota(jnp.int32, (C4, C4), 1)
    keep = jnp.where(rho >= C2, rho & (C - 1), (rho & (C - 1)) - 1) >= (sig & (C - 1))
    eye2 = (lax.broadcasted_iota(jnp.int32, (C2, C2), 0) == lax.broadcasted_iota(jnp.int32, (C2, C2), 1)).astype(f32)
    eye_l = (lax.broadcasted_iota(jnp.int32, (LANES, LANES), 0)
             == lax.broadcasted_iota(jnp.int32, (LANES, LANES), 1)).astype(f32)
    zeros2 = jnp.zeros((C2, LANES), f32)
    ch = range(chunks)
    rows = [slice(c * C, (c + 1) * C) for c in ch]
    cum = [_dot(tri, lw_ref[rows[c], :], precision=HI) for c in ch]
    st = []
    for c in ch:
        r, lw, k, v = r_ref[rows[c], :], lw_ref[rows[c], :], k_ref[rows[c], :], v_ref[rows[c], :]
        kn, bb = kn_ref[rows[c], :], b_ref[rows[c], :]
        cum_last = cum[c][C - 1:C, :]
        e_out = jnp.exp(-cum[c])
        e_tail = jnp.exp(cum_last - cum[c])
        r_t = r * jnp.exp(cum[c])
        st.append(dict(
            r_t=r_t, decay=jnp.exp(cum_last),
            a2=_stack_heads(-kn * jnp.exp(cum[c] - lw), lane), r2=_stack_heads(r_t, lane),
            b2=_stack_heads(bb * e_out, lane), k2=_stack_heads(k * e_out, lane), v2=_stack_heads(v, lane),
            bh2=_stack_heads(bb * e_tail, lane), kh2=_stack_heads(k * e_tail, lane),
            e=_dot_sel(r * k * rk_ref[...], bd_ref[...]) * v * g_ref[rows[c], :]))
    quad = [jnp.where(keep, _dot3(jnp.concatenate([s["a2"], s["r2"]], axis=0),
                                  jnp.concatenate([s["b2"], s["k2"]], axis=0), _NT), 0.0) for s in st]
    pw = [q[:C2, :C2] for q in quad]
    t_inv = [eye2 + p for p in pw]
    x = [_dot3(q[:C2, C2:], s["v2"]) for q, s in zip(quad, st)]
    for level in range(int(math.log2(C)) - 1):
        mm = _dot3 if level < 2 else (lambda a, b: _dot(a.astype(jnp.bfloat16), b.astype(jnp.bfloat16)))
        pw = [mm(p, p) for p in pw]
        t_inv = [t + mm(t, p) for t, p in zip(t_inv, pw)]
    au = [_dot3(t, jnp.concatenate([s["a2"], xx], axis=1)) for t, s, xx in zip(t_inv, st, x)]
    my = [_dot3(q[C2:, :], jnp.concatenate([a, jnp.concatenate([zeros2, s["v2"]], axis=1)], axis=0))
          for q, a, s in zip(quad, au, st)]
    gm = [eye_l * s["decay"] + _dot3(a[:, :LANES], s["bh2"], _TN) for a, s in zip(au, st)]
    dm = [_dot3(jnp.concatenate([a[:, LANES:], s["v2"]], axis=0),
                jnp.concatenate([s["bh2"], s["kh2"]], axis=0), _TN) for a, s in zip(au, st)]
    rhat_ref[...] = jnp.concatenate([s["r_t"] + m[:C, :LANES] + m[C:, :LANES] for s, m in zip(st, my)], axis=0)
    y0_ref[...] = jnp.concatenate([m[:C, LANES:] + m[C:, LANES:] for m in my], axis=0)
    e_ref[...] = jnp.concatenate([s["e"] for s in st], axis=0)
    g_out_ref[...] = jnp.stack(gm)
    d_out_ref[...] = jnp.stack(dm)


def rwkv_chunk(r, lw, k, v, kn, b, g, r_k, bd, batch, chunks=16):
    T, W = r.shape
    S = T // batch
    C = RWKV_CHUNK
    nc = S // C
    pairs = W // LANES
    steps = nc // chunks
    tile = pl.BlockSpec((chunks * C, LANES), lambda bi, h, c: (bi * steps + c, h))
    mat = pl.BlockSpec((None, None, chunks, LANES, LANES), lambda bi, h, c: (bi, h, c, 0, 0))
    tw = jax.ShapeDtypeStruct((T, W), jnp.float32)
    gd = jax.ShapeDtypeStruct((batch, pairs, nc, LANES, LANES), jnp.float32)
    return pl.pallas_call(
        functools.partial(_rwkv_chunk_kernel, chunks=chunks),
        grid=(batch, pairs, steps),
        in_specs=[tile] * 7 + [pl.BlockSpec((1, LANES), lambda bi, h, c: (0, h)),
                               pl.BlockSpec((LANES, LANES), lambda bi, h, c: (0, 0))],
        out_specs=[tile, tile, tile, mat, mat],
        out_shape=[tw, tw, tw, gd, gd],
        compiler_params=_params(("parallel", "parallel", "parallel")),
        name="rwkv_chunk",
    )(r, lw, k, v, kn, b, g, r_k, bd)


def _rwkv_state_kernel(rhat_ref, y0_ref, g_ref, e_ref, gm_ref, dm_ref, lng_ref, lnb_ref, bd_ref,
                       o_ref, state_ref, *, chunks, group):
    C = RWKV_CHUNK

    @pl.when(pl.program_id(2) == 0)
    def _():
        state_ref[...] = jnp.zeros_like(state_ref)

    bd = bd_ref[...]
    inv_n = 1.0 / HEAD_DIM

    def body(c, carry):
        rows = pl.ds(pl.multiple_of(c * C, C), C)
        pairs = range(group)
        state = [state_ref[p] for p in pairs]
        y = [_dot3(rhat_ref[rows, p * LANES:(p + 1) * LANES], state[p], _NT) for p in pairs]
        new_state = [_dot3(state[p], gm_ref[p, c]) + dm_ref[p, c] for p in pairs]
        y = jnp.concatenate(y, axis=1) + y0_ref[rows, :]
        state_ref[...] = jnp.stack(new_state)
        mean = jnp.concatenate([_dot_sel(y[:, p * LANES:(p + 1) * LANES], bd) for p in pairs], axis=1) * inv_n
        d = y - mean
        dd = d * d
        var = jnp.concatenate([_dot_sel(dd[:, p * LANES:(p + 1) * LANES], bd) for p in pairs], axis=1) * inv_n
        yn = d * lax.rsqrt(var + RWKV_GN_EPS) * lng_ref[...] + lnb_ref[...]
        o_ref[rows, :] = yn * g_ref[rows, :] + e_ref[rows, :]
        return carry

    lax.fori_loop(0, chunks, body, 0)


def rwkv_state(rhat, y0, g, e, gmat, dmat, ln_g, ln_b, bd, chunks=16, group=3):
    T, W = rhat.shape
    batch, pairs, nc = gmat.shape[:3]
    C = RWKV_CHUNK
    steps = nc // chunks
    gw = group * LANES
    tile = pl.BlockSpec((chunks * C, gw), lambda bi, h, c: (bi * steps + c, h))
    mat = pl.BlockSpec((None, group, chunks, LANES, LANES), lambda bi, h, c: (bi, h, c, 0, 0))
    row = pl.BlockSpec((1, gw), lambda bi, h, c: (0, h))
    return pl.pallas_call(
        functools.partial(_rwkv_state_kernel, chunks=chunks, group=group),
        grid=(batch, pairs // group, steps),
        in_specs=[tile] * 4 + [mat, mat, row, row, pl.BlockSpec((LANES, LANES), lambda bi, h, c: (0, 0))],
        out_specs=tile,
        out_shape=jax.ShapeDtypeStruct((T, W), jnp.float32),
        scratch_shapes=[pltpu.VMEM((group, LANES, LANES), jnp.float32)],
        compiler_params=_params(("parallel", "parallel", "arbitrary")),
        name="rwkv_state",
    )(rhat, y0, g, e, gmat, dmat, ln_g, ln_b, bd)


def _merge_kernel(oa_ref, ob_ref, oc0_ref, oc1_ref, oc2_ref, lw0_ref, lw1_ref, lw2_ref, ga_ref, gb_ref, gc_ref,
                  pa_ref, pb_ref, pc_ref, o_ref):
    lw = [lw0_ref[...], lw1_ref[...], lw2_ref[...]]
    mx = jnp.maximum(jnp.maximum(lw[0], lw[1]), lw[2])
    wgt = [jnp.exp(x - mx) for x in lw]
    o_c = ((wgt[0] * oc0_ref[...] + wgt[1] * oc1_ref[...] + wgt[2] * oc2_ref[...])
           / (wgt[0] + wgt[1] + wgt[2]))
    bf = jnp.bfloat16
    merged = (_sigmoid(ga_ref[...]) * _dot(oa_ref[...].astype(bf), pa_ref[...])
              + _sigmoid(gb_ref[...]) * _dot(ob_ref[...].astype(bf), pb_ref[...])
              + _sigmoid(gc_ref[...]) * _dot(o_c.astype(bf), pc_ref[...]))
    o_ref[...] = merged.astype(o_ref.dtype)


def merge_branches(o_a, o_b, o_c, lw_c, proj, gate_off, p_a, p_b, p_c, tm, tn):
    T = o_a.shape[0]
    D = p_a.shape[1]
    nj = D // tn
    grp = pl.BlockSpec((tm, o_c[0].shape[1]), lambda i, j: (i, 0))
    gate = lambda n: pl.BlockSpec((tm, tn), lambda i, j: (i, gate_off + n * nj + j))
    wspec = lambda a: pl.BlockSpec((a.shape[0], tn), lambda i, j: (0, j))
    return pl.pallas_call(
        _merge_kernel,
        grid=(T // tm, nj),
        in_specs=[pl.BlockSpec((tm, o_a.shape[1]), lambda i, j: (i, 0)),
                  pl.BlockSpec((tm, o_b.shape[1]), lambda i, j: (i, 0)),
                  grp, grp, grp, grp, grp, grp, gate(0), gate(1), gate(2), wspec(p_a), wspec(p_b), wspec(p_c)],
        out_specs=pl.BlockSpec((tm, tn), lambda i, j: (i, j)),
        out_shape=jax.ShapeDtypeStruct((T, D), jnp.bfloat16),
        compiler_params=_params(("parallel", "arbitrary")),
        name="merge",
    )(o_a, o_b, *o_c, *lw_c, proj, proj, proj, p_a, p_b, p_c)


def _out_ln_kernel(mg_ref, w_ref, x_ref, g1_ref, lg_ref, lb_ref, o_ref, *, alpha):
    mix = _dot(mg_ref[...], w_ref[...])
    y = alpha * x_ref[...] + (1.0 + g1_ref[...]) * mix
    o_ref[...] = _layer_norm(y, lg_ref[...], lb_ref[...])


def out_ln(merged, w_o, x2, g1, ln_g, ln_b, seq, tm, alpha):
    T, D = x2.shape
    per = seq // tm
    tile = pl.BlockSpec((tm, D), lambda i: (i, 0))
    row = pl.BlockSpec((1, D), lambda i: (0, 0))
    return pl.pallas_call(
        functools.partial(_out_ln_kernel, alpha=alpha),
        grid=(T // tm,),
        in_specs=[tile, pl.BlockSpec((D, D), lambda i: (0, 0)), tile,
                  pl.BlockSpec((None, 1, D), lambda i: (i // per, 0, 0)), row, row],
        out_specs=tile,
        out_shape=jax.ShapeDtypeStruct((T, D), jnp.float32),
        compiler_params=_params(("parallel",)),
        name="out_ln",
    )(merged, w_o, x2, g1, ln_g, ln_b)


def _router_kernel(x_ref, sc_ref, sh_ref, w_ref, b_ref, id_ref, gw_ref, h_ref):
    h = x_ref[...] * (1.0 + sc_ref[...]) + sh_ref[...]
    h_ref[...] = h.astype(h_ref.dtype)
    lg = _dot3(h, w_ref[...]) + b_ref[...]
    G, EPG = MOE_GROUPS, MOE_EXPERTS_PER_GROUP
    lane = lax.broadcasted_iota(jnp.int32, (1, LANES), 1).astype(jnp.float32)
    first = lambda hit: jnp.min(jnp.where(hit, lane, float(LANES)), axis=-1, keepdims=True)
    is_grp = lane < G
    gmax = jnp.max(jnp.where(is_grp, lg, -jnp.inf), axis=-1, keepdims=True)
    ge = jnp.where(is_grp, jnp.exp(jnp.where(is_grp, lg, gmax) - gmax), 0.0)
    prob = ge / jnp.sum(ge, axis=-1, keepdims=True)
    grp_p = jnp.max(prob, axis=-1, keepdims=True)
    grp_i = first(is_grp & (prob == grp_p))
    lo = G + grp_i * EPG
    el = jnp.where((lane >= lo) & (lane < lo + EPG), lg, -jnp.inf)
    l1 = jnp.max(el, axis=-1, keepdims=True)
    i1 = first(el == l1)
    el = jnp.where(lane == i1, -jnp.inf, el)
    l2 = jnp.max(el, axis=-1, keepdims=True)
    i2 = first(el == l2)
    t = jnp.exp(l2 - l1)
    w1 = grp_p / (1.0 + t)
    id_ref[...] = jnp.where(lane == 0, i1 - G, jnp.where(lane == 1, i2 - G, 0.0)).astype(jnp.int32)
    gw_ref[...] = jnp.where(lane == 0, w1, jnp.where(lane == 1, w1 * t, 0.0))


def router(x2, sc, sh, w_r, b_r, seq, tm):
    T, D = x2.shape
    per = seq // tm
    mod = pl.BlockSpec((None, 1, D), lambda i: (i // per, 0, 0))
    narrow = pl.BlockSpec((tm, LANES), lambda i: (i, 0))
    return pl.pallas_call(
        _router_kernel,
        grid=(T // tm,),
        in_specs=[pl.BlockSpec((tm, D), lambda i: (i, 0)), mod, mod,
                  pl.BlockSpec((D, LANES), lambda i: (0, 0)), pl.BlockSpec((1, LANES), lambda i: (0, 0))],
        out_specs=[narrow, narrow, pl.BlockSpec((tm, D), lambda i: (i, 0))],
        out_shape=[jax.ShapeDtypeStruct((T, LANES), jnp.int32), jax.ShapeDtypeStruct((T, LANES), jnp.float32),
                   jax.ShapeDtypeStruct((T, D), jnp.bfloat16)],
        compiler_params=_params(("parallel",)),
        name="router",
    )(x2, sc, sh, w_r, b_r)


def _experts_kernel(be_ref, nu_ref, x_ref, wg_ref, wu_ref, wd_ref, o_ref, wg_s, wu_s, wd_s):
    i = pl.program_id(0)
    used = i < nu_ref[0]
    new_expert = (i == 0) | (be_ref[i] != be_ref[jnp.maximum(i - 1, 0)])

    @pl.when(used & new_expert)
    def _():
        wg_s[...] = wg_ref[...].astype(wg_s.dtype)
        wu_s[...] = wu_ref[...].astype(wu_s.dtype)
        wd_s[...] = wd_ref[...].astype(wd_s.dtype)

    @pl.when(used)
    def _():
        x = x_ref[...]
        gate = _dot(x, wg_s[...])
        hid = gate * _sigmoid(gate) * _dot(x, wu_s[...])
        o_ref[...] = _dot(hid.astype(jnp.bfloat16), wd_s[...])

    @pl.when(jnp.logical_not(used))
    def _():
        o_ref[...] = jnp.zeros_like(o_ref)


def experts(xs, block_e, n_used, w_gate, w_up, w_down, layer):
    R, D = xs.shape
    F = w_gate.shape[-1]
    n_blocks = R // MOE_BLOCK
    grid_spec = pltpu.PrefetchScalarGridSpec(
        num_scalar_prefetch=2,
        grid=(n_blocks,),
        in_specs=[pl.BlockSpec((MOE_BLOCK, D), lambda i, be, nu: (i, 0)),
                  pl.BlockSpec((None, None, D, F), lambda i, be, nu: (layer, be[i], 0, 0)),
                  pl.BlockSpec((None, None, D, F), lambda i, be, nu: (layer, be[i], 0, 0)),
                  pl.BlockSpec((None, None, F, D), lambda i, be, nu: (layer, be[i], 0, 0))],
        out_specs=pl.BlockSpec((MOE_BLOCK, D), lambda i, be, nu: (i, 0)),
        scratch_shapes=[pltpu.VMEM((D, F), jnp.bfloat16), pltpu.VMEM((D, F), jnp.bfloat16),
                        pltpu.VMEM((F, D), jnp.bfloat16)],
    )
    return pl.pallas_call(
        _experts_kernel,
        grid_spec=grid_spec,
        out_shape=jax.ShapeDtypeStruct((R, D), jnp.float32),
        compiler_params=_params(("arbitrary",)),
        name="experts",
    )(block_e, n_used, xs, w_gate, w_up, w_down)


def _combine_ln_kernel(x_ref, f0_ref, f1_ref, gw_ref, g2_ref, lg_ref, lb_ref, o_ref, *, alpha):
    gw = gw_ref[...]
    ffn = gw[:, 0:1] * f0_ref[...] + gw[:, 1:2] * f1_ref[...]
    y = alpha * x_ref[...] + (1.0 + g2_ref[...]) * ffn
    o_ref[...] = _layer_norm(y, lg_ref[...], lb_ref[...])


def combine_ln(x2, f0, f1, gws, g2, ln_g, ln_b, seq, tm, alpha):
    T, D = x2.shape
    per = seq // tm
    tile = pl.BlockSpec((tm, D), lambda i: (i, 0))
    row = pl.BlockSpec((1, D), lambda i: (0, 0))
    return pl.pallas_call(
        functools.partial(_combine_ln_kernel, alpha=alpha),
        grid=(T // tm,),
        in_specs=[tile, tile, tile, pl.BlockSpec((tm, LANES), lambda i: (i, 0)),
                  pl.BlockSpec((None, 1, D), lambda i: (i // per, 0, 0)), row, row],
        out_specs=tile,
        out_shape=jax.ShapeDtypeStruct((T, D), jnp.float32),
        compiler_params=_params(("parallel",)),
        name="combine_ln",
    )(x2, f0, f1, gws, g2, ln_g, ln_b)


def _route(expert_id):
    T = expert_id.shape[0]
    E = MOE_GROUPS * MOE_EXPERTS_PER_GROUP
    A = T * MOE_TOPK
    n_blocks = (A + E * (MOE_BLOCK - 1) + MOE_BLOCK - 1) // MOE_BLOCK
    flat_e = expert_id.reshape(A)
    e_s, order = lax.sort_key_val(flat_e, jnp.arange(A, dtype=jnp.int32))
    bounds = jnp.searchsorted(e_s, jnp.arange(E + 1, dtype=jnp.int32), side='left').astype(jnp.int32)
    start = bounds[:E]
    counts = bounds[1:] - start
    padded = (counts + MOE_BLOCK - 1) // MOE_BLOCK * MOE_BLOCK
    pad_end = jnp.cumsum(padded)
    pad_start = pad_end - padded
    block_e = jnp.minimum(jnp.sum(pad_end[None, :] <= (jnp.arange(n_blocks) * MOE_BLOCK)[:, None], axis=1),
                          E - 1).astype(jnp.int32)
    blk_rank = jnp.arange(n_blocks, dtype=jnp.int32) * MOE_BLOCK - pad_start[block_e]
    rank = blk_rank[:, None] + jnp.arange(MOE_BLOCK, dtype=jnp.int32)[None, :]
    valid = (rank < counts[block_e][:, None]).reshape(-1)
    src = order[jnp.clip(start[block_e][:, None] + rank, 0, A - 1).reshape(-1)]
    spread = jnp.arange(n_blocks * MOE_BLOCK, dtype=jnp.int32) % T
    slot_tok = jnp.where(valid, src // MOE_TOPK, spread).astype(jnp.int32)
    pos = jnp.arange(A, dtype=jnp.int32)[None, :]
    in_e = (pos >= start[:, None]) & (pos < bounds[1:, None])
    dest = pos[0] + jnp.sum(jnp.where(in_e, (pad_start - start)[:, None], 0), axis=0).astype(jnp.int32)
    _, slot_of = lax.sort_key_val(order, dest)
    n_used = (pad_end[-1] // MOE_BLOCK).astype(jnp.int32).reshape(1)
    return slot_tok, slot_of.reshape(T, MOE_TOPK), block_e, n_used


def kernel(x, c, rel_bias, w_in, p_a, p_b, p_c, w_o, rwkv_mu, rwkv_w0, rwkv_w_up, rwkv_a0, rwkv_a_up,
           rwkv_g_up, rwkv_k_k, rwkv_k_a, rwkv_r_k, rwkv_ln_g, rwkv_ln_b, rwkv_v0, rwkv_mv_down,
           rwkv_mv_up, w_ada, b_ada, ln1_g, ln1_b, ln2_g, ln2_b, router_grp_w, router_grp_b,
           router_exp_w, router_exp_b, exp_w_gate, exp_w_up, exp_w_down):
    B, S, D = x.shape
    depth = w_in.shape[0]
    T = B * S
    bf = jnp.bfloat16
    W = RWKV_HEADS * HEAD_DIM
    wa_w = MOBA_HEADS * HEAD_DIM
    wc_w = len(DIL_GROUPS) * DIL_HEADS_PER_GROUP * HEAD_DIM
    rw_cols = rwkv_mu.shape[-1]
    alpha = (2 * depth) ** 0.25
    off_c = 3 * wa_w
    off_b = off_c + 3 * wc_w
    off_g = off_b + rw_cols
    new_a = rw_cols
    new_c = new_a + 3 * wa_w
    new_g = new_c + 3 * wc_w
    w_in_p = jnp.concatenate([w_in[:, :, off_b:off_g], w_in[:, :, :off_b], w_in[:, :, off_g:]], axis=-1).astype(bf)
    p_a_b, p_b_b, p_c_b, w_o_b = p_a.astype(bf), p_b.astype(bf), p_c.astype(bf), w_o.astype(bf)

    bias_h = rel_bias.T.astype(jnp.float32)
    moba_tab = moba_bias_table(bias_h[:MOBA_HEADS], S // MOBA_BLOCK)
    dil_tab = dil_bias_table(bias_h[MOBA_HEADS:])

    c8 = jnp.zeros((8, D), jnp.float32).at[:B].set(c)
    mod = ada_mod(c8, w_ada, b_ada)[:, :B]

    hd_idx = jnp.arange(LANES) // HEAD_DIM
    bd = (hd_idx[:, None] == hd_idx[None, :]).astype(jnp.float32)
    zeros_w = jnp.zeros((RWKV_DECAY_LORA, W), jnp.float32)
    pad_lora = LANES - RWKV_MV_LORA

    x2 = x.reshape(T, D)
    v_first = None
    for l in range(depth):
        sh1, sc1, g1, sh2, sc2, g2 = [m.reshape(B, 1, D) for m in jnp.split(mod[l], 6, axis=-1)]
        proj = in_proj(x2, sc1, sh1, w_in_p, l, S, 1024, 1024)
        proj3 = proj.reshape(B, S, -1)
        o_a = moba_attention(proj3, moba_tab, new_a // LANES, (new_a + wa_w) // LANES,
                             (new_a + 2 * wa_w) // LANES).reshape(T, wa_w)
        gw = DIL_HEADS_PER_GROUP * HEAD_DIM
        o_c, lw_c = [], []
        for g in range(len(DIL_GROUPS)):
            og, lwg = dilated_attention(proj3, dil_tab, (new_c + g * gw) // LANES,
                                        (new_c + wc_w + g * gw) // LANES, (new_c + 2 * wc_w + g * gw) // LANES, g)
            o_c.append(og.reshape(T, gw))
            lw_c.append(lwg.reshape(T, gw))
        wa_up = jnp.concatenate([jnp.concatenate([rwkv_w_up[l], zeros_w], axis=1),
                                 jnp.concatenate([zeros_w, rwkv_a_up[l]], axis=1)], axis=0)
        res = None
        if l > 0:
            res = (v_first, rwkv_v0[l - 1][None],
                   jnp.pad(rwkv_mv_down[l - 1], ((0, 0), (0, pad_lora))),
                   jnp.pad(rwkv_mv_up[l - 1], ((0, pad_lora), (0, 0))))
        r_, lw_, k_, v_, kn_, b_, g_ = rwkv_prep(proj, S, 512, rwkv_mu[l][None], rwkv_w0[l][None],
                                                 rwkv_a0[l][None], wa_up, rwkv_g_up[l], rwkv_k_k[l][None],
                                                 rwkv_k_a[l][None], bd, res)
        if l == 0:
            v_first = v_
        rhat, y0, e_, gmat, dmat = rwkv_chunk(r_, lw_, k_, v_, kn_, b_, g_, rwkv_r_k[l][None], bd, B)
        o_b = rwkv_state(rhat, y0, g_, e_, gmat, dmat, rwkv_ln_g[l][None], rwkv_ln_b[l][None], bd)
        merged = merge_branches(o_a, o_b, o_c, lw_c, proj, new_g // 1024, p_a_b[l], p_b_b[l], p_c_b[l],
                                512, 1024)
        x2 = out_ln(merged, w_o_b[l], x2, g1, ln1_g[l][None], ln1_b[l][None], S, 512, alpha)
        w_r = jnp.zeros((D, LANES), jnp.float32)
        w_r = w_r.at[:, :MOE_GROUPS].set(router_grp_w[l]).at[:, MOE_GROUPS:MOE_GROUPS + router_exp_w.shape[-1]].set(
            router_exp_w[l])
        b_r = jnp.zeros((1, LANES), jnp.float32)
        b_r = b_r.at[0, :MOE_GROUPS].set(router_grp_b[l]).at[0, MOE_GROUPS:MOE_GROUPS + router_exp_b.shape[-1]].set(
            router_exp_b[l])
        ids, gws, h2 = router(x2, sc2, sh2, w_r, b_r, S, 512)
        slot_tok, slot_of, block_e, n_used = _route(ids[:, :MOE_TOPK])
        y = experts(h2[slot_tok], block_e, n_used, exp_w_gate, exp_w_up, exp_w_down, l)
        x2 = combine_ln(x2, y[slot_of[:, 0]], y[slot_of[:, 1]], gws, g2, ln2_g[l][None], ln2_b[l][None], S, 512,
                        alpha)
    return x2.reshape(B, S, D)
```
